```python
import jax, jax.numpy as jnp
from jax import lax
import numpy as np

D_MODEL = 1024
BATCH = 32
SEQ = 256
DEPTH = 4
DEC_BATCH = 2
DEC_SEQ = 1024
PAST_LEN = 256

GRID_W = 64
H_A = 8
DK_A = 128
DV_A = 128
WA = H_A * DK_A
H_B = 8
DH_B = 128
WB = H_B * DH_B
CHUNK_A = 16
CHUNK_B = 64
D_FF = 2816
N_EXPERTS = 8
TOP_K = 2
D_FF_E = 2816
N_DENSE = (DEPTH + 1) // 2
N_MOE = DEPTH // 2
ALPHA = (2 * DEPTH) ** 0.25
BETA = (8 * DEPTH) ** -0.25
IN_SIZES = (WA, WA, WA, WA, WA, WB, WB, WB, 4 * H_B, D_MODEL, D_MODEL)
N_IN = sum(IN_SIZES)

kernel_name = 'hgrn2_mlstm_bidir_diffusion_step'


def layer_norm(x, g, b, eps=1e-5):
    xf = x.astype(jnp.float32)
    mu = jnp.mean(xf, axis=-1, keepdims=True)
    var = jnp.mean(jnp.square(xf - mu), axis=-1, keepdims=True)
    return ((xf - mu) * lax.rsqrt(var + eps) * g + b).astype(x.dtype)


def rms_norm_heads(x, g, eps=1e-6):
    return x * lax.rsqrt(jnp.mean(jnp.square(x), axis=-1, keepdims=True) + eps) * g


def group_norm_heads(x, g, eps=1e-6):
    mu = jnp.mean(x, axis=-1, keepdims=True)
    var = jnp.mean(jnp.square(x - mu), axis=-1, keepdims=True)
    return (x - mu) * lax.rsqrt(var + eps) * g


def split_heads(t, h):
    return t.reshape(*t.shape[:-1], h, t.shape[-1] // h)


def chunk(x, L):
    B, T, H = x.shape[:3]
    rest = x.shape[3:]
    x = x.reshape(B, T // L, L, H, *rest)
    return jnp.swapaxes(jnp.moveaxis(x, 1, 0), 2, 3)


def unchunk(y):
    Nc, B, H, L = y.shape[:4]
    rest = y.shape[4:]
    y = jnp.moveaxis(jnp.swapaxes(y, 2, 3), 0, 1)
    return y.reshape(B, Nc * L, H, *rest)


def hgrn2_scan(q, k, v, log_f, S0):
    L = CHUNK_A
    mask = jnp.tril(jnp.ones((L, L), dtype=bool))

    def step(S, inp):
        qc, kc, vc, gc = inp
        b = jnp.cumsum(gc, axis=2)
        diff = b[:, :, :, None, :] - b[:, :, None, :, :]
        decay = jnp.exp(jnp.where(mask[:, :, None], diff, -jnp.inf))
        att = jnp.einsum('bhtk,bhtsk,bhsk->bhts', qc, decay, kc)
        o = (jnp.einsum('bhtk,bhkv->bhtv', qc * jnp.exp(b), S)
             + jnp.einsum('bhts,bhsv->bhtv', att, vc))
        b_last = b[:, :, -1:, :]
        S_new = (jnp.exp(b_last[:, :, 0])[..., None] * S
                 + jnp.einsum('bhsk,bhsv->bhkv', kc * jnp.exp(b_last - b), vc))
        return S_new, o

    S_T, o = lax.scan(step, S0, (chunk(q, L), chunk(k, L), chunk(v, L), chunk(log_f, L)))
    return unchunk(o), S_T


def mlstm_scan(q, k, v, ig, lf, C0, n0, m0):
    L = CHUNK_B
    mask = jnp.tril(jnp.ones((L, L), dtype=bool))

    def step(carry, inp):
        C, n, m = carry
        qc, kc, vc, ic, fc = inp
        F = jnp.cumsum(fc, axis=-1)
        logd = jnp.where(mask, F[..., :, None] - F[..., None, :] + ic[..., None, :], -jnp.inf)
        log_inter = F + m[..., None]
        m_q = jnp.maximum(log_inter, jnp.max(logd, axis=-1))
        w = jnp.exp(logd - m_q[..., None])
        a = jnp.exp(log_inter - m_q)
        s = jnp.einsum('bhtd,bhsd->bhts', qc, kc) * w
        num = (a[..., None] * jnp.einsum('bhtd,bhde->bhte', qc, C)
               + jnp.einsum('bhts,bhse->bhte', s, vc))
        den = a * jnp.einsum('bhtd,bhd->bht', qc, n) + jnp.sum(s, axis=-1)
        h = num / jnp.maximum(jnp.abs(den), jnp.exp(-m_q))[..., None]
        F_last = F[..., -1]
        log_w = F_last[..., None] - F + ic
        m_new = jnp.maximum(F_last + m, jnp.max(log_w, axis=-1))
        ws = jnp.exp(log_w - m_new[..., None])
        d0 = jnp.exp(F_last + m - m_new)
        C_new = d0[..., None, None] * C + jnp.einsum('bhs,bhsd,bhse->bhde', ws, kc, vc)
        n_new = d0[..., None] * n + jnp.einsum('bhs,bhsd->bhd', ws, kc)
        return (C_new, n_new, m_new), h

    (C_T, n_T, m_T), h = lax.scan(step, (C0, n0, m0),
                                  (chunk(q, L), chunk(k, L), chunk(v, L), chunk(ig, L), chunk(lf, L)))
    return unchunk(h), (C_T, n_T, m_T)


def depthwise_conv(img, w, b):
    rhs = w[:, :, None, :].astype(img.dtype)
    y = lax.conv_general_dilated(img, rhs, window_strides=(1, 1), padding='SAME',
                                 dimension_numbers=('NHWC', 'HWIO', 'NHWC'),
                                 feature_group_count=img.shape[-1])
    return y + b


def mixer(h, rows, s_a, s_C, s_n, s_m, lb, w_in, b_in, a_norm_g, conv_w, conv_b,
          w_mq, w_mk, f_bias, b_norm_g, w_ba, w_bb, w_out):
    B, T, _ = h.shape
    f32 = jnp.float32
    proj = jnp.einsum('btd,dn->btn', h, w_in) + b_in
    (a_q, a_f_fwd, a_f_bwd, a_i, a_g, b_qk, b_v, b_o, b_gates, g_ma, g_mb) = jnp.split(
        proj, np.cumsum(IN_SIZES)[:-1].tolist(), axis=-1)

    def bidir(t_fwd, t_bwd):
        return jnp.stack([t_fwd, jnp.flip(t_bwd, axis=1)])

    z = bidir(a_f_fwd, a_f_bwd).astype(f32)
    lbb = lb.astype(f32)[:, None, None, :]
    log_f = jnp.logaddexp(jnp.log(lbb), jnp.log1p(-lbb) + jax.nn.log_sigmoid(z))
    k_a = (1.0 - lbb) * jax.nn.sigmoid(-z)
    q_a = split_heads(a_q.astype(f32), H_A)
    v_a = split_heads(a_i.astype(f32), H_A)
    o_a, sa_T = jax.vmap(hgrn2_scan)(bidir(q_a, q_a), split_heads(k_a, H_A), bidir(v_a, v_a),
                                     split_heads(log_f, H_A), jnp.moveaxis(s_a.astype(f32), 1, 0))
    o_a = o_a[0] + jnp.flip(o_a[1], axis=1)
    y_a = (rms_norm_heads(o_a, a_norm_g) * jax.nn.silu(split_heads(a_g.astype(f32), H_A))).reshape(B, T, WA)

    if rows is None:
        img, w_k = b_qk[:, None], conv_w[1:2]
    else:
        img, w_k = b_qk.reshape(B, rows, GRID_W, WB), conv_w
    conv_act = jax.nn.silu(depthwise_conv(img, w_k, conv_b).reshape(B, T, WB))
    ca = split_heads(conv_act, H_B)
    q_b = jnp.einsum('bthd,hde->bthe', ca, w_mq).astype(f32)
    k_b = jnp.einsum('bthd,hde->bthe', ca, w_mk).astype(f32) * (DH_B ** -0.5)
    v_b = split_heads(b_v.astype(f32), H_B)
    gts = b_gates.astype(f32).reshape(B, T, 4, H_B)
    fb = f_bias.astype(f32)
    i_g = bidir(gts[:, :, 0], gts[:, :, 2])
    log_fg = jax.nn.log_sigmoid(bidir(gts[:, :, 1] + fb[0], gts[:, :, 3] + fb[1]))
    h_b, (C_T, n_T, m_T) = jax.vmap(mlstm_scan)(
        bidir(q_b, q_b), bidir(k_b, k_b), bidir(v_b, v_b), i_g, log_fg,
        jnp.moveaxis(s_C.astype(f32), 1, 0), jnp.moveaxis(s_n.astype(f32), 1, 0),
        jnp.moveaxis(s_m.astype(f32), 1, 0))
    h_b = h_b[0] + jnp.flip(h_b[1], axis=1)
    y_b = (group_norm_heads(h_b, b_norm_g) * jax.nn.sigmoid(split_heads(b_o.astype(f32), H_B))).reshape(B, T, WB)

    merged = (jax.nn.sigmoid(g_ma.astype(f32)) * jnp.einsum('btc,cd->btd', y_a, w_ba)
              + jax.nn.sigmoid(g_mb.astype(f32)) * jnp.einsum('btc,cd->btd', y_b, w_bb))
    y = jnp.einsum('btd,de->bte', merged, w_out)
    new_states = (jnp.moveaxis(sa_T, 0, 1), jnp.moveaxis(C_T, 0, 1),
                  jnp.moveaxis(n_T, 0, 1), jnp.moveaxis(m_T, 0, 1))
    return y.astype(h.dtype), new_states


def swiglu(h, w1, w3, w2):
    return jnp.einsum('btf,fd->btd', jax.nn.silu(jnp.einsum('btd,df->btf', h, w1))
                      * jnp.einsum('btd,df->btf', h, w3), w2)


def moe_swiglu(h, wr, br, w1, w3, w2):
    B, T, D = h.shape
    t = h.reshape(B * T, D)
    logits = jnp.matmul(t, wr).astype(jnp.float32) + br
    probs = jax.nn.softmax(logits, axis=-1)
    top_p, top_i = lax.top_k(probs, TOP_K)
    top_p = top_p / jnp.sum(top_p, axis=-1, keepdims=True)
    comb = jnp.sum(jax.nn.one_hot(top_i, N_EXPERTS, dtype=jnp.float32) * top_p[..., None], axis=1)
    out = jnp.zeros((B * T, D), jnp.float32)
    for e in range(N_EXPERTS):
        ye = jnp.matmul(jax.nn.silu(jnp.matmul(t, w1[e])) * jnp.matmul(t, w3[e]), w2[e])
        out = out + comb[:, e:e + 1] * ye
    return out.reshape(B, T, D).astype(h.dtype)


def trunk(x, cond, rows, s_a, s_C, s_n, s_m, lb, w_ada, b_ada, w_in, b_in, hgrn_norm_g,
          conv_w, conv_b, w_mq, w_mk, mlstm_fbias, mlstm_norm_g, w_branch_a, w_branch_b, w_out,
          ln1_g, ln1_b, ln2_g, ln2_b, ffn_w1, ffn_w3, ffn_w2,
          moe_router_w, moe_router_b, moe_w1, moe_w3, moe_w2):
    out_a, out_C, out_n, out_m = [], [], [], []
    act = jax.nn.silu(cond)
    for l in range(DEPTH):
        mod = jnp.einsum('bd,dn->bn', act, w_ada[l]) + b_ada[l]
        sh1, sc1, g1, sh2, sc2, g2 = jnp.split(mod[:, None, :], 6, axis=-1)
        y, (sa, sC, sn, sm) = mixer(x * (1 + sc1) + sh1, rows, s_a[:, l], s_C[:, l], s_n[:, l], s_m[:, l],
                                    lb[:, l], w_in[l], b_in[l], hgrn_norm_g[l], conv_w[l], conv_b[l],
                                    w_mq[l], w_mk[l], mlstm_fbias[l], mlstm_norm_g[l],
                                    w_branch_a[l], w_branch_b[l], w_out[l])
        x = layer_norm(ALPHA * x + g1 * y, ln1_g[l], ln1_b[l])
        hh = x * (1 + sc2) + sh2
        j = l // 2
        if l % 2 == 0:
            f = swiglu(hh, ffn_w1[j], ffn_w3[j], ffn_w2[j])
        else:
            f = moe_swiglu(hh, moe_router_w[j], moe_router_b[j], moe_w1[j], moe_w3[j], moe_w2[j])
        x = layer_norm(ALPHA * x + g2 * f, ln2_g[l], ln2_b[l])
        out_a.append(sa)
        out_C.append(sC)
        out_n.append(sn)
        out_m.append(sm)
    return (x, jnp.stack(out_a, axis=1), jnp.stack(out_C, axis=1),
            jnp.stack(out_n, axis=1), jnp.stack(out_m, axis=1))


def setup_inputs(seed: int = 0) -> dict:
    key = jax.random.key(seed)
    ks = iter(jax.random.split(key, 48))
    f32 = jnp.float32

    def nrm(shape, s):
        return jax.random.normal(next(ks), shape, f32) * s

    D = D_MODEL
    return {
        'x_prompt': nrm((BATCH, SEQ, D), 1.0),
        'x_sample': nrm((DEC_BATCH, DEC_SEQ, D), 1.0),
        'c': nrm((DEC_BATCH, D), 1.0),
        'state_hgrn': nrm((DEC_BATCH, DEPTH, 2, H_A, DK_A, DV_A), 1.0),
        'state_mlstm_C': nrm((DEC_BATCH, DEPTH, 2, H_B, DH_B, DH_B), 0.2),
        'state_mlstm_n': nrm((DEC_BATCH, DEPTH, 2, H_B, DH_B), 0.2),
        'state_mlstm_m': nrm((DEC_BATCH, DEPTH, 2, H_B), 1.0),
        'c_ctx': nrm((D,), 1.0),
        'w_ada': nrm((DEPTH, D, 6 * D), 0.5 * D ** -0.5),
        'b_ada': nrm((DEPTH, 6 * D), 0.02),
        'w_in': nrm((DEPTH, D, N_IN), D ** -0.5),
        'b_in': nrm((DEPTH, N_IN), 0.02),
        'hgrn_lb_raw': nrm((2, DEPTH, WA), 0.5),
        'hgrn_norm_g': 1.0 + nrm((DEPTH, DV_A), 0.02),
        'conv_w': nrm((DEPTH, 3, 3, WB), 9 ** -0.5),
        'conv_b': nrm((DEPTH, WB), 0.02),
        'w_mq': nrm((DEPTH, H_B, DH_B, DH_B), DH_B ** -0.5),
        'w_mk': nrm((DEPTH, H_B, DH_B, DH_B), DH_B ** -0.5),
        'mlstm_fbias': jnp.linspace(3.0, 6.0, H_B, dtype=f32) + nrm((DEPTH, 2, H_B), 0.1),
        'mlstm_norm_g': 1.0 + nrm((DEPTH, DH_B), 0.02),
        'w_branch_a': nrm((DEPTH, WA, D), BETA * WA ** -0.5),
        'w_branch_b': nrm((DEPTH, WB, D), BETA * WB ** -0.5),
        'w_out': nrm((DEPTH, D, D), BETA * D ** -0.5),
        'ln1_g': 1.0 + nrm((DEPTH, D), 0.02),
        'ln1_b': nrm((DEPTH, D), 0.02),
        'ln2_g': 1.0 + nrm((DEPTH, D), 0.02),
        'ln2_b': nrm((DEPTH, D), 0.02),
        'ffn_w1': nrm((N_DENSE, D, D_FF), BETA * D ** -0.5),
        'ffn_w3': nrm((N_DENSE, D, D_FF), BETA * D ** -0.5),
        'ffn_w2': nrm((N_DENSE, D_FF, D), BETA * D_FF ** -0.5),
        'moe_router_w': nrm((N_MOE, D, N_EXPERTS), D ** -0.5),
        'moe_router_b': nrm((N_MOE, N_EXPERTS), 0.01),
        'moe_w1': nrm((N_MOE, N_EXPERTS, D, D_FF_E), BETA * D ** -0.5),
        'moe_w3': nrm((N_MOE, N_EXPERTS, D, D_FF_E), BETA * D ** -0.5),
        'moe_w2': nrm((N_MOE, N_EXPERTS, D_FF_E, D), BETA * D_FF_E ** -0.5),
    }


def reference(x_prompt, x_sample, c, state_hgrn, state_mlstm_C, state_mlstm_n, state_mlstm_m, c_ctx,
              w_ada, b_ada, w_in, b_in, hgrn_lb_raw, hgrn_norm_g, conv_w, conv_b, w_mq, w_mk,
              mlstm_fbias, mlstm_norm_g, w_branch_a, w_branch_b, w_out, ln1_g, ln1_b, ln2_g, ln2_b,
              ffn_w1, ffn_w3, ffn_w2, moe_router_w, moe_router_b, moe_w1, moe_w3, moe_w2):
    f32 = jnp.float32
    lb = jnp.cumsum(jax.nn.softmax(hgrn_lb_raw.astype(f32), axis=1), axis=1)
    lb = lb - lb[:, :1]
    weights = (lb, w_ada, b_ada, w_in, b_in, hgrn_norm_g, conv_w, conv_b, w_mq, w_mk,
               mlstm_fbias, mlstm_norm_g, w_branch_a, w_branch_b, w_out, ln1_g, ln1_b, ln2_g, ln2_b,
               ffn_w1, ffn_w3, ffn_w2, moe_router_w, moe_router_b, moe_w1, moe_w3, moe_w2)

    nb = x_prompt.shape[0]
    z_a = jnp.zeros((nb, DEPTH, 2, H_A, DK_A, DV_A), f32)
    z_C = jnp.zeros((nb, DEPTH, 2, H_B, DH_B, DH_B), f32)
    z_n = jnp.zeros((nb, DEPTH, 2, H_B, DH_B), f32)
    z_m = jnp.zeros((nb, DEPTH, 2, H_B), f32)
    cond_ctx = jnp.broadcast_to(c_ctx, (nb, c_ctx.shape[-1]))
    y_prompt, new_hgrn, new_C, new_n, new_m = trunk(x_prompt, cond_ctx, None, z_a, z_C, z_n, z_m, *weights)

    rows = x_sample.shape[1] // GRID_W
    y_sample, _, _, _, _ = trunk(x_sample, c, rows, state_hgrn, state_mlstm_C, state_mlstm_n,
                                 state_mlstm_m, *weights)
    return (y_prompt, y_sample, new_hgrn, new_C, new_n, new_m)
```

```python
import functools

import jax
import jax.numpy as jnp
import numpy as np
from jax import lax
from jax.experimental import pallas as pl
from jax.experimental.pallas import tpu as pltpu

D_MODEL = 1024
DEPTH = 4
GRID_W = 64
H_A = 8
DK_A = 128
DV_A = 128
WA = H_A * DK_A
H_B = 8
DH_B = 128
WB = H_B * DH_B
CHUNK_A = 16
CHUNK_B = 64
N_EXPERTS = 8
TOP_K = 2
ALPHA = (2 * DEPTH) ** 0.25
IN_SIZES = (WA, WA, WA, WA, WA, WB, WB, WB, 4 * H_B, D_MODEL, D_MODEL)

VMEM_LIMIT_BYTES = 48 * 1024 * 1024


def _mm_kernel(x_ref, w_ref, o_ref):
    k = pl.program_id(2)
    part = jnp.dot(x_ref[...].astype(jnp.bfloat16), w_ref[...].astype(jnp.bfloat16),
                   preferred_element_type=jnp.float32)

    @pl.when(k == 0)
    def _():
        o_ref[...] = part

    @pl.when(k != 0)
    def _():
        o_ref[...] += part


def _pick(n, cands):
    for c in cands:
        if n % c == 0:
            return c
    return n


def matmul(x, w, name, n_cols=None):
    M, K = x.shape
    N = w.shape[1] if n_cols is None else n_cols
    tm = _pick(M, (512, 256, 128, 8))
    tn = _pick(N, (512, 256, 128))
    tk = _pick(K, (1024, 1408, 512, 256, 128))
    return pl.pallas_call(
        _mm_kernel,
        grid=(M // tm, N // tn, K // tk),
        in_specs=[pl.BlockSpec((tm, tk), lambda i, j, k: (i, k)),
                  pl.BlockSpec((tk, tn), lambda i, j, k: (k, j))],
        out_specs=pl.BlockSpec((tm, tn), lambda i, j, k: (i, j)),
        out_shape=jax.ShapeDtypeStruct((M, N), jnp.float32),
        compiler_params=pltpu.CompilerParams(
            dimension_semantics=("parallel", "parallel", "arbitrary"),
            vmem_limit_bytes=VMEM_LIMIT_BYTES),
        name=name,
    )(x, w)


def layer_norm(x, g, b, eps=1e-5):
    mu = jnp.mean(x, axis=-1, keepdims=True)
    var = jnp.mean(jnp.square(x - mu), axis=-1, keepdims=True)
    return (x - mu) * lax.rsqrt(var + eps) * g + b


def rms_norm_heads(x, g, eps=1e-6):
    return x * lax.rsqrt(jnp.mean(jnp.square(x), axis=-1, keepdims=True) + eps) * g


def group_norm_heads(x, g, eps=1e-6):
    mu = jnp.mean(x, axis=-1, keepdims=True)
    var = jnp.mean(jnp.square(x - mu), axis=-1, keepdims=True)
    return (x - mu) * lax.rsqrt(var + eps) * g


def split_heads(t, h):
    return t.reshape(*t.shape[:-1], h, t.shape[-1] // h)


def chunk(x, L):
    B, T, H = x.shape[:3]
    rest = x.shape[3:]
    x = x.reshape(B, T // L, L, H, *rest)
    return jnp.swapaxes(jnp.moveaxis(x, 1, 0), 2, 3)


def unchunk(y):
    Nc, B, H, L = y.shape[:4]
    rest = y.shape[4:]
    y = jnp.moveaxis(jnp.swapaxes(y, 2, 3), 0, 1)
    return y.reshape(B, Nc * L, H, *rest)


def hgrn2_scan(q, k, v, log_f, S0):
    L = CHUNK_A
    mask = jnp.tril(jnp.ones((L, L), dtype=bool))

    def step(S, inp):
        qc, kc, vc, gc = inp
        b = jnp.cumsum(gc, axis=2)
        diff = b[:, :, :, None, :] - b[:, :, None, :, :]
        decay = jnp.exp(jnp.where(mask[:, :, None], diff, -jnp.inf))
        att = jnp.einsum('bhtk,bhtsk,bhsk->bhts', qc, decay, kc)
        o = (jnp.einsum('bhtk,bhkv->bhtv', qc * jnp.exp(b), S)
             + jnp.einsum('bhts,bhsv->bhtv', att, vc))
        b_last = b[:, :, -1:, :]
        S_new = (jnp.exp(b_last[:, :, 0])[..., None] * S
                 + jnp.einsum('bhsk,bhsv->bhkv', kc * jnp.exp(b_last - b), vc))
        return S_new, o

    S_T, o = lax.scan(step, S0, (chunk(q, L), chunk(k, L), chunk(v, L), chunk(log_f, L)))
    return unchunk(o), S_T


def mlstm_scan(q, k, v, ig, lf, C0, n0, m0):
    L = CHUNK_B
    mask = jnp.tril(jnp.ones((L, L), dtype=bool))

    def step(carry, inp):
        C, n, m = carry
        qc, kc, vc, ic, fc = inp
        F = jnp.cumsum(fc, axis=-1)
        logd = jnp.where(mask, F[..., :, None] - F[..., None, :] + ic[..., None, :], -jnp.inf)
        log_inter = F + m[..., None]
        m_q = jnp.maximum(log_inter, jnp.max(logd, axis=-1))
        w = jnp.exp(logd - m_q[..., None])
        a = jnp.exp(log_inter - m_q)
        s = jnp.einsum('bhtd,bhsd->bhts', qc, kc) * w
        num = (a[..., None] * jnp.einsum('bhtd,bhde->bhte', qc, C)
               + jnp.einsum('bhts,bhse->bhte', s, vc))
        den = a * jnp.einsum('bhtd,bhd->bht', qc, n) + jnp.sum(s, axis=-1)
        h = num / jnp.maximum(jnp.abs(den), jnp.exp(-m_q))[..., None]
        F_last = F[..., -1]
        log_w = F_last[..., None] - F + ic
        m_new = jnp.maximum(F_last + m, jnp.max(log_w, axis=-1))
        ws = jnp.exp(log_w - m_new[..., None])
        d0 = jnp.exp(F_last + m - m_new)
        C_new = d0[..., None, None] * C + jnp.einsum('bhs,bhsd,bhse->bhde', ws, kc, vc)
        n_new = d0[..., None] * n + jnp.einsum('bhs,bhsd->bhd', ws, kc)
        return (C_new, n_new, m_new), h

    (C_T, n_T, m_T), h = lax.scan(step, (C0, n0, m0),
                                  (chunk(q, L), chunk(k, L), chunk(v, L), chunk(ig, L), chunk(lf, L)))
    return unchunk(h), (C_T, n_T, m_T)


def depthwise_conv(img, w, b):
    rhs = w[:, :, None, :].astype(img.dtype)
    y = lax.conv_general_dilated(img, rhs, window_strides=(1, 1), padding='SAME',
                                 dimension_numbers=('NHWC', 'HWIO', 'NHWC'),
                                 feature_group_count=img.shape[-1])
    return y + b


def mixer(h, rows, s_a, s_C, s_n, s_m, lb, w_in, b_in, a_norm_g, conv_w, conv_b,
          w_mq, w_mk, f_bias, b_norm_g, w_ba, w_bb, w_out, tag):
    B, T, D = h.shape
    f32 = jnp.float32
    n_main = sum(IN_SIZES[:8])
    h2 = h.reshape(B * T, D)
    w_tail = jnp.pad(w_in[:, n_main:], ((0, 0), (0, -(w_in.shape[1] - n_main) % 128)))
    proj = jnp.concatenate(
        [matmul(h2, w_in, "w_in_" + tag, n_cols=n_main),
         matmul(h2, w_tail, "w_tail_" + tag)[:, :w_in.shape[1] - n_main]], axis=-1)
    proj = proj.reshape(B, T, -1) + b_in
    (a_q, a_f_fwd, a_f_bwd, a_i, a_g, b_qk, b_v, b_o, b_gates, g_ma, g_mb) = jnp.split(
        proj, np.cumsum(IN_SIZES)[:-1].tolist(), axis=-1)

    def bidir(t_fwd, t_bwd):
        return jnp.stack([t_fwd, jnp.flip(t_bwd, axis=1)])

    z = bidir(a_f_fwd, a_f_bwd)
    lbb = lb[:, None, None, :]
    log_f = jnp.logaddexp(jnp.log(lbb), jnp.log1p(-lbb) + jax.nn.log_sigmoid(z))
    k_a = (1.0 - lbb) * jax.nn.sigmoid(-z)
    q_a = split_heads(a_q, H_A)
    v_a = split_heads(a_i, H_A)
    o_a, sa_T = jax.vmap(hgrn2_scan)(bidir(q_a, q_a), split_heads(k_a, H_A), bidir(v_a, v_a),
                                     split_heads(log_f, H_A), jnp.moveaxis(s_a, 1, 0))
    o_a = o_a[0] + jnp.flip(o_a[1], axis=1)
    y_a = (rms_norm_heads(o_a, a_norm_g) * jax.nn.silu(split_heads(a_g, H_A))).reshape(B, T, WA)

    if rows is None:
        img, w_k = b_qk[:, None], conv_w[1:2]
    else:
        img, w_k = b_qk.reshape(B, rows, GRID_W, WB), conv_w
    conv_act = jax.nn.silu(depthwise_conv(img, w_k, conv_b).reshape(B, T, WB))
    ca = split_heads(conv_act, H_B)
    q_b = jnp.einsum('bthd,hde->bthe', ca, w_mq)
    k_b = jnp.einsum('bthd,hde->bthe', ca, w_mk) * (DH_B ** -0.5)
    v_b = split_heads(b_v, H_B)
    gts = b_gates.reshape(B, T, 4, H_B)
    i_g = bidir(gts[:, :, 0], gts[:, :, 2])
    log_fg = jax.nn.log_sigmoid(bidir(gts[:, :, 1] + f_bias[0], gts[:, :, 3] + f_bias[1]))
    h_b, (C_T, n_T, m_T) = jax.vmap(mlstm_scan)(
        bidir(q_b, q_b), bidir(k_b, k_b), bidir(v_b, v_b), i_g, log_fg,
        jnp.moveaxis(s_C, 1, 0), jnp.moveaxis(s_n, 1, 0), jnp.moveaxis(s_m, 1, 0))
    h_b = h_b[0] + jnp.flip(h_b[1], axis=1)
    y_b = (group_norm_heads(h_b, b_norm_g) * jax.nn.sigmoid(split_heads(b_o, H_B))).reshape(B, T, WB)

    pa = matmul(y_a.reshape(B * T, WA), w_ba, "w_ba_" + tag).reshape(B, T, D)
    pb = matmul(y_b.reshape(B * T, WB), w_bb, "w_bb_" + tag).reshape(B, T, D)
    merged = jax.nn.sigmoid(g_ma) * pa + jax.nn.sigmoid(g_mb) * pb
    y = matmul(merged.reshape(B * T, D), w_out, "w_out_" + tag).reshape(B, T, D)
    new_states = (jnp.moveaxis(sa_T, 0, 1), jnp.moveaxis(C_T, 0, 1),
                  jnp.moveaxis(n_T, 0, 1), jnp.moveaxis(m_T, 0, 1))
    return y, new_states


def swiglu_2d(t, w1, w3, w2, tag):
    a = matmul(t, w1, "ffn_w1_" + tag)
    b = matmul(t, w3, "ffn_w3_" + tag)
    return matmul(jax.nn.silu(a) * b, w2, "ffn_w2_" + tag)


def moe_swiglu(h, wr, br, w1, w3, w2, tag):
    B, T, D = h.shape
    t = h.reshape(B * T, D)
    logits = jnp.matmul(t, wr, precision=lax.Precision.HIGHEST) + br
    probs = jax.nn.softmax(logits, axis=-1)
    top_p, top_i = lax.top_k(probs, TOP_K)
    top_p = top_p / jnp.sum(top_p, axis=-1, keepdims=True)
    comb = jnp.sum(jax.nn.one_hot(top_i, N_EXPERTS, dtype=jnp.float32) * top_p[..., None], axis=1)
    out = jnp.zeros((B * T, D), jnp.float32)
    for e in range(N_EXPERTS):
        ye = swiglu_2d(t, w1[e], w3[e], w2[e], tag)
        out = out + comb[:, e:e + 1] * ye
    return out.reshape(B, T, D)


def trunk(x, cond, rows, s_a, s_C, s_n, s_m, lb, w_ada, b_ada, w_in, b_in, hgrn_norm_g,
          conv_w, conv_b, w_mq, w_mk, mlstm_fbias, mlstm_norm_g, w_branch_a, w_branch_b, w_out,
          ln1_g, ln1_b, ln2_g, ln2_b, ffn_w1, ffn_w3, ffn_w2,
          moe_router_w, moe_router_b, moe_w1, moe_w3, moe_w2, tag):
    out_a, out_C, out_n, out_m = [], [], [], []
    act = jax.nn.silu(cond)
    B, T, D = x.shape
    for l in range(DEPTH):
        mod = jnp.einsum('bd,dn->bn', act, w_ada[l], precision=lax.Precision.HIGHEST) + b_ada[l]
        sh1, sc1, g1, sh2, sc2, g2 = jnp.split(mod[:, None, :], 6, axis=-1)
        y, (sa, sC, sn, sm) = mixer(x * (1 + sc1) + sh1, rows, s_a[:, l], s_C[:, l], s_n[:, l], s_m[:, l],
                                    lb[:, l], w_in[l], b_in[l], hgrn_norm_g[l], conv_w[l], conv_b[l],
                                    w_mq[l], w_mk[l], mlstm_fbias[l], mlstm_norm_g[l],
                                    w_branch_a[l], w_branch_b[l], w_out[l], tag)
        x = layer_norm(ALPHA * x + g1 * y, ln1_g[l], ln1_b[l])
        hh = x * (1 + sc2) + sh2
        j = l // 2
        if l % 2 == 0:
            f = swiglu_2d(hh.reshape(B * T, D), ffn_w1[j], ffn_w3[j], ffn_w2[j], tag).reshape(B, T, D)
        else:
            f = moe_swiglu(hh, moe_router_w[j], moe_router_b[j], moe_w1[j], moe_w3[j], moe_w2[j], tag)
        x = layer_norm(ALPHA * x + g2 * f, ln2_g[l], ln2_b[l])
        out_a.append(sa)
        out_C.append(sC)
        out_n.append(sn)
        out_m.append(sm)
    return (x, jnp.stack(out_a, axis=1), jnp.stack(out_C, axis=1),
            jnp.stack(out_n, axis=1), jnp.stack(out_m, axis=1))


def kernel(x_prompt, x_sample, c, state_hgrn, state_mlstm_C, state_mlstm_n, state_mlstm_m, c_ctx,
           w_ada, b_ada, w_in, b_in, hgrn_lb_raw, hgrn_norm_g, conv_w, conv_b, w_mq, w_mk,
           mlstm_fbias, mlstm_norm_g, w_branch_a, w_branch_b, w_out, ln1_g, ln1_b, ln2_g, ln2_b,
           ffn_w1, ffn_w3, ffn_w2, moe_router_w, moe_router_b, moe_w1, moe_w3, moe_w2):
    f32 = jnp.float32
    lb = jnp.cumsum(jax.nn.softmax(hgrn_lb_raw.astype(f32), axis=1), axis=1)
    lb = lb - lb[:, :1]
    weights = (lb, w_ada, b_ada, w_in, b_in, hgrn_norm_g, conv_w, conv_b, w_mq, w_mk,
               mlstm_fbias, mlstm_norm_g, w_branch_a, w_branch_b, w_out, ln1_g, ln1_b, ln2_g, ln2_b,
               ffn_w1, ffn_w3, ffn_w2, moe_router_w, moe_router_b, moe_w1, moe_w3, moe_w2)

    nb = x_prompt.shape[0]
    z_a = jnp.zeros((nb, DEPTH, 2, H_A, DK_A, DV_A), f32)
    z_C = jnp.zeros((nb, DEPTH, 2, H_B, DH_B, DH_B), f32)
    z_n = jnp.zeros((nb, DEPTH, 2, H_B, DH_B), f32)
    z_m = jnp.zeros((nb, DEPTH, 2, H_B), f32)
    cond_ctx = jnp.broadcast_to(c_ctx, (nb, c_ctx.shape[-1]))
    y_prompt, new_hgrn, new_C, new_n, new_m = trunk(x_prompt, cond_ctx, None, z_a, z_C, z_n, z_m,
                                                    *weights, tag="ctx")
    rows = x_sample.shape[1] // GRID_W
    y_sample, _, _, _, _ = trunk(x_sample, c, rows, state_hgrn, state_mlstm_C, state_mlstm_n,
                                 state_mlstm_m, *weights, tag="lat")
    return (y_prompt, y_sample, new_hgrn, new_C, new_n, new_m)
```

```python
import functools

import jax
import jax.numpy as jnp
from jax import lax
from jax.experimental import pallas as pl
from jax.experimental.pallas import tpu as pltpu

f32 = jnp.float32
bf16 = jnp.bfloat16

D = 1024
DEPTH = 4
GRID_W = 64
NH = 8
DH = 128
CH_A = 16
HALF_A = 8
CH_B = 256
N_EXPERTS = 8
D_FF = 2816
ALPHA = (2 * DEPTH) ** 0.25
SEQ_GROUP = 1024
EXP_CLAMP = 80.0
MOE_TM = 512

VMEM_LIMIT = 56 * 1024 * 1024
NT = (((1,), (1,)), ((), ()))
TN = (((0,), (0,)), ((), ()))


def _cparams(*sem):
    return pltpu.CompilerParams(dimension_semantics=sem, vmem_limit_bytes=VMEM_LIMIT)


def _sigmoid(x):
    return 1.0 / (1.0 + jnp.exp(-x))


def _log_sigmoid(t):
    return jnp.minimum(t, 0.0) - jnp.log1p(jnp.exp(-jnp.abs(t)))


def _layer_norm(r, g, b):
    mu = jnp.mean(r, axis=-1, keepdims=True)
    d = r - mu
    var = jnp.mean(d * d, axis=-1, keepdims=True)
    return d * lax.rsqrt(var + 1e-5) * g + b


def _mod_kernel(c_ref, w_ref, b_ref, o_ref):
    c = c_ref[...]
    act = (c * _sigmoid(c)).astype(bf16)
    o_ref[...] = jnp.dot(act, w_ref[...].astype(bf16), preferred_element_type=f32) + b_ref[...]


def modulation(cond8, w_ada, b_ada):
    L, _, n = w_ada.shape
    tn = 1536
    return pl.pallas_call(
        _mod_kernel,
        grid=(L, n // tn),
        in_specs=[pl.BlockSpec((8, D), lambda l, j: (0, 0)),
                  pl.BlockSpec((None, D, tn), lambda l, j: (l, 0, j)),
                  pl.BlockSpec((None, 1, tn), lambda l, j: (l, 0, j))],
        out_specs=pl.BlockSpec((None, 8, tn), lambda l, j: (l, 0, j)),
        out_shape=jax.ShapeDtypeStruct((L, 8, n), f32),
        compiler_params=_cparams("parallel", "parallel"),
        name="modulation",
    )(cond8, w_ada, b_ada.reshape(L, 1, n))


N_CTX = 32 * 256


def _group_of_tile(tm):
    def idx(i):
        return jnp.maximum(i * tm // SEQ_GROUP - (N_CTX // SEQ_GROUP - 1), 0)
    return idx


def _modulate_kernel(x_ref, mod_ref, h_ref):
    h_ref[...] = (x_ref[...] * (1.0 + mod_ref[1:2, :]) + mod_ref[0:1, :]).astype(h_ref.dtype)


def modulate(x, mod_l):
    n = x.shape[0]
    tm = 1024
    grp = _group_of_tile(tm)
    return pl.pallas_call(
        _modulate_kernel,
        grid=(n // tm,),
        in_specs=[pl.BlockSpec((tm, D), lambda i: (i, 0)),
                  pl.BlockSpec((None, 8, D), lambda i: (grp(i), 0, 0))],
        out_specs=pl.BlockSpec((tm, D), lambda i: (i, 0)),
        out_shape=jax.ShapeDtypeStruct((n, D), bf16),
        compiler_params=_cparams("parallel"),
        name="modulate",
    )(x, mod_l)


def _linear_kernel(x_ref, w_ref, b_ref, o_ref):
    acc = jnp.dot(x_ref[...], w_ref[...], preferred_element_type=f32)
    o_ref[...] = (acc + b_ref[...]).astype(o_ref.dtype)


def linear(x, w, b, out_dtype, name):
    m, k = x.shape
    n = w.shape[1]
    tm = 1024
    tn = 1024 if n % 1024 == 0 else n
    return pl.pallas_call(
        _linear_kernel,
        grid=(m // tm, n // tn),
        in_specs=[pl.BlockSpec((tm, k), lambda i, j: (i, 0)),
                  pl.BlockSpec((k, tn), lambda i, j: (0, j)),
                  pl.BlockSpec((1, tn), lambda i, j: (0, j))],
        out_specs=pl.BlockSpec((tm, tn), lambda i, j: (i, j)),
        out_shape=jax.ShapeDtypeStruct((m, n), out_dtype),
        compiler_params=_cparams("parallel", "parallel"),
        name=name,
    )(x, w, b.reshape(1, n))


def _prefix8(x):
    row = lax.broadcasted_iota(jnp.int32, x.shape, 1)
    for k in (1, 2, 4):
        x = x + jnp.where(row >= k, pltpu.roll(x, k, axis=1), 0.0)
    return x


def _hgrn_kernel(*refs, has_state, want_state):
    q_ref, v_ref, g_ref, zf_ref, zb_ref, lbp_ref, gain_ref = refs[:7]
    pos = 7
    s0_ref = None
    if has_state:
        s0_ref = refs[pos]
        pos += 1
    y_ref = refs[pos]
    pos += 1
    st_ref = None
    if want_state:
        st_ref = refs[pos]
        pos += 1
    qh_ref, kh_ref, qb_ref, kl_ref, ds_ref, o_ref = refs[pos:]

    T = q_ref.shape[0]
    nc = T // CH_A
    q = q_ref[...].astype(f32)
    half = lax.broadcasted_iota(jnp.int32, (nc, 2, HALF_A, DH), 1)

    for d, z_ref in enumerate((zf_ref, zb_ref)):
        z = z_ref[...]
        log_lb = lbp_ref[d:d + 1, :]
        log_1m_lb = lbp_ref[2 + d:3 + d, :]
        one_m_lb = lbp_ref[4 + d:5 + d, :]
        e = jnp.exp(-jnp.abs(z))
        log_sig = jnp.minimum(z, 0.0) - jnp.log1p(e)
        u = log_1m_lb + log_sig
        lf = jnp.maximum(log_lb, u) + jnp.log1p(jnp.exp(-jnp.abs(log_lb - u)))
        kk = one_m_lb * (jnp.where(z >= 0.0, e, 1.0) / (1.0 + e))

        lf4 = lf.reshape(nc, 2, HALF_A, DH)
        p = _prefix8(lf.reshape(T // HALF_A, HALF_A, DH)).reshape(nc, 2, HALF_A, DH)
        tot = jnp.broadcast_to(p[:, :, HALF_A - 1:HALF_A, :], p.shape)
        tot0 = jnp.broadcast_to(tot[:, 0:1], p.shape)
        tot1 = jnp.broadcast_to(tot[:, 1:2], p.shape)
        if d == 0:
            e_mid = jnp.where(half == 0, p - tot, p)
            b = jnp.where(half == 0, p, p + tot0)
        else:
            sfx = tot - p + lf4
            e_mid = jnp.where(half == 0, sfx, lf4 - p)
            b = jnp.where(half == 0, sfx + tot1, sfx)
        b_all = tot0 + tot1
        e_mid = e_mid.reshape(T, DH)
        b2 = b.reshape(T, DH)
        b_all2 = b_all.reshape(T, DH)
        qh_ref[d] = (q * jnp.exp(jnp.minimum(e_mid, EXP_CLAMP))).astype(bf16)
        kh_ref[d] = (kk * jnp.exp(jnp.minimum(-e_mid, EXP_CLAMP))).astype(bf16)
        qb_ref[d] = (q * jnp.exp(b2)).astype(bf16)
        kl_ref[d] = (kk * jnp.exp(b_all2 - b2)).astype(bf16)
        ds_ref[d] = jnp.exp(b_all[:, 0, 0, :])

    row = lax.broadcasted_iota(jnp.int32, (CH_A, CH_A), 0)
    col = lax.broadcasted_iota(jnp.int32, (CH_A, CH_A), 1)
    masks = (row >= col, row <= col)

    if has_state:
        st0 = (s0_ref[0].T, s0_ref[1].T)
    else:
        st0 = (jnp.zeros((DH, DH), f32), jnp.zeros((DH, DH), f32))

    def body(c, carry):
        new = []
        for d in range(2):
            cidx = c if d == 0 else nc - 1 - c
            r = pl.ds(pl.multiple_of(cidx * CH_A, CH_A), CH_A)
            st = carry[d]
            vc = v_ref[r, :]
            att = lax.dot_general(qh_ref[d, r, :], kh_ref[d, r, :], NT, preferred_element_type=f32)
            att = jnp.where(masks[d], att, 0.0).astype(bf16)
            o = (lax.dot_general(qb_ref[d, r, :], st.astype(bf16), NT, preferred_element_type=f32)
                 + jnp.dot(att, vc, preferred_element_type=f32))
            o_ref[d, r, :] = o
            upd = lax.dot_general(vc, kl_ref[d, r, :], TN, preferred_element_type=f32)
            new.append(ds_ref[d, pl.ds(cidx, 1), :] * st + upd)
        return tuple(new)

    st_f, st_b = lax.fori_loop(0, nc, body, st0, unroll=4)

    o = o_ref[0] + o_ref[1]
    ms = jnp.mean(o * o, axis=-1, keepdims=True)
    g = g_ref[...].astype(f32)
    y_ref[...] = (o * lax.rsqrt(ms + 1e-6) * gain_ref[...] * (g * _sigmoid(g))).astype(y_ref.dtype)
    if want_state:
        st_ref[0] = st_f.T
        st_ref[1] = st_b.T


def hgrn_scan(qig, zz, lbp, gain, s0, *, n_seq, T, row_blk0, want_state, name):
    has_state = s0 is not None
    nc = T // CH_A

    def col(off):
        return lambda b, h: (row_blk0 + b, off + h)

    in_specs = [pl.BlockSpec((T, DH), col(0)), pl.BlockSpec((T, DH), col(NH)),
                pl.BlockSpec((T, DH), col(2 * NH)),
                pl.BlockSpec((T, DH), col(0)), pl.BlockSpec((T, DH), col(NH)),
                pl.BlockSpec((8, DH), lambda b, h: (0, h)),
                pl.BlockSpec((1, DH), lambda b, h: (0, 0))]
    args = [qig, qig, qig, zz, zz, lbp, gain]
    if has_state:
        in_specs.append(pl.BlockSpec((None, 2, None, DH, DH), lambda b, h: (b, 0, h, 0, 0)))
        args.append(s0)
    out_specs = [pl.BlockSpec((T, DH), lambda b, h: (b, h))]
    out_shape = [jax.ShapeDtypeStruct((n_seq * T, D), bf16)]
    if want_state:
        out_specs.append(pl.BlockSpec((None, 2, None, DH, DH), lambda b, h: (b, 0, h, 0, 0)))
        out_shape.append(jax.ShapeDtypeStruct((n_seq, 2, NH, DH, DH), f32))
    scratch = [pltpu.VMEM((2, T, DH), bf16)] * 4 + [pltpu.VMEM((2, nc, DH), f32),
                                                     pltpu.VMEM((2, T, DH), f32)]
    return pl.pallas_call(
        functools.partial(_hgrn_kernel, has_state=has_state, want_state=want_state),
        grid=(n_seq, NH),
        in_specs=in_specs, out_specs=out_specs, out_shape=out_shape,
        scratch_shapes=scratch,
        compiler_params=_cparams("parallel", "parallel"),
        name=name,
    )(*args)


def _split3(x):
    x0 = x.astype(bf16)
    r = x - x0.astype(f32)
    x1 = r.astype(bf16)
    x2 = (r - x1.astype(f32)).astype(bf16)
    return x0, x1, x2


def _tri_matmul(tri, x):
    return sum(jnp.dot(tri, p, preferred_element_type=f32) for p in _split3(x))


def _matmul_tri(x, tri):
    return sum(jnp.dot(p, tri, preferred_element_type=f32) for p in _split3(x))


def _mlstm_kernel(*refs, grid_rows, has_state, want_state):
    x_ref, v_ref, og_ref, gc_ref, gr_ref, cw_ref, cb_ref, wq_ref, wk_ref, fbr_ref, fbc_ref, gain_ref = refs[:12]
    pos = 12
    if has_state:
        c0_ref, m0_ref = refs[pos:pos + 2]
        pos += 2
    y_ref = refs[pos]
    pos += 1
    if want_state:
        ct_ref, nt_ref, mt_ref = refs[pos:pos + 3]

    head = pl.program_id(1)
    T = x_ref.shape[0]
    L = min(CH_B, T)
    nchunk = T // L

    x = x_ref[...].astype(f32)
    t_idx = lax.broadcasted_iota(jnp.int32, (T, DH), 0)
    acc = jnp.zeros((T, DH), f32) + cb_ref[...]
    if grid_rows is None:
        taps = [(0, dc) for dc in (-1, 0, 1)]
    else:
        taps = [(dr, dc) for dr in (-1, 0, 1) for dc in (-1, 0, 1)]
        c_idx = t_idx % GRID_W
    for dr, dc in taps:
        off = dr * GRID_W + dc
        tap = (dr + 1) * 3 + (dc + 1)
        w = cw_ref[tap:tap + 1, :]
        xs = x if off == 0 else pltpu.roll(x, (-off) % T, axis=0)
        ok = (t_idx + off >= 0) & (t_idx + off < T)
        if grid_rows is not None:
            ok = ok & (c_idx + dc >= 0) & (c_idx + dc < GRID_W)
        acc = acc + jnp.where(ok, xs, 0.0) * w
    ca = (acc * _sigmoid(acc)).astype(bf16)
    q = jnp.dot(ca, wq_ref[...], preferred_element_type=f32).astype(bf16)
    k = (jnp.dot(ca, wk_ref[...], preferred_element_type=f32) * (DH ** -0.5)).astype(bf16)
    lane = lax.broadcasted_iota(jnp.int32, (T, DH), 1)
    vaug = jnp.concatenate([v_ref[...].astype(f32), jnp.where(lane == 0, 1.0, 0.0)], axis=1)

    gate_lane = lax.broadcasted_iota(jnp.int32, (T, DH), 1) // NH
    is_f = (gate_lane == 1) | (gate_lane == 3)
    gc = gc_ref[...] + fbr_ref[...]
    gc = jnp.where(is_f, _log_sigmoid(gc), gc)

    def pick_lane(a, ln):
        sel = lax.broadcasted_iota(jnp.int32, a.shape, 1) == ln
        return jnp.sum(jnp.where(sel, a, 0.0), axis=1, keepdims=True)

    def pick_row(a, rw):
        sel = lax.broadcasted_iota(jnp.int32, a.shape, 0) == rw
        return jnp.sum(jnp.where(sel, a, 0.0), axis=0, keepdims=True)

    r_i = lax.broadcasted_iota(jnp.int32, (L, L), 0)
    c_i = lax.broadcasted_iota(jnp.int32, (L, L), 1)
    lower = r_i >= c_i
    upper = r_i <= c_i
    tri_lo = jnp.where(lower, 1.0, 0.0).astype(bf16)
    tri_up = jnp.where(upper, 1.0, 0.0).astype(bf16)

    h_parts = [[None] * nchunk for _ in range(2)]
    finals = []
    for d in range(2):
        if has_state:
            caug = c0_ref[d]
            m_prev = m0_ref[d]
        else:
            caug = jnp.zeros((DH, 2 * DH), f32)
            m_prev = jnp.zeros((1, 1), f32)
        order = range(nchunk) if d == 0 else range(nchunk - 1, -1, -1)
        mask = lower if d == 0 else upper
        tri_c = tri_lo if d == 0 else tri_up
        tri_r = tri_up if d == 0 else tri_lo
        for cidx in order:
            rs = slice(cidx * L, (cidx + 1) * L)
            gcc = gc[rs]
            ig_c = pick_lane(gcc, 2 * d * NH + head)
            F_c = pick_lane(_tri_matmul(tri_c, gcc), (2 * d + 1) * NH + head)
            ig_r = pick_row(gr_ref[2 * d * NH:(2 * d + 1) * NH, rs], head)
            lf_r8 = _log_sigmoid(gr_ref[(2 * d + 1) * NH:(2 * d + 2) * NH, rs] + fbc_ref[d])
            F_r = pick_row(_matmul_tri(lf_r8, tri_r), head)
            F_end = F_c[L - 1:L, :] if d == 0 else F_c[0:1, :]

            logd = jnp.where(mask, F_c - F_r + ig_r, -jnp.inf)
            log_inter = F_c + m_prev
            m_q = jnp.maximum(log_inter, jnp.max(logd, axis=-1, keepdims=True))
            w = jnp.exp(logd - m_q)
            a = jnp.exp(log_inter - m_q)
            qc, kc = q[rs], k[rs]
            va = vaug[rs]
            s = lax.dot_general(qc, kc, NT, preferred_element_type=f32) * w
            nd = (a * jnp.dot(qc, caug.astype(bf16), preferred_element_type=f32)
                  + jnp.dot(s.astype(bf16), va.astype(bf16), preferred_element_type=f32))
            num = nd[:, :DH]
            den = nd[:, DH:DH + 1]
            h_parts[d][cidx] = num / jnp.maximum(jnp.abs(den), jnp.exp(-m_q))
            log_w_c = F_end - F_c + ig_c
            log_w_r = F_end - F_r + ig_r
            m_new = jnp.maximum(F_end + m_prev, jnp.max(log_w_r, axis=-1, keepdims=True))
            ws_c = jnp.exp(log_w_c - m_new)
            d0 = jnp.exp(F_end + m_prev - m_new)
            caug = d0 * caug + lax.dot_general(kc, (ws_c * va).astype(bf16), TN,
                                               preferred_element_type=f32)
            m_prev = m_new
        finals.append((caug, m_prev))

    hsum = jnp.concatenate([h_parts[0][c] + h_parts[1][c] for c in range(nchunk)], axis=0)
    mu = jnp.mean(hsum, axis=-1, keepdims=True)
    dv = hsum - mu
    var = jnp.mean(dv * dv, axis=-1, keepdims=True)
    og = og_ref[...].astype(f32)
    y_ref[...] = (dv * lax.rsqrt(var + 1e-6) * gain_ref[...] * _sigmoid(og)).astype(y_ref.dtype)
    if want_state:
        for d in range(2):
            caug, m_fin = finals[d]
            ct_ref[d] = caug[:, :DH]
            nt_ref[d] = caug[:, DH:2 * DH].T[0:1, :]
            mt_ref[d] = jnp.broadcast_to(m_fin, (1, DH))


def mlstm_scan(bb, gates, gates_t, conv_w9, conv_b, wq, wk, fb_row, fb_col, gain, state, *,
               n_seq, T, row_blk0, grid_rows, want_state, name):
    has_state = state is not None

    def col(off):
        return lambda b, h: (row_blk0 + b, off + h)

    in_specs = [pl.BlockSpec((T, DH), col(0)), pl.BlockSpec((T, DH), col(NH)),
                pl.BlockSpec((T, DH), col(2 * NH)),
                pl.BlockSpec((T, DH), lambda b, h: (row_blk0 + b, 0)),
                pl.BlockSpec((DH, T), lambda b, h: (0, row_blk0 + b)),
                pl.BlockSpec((9, DH), lambda b, h: (0, h)),
                pl.BlockSpec((1, DH), lambda b, h: (0, h)),
                pl.BlockSpec((None, DH, DH), lambda b, h: (h, 0, 0)),
                pl.BlockSpec((None, DH, DH), lambda b, h: (h, 0, 0)),
                pl.BlockSpec((1, DH), lambda b, h: (0, 0)),
                pl.BlockSpec((2, NH, 1), lambda b, h: (0, 0, 0)),
                pl.BlockSpec((1, DH), lambda b, h: (0, 0))]
    args = [bb, bb, bb, gates, gates_t, conv_w9, conv_b, wq, wk, fb_row, fb_col, gain]
    if has_state:
        c0aug, m0 = state
        in_specs += [pl.BlockSpec((None, 2, None, DH, 2 * DH), lambda b, h: (b, 0, h, 0, 0)),
                     pl.BlockSpec((None, 2, None, 1, 1), lambda b, h: (b, 0, h, 0, 0))]
        args += [c0aug, m0]
    out_specs = [pl.BlockSpec((T, DH), lambda b, h: (b, h))]
    out_shape = [jax.ShapeDtypeStruct((n_seq * T, D), bf16)]
    if want_state:
        out_specs += [pl.BlockSpec((None, 2, None, DH, DH), lambda b, h: (b, 0, h, 0, 0)),
                      pl.BlockSpec((None, 2, None, 1, DH), lambda b, h: (b, 0, h, 0, 0)),
                      pl.BlockSpec((None, 2, None, 1, DH), lambda b, h: (b, 0, h, 0, 0))]
        out_shape += [jax.ShapeDtypeStruct((n_seq, 2, NH, DH, DH), f32),
                      jax.ShapeDtypeStruct((n_seq, 2, NH, 1, DH), f32),
                      jax.ShapeDtypeStruct((n_seq, 2, NH, 1, DH), f32)]
    return pl.pallas_call(
        functools.partial(_mlstm_kernel, grid_rows=grid_rows, has_state=has_state,
                          want_state=want_state),
        grid=(n_seq, NH),
        in_specs=in_specs, out_specs=out_specs, out_shape=out_shape,
        compiler_params=_cparams("parallel", "parallel"),
        name=name,
    )(*args)


def _merge_kernel(ya_ref, yb_ref, gm_ref, x_ref, mod_ref, wa_ref, wb_ref, wo_ref, lg_ref, lb_ref,
                  x1_ref, hh_ref):
    pa = jnp.dot(ya_ref[...], wa_ref[...], preferred_element_type=f32)
    pb = jnp.dot(yb_ref[...], wb_ref[...], preferred_element_type=f32)
    gma = gm_ref[:, :D].astype(f32)
    gmb = gm_ref[:, D:].astype(f32)
    merged = (_sigmoid(gma) * pa + _sigmoid(gmb) * pb).astype(bf16)
    y = jnp.dot(merged, wo_ref[...], preferred_element_type=f32)
    x1 = _layer_norm(ALPHA * x_ref[...] + mod_ref[2:3, :] * y, lg_ref[...], lb_ref[...])
    x1_ref[...] = x1
    hh_ref[...] = (x1 * (1.0 + mod_ref[4:5, :]) + mod_ref[3:4, :]).astype(hh_ref.dtype)


def merge(ya, yb, gm, x, mod_l, wa, wb, wo, lg, lb):
    n = x.shape[0]
    tm = 512
    grp = _group_of_tile(tm)
    row = lambda i: (i, 0)
    const = lambda i: (0, 0)
    return pl.pallas_call(
        _merge_kernel,
        grid=(n // tm,),
        in_specs=[pl.BlockSpec((tm, D), row), pl.BlockSpec((tm, D), row),
                  pl.BlockSpec((tm, 2 * D), row), pl.BlockSpec((tm, D), row),
                  pl.BlockSpec((None, 8, D), lambda i: (grp(i), 0, 0)),
                  pl.BlockSpec((D, D), const), pl.BlockSpec((D, D), const), pl.BlockSpec((D, D), const),
                  pl.BlockSpec((1, D), const), pl.BlockSpec((1, D), const)],
        out_specs=[pl.BlockSpec((tm, D), row), pl.BlockSpec((tm, D), row)],
        out_shape=[jax.ShapeDtypeStruct((n, D), f32), jax.ShapeDtypeStruct((n, D), bf16)],
        compiler_params=_cparams("parallel"),
        name="merge",
    )(ya, yb, gm, x, mod_l, wa, wb, wo, lg.reshape(1, D), lb.reshape(1, D))


def _post_ffn(f, x1, mod, mod_next, lg, lb, x2_ref, hn_ref):
    x2 = _layer_norm(ALPHA * x1 + mod[5:6, :] * f, lg, lb)
    x2_ref[...] = x2
    hn_ref[...] = (x2 * (1.0 + mod_next[1:2, :]) + mod_next[0:1, :]).astype(hn_ref.dtype)


def _ffn_kernel(hh_ref, w1_ref, w3_ref, w2_ref, x1_ref, mod_ref, modn_ref, lg_ref, lb_ref,
                x2_ref, hn_ref, acc_ref):
    j = pl.program_id(1)
    hh = hh_ref[...]
    a = jnp.dot(hh, w1_ref[...], preferred_element_type=f32)
    b = jnp.dot(hh, w3_ref[...], preferred_element_type=f32)
    part = jnp.dot((a * _sigmoid(a) * b).astype(bf16), w2_ref[...], preferred_element_type=f32)

    @pl.when(j == 0)
    def _():
        acc_ref[...] = part

    @pl.when(j != 0)
    def _():
        acc_ref[...] += part

    @pl.when(j == pl.num_programs(1) - 1)
    def _():
        _post_ffn(acc_ref[...], x1_ref[...], mod_ref, modn_ref, lg_ref[...], lb_ref[...],
                  x2_ref, hn_ref)


def ffn(hh, w1, w3, w2, x1, mod_l, mod_next, lg, lb):
    n = hh.shape[0]
    tm, tf = 512, 1408
    grp = _group_of_tile(tm)
    row = lambda i, j: (i, 0)
    const = lambda i, j: (0, 0)
    return pl.pallas_call(
        _ffn_kernel,
        grid=(n // tm, D_FF // tf),
        in_specs=[pl.BlockSpec((tm, D), row),
                  pl.BlockSpec((D, tf), lambda i, j: (0, j)),
                  pl.BlockSpec((D, tf), lambda i, j: (0, j)),
                  pl.BlockSpec((tf, D), lambda i, j: (j, 0)),
                  pl.BlockSpec((tm, D), row),
                  pl.BlockSpec((None, 8, D), lambda i, j: (grp(i), 0, 0)),
                  pl.BlockSpec((None, 8, D), lambda i, j: (grp(i), 0, 0)),
                  pl.BlockSpec((1, D), const), pl.BlockSpec((1, D), const)],
        out_specs=[pl.BlockSpec((tm, D), row), pl.BlockSpec((tm, D), row)],
        out_shape=[jax.ShapeDtypeStruct((n, D), f32), jax.ShapeDtypeStruct((n, D), bf16)],
        scratch_shapes=[pltpu.VMEM((tm, D), f32)],
        compiler_params=_cparams("parallel", "arbitrary"),
        name="ffn",
    )(hh, w1, w3, w2, x1, mod_l, mod_next, lg.reshape(1, D), lb.reshape(1, D))


def _router_kernel(hh_ref, wr_ref, br_ref, o_ref):
    logits = jnp.dot(hh_ref[...].astype(f32), wr_ref[...], preferred_element_type=f32,
                     precision=lax.Precision.HIGHEST) + br_ref[...]
    lane = lax.broadcasted_iota(jnp.int32, logits.shape, 1).astype(f32)
    m1 = jnp.max(logits, axis=-1, keepdims=True)
    i1 = jnp.min(jnp.where(logits == m1, lane, float(DH)), axis=-1, keepdims=True)
    rest = jnp.where(lane == i1, -jnp.inf, logits)
    m2 = jnp.max(rest, axis=-1, keepdims=True)
    i2 = jnp.min(jnp.where(rest == m2, lane, float(DH)), axis=-1, keepdims=True)
    e21 = jnp.exp(m2 - m1)
    p1 = 1.0 / (1.0 + e21)
    p2 = e21 * p1
    o_ref[...] = (jnp.where(lane == 0.0, i1, 0.0) + jnp.where(lane == 1.0, i2, 0.0)
                  + jnp.where(lane == 2.0, p1, 0.0) + jnp.where(lane == 3.0, p2, 0.0))


def router(hh, wr_pad, br_pad):
    n = hh.shape[0]
    tm = 1024
    return pl.pallas_call(
        _router_kernel,
        grid=(n // tm,),
        in_specs=[pl.BlockSpec((tm, D), lambda i: (i, 0)),
                  pl.BlockSpec((D, DH), lambda i: (0, 0)),
                  pl.BlockSpec((1, DH), lambda i: (0, 0))],
        out_specs=pl.BlockSpec((tm, DH), lambda i: (i, 0)),
        out_shape=jax.ShapeDtypeStruct((n, DH), f32),
        compiler_params=_cparams("parallel"),
        name="router",
    )(hh, wr_pad, br_pad)


def _moe_up_kernel(te_ref, x_ref, w1_ref, w3_ref, a_ref):
    del te_ref
    x = x_ref[...]
    a = jnp.dot(x, w1_ref[...], preferred_element_type=f32)
    b = jnp.dot(x, w3_ref[...], preferred_element_type=f32)
    a_ref[...] = (a * _sigmoid(a) * b).astype(a_ref.dtype)


def _moe_down_kernel(te_ref, a_ref, w2_ref, y_ref):
    del te_ref
    y_ref[...] = jnp.dot(a_ref[...], w2_ref[...], preferred_element_type=f32).astype(y_ref.dtype)


def moe_experts(tile_expert, xs, w1, w3, w2):
    r = xs.shape[0]
    nt = r // MOE_TM
    tf = 1408
    up = pl.pallas_call(
        _moe_up_kernel,
        grid_spec=pltpu.PrefetchScalarGridSpec(
            num_scalar_prefetch=1,
            grid=(D_FF // tf, nt),
            in_specs=[pl.BlockSpec((MOE_TM, D), lambda j, i, te: (i, 0)),
                      pl.BlockSpec((None, D, tf), lambda j, i, te: (te[i], 0, j)),
                      pl.BlockSpec((None, D, tf), lambda j, i, te: (te[i], 0, j))],
            out_specs=pl.BlockSpec((MOE_TM, tf), lambda j, i, te: (i, j))),
        out_shape=jax.ShapeDtypeStruct((r, D_FF), bf16),
        compiler_params=_cparams("arbitrary", "arbitrary"),
        name="moe_up",
    )(tile_expert, xs, w1, w3)
    return pl.pallas_call(
        _moe_down_kernel,
        grid_spec=pltpu.PrefetchScalarGridSpec(
            num_scalar_prefetch=1,
            grid=(nt,),
            in_specs=[pl.BlockSpec((MOE_TM, D_FF), lambda i, te: (i, 0)),
                      pl.BlockSpec((None, D_FF, D), lambda i, te: (te[i], 0, 0))],
            out_specs=pl.BlockSpec((MOE_TM, D), lambda i, te: (i, 0))),
        out_shape=jax.ShapeDtypeStruct((r, D), f32),
        compiler_params=_cparams("arbitrary"),
        name="moe_down",
    )(tile_expert, up, w2)


def _combine_kernel(ya_ref, yb_ref, rt_ref, x1_ref, mod_ref, modn_ref, lg_ref, lb_ref, x2_ref, hn_ref):
    f = rt_ref[:, 2:3] * ya_ref[...] + rt_ref[:, 3:4] * yb_ref[...]
    _post_ffn(f, x1_ref[...], mod_ref, modn_ref, lg_ref[...], lb_ref[...], x2_ref, hn_ref)


def combine(ya, yb, rt, x1, mod_l, mod_next, lg, lb):
    n = x1.shape[0]
    tm = 1024
    grp = _group_of_tile(tm)
    row = lambda i: (i, 0)
    const = lambda i: (0, 0)
    return pl.pallas_call(
        _combine_kernel,
        grid=(n // tm,),
        in_specs=[pl.BlockSpec((tm, D), row), pl.BlockSpec((tm, D), row),
                  pl.BlockSpec((tm, DH), row), pl.BlockSpec((tm, D), row),
                  pl.BlockSpec((None, 8, D), lambda i: (grp(i), 0, 0)),
                  pl.BlockSpec((None, 8, D), lambda i: (grp(i), 0, 0)),
                  pl.BlockSpec((1, D), const), pl.BlockSpec((1, D), const)],
        out_specs=[pl.BlockSpec((tm, D), row), pl.BlockSpec((tm, D), row)],
        out_shape=[jax.ShapeDtypeStruct((n, D), f32), jax.ShapeDtypeStruct((n, D), bf16)],
        compiler_params=_cparams("parallel"),
        name="combine",
    )(ya, yb, rt, x1, mod_l, mod_next, lg.reshape(1, D), lb.reshape(1, D))


def moe_dispatch(rt, n):
    r_total = 2 * n + N_EXPERTS * MOE_TM
    nt = r_total // MOE_TM
    e_idx = rt[:, :2].astype(jnp.int32)
    flat_e = e_idx.reshape(-1)
    onehot = (flat_e[:, None] == jnp.arange(N_EXPERTS)[None, :]).astype(jnp.int32)
    rank = jnp.cumsum(onehot, axis=0) - onehot
    counts = jnp.sum(onehot, axis=0)
    padded = (counts + MOE_TM - 1) // MOE_TM * MOE_TM
    starts = jnp.cumsum(padded) - padded
    dest = jnp.sum(onehot * (starts[None, :] + rank), axis=1)
    src_tok = jnp.zeros((r_total,), jnp.int32).at[dest].set(jnp.arange(2 * n, dtype=jnp.int32) // 2)
    tile_start = jnp.arange(nt, dtype=jnp.int32) * MOE_TM
    ends = starts + padded
    tile_expert = jnp.minimum(jnp.sum((tile_start[:, None] >= ends[None, :]).astype(jnp.int32), axis=1),
                              N_EXPERTS - 1)
    return src_tok, dest.reshape(n, 2), tile_expert


def kernel(x_prompt, x_sample, c, state_hgrn, state_mlstm_C, state_mlstm_n, state_mlstm_m, c_ctx,
           w_ada, b_ada, w_in, b_in, hgrn_lb_raw, hgrn_norm_g, conv_w, conv_b, w_mq, w_mk,
           mlstm_fbias, mlstm_norm_g, w_branch_a, w_branch_b, w_out, ln1_g, ln1_b, ln2_g, ln2_b,
           ffn_w1, ffn_w3, ffn_w2, moe_router_w, moe_router_b, moe_w1, moe_w3, moe_w2):
    nb, seq, _ = x_prompt.shape
    db, dseq, _ = x_sample.shape
    n_ctx, n_lat = nb * seq, db * dseq
    n = n_ctx + n_lat
    assert n_ctx == N_CTX and dseq == SEQ_GROUP and db == 2

    lb = jnp.cumsum(jax.nn.softmax(hgrn_lb_raw.astype(f32), axis=1), axis=1)
    lb = lb - lb[:, :1]

    cond8 = jnp.concatenate([c_ctx[None, :], c, jnp.zeros((5, D), f32)], axis=0)
    mod = modulation(cond8, w_ada, b_ada)[:, :3, :].reshape(DEPTH, 3, 6, D)
    mod = jnp.pad(mod, ((0, 0), (0, 0), (0, 2), (0, 0)))

    x = jnp.concatenate([x_prompt.reshape(n_ctx, D), x_sample.reshape(n_lat, D)], axis=0)
    h = modulate(x, mod[0])

    sizes = (D,) * 8 + (4 * NH, D, D)
    offs = [0]
    for s in sizes:
        offs.append(offs[-1] + s)

    def cols(w, idxs):
        return jnp.concatenate([w[..., offs[i]:offs[i + 1]] for i in idxs], axis=-1)

    out_a, out_c, out_n, out_m = [], [], [], []
    for l in range(DEPTH):
        wl, bl = w_in[l], b_in[l]
        qig = linear(h, cols(wl, (0, 3, 4)).astype(bf16), cols(bl, (0, 3, 4)), bf16, "proj_qig")
        zz = linear(h, cols(wl, (1, 2)).astype(bf16), cols(bl, (1, 2)), f32, "proj_z")
        bb = linear(h, cols(wl, (5, 6, 7)).astype(bf16), cols(bl, (5, 6, 7)), bf16, "proj_b")
        gm = linear(h, cols(wl, (9, 10)).astype(bf16), cols(bl, (9, 10)), bf16, "proj_gm")
        w_g = jnp.pad(cols(wl, (8,)), ((0, 0), (0, DH - 4 * NH))).astype(bf16)
        gates = linear(h, w_g, jnp.pad(cols(bl, (8,)), (0, DH - 4 * NH)), f32, "proj_gates")
        gates_t = gates.T

        lbl = lb[:, l]
        lbp = jnp.concatenate([jnp.log(lbl), jnp.log1p(-lbl), 1.0 - lbl, jnp.zeros((2, D), f32)], axis=0)
        gain_a = hgrn_norm_g[l].reshape(1, DH)
        ya_ctx, sa = hgrn_scan(qig, zz, lbp, gain_a, None, n_seq=nb, T=seq, row_blk0=0,
                               want_state=True, name="hgrn_ctx")
        (ya_lat,) = hgrn_scan(qig, zz, lbp, gain_a, state_hgrn[:, l], n_seq=db, T=dseq,
                              row_blk0=n_ctx // dseq, want_state=False, name="hgrn_lat")
        ya = jnp.concatenate([ya_ctx, ya_lat], axis=0)

        cw9 = conv_w[l].reshape(9, D)
        cb = conv_b[l].reshape(1, D)
        wq = w_mq[l].astype(bf16)
        wk = w_mk[l].astype(bf16)
        fb = mlstm_fbias[l]
        zero8 = jnp.zeros((NH,), f32)
        fb_row = jnp.concatenate([zero8, fb[0], zero8, fb[1], jnp.zeros((DH - 4 * NH,), f32)]).reshape(1, DH)
        fb_col = fb.reshape(2, NH, 1)
        gain_b = mlstm_norm_g[l].reshape(1, DH)
        yb_ctx, s_c, s_n, s_m = mlstm_scan(bb, gates, gates_t, cw9, cb, wq, wk, fb_row, fb_col, gain_b,
                                           None, n_seq=nb, T=seq, row_blk0=0, grid_rows=None,
                                           want_state=True, name="mlstm_ctx")
        c0aug = jnp.concatenate([state_mlstm_C[:, l], state_mlstm_n[:, l][..., None],
                                 jnp.zeros((db, 2, NH, DH, DH - 1), f32)], axis=-1)
        m0 = state_mlstm_m[:, l].reshape(db, 2, NH, 1, 1)
        (yb_lat,) = mlstm_scan(bb, gates, gates_t, cw9, cb, wq, wk, fb_row, fb_col, gain_b,
                               (c0aug, m0), n_seq=db, T=dseq, row_blk0=n_ctx // dseq,
                               grid_rows=dseq // GRID_W, want_state=False, name="mlstm_lat")
        yb = jnp.concatenate([yb_ctx, yb_lat], axis=0)

        x1, hh = merge(ya, yb, gm, x, mod[l], w_branch_a[l].astype(bf16), w_branch_b[l].astype(bf16),
                       w_out[l].astype(bf16), ln1_g[l], ln1_b[l])

        mod_next = mod[(l + 1) % DEPTH]
        j = l // 2
        if l % 2 == 0:
            x, h = ffn(hh, ffn_w1[j].astype(bf16), ffn_w3[j].astype(bf16), ffn_w2[j].astype(bf16),
                       x1, mod[l], mod_next, ln2_g[l], ln2_b[l])
        else:
            wr = jnp.pad(moe_router_w[j], ((0, 0), (0, DH - N_EXPERTS)))
            br = jnp.concatenate([moe_router_b[j], jnp.full((DH - N_EXPERTS,), -1e30, f32)]).reshape(1, DH)
            rt = router(hh, wr, br)
            src_tok, dest, tile_expert = moe_dispatch(rt, n)
            xs = jnp.take(hh, src_tok, axis=0)
            ys = moe_experts(tile_expert, xs, moe_w1[j].astype(bf16), moe_w3[j].astype(bf16),
                             moe_w2[j].astype(bf16))
            x, h = combine(jnp.take(ys, dest[:, 0], axis=0), jnp.take(ys, dest[:, 1], axis=0), rt,
                           x1, mod[l], mod_next, ln2_g[l], ln2_b[l])

        out_a.append(sa)
        out_c.append(s_c)
        out_n.append(s_n.reshape(nb, 2, NH, DH))
        out_m.append(s_m[..., 0, 0])

    y_prompt = x[:n_ctx].reshape(nb, seq, D)
    y_sample = x[n_ctx:].reshape(db, dseq, D)
    return (y_prompt, y_sample, jnp.stack(out_a, axis=1), jnp.stack(out_c, axis=1),
            jnp.stack(out_n, axis=1), jnp.stack(out_m, axis=1))
```

```python
import functools

import jax
import jax.numpy as jnp
from jax import lax
from jax.experimental import pallas as pl
from jax.experimental.pallas import tpu as pltpu

f32 = jnp.float32
bf16 = jnp.bfloat16

D = 1024
DEPTH = 4
GRID_W = 64
NH = 8
DH = 128
CH_A = 16
HALF_A = 8
SC_A = 128
CH_B = 256
N_EXPERTS = 8
D_FF = 2816
ALPHA = (2 * DEPTH) ** 0.25
SEQ_GROUP = 1024
EXP_CLAMP = 80.0
MOE_TM = 512

VMEM_LIMIT = 56 * 1024 * 1024
NT = (((1,), (1,)), ((), ()))
TN = (((0,), (0,)), ((), ()))


def _cparams(*sem):
    return pltpu.CompilerParams(dimension_semantics=sem, vmem_limit_bytes=VMEM_LIMIT)


def _sigmoid(x):
    return 1.0 / (1.0 + jnp.exp(-x))


def _log_sigmoid(t):
    return jnp.minimum(t, 0.0) - jnp.log1p(jnp.exp(-jnp.abs(t)))


def _layer_norm(r, g, b):
    mu = jnp.mean(r, axis=-1, keepdims=True)
    d = r - mu
    var = jnp.mean(d * d, axis=-1, keepdims=True)
    return d * lax.rsqrt(var + 1e-5) * g + b


def _mod_kernel(c_ref, w_ref, b_ref, o_ref):
    c = c_ref[...]
    act = (c * _sigmoid(c)).astype(bf16)
    o_ref[...] = jnp.dot(act, w_ref[...].astype(bf16), preferred_element_type=f32) + b_ref[...]


def modulation(cond8, w_ada, b_ada):
    L, _, n = w_ada.shape
    tn = 1536
    return pl.pallas_call(
        _mod_kernel,
        grid=(L, n // tn),
        in_specs=[pl.BlockSpec((8, D), lambda l, j: (0, 0)),
                  pl.BlockSpec((None, D, tn), lambda l, j: (l, 0, j)),
                  pl.BlockSpec((None, 1, tn), lambda l, j: (l, 0, j))],
        out_specs=pl.BlockSpec((None, 8, tn), lambda l, j: (l, 0, j)),
        out_shape=jax.ShapeDtypeStruct((L, 8, n), f32),
        compiler_params=_cparams("parallel", "parallel"),
        name="modulation",
    )(cond8, w_ada, b_ada.reshape(L, 1, n))


N_CTX = 32 * 256


def _group_of_tile(tm):
    def idx(i):
        return jnp.maximum(i * tm // SEQ_GROUP - (N_CTX // SEQ_GROUP - 1), 0)
    return idx


def _modulate_kernel(x_ref, mod_ref, h_ref):
    h_ref[...] = (x_ref[...] * (1.0 + mod_ref[1:2, :]) + mod_ref[0:1, :]).astype(h_ref.dtype)


def modulate(x, mod_l):
    n = x.shape[0]
    tm = 1024
    grp = _group_of_tile(tm)
    return pl.pallas_call(
        _modulate_kernel,
        grid=(n // tm,),
        in_specs=[pl.BlockSpec((tm, D), lambda i: (i, 0)),
                  pl.BlockSpec((None, 8, D), lambda i: (grp(i), 0, 0))],
        out_specs=pl.BlockSpec((tm, D), lambda i: (i, 0)),
        out_shape=jax.ShapeDtypeStruct((n, D), bf16),
        compiler_params=_cparams("parallel"),
        name="modulate",
    )(x, mod_l)


def _linear_kernel(x_ref, w_ref, b_ref, o_ref):
    acc = jnp.dot(x_ref[...], w_ref[...], preferred_element_type=f32)
    o_ref[...] = (acc + b_ref[...]).astype(o_ref.dtype)


def linear(x, w, b, out_dtype, name):
    m, k = x.shape
    n = w.shape[1]
    tm = 1024
    tn = 1024 if n % 1024 == 0 else n
    return pl.pallas_call(
        _linear_kernel,
        grid=(m // tm, n // tn),
        in_specs=[pl.BlockSpec((tm, k), lambda i, j: (i, 0)),
                  pl.BlockSpec((k, tn), lambda i, j: (0, j)),
                  pl.BlockSpec((1, tn), lambda i, j: (0, j))],
        out_specs=pl.BlockSpec((tm, tn), lambda i, j: (i, j)),
        out_shape=jax.ShapeDtypeStruct((m, n), out_dtype),
        compiler_params=_cparams("parallel", "parallel"),
        name=name,
    )(x, w, b.reshape(1, n))


def _hgrn_kernel(*refs, has_state, want_state):
    q_ref, v_ref, g_ref, zf_ref, zb_ref, lbp_ref, gain_ref = refs[:7]
    pos = 7
    s0_ref = None
    if has_state:
        s0_ref = refs[pos]
        pos += 1
    if want_state:
        pos += 1
    y_ref = refs[pos]
    pos += 1
    st_ref = None
    if want_state:
        st_ref = refs[pos]
        pos += 1
    lf_ref, kk_ref, o_ref = refs[pos:]

    T = q_ref.shape[0]
    nsc = T // SC_A

    for d, z_ref in enumerate((zf_ref, zb_ref)):
        z = z_ref[...]
        log_lb = lbp_ref[d:d + 1, :]
        log_1m_lb = lbp_ref[2 + d:3 + d, :]
        one_m_lb = lbp_ref[4 + d:5 + d, :]
        e = jnp.exp(-jnp.abs(z))
        log_sig = jnp.minimum(z, 0.0) - jnp.log1p(e)
        u = log_1m_lb + log_sig
        lf = jnp.maximum(log_lb, u) + jnp.log1p(jnp.exp(-jnp.abs(log_lb - u)))
        kk = one_m_lb * (jnp.where(z >= 0.0, e, 1.0) / (1.0 + e))
        lf_ref[d] = lf
        kk_ref[d] = kk

    row = lax.broadcasted_iota(jnp.int32, (SC_A, SC_A), 0)
    col = lax.broadcasted_iota(jnp.int32, (SC_A, SC_A), 1)
    masks = (row >= col, row <= col)
    tris = tuple(jnp.where(m, 1.0, 0.0).astype(bf16) for m in masks)
    nblk = SC_A // CH_A

    if has_state:
        st0 = (s0_ref[0].T, s0_ref[1].T)
    else:
        st0 = (jnp.zeros((DH, DH), f32), jnp.zeros((DH, DH), f32))

    def body(c, carry):
        new = []
        for d in range(2):
            cidx = c if d == 0 else nsc - 1 - c
            r = pl.ds(pl.multiple_of(cidx * SC_A, SC_A), SC_A)
            st = carry[d]
            q = q_ref[r, :].astype(f32)
            v = v_ref[r, :]
            kk = kk_ref[d, r, :]
            b = _tri_matmul(tris[d], lf_ref[d, r, :])
            b_end = b[SC_A - 1:SC_A, :] if d == 0 else b[0:1, :]
            rows = []
            for i in range(nblk):
                mid = i * CH_A + (HALF_A - 1 if d == 0 else HALF_A)
                r_i = b[mid:mid + 1, :]
                ks = (kk * jnp.exp(jnp.minimum(r_i - b, EXP_CLAMP))).astype(bf16)
                blk = slice(i * CH_A, (i + 1) * CH_A)
                qi = (q[blk] * jnp.exp(jnp.minimum(b[blk] - r_i, EXP_CLAMP))).astype(bf16)
                rows.append(lax.dot_general(qi, ks, NT, preferred_element_type=f32))
            att = jnp.where(masks[d], jnp.concatenate(rows, axis=0), 0.0).astype(bf16)
            qb = (q * jnp.exp(b)).astype(bf16)
            o_ref[d, r, :] = (jnp.dot(att, v, preferred_element_type=f32)
                              + lax.dot_general(qb, st.astype(bf16), NT, preferred_element_type=f32))
            kl = (kk * jnp.exp(b_end - b)).astype(bf16)
            new.append(jnp.exp(b_end) * st + lax.dot_general(v, kl, TN, preferred_element_type=f32))
        return tuple(new)

    st_f, st_b = lax.fori_loop(0, nsc, body, st0, unroll=2)

    o = o_ref[0] + o_ref[1]
    ms = jnp.mean(o * o, axis=-1, keepdims=True)
    g = g_ref[...].astype(f32)
    y_ref[...] = (o * lax.rsqrt(ms + 1e-6) * gain_ref[...] * (g * _sigmoid(g))).astype(y_ref.dtype)
    if want_state:
        st_ref[0] = st_f.T
        st_ref[1] = st_b.T


def hgrn_scan(qig, zz, lbp, gain, s0, state_buf, layer, *, n_seq, T, row_blk0, name):
    has_state = s0 is not None
    want_state = state_buf is not None

    def col(off):
        return lambda b, h: (row_blk0 + b, off + h)

    in_specs = [pl.BlockSpec((T, DH), col(0)), pl.BlockSpec((T, DH), col(NH)),
                pl.BlockSpec((T, DH), col(2 * NH)),
                pl.BlockSpec((T, DH), col(0)), pl.BlockSpec((T, DH), col(NH)),
                pl.BlockSpec((8, DH), lambda b, h: (0, h)),
                pl.BlockSpec((1, DH), lambda b, h: (0, 0))]
    args = [qig, qig, qig, zz, zz, lbp, gain]
    if has_state:
        in_specs.append(pl.BlockSpec((None, 2, None, DH, DH), lambda b, h: (b, 0, h, 0, 0)))
        args.append(s0)
    out_specs = [pl.BlockSpec((T, DH), lambda b, h: (b, h))]
    out_shape = [jax.ShapeDtypeStruct((n_seq * T, D), bf16)]
    aliases = {}
    if want_state:
        in_specs.append(pl.BlockSpec(memory_space=pl.ANY))
        args.append(state_buf)
        aliases = {len(args) - 1: 1}
        out_specs.append(pl.BlockSpec((None, None, 2, None, DH, DH),
                                      lambda b, h: (b, layer, 0, h, 0, 0)))
        out_shape.append(jax.ShapeDtypeStruct(state_buf.shape, f32))
    scratch = [pltpu.VMEM((2, T, DH), f32)] * 3
    return pl.pallas_call(
        functools.partial(_hgrn_kernel, has_state=has_state, want_state=want_state),
        grid=(n_seq, NH),
        in_specs=in_specs, out_specs=out_specs, out_shape=out_shape,
        scratch_shapes=scratch,
        input_output_aliases=aliases,
        compiler_params=_cparams("parallel", "parallel"),
        name=name,
    )(*args)


def _split3(x):
    x0 = x.astype(bf16)
    r = x - x0.astype(f32)
    x1 = r.astype(bf16)
    x2 = (r - x1.astype(f32)).astype(bf16)
    return x0, x1, x2


def _tri_matmul(tri, x):
    return sum(jnp.dot(tri, p, preferred_element_type=f32) for p in _split3(x))


def _matmul_tri(x, tri):
    return sum(jnp.dot(p, tri, preferred_element_type=f32) for p in _split3(x))


def _mlstm_kernel(*refs, grid_rows, has_state, want_state):
    x_ref, v_ref, og_ref, gc_ref, gr_ref, cw_ref, cb_ref, wq_ref, wk_ref, fbr_ref, fbc_ref, gain_ref = refs[:12]
    pos = 12
    if has_state:
        c0_ref, m0_ref = refs[pos:pos + 2]
        pos += 2
    if want_state:
        pos += 1
    y_ref = refs[pos]
    pos += 1
    if want_state:
        ct_ref, nt_ref, mt_ref = refs[pos:pos + 3]

    head = pl.program_id(1)
    T = x_ref.shape[0]
    L = min(CH_B, T)
    nchunk = T // L

    x = x_ref[...].astype(f32)
    t_idx = lax.broadcasted_iota(jnp.int32, (T, DH), 0)
    acc = jnp.zeros((T, DH), f32) + cb_ref[...]
    if grid_rows is None:
        taps = [(0, dc) for dc in (-1, 0, 1)]
    else:
        taps = [(dr, dc) for dr in (-1, 0, 1) for dc in (-1, 0, 1)]
        c_idx = t_idx % GRID_W
    for dr, dc in taps:
        off = dr * GRID_W + dc
        tap = (dr + 1) * 3 + (dc + 1)
        w = cw_ref[tap:tap + 1, :]
        xs = x if off == 0 else pltpu.roll(x, (-off) % T, axis=0)
        ok = (t_idx + off >= 0) & (t_idx + off < T)
        if grid_rows is not None:
            ok = ok & (c_idx + dc >= 0) & (c_idx + dc < GRID_W)
        acc = acc + jnp.where(ok, xs, 0.0) * w
    ca = (acc * _sigmoid(acc)).astype(bf16)
    q = jnp.dot(ca, wq_ref[...], preferred_element_type=f32).astype(bf16)
    k = (jnp.dot(ca, wk_ref[...], preferred_element_type=f32) * (DH ** -0.5)).astype(bf16)
    lane = lax.broadcasted_iota(jnp.int32, (T, DH), 1)
    vaug = jnp.concatenate([v_ref[...].astype(f32), jnp.where(lane == 0, 1.0, 0.0)], axis=1)

    gate_lane = lax.broadcasted_iota(jnp.int32, (T, DH), 1) // NH
    is_f = (gate_lane == 1) | (gate_lane == 3)
    gc = gc_ref[...] + fbr_ref[...]
    gc = jnp.where(is_f, _log_sigmoid(gc), gc)

    def pick_lane(a, ln):
        sel = lax.broadcasted_iota(jnp.int32, a.shape, 1) == ln
        return jnp.sum(jnp.where(sel, a, 0.0), axis=1, keepdims=True)

    def pick_row(a, rw):
        sel = lax.broadcasted_iota(jnp.int32, a.shape, 0) == rw
        return jnp.sum(jnp.where(sel, a, 0.0), axis=0, keepdims=True)

    r_i = lax.broadcasted_iota(jnp.int32, (L, L), 0)
    c_i = lax.broadcasted_iota(jnp.int32, (L, L), 1)
    lower = r_i >= c_i
    upper = r_i <= c_i
    tri_lo = jnp.where(lower, 1.0, 0.0).astype(bf16)
    tri_up = jnp.where(upper, 1.0, 0.0).astype(bf16)

    h_parts = [[None] * nchunk for _ in range(2)]
    finals = []
    for d in range(2):
        if has_state:
            caug = c0_ref[d]
            m_prev = m0_ref[d]
        else:
            caug = jnp.zeros((DH, 2 * DH), f32)
            m_prev = jnp.zeros((1, 1), f32)
        order = range(nchunk) if d == 0 else range(nchunk - 1, -1, -1)
        mask = lower if d == 0 else upper
        tri_c = tri_lo if d == 0 else tri_up
        tri_r = tri_up if d == 0 else tri_lo
        for cidx in order:
            rs = slice(cidx * L, (cidx + 1) * L)
            gcc = gc[rs]
            ig_c = pick_lane(gcc, 2 * d * NH + head)
            F_c = pick_lane(_tri_matmul(tri_c, gcc), (2 * d + 1) * NH + head)
            ig_r = pick_row(gr_ref[2 * d * NH:(2 * d + 1) * NH, rs], head)
            lf_r8 = _log_sigmoid(gr_ref[(2 * d + 1) * NH:(2 * d + 2) * NH, rs] + fbc_ref[d])
            F_r = pick_row(_matmul_tri(lf_r8, tri_r), head)
            F_end = F_c[L - 1:L, :] if d == 0 else F_c[0:1, :]

            logd = jnp.where(mask, F_c - F_r + ig_r, -jnp.inf)
            log_inter = F_c + m_prev
            m_q = jnp.maximum(log_inter, jnp.max(logd, axis=-1, keepdims=True))
            w = jnp.exp(logd - m_q)
            a = jnp.exp(log_inter - m_q)
            qc, kc = q[rs], k[rs]
            va = vaug[rs]
            s = lax.dot_general(qc, kc, NT, preferred_element_type=f32) * w
            nd = (a * jnp.dot(qc, caug.astype(bf16), preferred_element_type=f32)
                  + jnp.dot(s.astype(bf16), va.astype(bf16), preferred_element_type=f32))
            num = nd[:, :DH]
            den = nd[:, DH:DH + 1]
            h_parts[d][cidx] = num / jnp.maximum(jnp.abs(den), jnp.exp(-m_q))
            log_w_c = F_end - F_c + ig_c
            log_w_r = F_end - F_r + ig_r
            m_new = jnp.maximum(F_end + m_prev, jnp.max(log_w_r, axis=-1, keepdims=True))
            ws_c = jnp.exp(log_w_c - m_new)
            d0 = jnp.exp(F_end + m_prev - m_new)
            caug = d0 * caug + lax.dot_general(kc, (ws_c * va).astype(bf16), TN,
                                               preferred_element_type=f32)
            m_prev = m_new
        finals.append((caug, m_prev))

    hsum = jnp.concatenate([h_parts[0][c] + h_parts[1][c] for c in range(nchunk)], axis=0)
    mu = jnp.mean(hsum, axis=-1, keepdims=True)
    dv = hsum - mu
    var = jnp.mean(dv * dv, axis=-1, keepdims=True)
    og = og_ref[...].astype(f32)
    y_ref[...] = (dv * lax.rsqrt(var + 1e-6) * gain_ref[...] * _sigmoid(og)).astype(y_ref.dtype)
    if want_state:
        for d in range(2):
            caug, m_fin = finals[d]
            ct_ref[d] = caug[:, :DH]
            nt_ref[d] = caug[:, DH:2 * DH].T[0:1, :]
            mt_ref[d] = jnp.broadcast_to(m_fin, (1, DH))


def mlstm_scan(bb, gates, gates_t, conv_w9, conv_b, wq, wk, fb_row, fb_col, gain, state, c_buf, layer,
               *, n_seq, T, row_blk0, grid_rows, name):
    has_state = state is not None
    want_state = c_buf is not None

    def col(off):
        return lambda b, h: (row_blk0 + b, off + h)

    in_specs = [pl.BlockSpec((T, DH), col(0)), pl.BlockSpec((T, DH), col(NH)),
                pl.BlockSpec((T, DH), col(2 * NH)),
                pl.BlockSpec((T, DH), lambda b, h: (row_blk0 + b, 0)),
                pl.BlockSpec((DH, T), lambda b, h: (0, row_blk0 + b)),
                pl.BlockSpec((9, DH), lambda b, h: (0, h)),
                pl.BlockSpec((1, DH), lambda b, h: (0, h)),
                pl.BlockSpec((None, DH, DH), lambda b, h: (h, 0, 0)),
                pl.BlockSpec((None, DH, DH), lambda b, h: (h, 0, 0)),
                pl.BlockSpec((1, DH), lambda b, h: (0, 0)),
                pl.BlockSpec((2, NH, 1), lambda b, h: (0, 0, 0)),
                pl.BlockSpec((1, DH), lambda b, h: (0, 0))]
    args = [bb, bb, bb, gates, gates_t, conv_w9, conv_b, wq, wk, fb_row, fb_col, gain]
    if has_state:
        c0aug, m0 = state
        in_specs += [pl.BlockSpec((None, 2, None, DH, 2 * DH), lambda b, h: (b, 0, h, 0, 0)),
                     pl.BlockSpec((None, 2, None, 1, 1), lambda b, h: (b, 0, h, 0, 0))]
        args += [c0aug, m0]
    out_specs = [pl.BlockSpec((T, DH), lambda b, h: (b, h))]
    out_shape = [jax.ShapeDtypeStruct((n_seq * T, D), bf16)]
    aliases = {}
    if want_state:
        in_specs.append(pl.BlockSpec(memory_space=pl.ANY))
        args.append(c_buf)
        aliases = {len(args) - 1: 1}
        out_specs += [pl.BlockSpec((None, None, 2, None, DH, DH), lambda b, h: (b, layer, 0, h, 0, 0)),
                      pl.BlockSpec((None, 2, None, 1, DH), lambda b, h: (b, 0, h, 0, 0)),
                      pl.BlockSpec((None, 2, None, 1, DH), lambda b, h: (b, 0, h, 0, 0))]
        out_shape += [jax.ShapeDtypeStruct(c_buf.shape, f32),
                      jax.ShapeDtypeStruct((n_seq, 2, NH, 1, DH), f32),
                      jax.ShapeDtypeStruct((n_seq, 2, NH, 1, DH), f32)]
    return pl.pallas_call(
        functools.partial(_mlstm_kernel, grid_rows=grid_rows, has_state=has_state,
                          want_state=want_state),
        grid=(n_seq, NH),
        in_specs=in_specs, out_specs=out_specs, out_shape=out_shape,
        input_output_aliases=aliases,
        compiler_params=_cparams("parallel", "parallel"),
        name=name,
    )(*args)


def _merge_kernel(ya_ref, yb_ref, gm_ref, x_ref, mod_ref, wa_ref, wb_ref, wo_ref, lg_ref, lb_ref,
                  x1_ref, hh_ref):
    pa = jnp.dot(ya_ref[...], wa_ref[...], preferred_element_type=f32)
    pb = jnp.dot(yb_ref[...], wb_ref[...], preferred_element_type=f32)
    gma = gm_ref[:, :D].astype(f32)
    gmb = gm_ref[:, D:].astype(f32)
    merged = (_sigmoid(gma) * pa + _sigmoid(gmb) * pb).astype(bf16)
    y = jnp.dot(merged, wo_ref[...], preferred_element_type=f32)
    x1 = _layer_norm(ALPHA * x_ref[...] + mod_ref[2:3, :] * y, lg_ref[...], lb_ref[...])
    x1_ref[...] = x1
    hh_ref[...] = (x1 * (1.0 + mod_ref[4:5, :]) + mod_ref[3:4, :]).astype(hh_ref.dtype)


def merge(ya, yb, gm, x, mod_l, wa, wb, wo, lg, lb):
    n = x.shape[0]
    tm = 512
    grp = _group_of_tile(tm)
    row = lambda i: (i, 0)
    const = lambda i: (0, 0)
    return pl.pallas_call(
        _merge_kernel,
        grid=(n // tm,),
        in_specs=[pl.BlockSpec((tm, D), row), pl.BlockSpec((tm, D), row),
                  pl.BlockSpec((tm, 2 * D), row), pl.BlockSpec((tm, D), row),
                  pl.BlockSpec((None, 8, D), lambda i: (grp(i), 0, 0)),
                  pl.BlockSpec((D, D), const), pl.BlockSpec((D, D), const), pl.BlockSpec((D, D), const),
                  pl.BlockSpec((1, D), const), pl.BlockSpec((1, D), const)],
        out_specs=[pl.BlockSpec((tm, D), row), pl.BlockSpec((tm, D), row)],
        out_shape=[jax.ShapeDtypeStruct((n, D), f32), jax.ShapeDtypeStruct((n, D), bf16)],
        compiler_params=_cparams("parallel"),
        name="merge",
    )(ya, yb, gm, x, mod_l, wa, wb, wo, lg.reshape(1, D), lb.reshape(1, D))


def _post_ffn(f, x1, mod, mod_next, lg, lb, x2_ref, hn_ref):
    x2 = _layer_norm(ALPHA * x1 + mod[5:6, :] * f, lg, lb)
    x2_ref[...] = x2
    hn_ref[...] = (x2 * (1.0 + mod_next[1:2, :]) + mod_next[0:1, :]).astype(hn_ref.dtype)


def _ffn_kernel(hh_ref, w1_ref, w3_ref, w2_ref, x1_ref, mod_ref, modn_ref, lg_ref, lb_ref,
                x2_ref, hn_ref, acc_ref):
    j = pl.program_id(1)
    hh = hh_ref[...]
    a = jnp.dot(hh, w1_ref[...], preferred_element_type=f32)
    b = jnp.dot(hh, w3_ref[...], preferred_element_type=f32)
    part = jnp.dot((a * _sigmoid(a) * b).astype(bf16), w2_ref[...], preferred_element_type=f32)

    @pl.when(j == 0)
    def _():
        acc_ref[...] = part

    @pl.when(j != 0)
    def _():
        acc_ref[...] += part

    @pl.when(j == pl.num_programs(1) - 1)
    def _():
        _post_ffn(acc_ref[...], x1_ref[...], mod_ref, modn_ref, lg_ref[...], lb_ref[...],
                  x2_ref, hn_ref)


def ffn(hh, w1, w3, w2, x1, mod_l, mod_next, lg, lb):
    n = hh.shape[0]
    tm, tf = 512, 1408
    grp = _group_of_tile(tm)
    row = lambda i, j: (i, 0)
    const = lambda i, j: (0, 0)
    return pl.pallas_call(
        _ffn_kernel,
        grid=(n // tm, D_FF // tf),
        in_specs=[pl.BlockSpec((tm, D), row),
                  pl.BlockSpec((D, tf), lambda i, j: (0, j)),
                  pl.BlockSpec((D, tf), lambda i, j: (0, j)),
                  pl.BlockSpec((tf, D), lambda i, j: (j, 0)),
                  pl.BlockSpec((tm, D), row),
                  pl.BlockSpec((None, 8, D), lambda i, j: (grp(i), 0, 0)),
                  pl.BlockSpec((None, 8, D), lambda i, j: (grp(i), 0, 0)),
                  pl.BlockSpec((1, D), const), pl.BlockSpec((1, D), const)],
        out_specs=[pl.BlockSpec((tm, D), row), pl.BlockSpec((tm, D), row)],
        out_shape=[jax.ShapeDtypeStruct((n, D), f32), jax.ShapeDtypeStruct((n, D), bf16)],
        scratch_shapes=[pltpu.VMEM((tm, D), f32)],
        compiler_params=_cparams("parallel", "arbitrary"),
        name="ffn",
    )(hh, w1, w3, w2, x1, mod_l, mod_next, lg.reshape(1, D), lb.reshape(1, D))


def _router_kernel(hh_ref, wr_ref, br_ref, o_ref, cnt_ref, run_ref):
    logits = jnp.dot(hh_ref[...].astype(f32), wr_ref[...], preferred_element_type=f32,
                     precision=lax.Precision.HIGHEST) + br_ref[...]
    lane = lax.broadcasted_iota(jnp.int32, logits.shape, 1).astype(f32)
    m1 = jnp.max(logits, axis=-1, keepdims=True)
    i1 = jnp.min(jnp.where(logits == m1, lane, float(DH)), axis=-1, keepdims=True)
    rest = jnp.where(lane == i1, -jnp.inf, logits)
    m2 = jnp.max(rest, axis=-1, keepdims=True)
    i2 = jnp.min(jnp.where(rest == m2, lane, float(DH)), axis=-1, keepdims=True)
    e21 = jnp.exp(m2 - m1)
    p1 = 1.0 / (1.0 + e21)
    p2 = e21 * p1

    @pl.when(pl.program_id(0) == 0)
    def _():
        run_ref[...] = jnp.zeros_like(run_ref)

    tm = logits.shape[0]
    r_i = lax.broadcasted_iota(jnp.int32, (tm, tm), 0)
    c_i = lax.broadcasted_iota(jnp.int32, (tm, tm), 1)
    earlier = jnp.where(c_i < r_i, 1.0, 0.0).astype(bf16)
    oh1 = jnp.where(lane == i1, 1.0, 0.0)
    oh2 = jnp.where(lane == i2, 1.0, 0.0)
    pre1 = jnp.dot(earlier, oh1.astype(bf16), preferred_element_type=f32)
    pre2 = jnp.dot(earlier, oh2.astype(bf16), preferred_element_type=f32)
    cnt1 = jnp.sum(oh1, axis=0, keepdims=True)
    cnt2 = jnp.sum(oh2, axis=0, keepdims=True)
    run = run_ref[...]
    rank1 = jnp.sum(oh1 * (run + pre1), axis=-1, keepdims=True)
    rank2 = jnp.sum(oh2 * (run + cnt1 + pre2), axis=-1, keepdims=True)
    run_ref[...] = run + cnt1 + cnt2
    cnt_ref[...] = jnp.broadcast_to(run + cnt1 + cnt2, cnt_ref.shape)
    o_ref[...] = (jnp.where(lane == 0.0, i1, 0.0) + jnp.where(lane == 1.0, i2, 0.0)
                  + jnp.where(lane == 2.0, p1, 0.0) + jnp.where(lane == 3.0, p2, 0.0)
                  + jnp.where(lane == 4.0, rank1, 0.0) + jnp.where(lane == 5.0, rank2, 0.0))


def router(hh, wr_pad, br_pad):
    n = hh.shape[0]
    tm = 1024
    return pl.pallas_call(
        _router_kernel,
        grid=(n // tm,),
        in_specs=[pl.BlockSpec((tm, D), lambda i: (i, 0)),
                  pl.BlockSpec((D, DH), lambda i: (0, 0)),
                  pl.BlockSpec((1, DH), lambda i: (0, 0))],
        out_specs=[pl.BlockSpec((tm, DH), lambda i: (i, 0)),
                   pl.BlockSpec((8, DH), lambda i: (0, 0))],
        out_shape=[jax.ShapeDtypeStruct((n, DH), f32), jax.ShapeDtypeStruct((8, DH), f32)],
        scratch_shapes=[pltpu.VMEM((1, DH), f32)],
        compiler_params=_cparams("arbitrary"),
        name="router",
    )(hh, wr_pad, br_pad)


def _moe_up_kernel(te_ref, x_ref, w1_ref, w3_ref, a_ref):
    del te_ref
    x = x_ref[...]
    a = jnp.dot(x, w1_ref[...].astype(bf16), preferred_element_type=f32)
    b = jnp.dot(x, w3_ref[...].astype(bf16), preferred_element_type=f32)
    a_ref[...] = (a * _sigmoid(a) * b).astype(a_ref.dtype)


def _moe_down_kernel(te_ref, a_ref, w2_ref, y_ref):
    del te_ref
    y_ref[...] = jnp.dot(a_ref[...], w2_ref[...].astype(bf16),
                         preferred_element_type=f32).astype(y_ref.dtype)


def moe_experts(tile_expert, xs, w1, w3, w2):
    r = xs.shape[0]
    nt = r // MOE_TM
    tf = 1408
    up = pl.pallas_call(
        _moe_up_kernel,
        grid_spec=pltpu.PrefetchScalarGridSpec(
            num_scalar_prefetch=1,
            grid=(D_FF // tf, nt),
            in_specs=[pl.BlockSpec((MOE_TM, D), lambda j, i, te: (i, 0)),
                      pl.BlockSpec((None, D, tf), lambda j, i, te: (te[i], 0, j)),
                      pl.BlockSpec((None, D, tf), lambda j, i, te: (te[i], 0, j))],
            out_specs=pl.BlockSpec((MOE_TM, tf), lambda j, i, te: (i, j))),
        out_shape=jax.ShapeDtypeStruct((r, D_FF), bf16),
        compiler_params=_cparams("arbitrary", "arbitrary"),
        name="moe_up",
    )(tile_expert, xs, w1, w3)
    return pl.pallas_call(
        _moe_down_kernel,
        grid_spec=pltpu.PrefetchScalarGridSpec(
            num_scalar_prefetch=1,
            grid=(nt,),
            in_specs=[pl.BlockSpec((MOE_TM, D_FF), lambda i, te: (i, 0)),
                      pl.BlockSpec((None, D_FF, D), lambda i, te: (te[i], 0, 0))],
            out_specs=pl.BlockSpec((MOE_TM, D), lambda i, te: (i, 0))),
        out_shape=jax.ShapeDtypeStruct((r, D), f32),
        compiler_params=_cparams("arbitrary"),
        name="moe_down",
    )(tile_expert, up, w2)


def _combine_kernel(ya_ref, yb_ref, rt_ref, x1_ref, mod_ref, modn_ref, lg_ref, lb_ref, x2_ref, hn_ref):
    f = rt_ref[:, 2:3] * ya_ref[...] + rt_ref[:, 3:4] * yb_ref[...]
    _post_ffn(f, x1_ref[...], mod_ref, modn_ref, lg_ref[...], lb_ref[...], x2_ref, hn_ref)


def combine(ya, yb, rt, x1, mod_l, mod_next, lg, lb):
    n = x1.shape[0]
    tm = 1024
    grp = _group_of_tile(tm)
    row = lambda i: (i, 0)
    const = lambda i: (0, 0)
    return pl.pallas_call(
        _combine_kernel,
        grid=(n // tm,),
        in_specs=[pl.BlockSpec((tm, D), row), pl.BlockSpec((tm, D), row),
                  pl.BlockSpec((tm, DH), row), pl.BlockSpec((tm, D), row),
                  pl.BlockSpec((None, 8, D), lambda i: (grp(i), 0, 0)),
                  pl.BlockSpec((None, 8, D), lambda i: (grp(i), 0, 0)),
                  pl.BlockSpec((1, D), const), pl.BlockSpec((1, D), const)],
        out_specs=[pl.BlockSpec((tm, D), row), pl.BlockSpec((tm, D), row)],
        out_shape=[jax.ShapeDtypeStruct((n, D), f32), jax.ShapeDtypeStruct((n, D), bf16)],
        compiler_params=_cparams("parallel"),
        name="combine",
    )(ya, yb, rt, x1, mod_l, mod_next, lg.reshape(1, D), lb.reshape(1, D))


def moe_dispatch(rt, counts, n):
    r_total = 2 * n + N_EXPERTS * MOE_TM
    nt = r_total // MOE_TM
    cnt = counts[0, :N_EXPERTS].astype(jnp.int32)
    padded = (cnt + MOE_TM - 1) // MOE_TM * MOE_TM
    ends = jnp.cumsum(padded)
    starts = ends - padded
    e_idx = rt[:, 0:2].astype(jnp.int32)
    rank = rt[:, 4:6].astype(jnp.int32)
    onehot = e_idx[..., None] == jnp.arange(N_EXPERTS)[None, None, :]
    dest = rank + jnp.sum(jnp.where(onehot, starts[None, None, :], 0), axis=-1)
    tok = jnp.broadcast_to(jnp.arange(n, dtype=jnp.int32)[:, None], (n, 2))
    src_tok = jnp.zeros((r_total,), jnp.int32).at[dest.reshape(-1)].set(
        tok.reshape(-1), unique_indices=True, mode="promise_in_bounds")
    tile_start = jnp.arange(nt, dtype=jnp.int32) * MOE_TM
    tile_expert = jnp.minimum(jnp.sum((tile_start[:, None] >= ends[None, :]).astype(jnp.int32), axis=1),
                              N_EXPERTS - 1)
    return src_tok, dest, tile_expert


def _take_rows(a, idx):
    return a.at[idx].get(mode="promise_in_bounds")


def kernel(x_prompt, x_sample, c, state_hgrn, state_mlstm_C, state_mlstm_n, state_mlstm_m, c_ctx,
           w_ada, b_ada, w_in, b_in, hgrn_lb_raw, hgrn_norm_g, conv_w, conv_b, w_mq, w_mk,
           mlstm_fbias, mlstm_norm_g, w_branch_a, w_branch_b, w_out, ln1_g, ln1_b, ln2_g, ln2_b,
           ffn_w1, ffn_w3, ffn_w2, moe_router_w, moe_router_b, moe_w1, moe_w3, moe_w2):
    nb, seq, _ = x_prompt.shape
    db, dseq, _ = x_sample.shape
    n_ctx, n_lat = nb * seq, db * dseq
    n = n_ctx + n_lat
    assert n_ctx == N_CTX and dseq == SEQ_GROUP and db == 2

    lb = jnp.cumsum(jax.nn.softmax(hgrn_lb_raw.astype(f32), axis=1), axis=1)
    lb = lb - lb[:, :1]

    cond8 = jnp.concatenate([c_ctx[None, :], c, jnp.zeros((5, D), f32)], axis=0)
    mod = modulation(cond8, w_ada, b_ada)[:, :3, :].reshape(DEPTH, 3, 6, D)
    mod = jnp.pad(mod, ((0, 0), (0, 0), (0, 2), (0, 0)))

    x = jnp.concatenate([x_prompt.reshape(n_ctx, D), x_sample.reshape(n_lat, D)], axis=0)
    h = modulate(x, mod[0])

    sizes = (D,) * 8 + (4 * NH, D, D)
    offs = [0]
    for s in sizes:
        offs.append(offs[-1] + s)

    def cols(w, idxs):
        return jnp.concatenate([w[..., offs[i]:offs[i + 1]] for i in idxs], axis=-1)

    out_n, out_m = [], []
    buf_a = jnp.zeros((nb, DEPTH, 2, NH, DH, DH), f32)
    buf_c = jnp.zeros((nb, DEPTH, 2, NH, DH, DH), f32)
    for l in range(DEPTH):
        wl, bl = w_in[l], b_in[l]
        qig = linear(h, cols(wl, (0, 3, 4)).astype(bf16), cols(bl, (0, 3, 4)), bf16, "proj_qig")
        zz = linear(h, cols(wl, (1, 2)).astype(bf16), cols(bl, (1, 2)), f32, "proj_z")
        bb = linear(h, cols(wl, (5, 6, 7)).astype(bf16), cols(bl, (5, 6, 7)), bf16, "proj_b")
        gm = linear(h, cols(wl, (9, 10)).astype(bf16), cols(bl, (9, 10)), bf16, "proj_gm")
        w_g = jnp.pad(cols(wl, (8,)), ((0, 0), (0, DH - 4 * NH))).astype(bf16)
        gates = linear(h, w_g, jnp.pad(cols(bl, (8,)), (0, DH - 4 * NH)), f32, "proj_gates")
        gates_t = gates.T

        lbl = lb[:, l]
        lbp = jnp.concatenate([jnp.log(lbl), jnp.log1p(-lbl), 1.0 - lbl, jnp.zeros((2, D), f32)], axis=0)
        gain_a = hgrn_norm_g[l].reshape(1, DH)
        ya_ctx, buf_a = hgrn_scan(qig, zz, lbp, gain_a, None, buf_a, l, n_seq=nb, T=seq, row_blk0=0,
                                  name="hgrn_ctx")
        (ya_lat,) = hgrn_scan(qig, zz, lbp, gain_a, state_hgrn[:, l], None, l, n_seq=db, T=dseq,
                              row_blk0=n_ctx // dseq, name="hgrn_lat")
        ya = jnp.concatenate([ya_ctx, ya_lat], axis=0)

        cw9 = conv_w[l].reshape(9, D)
        cb = conv_b[l].reshape(1, D)
        wq = w_mq[l].astype(bf16)
        wk = w_mk[l].astype(bf16)
        fb = mlstm_fbias[l]
        zero8 = jnp.zeros((NH,), f32)
        fb_row = jnp.concatenate([zero8, fb[0], zero8, fb[1], jnp.zeros((DH - 4 * NH,), f32)]).reshape(1, DH)
        fb_col = fb.reshape(2, NH, 1)
        gain_b = mlstm_norm_g[l].reshape(1, DH)
        yb_ctx, buf_c, s_n, s_m = mlstm_scan(bb, gates, gates_t, cw9, cb, wq, wk, fb_row, fb_col, gain_b,
                                             None, buf_c, l, n_seq=nb, T=seq, row_blk0=0,
                                             grid_rows=None, name="mlstm_ctx")
        c0aug = jnp.concatenate([state_mlstm_C[:, l], state_mlstm_n[:, l][..., None],
                                 jnp.zeros((db, 2, NH, DH, DH - 1), f32)], axis=-1)
        m0 = state_mlstm_m[:, l].reshape(db, 2, NH, 1, 1)
        (yb_lat,) = mlstm_scan(bb, gates, gates_t, cw9, cb, wq, wk, fb_row, fb_col, gain_b,
                               (c0aug, m0), None, l, n_seq=db, T=dseq, row_blk0=n_ctx // dseq,
                               grid_rows=dseq // GRID_W, name="mlstm_lat")
        yb = jnp.concatenate([yb_ctx, yb_lat], axis=0)

        x1, hh = merge(ya, yb, gm, x, mod[l], w_branch_a[l].astype(bf16), w_branch_b[l].astype(bf16),
                       w_out[l].astype(bf16), ln1_g[l], ln1_b[l])

        mod_next = mod[(l + 1) % DEPTH]
        j = l // 2
        if l % 2 == 0:
            x, h = ffn(hh, ffn_w1[j].astype(bf16), ffn_w3[j].astype(bf16), ffn_w2[j].astype(bf16),
                       x1, mod[l], mod_next, ln2_g[l], ln2_b[l])
        else:
            wr = jnp.pad(moe_router_w[j], ((0, 0), (0, DH - N_EXPERTS)))
            br = jnp.concatenate([moe_router_b[j], jnp.full((DH - N_EXPERTS,), -1e30, f32)]).reshape(1, DH)
            rt, counts = router(hh, wr, br)
            src_tok, dest, tile_expert = moe_dispatch(rt, counts, n)
            ys = moe_experts(tile_expert, _take_rows(hh, src_tok), moe_w1[j], moe_w3[j], moe_w2[j])
            x, h = combine(_take_rows(ys, dest[:, 0]), _take_rows(ys, dest[:, 1]), rt,
                           x1, mod[l], mod_next, ln2_g[l], ln2_b[l])

        out_n.append(s_n.reshape(nb, 2, NH, DH))
        out_m.append(s_m[..., 0, 0])

    y_prompt = x[:n_ctx].reshape(nb, seq, D)
    y_sample = x[n_ctx:].reshape(db, dseq, D)
    return (y_prompt, y_sample, buf_a, buf_c, jnp.stack(out_n, axis=1), jnp.stack(out_m, axis=1))
```

```python
import functools

import jax
import jax.numpy as jnp
from jax import lax
from jax.experimental import pallas as pl
from jax.experimental.pallas import tpu as pltpu

f32 = jnp.float32
bf16 = jnp.bfloat16

D = 1024
DEPTH = 4
GRID_W = 64
NH = 8
DH = 128
CH_A = 16
HALF_A = 8
SC_A = 128
HPS = 4
CH_B = 256
N_EXPERTS = 8
D_FF = 2816
ALPHA = (2 * DEPTH) ** 0.25
SEQ_GROUP = 1024
LOG2E = 1.4426950408889634
EXP2_CLAMP = 115.0
MOE_TM = 512

VMEM_LIMIT = 56 * 1024 * 1024
NT = (((1,), (1,)), ((), ()))
TN = (((0,), (0,)), ((), ()))


def _cparams(*sem):
    return pltpu.CompilerParams(dimension_semantics=sem, vmem_limit_bytes=VMEM_LIMIT)


def _sigmoid(x):
    return 1.0 / (1.0 + jnp.exp(-x))


def _log_sigmoid(t):
    return jnp.minimum(t, 0.0) - jnp.log1p(jnp.exp(-jnp.abs(t)))


def _layer_norm(r, g, b):
    mu = jnp.mean(r, axis=-1, keepdims=True)
    d = r - mu
    var = jnp.mean(d * d, axis=-1, keepdims=True)
    return d * lax.rsqrt(var + 1e-5) * g + b


def _mod_kernel(c_ref, w_ref, b_ref, o_ref):
    c = c_ref[...]
    act = (c * _sigmoid(c)).astype(bf16)
    o_ref[...] = jnp.dot(act, w_ref[...].astype(bf16), preferred_element_type=f32) + b_ref[...]


def modulation(cond8, w_ada, b_ada):
    L, _, n = w_ada.shape
    tn = 1536
    return pl.pallas_call(
        _mod_kernel,
        grid=(L, n // tn),
        in_specs=[pl.BlockSpec((8, D), lambda l, j: (0, 0)),
                  pl.BlockSpec((None, D, tn), lambda l, j: (l, 0, j)),
                  pl.BlockSpec((None, 1, tn), lambda l, j: (l, 0, j))],
        out_specs=pl.BlockSpec((None, 8, tn), lambda l, j: (l, 0, j)),
        out_shape=jax.ShapeDtypeStruct((L, 8, n), f32),
        compiler_params=_cparams("parallel", "parallel"),
        name="modulation",
    )(cond8, w_ada, b_ada.reshape(L, 1, n))


N_CTX = 32 * 256


def _group_of_tile(tm):
    def idx(i):
        return jnp.maximum(i * tm // SEQ_GROUP - (N_CTX // SEQ_GROUP - 1), 0)
    return idx


def _modulate_kernel(x_ref, mod_ref, h_ref):
    h_ref[...] = (x_ref[...] * (1.0 + mod_ref[1:2, :]) + mod_ref[0:1, :]).astype(h_ref.dtype)


def modulate(x, mod_l):
    n = x.shape[0]
    tm = 1024
    grp = _group_of_tile(tm)
    return pl.pallas_call(
        _modulate_kernel,
        grid=(n // tm,),
        in_specs=[pl.BlockSpec((tm, D), lambda i: (i, 0)),
                  pl.BlockSpec((None, 8, D), lambda i: (grp(i), 0, 0))],
        out_specs=pl.BlockSpec((tm, D), lambda i: (i, 0)),
        out_shape=jax.ShapeDtypeStruct((n, D), bf16),
        compiler_params=_cparams("parallel"),
        name="modulate",
    )(x, mod_l)


def _linear_kernel(x_ref, w_ref, b_ref, o_ref):
    acc = jnp.dot(x_ref[...], w_ref[...], preferred_element_type=f32)
    o_ref[...] = (acc + b_ref[...]).astype(o_ref.dtype)


def linear(x, w, b, out_dtype, name):
    m, k = x.shape
    n = w.shape[1]
    tm = 1024
    tn = 1024 if n % 1024 == 0 else n
    return pl.pallas_call(
        _linear_kernel,
        grid=(m // tm, n // tn),
        in_specs=[pl.BlockSpec((tm, k), lambda i, j: (i, 0)),
                  pl.BlockSpec((k, tn), lambda i, j: (0, j)),
                  pl.BlockSpec((1, tn), lambda i, j: (0, j))],
        out_specs=pl.BlockSpec((tm, tn), lambda i, j: (i, j)),
        out_shape=jax.ShapeDtypeStruct((m, n), out_dtype),
        compiler_params=_cparams("parallel", "parallel"),
        name=name,
    )(x, w, b.reshape(1, n))


def _hgrn_kernel(*refs, has_state, want_state, n_aliased):
    q_ref, v_ref, g_ref, zf_ref, zb_ref, lbp_ref, gain_ref = refs[:7]
    pos = 7
    s0_ref = None
    if has_state:
        s0_ref = refs[pos]
        pos += 1
    pos += n_aliased
    y_ref = refs[pos]
    pos += 1
    st_ref = None
    if want_state:
        st_ref = refs[pos]
        pos += 1
    lf_ref, kk_ref, o_ref = refs[pos:]

    T = q_ref.shape[0]
    nsc = T // SC_A

    for d, z_ref in enumerate((zf_ref, zb_ref)):
        z = z_ref[...]
        log_lb = lbp_ref[d:d + 1, :]
        log_1m_lb = lbp_ref[2 + d:3 + d, :]
        one_m_lb = lbp_ref[4 + d:5 + d, :]
        e = jnp.exp(-jnp.abs(z))
        log_sig = jnp.minimum(z, 0.0) - jnp.log(1.0 + e)
        u = log_1m_lb + log_sig
        lf = jnp.maximum(log_lb, u) + jnp.log(1.0 + jnp.exp(-jnp.abs(log_lb - u)))
        kk = one_m_lb * (jnp.where(z >= 0.0, e, 1.0) / (1.0 + e))
        lf_ref[d] = lf * LOG2E
        kk_ref[d] = kk

    row = lax.broadcasted_iota(jnp.int32, (SC_A, SC_A), 0)
    col = lax.broadcasted_iota(jnp.int32, (SC_A, SC_A), 1)
    masks = (row >= col, row <= col)
    tris = tuple(jnp.where(m, 1.0, 0.0).astype(bf16) for m in masks)
    nblk = SC_A // CH_A

    if has_state:
        st0 = tuple(s0_ref[d, hh].T for hh in range(HPS) for d in range(2))
    else:
        st0 = tuple(jnp.zeros((DH, DH), f32) for _ in range(2 * HPS))

    def body(c, carry):
        new = []
        for hd in range(2 * HPS):
            hh, d = divmod(hd, 2)
            ls = slice(hh * DH, (hh + 1) * DH)
            cidx = c if d == 0 else nsc - 1 - c
            r = pl.ds(pl.multiple_of(cidx * SC_A, SC_A), SC_A)
            st = carry[hd]
            q = q_ref[r, ls].astype(f32)
            v = v_ref[r, ls]
            kk = kk_ref[d, r, ls]
            b = _tri_matmul(tris[d], lf_ref[d, r, ls])
            b_end = b[SC_A - 1:SC_A, :] if d == 0 else b[0:1, :]
            rows = []
            for i in range(nblk):
                mid = i * CH_A + (HALF_A - 1 if d == 0 else HALF_A)
                r_i = b[mid:mid + 1, :]
                ks = (kk * jnp.exp2(jnp.minimum(r_i - b, EXP2_CLAMP))).astype(bf16)
                blk = slice(i * CH_A, (i + 1) * CH_A)
                qi = (q[blk] * jnp.exp2(jnp.minimum(b[blk] - r_i, EXP2_CLAMP))).astype(bf16)
                rows.append(lax.dot_general(qi, ks, NT, preferred_element_type=f32))
            att = jnp.where(masks[d], jnp.concatenate(rows, axis=0), 0.0).astype(bf16)
            qb = (q * jnp.exp2(b)).astype(bf16)
            o_ref[d, r, ls] = (jnp.dot(att, v, preferred_element_type=f32)
                               + lax.dot_general(qb, st.astype(bf16), NT, preferred_element_type=f32))
            kl = (kk * jnp.exp2(b_end - b)).astype(bf16)
            new.append(jnp.exp2(b_end) * st + lax.dot_general(v, kl, TN, preferred_element_type=f32))
        return tuple(new)

    st_fin = lax.fori_loop(0, nsc, body, st0, unroll=2)

    for hh in range(HPS):
        ls = slice(hh * DH, (hh + 1) * DH)
        o = o_ref[0, :, ls] + o_ref[1, :, ls]
        ms = jnp.mean(o * o, axis=-1, keepdims=True)
        g = g_ref[:, ls].astype(f32)
        y_ref[:, ls] = (o * lax.rsqrt(ms + 1e-6) * gain_ref[...] * (g * _sigmoid(g))).astype(y_ref.dtype)
        if want_state:
            for d in range(2):
                st_ref[d, hh] = st_fin[2 * hh + d].T


def hgrn_scan(qig, zz, lbp, gain, s0, y_buf, state_buf, layer, *, n_seq, T, row_blk0, name):
    has_state = s0 is not None
    want_state = layer is not None

    wh = HPS * DH

    def col(off):
        return lambda b, h: (row_blk0 + b, off // HPS + h)

    in_specs = [pl.BlockSpec((T, wh), col(0)), pl.BlockSpec((T, wh), col(NH)),
                pl.BlockSpec((T, wh), col(2 * NH)),
                pl.BlockSpec((T, wh), col(0)), pl.BlockSpec((T, wh), col(NH)),
                pl.BlockSpec((8, wh), lambda b, h: (0, h)),
                pl.BlockSpec((1, DH), lambda b, h: (0, 0))]
    args = [qig, qig, qig, zz, zz, lbp, gain]
    if has_state:
        in_specs.append(pl.BlockSpec((None, 2, HPS, DH, DH), lambda b, h: (b, 0, h, 0, 0)))
        args.append(s0)
    out_specs = [pl.BlockSpec((T, wh), col(0))]
    out_shape = [jax.ShapeDtypeStruct((qig.shape[0], D), bf16)]
    aliases = {}
    if y_buf is not None:
        in_specs.append(pl.BlockSpec(memory_space=pl.ANY))
        args.append(y_buf)
        aliases[len(args) - 1] = 0
    if want_state:
        out_specs.append(pl.BlockSpec((None, None, 2, HPS, DH, DH),
                                      lambda b, h: (b, layer, 0, h, 0, 0)))
        out_shape.append(jax.ShapeDtypeStruct((n_seq, DEPTH, 2, NH, DH, DH), f32))
        if state_buf is not None:
            in_specs.append(pl.BlockSpec(memory_space=pl.ANY))
            args.append(state_buf)
            aliases[len(args) - 1] = 1
    scratch = [pltpu.VMEM((2, T, wh), f32)] * 3
    return pl.pallas_call(
        functools.partial(_hgrn_kernel, has_state=has_state, want_state=want_state,
                          n_aliased=len(aliases)),
        grid=(n_seq, NH // HPS),
        in_specs=in_specs, out_specs=out_specs, out_shape=out_shape,
        scratch_shapes=scratch,
        input_output_aliases=aliases,
        compiler_params=_cparams("parallel", "parallel"),
        name=name,
    )(*args)


def _split3(x):
    x0 = x.astype(bf16)
    r = x - x0.astype(f32)
    x1 = r.astype(bf16)
    x2 = (r - x1.astype(f32)).astype(bf16)
    return x0, x1, x2


def _tri_matmul(tri, x):
    return sum(jnp.dot(tri, p, preferred_element_type=f32) for p in _split3(x))


def _matmul_tri(x, tri):
    return sum(jnp.dot(p, tri, preferred_element_type=f32) for p in _split3(x))


def _mlstm_kernel(*refs, grid_rows, has_state, want_state, n_aliased):
    x_ref, v_ref, og_ref, gc_ref, gr_ref, cw_ref, cb_ref, wq_ref, wk_ref, fbr_ref, fbc_ref, gain_ref = refs[:12]
    pos = 12
    if has_state:
        c0_ref, m0_ref = refs[pos:pos + 2]
        pos += 2
    pos += n_aliased
    y_ref = refs[pos]
    pos += 1
    if want_state:
        ct_ref, nt_ref, mt_ref = refs[pos:pos + 3]

    head0 = pl.program_id(1) * HPS
    T = x_ref.shape[0]
    L = min(CH_B, T)
    nchunk = T // L

    t_idx = lax.broadcasted_iota(jnp.int32, (T, DH), 0)
    lane = lax.broadcasted_iota(jnp.int32, (T, DH), 1)
    ones_col = jnp.where(lane == 0, 1.0, 0.0)
    if grid_rows is None:
        taps = [(0, dc) for dc in (-1, 0, 1)]
    else:
        taps = [(dr, dc) for dr in (-1, 0, 1) for dc in (-1, 0, 1)]
        c_idx = t_idx % GRID_W
    qs, ks, vaugs = [], [], []
    for hh in range(HPS):
        ls = slice(hh * DH, (hh + 1) * DH)
        x = x_ref[:, ls].astype(f32)
        acc = jnp.zeros((T, DH), f32) + cb_ref[:, ls]
        for dr, dc in taps:
            off = dr * GRID_W + dc
            tap = (dr + 1) * 3 + (dc + 1)
            w = cw_ref[tap:tap + 1, ls]
            xs = x if off == 0 else pltpu.roll(x, (-off) % T, axis=0)
            ok = (t_idx + off >= 0) & (t_idx + off < T)
            if grid_rows is not None:
                ok = ok & (c_idx + dc >= 0) & (c_idx + dc < GRID_W)
            acc = acc + jnp.where(ok, xs, 0.0) * w
        ca = (acc * _sigmoid(acc)).astype(bf16)
        qs.append(jnp.dot(ca, wq_ref[hh], preferred_element_type=f32).astype(bf16))
        ks.append((jnp.dot(ca, wk_ref[hh], preferred_element_type=f32) * (DH ** -0.5)).astype(bf16))
        vaugs.append(jnp.concatenate([v_ref[:, ls].astype(f32), ones_col], axis=1))

    gate_lane = lax.broadcasted_iota(jnp.int32, (T, DH), 1) // NH
    is_f = (gate_lane == 1) | (gate_lane == 3)
    gc = gc_ref[...] + fbr_ref[...]
    gc = jnp.where(is_f, _log_sigmoid(gc), gc)

    def pick_lane(a, ln):
        sel = lax.broadcasted_iota(jnp.int32, a.shape, 1) == ln
        return jnp.sum(jnp.where(sel, a, 0.0), axis=1, keepdims=True)

    def pick_row(a, rw):
        sel = lax.broadcasted_iota(jnp.int32, a.shape, 0) == rw
        return jnp.sum(jnp.where(sel, a, 0.0), axis=0, keepdims=True)

    r_i = lax.broadcasted_iota(jnp.int32, (L, L), 0)
    c_i = lax.broadcasted_iota(jnp.int32, (L, L), 1)
    lower = r_i >= c_i
    upper = r_i <= c_i
    tri_lo = jnp.where(lower, 1.0, 0.0).astype(bf16)
    tri_up = jnp.where(upper, 1.0, 0.0).astype(bf16)

    h_parts = [[[None] * nchunk for _ in range(2)] for _ in range(HPS)]
    finals = [[None, None] for _ in range(HPS)]
    for d in range(2):
        if has_state:
            carry = [(c0_ref[d, hh], m0_ref[d, hh]) for hh in range(HPS)]
        else:
            carry = [(jnp.zeros((DH, 2 * DH), f32), jnp.zeros((1, 1), f32)) for _ in range(HPS)]
        order = range(nchunk) if d == 0 else range(nchunk - 1, -1, -1)
        mask = lower if d == 0 else upper
        tri_c = tri_lo if d == 0 else tri_up
        tri_r = tri_up if d == 0 else tri_lo
        for cidx in order:
            rs = slice(cidx * L, (cidx + 1) * L)
            gcc = gc[rs]
            F_c_all = _tri_matmul(tri_c, gcc)
            lf_r8 = _log_sigmoid(gr_ref[(2 * d + 1) * NH:(2 * d + 2) * NH, rs] + fbc_ref[d])
            F_r_all = _matmul_tri(lf_r8, tri_r)
            ig_r_all = gr_ref[2 * d * NH:(2 * d + 1) * NH, rs]
            for hh in range(HPS):
                head = head0 + hh
                caug, m_prev = carry[hh]
                ig_c = pick_lane(gcc, 2 * d * NH + head)
                F_c = pick_lane(F_c_all, (2 * d + 1) * NH + head)
                ig_r = pick_row(ig_r_all, head)
                F_r = pick_row(F_r_all, head)
                F_end = F_c[L - 1:L, :] if d == 0 else F_c[0:1, :]

                logd = jnp.where(mask, F_c - F_r + ig_r, -jnp.inf)
                log_inter = F_c + m_prev
                m_q = jnp.maximum(log_inter, jnp.max(logd, axis=-1, keepdims=True))
                w = jnp.exp(logd - m_q)
                a = jnp.exp(log_inter - m_q)
                qc, kc = qs[hh][rs], ks[hh][rs]
                va = vaugs[hh][rs]
                s = lax.dot_general(qc, kc, NT, preferred_element_type=f32) * w
                nd = (a * jnp.dot(qc, caug.astype(bf16), preferred_element_type=f32)
                      + jnp.dot(s.astype(bf16), va.astype(bf16), preferred_element_type=f32))
                num = nd[:, :DH]
                den = nd[:, DH:DH + 1]
                h_parts[hh][d][cidx] = num / jnp.maximum(jnp.abs(den), jnp.exp(-m_q))
                log_w_c = F_end - F_c + ig_c
                log_w_r = F_end - F_r + ig_r
                m_new = jnp.maximum(F_end + m_prev, jnp.max(log_w_r, axis=-1, keepdims=True))
                ws_c = jnp.exp(log_w_c - m_new)
                d0 = jnp.exp(F_end + m_prev - m_new)
                caug = d0 * caug + lax.dot_general(kc, (ws_c * va).astype(bf16), TN,
                                                   preferred_element_type=f32)
                carry[hh] = (caug, m_new)
        for hh in range(HPS):
            finals[hh][d] = carry[hh]

    for hh in range(HPS):
        ls = slice(hh * DH, (hh + 1) * DH)
        hsum = jnp.concatenate([h_parts[hh][0][c] + h_parts[hh][1][c] for c in range(nchunk)], axis=0)
        mu = jnp.mean(hsum, axis=-1, keepdims=True)
        dv = hsum - mu
        var = jnp.mean(dv * dv, axis=-1, keepdims=True)
        og = og_ref[:, ls].astype(f32)
        y_ref[:, ls] = (dv * lax.rsqrt(var + 1e-6) * gain_ref[...] * _sigmoid(og)).astype(y_ref.dtype)
        if want_state:
            for d in range(2):
                caug, m_fin = finals[hh][d]
                ct_ref[d, hh] = caug[:, :DH]
                nt_ref[d, hh] = caug[:, DH:2 * DH].T[0:1, :]
                mt_ref[d, hh] = jnp.broadcast_to(m_fin, (1, DH))


def mlstm_scan(bb, gates, gates_t, conv_w9, conv_b, wq, wk, fb_row, fb_col, gain, state, y_buf, c_buf,
               layer, *, n_seq, T, row_blk0, grid_rows, name):
    has_state = state is not None
    want_state = layer is not None

    wh = HPS * DH

    def col(off):
        return lambda b, h: (row_blk0 + b, off // HPS + h)

    in_specs = [pl.BlockSpec((T, wh), col(0)), pl.BlockSpec((T, wh), col(NH)),
                pl.BlockSpec((T, wh), col(2 * NH)),
                pl.BlockSpec((T, DH), lambda b, h: (row_blk0 + b, 0)),
                pl.BlockSpec((DH, T), lambda b, h: (0, row_blk0 + b)),
                pl.BlockSpec((9, wh), lambda b, h: (0, h)),
                pl.BlockSpec((1, wh), lambda b, h: (0, h)),
                pl.BlockSpec((HPS, DH, DH), lambda b, h: (h, 0, 0)),
                pl.BlockSpec((HPS, DH, DH), lambda b, h: (h, 0, 0)),
                pl.BlockSpec((1, DH), lambda b, h: (0, 0)),
                pl.BlockSpec((2, NH, 1), lambda b, h: (0, 0, 0)),
                pl.BlockSpec((1, DH), lambda b, h: (0, 0))]
    args = [bb, bb, bb, gates, gates_t, conv_w9, conv_b, wq, wk, fb_row, fb_col, gain]
    if has_state:
        c0aug, m0 = state
        in_specs += [pl.BlockSpec((None, 2, HPS, DH, 2 * DH), lambda b, h: (b, 0, h, 0, 0)),
                     pl.BlockSpec((None, 2, HPS, 1, 1), lambda b, h: (b, 0, h, 0, 0))]
        args += [c0aug, m0]
    out_specs = [pl.BlockSpec((T, wh), col(0))]
    out_shape = [jax.ShapeDtypeStruct((bb.shape[0], D), bf16)]
    aliases = {}
    if y_buf is not None:
        in_specs.append(pl.BlockSpec(memory_space=pl.ANY))
        args.append(y_buf)
        aliases[len(args) - 1] = 0
    if want_state:
        out_specs += [pl.BlockSpec((None, None, 2, HPS, DH, DH), lambda b, h: (b, layer, 0, h, 0, 0)),
                      pl.BlockSpec((None, 2, HPS, 1, DH), lambda b, h: (b, 0, h, 0, 0)),
                      pl.BlockSpec((None, 2, HPS, 1, DH), lambda b, h: (b, 0, h, 0, 0))]
        out_shape += [jax.ShapeDtypeStruct((n_seq, DEPTH, 2, NH, DH, DH), f32),
                      jax.ShapeDtypeStruct((n_seq, 2, NH, 1, DH), f32),
                      jax.ShapeDtypeStruct((n_seq, 2, NH, 1, DH), f32)]
        if c_buf is not None:
            in_specs.append(pl.BlockSpec(memory_space=pl.ANY))
            args.append(c_buf)
            aliases[len(args) - 1] = 1
    return pl.pallas_call(
        functools.partial(_mlstm_kernel, grid_rows=grid_rows, has_state=has_state,
                          want_state=want_state, n_aliased=len(aliases)),
        grid=(n_seq, NH // HPS),
        in_specs=in_specs, out_specs=out_specs, out_shape=out_shape,
        input_output_aliases=aliases,
        compiler_params=_cparams("parallel", "parallel"),
        name=name,
    )(*args)


def _merge_kernel(ya_ref, yb_ref, gm_ref, x_ref, mod_ref, wa_ref, wb_ref, wo_ref, lg_ref, lb_ref,
                  x1_ref, hh_ref):
    pa = jnp.dot(ya_ref[...], wa_ref[...], preferred_element_type=f32)
    pb = jnp.dot(yb_ref[...], wb_ref[...], preferred_element_type=f32)
    gma = gm_ref[:, :D].astype(f32)
    gmb = gm_ref[:, D:].astype(f32)
    merged = (_sigmoid(gma) * pa + _sigmoid(gmb) * pb).astype(bf16)
    y = jnp.dot(merged, wo_ref[...], preferred_element_type=f32)
    x1 = _layer_norm(ALPHA * x_ref[...] + mod_ref[2:3, :] * y, lg_ref[...], lb_ref[...])
    x1_ref[...] = x1
    hh_ref[...] = (x1 * (1.0 + mod_ref[4:5, :]) + mod_ref[3:4, :]).astype(hh_ref.dtype)


def merge(ya, yb, gm, x, mod_l, wa, wb, wo, lg, lb):
    n = x.shape[0]
    tm = 512
    grp = _group_of_tile(tm)
    row = lambda i: (i, 0)
    const = lambda i: (0, 0)
    return pl.pallas_call(
        _merge_kernel,
        grid=(n // tm,),
        in_specs=[pl.BlockSpec((tm, D), row), pl.BlockSpec((tm, D), row),
                  pl.BlockSpec((tm, 2 * D), row), pl.BlockSpec((tm, D), row),
                  pl.BlockSpec((None, 8, D), lambda i: (grp(i), 0, 0)),
                  pl.BlockSpec((D, D), const), pl.BlockSpec((D, D), const), pl.BlockSpec((D, D), const),
                  pl.BlockSpec((1, D), const), pl.BlockSpec((1, D), const)],
        out_specs=[pl.BlockSpec((tm, D), row), pl.BlockSpec((tm, D), row)],
        out_shape=[jax.ShapeDtypeStruct((n, D), f32), jax.ShapeDtypeStruct((n, D), bf16)],
        compiler_params=_cparams("parallel"),
        name="merge",
    )(ya, yb, gm, x, mod_l, wa, wb, wo, lg.reshape(1, D), lb.reshape(1, D))


def _post_ffn(f, x1, mod, mod_next, lg, lb, x2_ref, hn_ref):
    x2 = _layer_norm(ALPHA * x1 + mod[5:6, :] * f, lg, lb)
    x2_ref[...] = x2
    hn_ref[...] = (x2 * (1.0 + mod_next[1:2, :]) + mod_next[0:1, :]).astype(hn_ref.dtype)


def _ffn_kernel(hh_ref, w1_ref, w3_ref, w2_ref, x1_ref, mod_ref, modn_ref, lg_ref, lb_ref,
                x2_ref, hn_ref, acc_ref):
    j = pl.program_id(1)
    hh = hh_ref[...]
    a = jnp.dot(hh, w1_ref[...], preferred_element_type=f32)
    b = jnp.dot(hh, w3_ref[...], preferred_element_type=f32)
    part = jnp.dot((a * _sigmoid(a) * b).astype(bf16), w2_ref[...], preferred_element_type=f32)

    @pl.when(j == 0)
    def _():
        acc_ref[...] = part

    @pl.when(j != 0)
    def _():
        acc_ref[...] += part

    @pl.when(j == pl.num_programs(1) - 1)
    def _():
        _post_ffn(acc_ref[...], x1_ref[...], mod_ref, modn_ref, lg_ref[...], lb_ref[...],
                  x2_ref, hn_ref)


def ffn(hh, w1, w3, w2, x1, mod_l, mod_next, lg, lb):
    n = hh.shape[0]
    tm, tf = 512, 1408
    grp = _group_of_tile(tm)
    row = lambda i, j: (i, 0)
    const = lambda i, j: (0, 0)
    return pl.pallas_call(
        _ffn_kernel,
        grid=(n // tm, D_FF // tf),
        in_specs=[pl.BlockSpec((tm, D), row),
                  pl.BlockSpec((D, tf), lambda i, j: (0, j)),
                  pl.BlockSpec((D, tf), lambda i, j: (0, j)),
                  pl.BlockSpec((tf, D), lambda i, j: (j, 0)),
                  pl.BlockSpec((tm, D), row),
                  pl.BlockSpec((None, 8, D), lambda i, j: (grp(i), 0, 0)),
                  pl.BlockSpec((None, 8, D), lambda i, j: (grp(i), 0, 0)),
                  pl.BlockSpec((1, D), const), pl.BlockSpec((1, D), const)],
        out_specs=[pl.BlockSpec((tm, D), row), pl.BlockSpec((tm, D), row)],
        out_shape=[jax.ShapeDtypeStruct((n, D), f32), jax.ShapeDtypeStruct((n, D), bf16)],
        scratch_shapes=[pltpu.VMEM((tm, D), f32)],
        compiler_params=_cparams("parallel", "arbitrary"),
        name="ffn",
    )(hh, w1, w3, w2, x1, mod_l, mod_next, lg.reshape(1, D), lb.reshape(1, D))


def _router_kernel(hh_ref, wr_ref, br_ref, o_ref, cnt_ref, run_ref):
    logits = jnp.dot(hh_ref[...].astype(f32), wr_ref[...], preferred_element_type=f32,
                     precision=lax.Precision.HIGHEST) + br_ref[...]
    lane = lax.broadcasted_iota(jnp.int32, logits.shape, 1).astype(f32)
    m1 = jnp.max(logits, axis=-1, keepdims=True)
    i1 = jnp.min(jnp.where(logits == m1, lane, float(DH)), axis=-1, keepdims=True)
    rest = jnp.where(lane == i1, -jnp.inf, logits)
    m2 = jnp.max(rest, axis=-1, keepdims=True)
    i2 = jnp.min(jnp.where(rest == m2, lane, float(DH)), axis=-1, keepdims=True)
    e21 = jnp.exp(m2 - m1)
    p1 = 1.0 / (1.0 + e21)
    p2 = e21 * p1

    @pl.when(pl.program_id(0) == 0)
    def _():
        run_ref[...] = jnp.zeros_like(run_ref)

    tm = logits.shape[0]
    r_i = lax.broadcasted_iota(jnp.int32, (tm, tm), 0)
    c_i = lax.broadcasted_iota(jnp.int32, (tm, tm), 1)
    earlier = jnp.where(c_i < r_i, 1.0, 0.0).astype(bf16)
    oh1 = jnp.where(lane == i1, 1.0, 0.0)
    oh2 = jnp.where(lane == i2, 1.0, 0.0)
    pre1 = jnp.dot(earlier, oh1.astype(bf16), preferred_element_type=f32)
    pre2 = jnp.dot(earlier, oh2.astype(bf16), preferred_element_type=f32)
    cnt1 = jnp.sum(oh1, axis=0, keepdims=True)
    cnt2 = jnp.sum(oh2, axis=0, keepdims=True)
    run = run_ref[...]
    rank1 = jnp.sum(oh1 * (run + pre1), axis=-1, keepdims=True)
    rank2 = jnp.sum(oh2 * (run + cnt1 + pre2), axis=-1, keepdims=True)
    run_ref[...] = run + cnt1 + cnt2
    cnt_ref[...] = jnp.broadcast_to(run + cnt1 + cnt2, cnt_ref.shape)
    o_ref[...] = (jnp.where(lane == 0.0, i1, 0.0) + jnp.where(lane == 1.0, i2, 0.0)
                  + jnp.where(lane == 2.0, p1, 0.0) + jnp.where(lane == 3.0, p2, 0.0)
                  + jnp.where(lane == 4.0, rank1, 0.0) + jnp.where(lane == 5.0, rank2, 0.0))


def router(hh, wr_pad, br_pad):
    n = hh.shape[0]
    tm = 1024
    return pl.pallas_call(
        _router_kernel,
        grid=(n // tm,),
        in_specs=[pl.BlockSpec((tm, D), lambda i: (i, 0)),
                  pl.BlockSpec((D, DH), lambda i: (0, 0)),
                  pl.BlockSpec((1, DH), lambda i: (0, 0))],
        out_specs=[pl.BlockSpec((tm, DH), lambda i: (i, 0)),
                   pl.BlockSpec((8, DH), lambda i: (0, 0))],
        out_shape=[jax.ShapeDtypeStruct((n, DH), f32), jax.ShapeDtypeStruct((8, DH), f32)],
        scratch_shapes=[pltpu.VMEM((1, DH), f32)],
        compiler_params=_cparams("arbitrary"),
        name="router",
    )(hh, wr_pad, br_pad)


def _moe_up_kernel(te_ref, x_ref, w1_ref, w3_ref, a_ref):
    del te_ref
    x = x_ref[...]
    a = jnp.dot(x, w1_ref[...].astype(bf16), preferred_element_type=f32)
    b = jnp.dot(x, w3_ref[...].astype(bf16), preferred_element_type=f32)
    a_ref[...] = (a * _sigmoid(a) * b).astype(a_ref.dtype)


def _moe_down_kernel(te_ref, a_ref, w2_ref, y_ref):
    del te_ref
    y_ref[...] = jnp.dot(a_ref[...], w2_ref[...].astype(bf16),
                         preferred_element_type=f32).astype(y_ref.dtype)


def moe_experts(tile_expert, xs, w1, w3, w2, lyr):
    r = xs.shape[0]
    nt = r // MOE_TM
    tf = 1408
    up = pl.pallas_call(
        _moe_up_kernel,
        grid_spec=pltpu.PrefetchScalarGridSpec(
            num_scalar_prefetch=1,
            grid=(D_FF // tf, nt),
            in_specs=[pl.BlockSpec((MOE_TM, D), lambda j, i, te: (i, 0)),
                      pl.BlockSpec((None, None, D, tf), lambda j, i, te: (lyr, te[i], 0, j)),
                      pl.BlockSpec((None, None, D, tf), lambda j, i, te: (lyr, te[i], 0, j))],
            out_specs=pl.BlockSpec((MOE_TM, tf), lambda j, i, te: (i, j))),
        out_shape=jax.ShapeDtypeStruct((r, D_FF), bf16),
        compiler_params=_cparams("arbitrary", "arbitrary"),
        name="moe_up",
    )(tile_expert, xs, w1, w3)
    return pl.pallas_call(
        _moe_down_kernel,
        grid_spec=pltpu.PrefetchScalarGridSpec(
            num_scalar_prefetch=1,
            grid=(nt,),
            in_specs=[pl.BlockSpec((MOE_TM, D_FF), lambda i, te: (i, 0)),
                      pl.BlockSpec((None, None, D_FF, D), lambda i, te: (lyr, te[i], 0, 0))],
            out_specs=pl.BlockSpec((MOE_TM, D), lambda i, te: (i, 0))),
        out_shape=jax.ShapeDtypeStruct((r, D), f32),
        compiler_params=_cparams("arbitrary"),
        name="moe_down",
    )(tile_expert, up, w2)


def _combine_kernel(ya_ref, yb_ref, rt_ref, x1_ref, mod_ref, modn_ref, lg_ref, lb_ref, x2_ref, hn_ref):
    f = rt_ref[:, 2:3] * ya_ref[...] + rt_ref[:, 3:4] * yb_ref[...]
    _post_ffn(f, x1_ref[...], mod_ref, modn_ref, lg_ref[...], lb_ref[...], x2_ref, hn_ref)


def combine(ya, yb, rt, x1, mod_l, mod_next, lg, lb):
    n = x1.shape[0]
    tm = 1024
    grp = _group_of_tile(tm)
    row = lambda i: (i, 0)
    const = lambda i: (0, 0)
    return pl.pallas_call(
        _combine_kernel,
        grid=(n // tm,),
        in_specs=[pl.BlockSpec((tm, D), row), pl.BlockSpec((tm, D), row),
                  pl.BlockSpec((tm, DH), row), pl.BlockSpec((tm, D), row),
                  pl.BlockSpec((None, 8, D), lambda i: (grp(i), 0, 0)),
                  pl.BlockSpec((None, 8, D), lambda i: (grp(i), 0, 0)),
                  pl.BlockSpec((1, D), const), pl.BlockSpec((1, D), const)],
        out_specs=[pl.BlockSpec((tm, D), row), pl.BlockSpec((tm, D), row)],
        out_shape=[jax.ShapeDtypeStruct((n, D), f32), jax.ShapeDtypeStruct((n, D), bf16)],
        compiler_params=_cparams("parallel"),
        name="combine",
    )(ya, yb, rt, x1, mod_l, mod_next, lg.reshape(1, D), lb.reshape(1, D))


def moe_dispatch(rt, counts, n):
    r_total = 2 * n + N_EXPERTS * MOE_TM
    nt = r_total // MOE_TM
    cnt = counts[0, :N_EXPERTS].astype(jnp.int32)
    padded = (cnt + MOE_TM - 1) // MOE_TM * MOE_TM
    ends = jnp.cumsum(padded)
    starts = ends - padded
    e_idx = rt[:, 0:2].astype(jnp.int32)
    rank = rt[:, 4:6].astype(jnp.int32)
    onehot = e_idx[..., None] == jnp.arange(N_EXPERTS)[None, None, :]
    dest = rank + jnp.sum(jnp.where(onehot, starts[None, None, :], 0), axis=-1)
    tok = jnp.broadcast_to(jnp.arange(n, dtype=jnp.int32)[:, None], (n, 2))
    src_tok = jnp.zeros((r_total,), jnp.int32).at[dest.reshape(-1)].set(
        tok.reshape(-1), unique_indices=True, mode="promise_in_bounds")
    tile_start = jnp.arange(nt, dtype=jnp.int32) * MOE_TM
    tile_expert = jnp.minimum(jnp.sum((tile_start[:, None] >= ends[None, :]).astype(jnp.int32), axis=1),
                              N_EXPERTS - 1)
    return src_tok, dest, tile_expert


def _take_rows(a, idx):
    return a.at[idx].get(mode="promise_in_bounds")


def kernel(x_prompt, x_sample, c, state_hgrn, state_mlstm_C, state_mlstm_n, state_mlstm_m, c_ctx,
           w_ada, b_ada, w_in, b_in, hgrn_lb_raw, hgrn_norm_g, conv_w, conv_b, w_mq, w_mk,
           mlstm_fbias, mlstm_norm_g, w_branch_a, w_branch_b, w_out, ln1_g, ln1_b, ln2_g, ln2_b,
           ffn_w1, ffn_w3, ffn_w2, moe_router_w, moe_router_b, moe_w1, moe_w3, moe_w2):
    nb, seq, _ = x_prompt.shape
    db, dseq, _ = x_sample.shape
    n_ctx, n_lat = nb * seq, db * dseq
    n = n_ctx + n_lat
    assert n_ctx == N_CTX and dseq == SEQ_GROUP and db == 2

    lb = jnp.cumsum(jax.nn.softmax(hgrn_lb_raw.astype(f32), axis=1), axis=1)
    lb = lb - lb[:, :1]

    cond8 = jnp.concatenate([c_ctx[None, :], c, jnp.zeros((5, D), f32)], axis=0)
    mod = modulation(cond8, w_ada, b_ada)[:, :3, :].reshape(DEPTH, 3, 6, D)
    mod = jnp.pad(mod, ((0, 0), (0, 0), (0, 2), (0, 0)))

    x = jnp.concatenate([x_prompt.reshape(n_ctx, D), x_sample.reshape(n_lat, D)], axis=0)
    h = modulate(x, mod[0])

    sizes = (D,) * 8 + (4 * NH, D, D)
    offs = [0]
    for s in sizes:
        offs.append(offs[-1] + s)

    def cols(w, idxs):
        return jnp.concatenate([w[..., offs[i]:offs[i + 1]] for i in idxs], axis=-1)

    out_n, out_m = [], []
    buf_a = buf_c = None
    for l in range(DEPTH):
        wl, bl = w_in[l], b_in[l]
        qig = linear(h, cols(wl, (0, 3, 4)).astype(bf16), cols(bl, (0, 3, 4)), bf16, "proj_qig")
        zz = linear(h, cols(wl, (1, 2)).astype(bf16), cols(bl, (1, 2)), f32, "proj_z")
        bb = linear(h, cols(wl, (5, 6, 7)).astype(bf16), cols(bl, (5, 6, 7)), bf16, "proj_b")
        gm = linear(h, cols(wl, (9, 10)).astype(bf16), cols(bl, (9, 10)), bf16, "proj_gm")
        w_g = jnp.pad(cols(wl, (8,)), ((0, 0), (0, DH - 4 * NH))).astype(bf16)
        gates = linear(h, w_g, jnp.pad(cols(bl, (8,)), (0, DH - 4 * NH)), f32, "proj_gates")
        gates_t = gates.T

        lbl = lb[:, l]
        lbp = jnp.concatenate([jnp.log(lbl), jnp.log1p(-lbl), 1.0 - lbl, jnp.zeros((2, D), f32)], axis=0)
        gain_a = hgrn_norm_g[l].reshape(1, DH)
        ya, buf_a = hgrn_scan(qig, zz, lbp, gain_a, None, None, buf_a, l, n_seq=nb, T=seq,
                              row_blk0=0, name="hgrn_ctx")
        (ya,) = hgrn_scan(qig, zz, lbp, gain_a, state_hgrn[:, l], ya, None, None, n_seq=db, T=dseq,
                          row_blk0=n_ctx // dseq, name="hgrn_lat")

        cw9 = conv_w[l].reshape(9, D)
        cb = conv_b[l].reshape(1, D)
        wq = w_mq[l].astype(bf16)
        wk = w_mk[l].astype(bf16)
        fb = mlstm_fbias[l]
        zero8 = jnp.zeros((NH,), f32)
        fb_row = jnp.concatenate([zero8, fb[0], zero8, fb[1], jnp.zeros((DH - 4 * NH,), f32)]).reshape(1, DH)
        fb_col = fb.reshape(2, NH, 1)
        gain_b = mlstm_norm_g[l].reshape(1, DH)
        yb, buf_c, s_n, s_m = mlstm_scan(bb, gates, gates_t, cw9, cb, wq, wk, fb_row, fb_col, gain_b,
                                         None, None, buf_c, l, n_seq=nb, T=seq, row_blk0=0,
                                         grid_rows=None, name="mlstm_ctx")
        c0aug = jnp.concatenate([state_mlstm_C[:, l], state_mlstm_n[:, l][..., None],
                                 jnp.zeros((db, 2, NH, DH, DH - 1), f32)], axis=-1)
        m0 = state_mlstm_m[:, l].reshape(db, 2, NH, 1, 1)
        (yb,) = mlstm_scan(bb, gates, gates_t, cw9, cb, wq, wk, fb_row, fb_col, gain_b,
                           (c0aug, m0), yb, None, None, n_seq=db, T=dseq, row_blk0=n_ctx // dseq,
                           grid_rows=dseq // GRID_W, name="mlstm_lat")

        x1, hh = merge(ya, yb, gm, x, mod[l], w_branch_a[l].astype(bf16), w_branch_b[l].astype(bf16),
                       w_out[l].astype(bf16), ln1_g[l], ln1_b[l])

        mod_next = mod[(l + 1) % DEPTH]
        j = l // 2
        if l % 2 == 0:
            x, h = ffn(hh, ffn_w1[j].astype(bf16), ffn_w3[j].astype(bf16), ffn_w2[j].astype(bf16),
                       x1, mod[l], mod_next, ln2_g[l], ln2_b[l])
        else:
            wr = jnp.pad(moe_router_w[j], ((0, 0), (0, DH - N_EXPERTS)))
            br = jnp.concatenate([moe_router_b[j], jnp.full((DH - N_EXPERTS,), -1e30, f32)]).reshape(1, DH)
            rt, counts = router(hh, wr, br)
            src_tok, dest, tile_expert = moe_dispatch(rt, counts, n)
            ys = moe_experts(tile_expert, _take_rows(hh, src_tok), moe_w1, moe_w3, moe_w2, j)
            x, h = combine(_take_rows(ys, dest[:, 0]), _take_rows(ys, dest[:, 1]), rt,
                           x1, mod[l], mod_next, ln2_g[l], ln2_b[l])

        out_n.append(s_n.reshape(nb, 2, NH, DH))
        out_m.append(s_m[..., 0, 0])

    y_prompt = x[:n_ctx].reshape(nb, seq, D)
    y_sample = x[n_ctx:].reshape(db, dseq, D)
    return (y_prompt, y_sample, buf_a, buf_c, jnp.stack(out_n, axis=1), jnp.stack(out_m, axis=1))
```

```python
import functools

import jax
import jax.numpy as jnp
from jax import lax
from jax.experimental import pallas as pl
from jax.experimental.pallas import tpu as pltpu

f32 = jnp.float32
bf16 = jnp.bfloat16

D = 1024
DEPTH = 4
GRID_W = 64
NH = 8
DH = 128
CH_A = 16
HALF_A = 8
SC_A = 128
HPS = 4
CH_B = 256
N_EXPERTS = 8
D_FF = 2816
ALPHA = (2 * DEPTH) ** 0.25
SEQ_GROUP = 1024
LOG2E = 1.4426950408889634
LOG2_F_FLOOR = -1e30
MOE_TM = 512

VMEM_LIMIT = 56 * 1024 * 1024
NT = (((1,), (1,)), ((), ()))
TN = (((0,), (0,)), ((), ()))


def _cparams(*sem):
    return pltpu.CompilerParams(dimension_semantics=sem, vmem_limit_bytes=VMEM_LIMIT)


def _sigmoid(x):
    return 1.0 / (1.0 + jnp.exp(-x))


def _log_sigmoid(t):
    return jnp.minimum(t, 0.0) - jnp.log1p(jnp.exp(-jnp.abs(t)))


def _layer_norm(r, g, b):
    mu = jnp.mean(r, axis=-1, keepdims=True)
    d = r - mu
    var = jnp.mean(d * d, axis=-1, keepdims=True)
    return d * lax.rsqrt(var + 1e-5) * g + b


def _mod_kernel(c_ref, w_ref, b_ref, o_ref):
    c = c_ref[...]
    act = (c * _sigmoid(c)).astype(bf16)
    o_ref[...] = jnp.dot(act, w_ref[...].astype(bf16), preferred_element_type=f32) + b_ref[...]


def modulation(cond8, w_ada, b_ada):
    L, _, n = w_ada.shape
    tn = 1536
    return pl.pallas_call(
        _mod_kernel,
        grid=(L, n // tn),
        in_specs=[pl.BlockSpec((8, D), lambda l, j: (0, 0)),
                  pl.BlockSpec((None, D, tn), lambda l, j: (l, 0, j)),
                  pl.BlockSpec((None, 1, tn), lambda l, j: (l, 0, j))],
        out_specs=pl.BlockSpec((None, 8, tn), lambda l, j: (l, 0, j)),
        out_shape=jax.ShapeDtypeStruct((L, 8, n), f32),
        compiler_params=_cparams("parallel", "parallel"),
        name="modulation",
    )(cond8, w_ada, b_ada.reshape(L, 1, n))


N_CTX = 32 * 256


def _group_of_tile(tm):
    def idx(i):
        return jnp.maximum(i * tm // SEQ_GROUP - (N_CTX // SEQ_GROUP - 1), 0)
    return idx


def _modulate_kernel(x_ref, mod_ref, h_ref):
    h_ref[...] = (x_ref[...] * (1.0 + mod_ref[1:2, :]) + mod_ref[0:1, :]).astype(h_ref.dtype)


def modulate(x, mod_l):
    n = x.shape[0]
    tm = 1024
    grp = _group_of_tile(tm)
    return pl.pallas_call(
        _modulate_kernel,
        grid=(n // tm,),
        in_specs=[pl.BlockSpec((tm, D), lambda i: (i, 0)),
                  pl.BlockSpec((None, 8, D), lambda i: (grp(i), 0, 0))],
        out_specs=pl.BlockSpec((tm, D), lambda i: (i, 0)),
        out_shape=jax.ShapeDtypeStruct((n, D), bf16),
        compiler_params=_cparams("parallel"),
        name="modulate",
    )(x, mod_l)


def _linear_kernel(x_ref, w_ref, b_ref, o_ref):
    acc = jnp.dot(x_ref[...], w_ref[...], preferred_element_type=f32)
    o_ref[...] = (acc + b_ref[...]).astype(o_ref.dtype)


def linear(x, w, b, out_dtype, name):
    m, k = x.shape
    n = w.shape[1]
    tm = 2048
    tn = 1024 if n % 1024 == 0 else n
    return pl.pallas_call(
        _linear_kernel,
        grid=(m // tm, n // tn),
        in_specs=[pl.BlockSpec((tm, k), lambda i, j: (i, 0)),
                  pl.BlockSpec((k, tn), lambda i, j: (0, j)),
                  pl.BlockSpec((1, tn), lambda i, j: (0, j))],
        out_specs=pl.BlockSpec((tm, tn), lambda i, j: (i, j)),
        out_shape=jax.ShapeDtypeStruct((m, n), out_dtype),
        compiler_params=_cparams("parallel", "parallel"),
        name=name,
    )(x, w, b.reshape(1, n))


def _hgrn_kernel(*refs, has_state, want_state, n_aliased):
    q_ref, v_ref, g_ref, zf_ref, zb_ref, lbp_ref, gain_ref = refs[:7]
    pos = 7
    s0_ref = None
    if has_state:
        s0_ref = refs[pos]
        pos += 1
    pos += n_aliased
    y_ref = refs[pos]
    pos += 1
    st_ref = None
    if want_state:
        st_ref = refs[pos]
        pos += 1
    b_ref, kk_ref, o_ref = refs[pos:]

    T = q_ref.shape[0]
    nsc = T // SC_A

    row = lax.broadcasted_iota(jnp.int32, (SC_A, SC_A), 0)
    col = lax.broadcasted_iota(jnp.int32, (SC_A, SC_A), 1)
    tris = tuple(jnp.where(m, 1.0, 0.0).astype(bf16) for m in (row >= col, row <= col))
    nblk = SC_A // CH_A
    half = SC_A // 2
    r_sq = lax.broadcasted_iota(jnp.int32, (half, half), 0)
    c_sq = lax.broadcasted_iota(jnp.int32, (half, half), 1)
    r_w = lax.broadcasted_iota(jnp.int32, (half, SC_A), 0)
    c_w = lax.broadcasted_iota(jnp.int32, (half, SC_A), 1)
    masks_lo = (r_sq >= c_sq, r_w <= c_w)
    masks_hi = (r_w + half >= c_w, r_sq <= c_sq)

    for d, z_ref in enumerate((zf_ref, zb_ref)):
        z = z_ref[...]
        lb = lbp_ref[d:d + 1, :]
        one_m_lb = lbp_ref[2 + d:3 + d, :]
        e = jnp.exp(-jnp.abs(z))
        inv = 1.0 / (1.0 + e)
        sig = jnp.where(z >= 0.0, 1.0, e) * inv
        kk = one_m_lb * (jnp.where(z >= 0.0, e, 1.0) * inv)
        lf2 = jnp.maximum(jnp.log(lb + one_m_lb * sig) * LOG2E, LOG2_F_FLOOR)
        for sc in range(nsc):
            rows = slice(sc * SC_A, (sc + 1) * SC_A)
            b_ref[d, rows, :] = _tri_matmul(tris[d], lf2[rows])
        kk_ref[d] = kk

    if has_state:
        st0 = tuple(s0_ref[d, hh].T for hh in range(HPS) for d in range(2))
    else:
        st0 = tuple(jnp.zeros((DH, DH), f32) for _ in range(2 * HPS))

    def body(c, carry):
        new = []
        for hd in range(2 * HPS):
            hh, d = divmod(hd, 2)
            ls = slice(hh * DH, (hh + 1) * DH)
            cidx = c if d == 0 else nsc - 1 - c
            r = pl.ds(pl.multiple_of(cidx * SC_A, SC_A), SC_A)
            st = carry[hd]
            q = q_ref[r, ls].astype(f32)
            v = v_ref[r, ls]
            kk = kk_ref[d, r, ls]
            b = b_ref[d, r, ls]
            b_end = b[SC_A - 1:SC_A, :] if d == 0 else b[0:1, :]
            att_rows = ([], [])
            for i in range(nblk):
                mid = i * CH_A + (HALF_A - 1 if d == 0 else HALF_A)
                r_i = b[mid:mid + 1, :]
                blk = slice(i * CH_A, (i + 1) * CH_A)
                qi = (q[blk] * jnp.exp2(b[blk] - r_i)).astype(bf16)
                if d == 0 and i < nblk // 2:
                    ksl = slice(0, half)
                elif d == 1 and i >= nblk // 2:
                    ksl = slice(half, SC_A)
                else:
                    ksl = slice(0, SC_A)
                ks = (kk[ksl] * jnp.exp2(r_i - b[ksl])).astype(bf16)
                att_rows[i >= nblk // 2].append(lax.dot_general(qi, ks, NT, preferred_element_type=f32))
            a_lo = jnp.where(masks_lo[d], jnp.concatenate(att_rows[0], axis=0), 0.0).astype(bf16)
            a_hi = jnp.where(masks_hi[d], jnp.concatenate(att_rows[1], axis=0), 0.0).astype(bf16)
            o_lo = jnp.dot(a_lo, v[:half] if d == 0 else v, preferred_element_type=f32)
            o_hi = jnp.dot(a_hi, v if d == 0 else v[half:], preferred_element_type=f32)
            qb = (q * jnp.exp2(b)).astype(bf16)
            o_ref[d, r, ls] = (jnp.concatenate([o_lo, o_hi], axis=0)
                               + lax.dot_general(qb, st.astype(bf16), NT, preferred_element_type=f32))
            kl = (kk * jnp.exp2(b_end - b)).astype(bf16)
            new.append(jnp.exp2(b_end) * st + lax.dot_general(v, kl, TN, preferred_element_type=f32))
        return tuple(new)

    st_fin = lax.fori_loop(0, nsc, body, st0, unroll=2)

    for hh in range(HPS):
        ls = slice(hh * DH, (hh + 1) * DH)
        o = o_ref[0, :, ls] + o_ref[1, :, ls]
        ms = jnp.mean(o * o, axis=-1, keepdims=True)
        g = g_ref[:, ls].astype(f32)
        y_ref[:, ls] = (o * lax.rsqrt(ms + 1e-6) * gain_ref[...] * (g * _sigmoid(g))).astype(y_ref.dtype)
        if want_state:
            for d in range(2):
                st_ref[d, hh] = st_fin[2 * hh + d].T


def hgrn_scan(qig, zz, lbp, gain, s0, y_buf, state_buf, layer, *, n_seq, T, row_blk0, name):
    has_state = s0 is not None
    want_state = layer is not None

    wh = HPS * DH

    def col(off):
        return lambda b, h: (row_blk0 + b, off // HPS + h)

    in_specs = [pl.BlockSpec((T, wh), col(0)), pl.BlockSpec((T, wh), col(NH)),
                pl.BlockSpec((T, wh), col(2 * NH)),
                pl.BlockSpec((T, wh), col(0)), pl.BlockSpec((T, wh), col(NH)),
                pl.BlockSpec((8, wh), lambda b, h: (0, h)),
                pl.BlockSpec((1, DH), lambda b, h: (0, 0))]
    args = [qig, qig, qig, zz, zz, lbp, gain]
    if has_state:
        in_specs.append(pl.BlockSpec((None, 2, HPS, DH, DH), lambda b, h: (b, 0, h, 0, 0)))
        args.append(s0)
    out_specs = [pl.BlockSpec((T, wh), col(0))]
    out_shape = [jax.ShapeDtypeStruct((qig.shape[0], D), bf16)]
    aliases = {}
    if y_buf is not None:
        in_specs.append(pl.BlockSpec(memory_space=pl.ANY))
        args.append(y_buf)
        aliases[len(args) - 1] = 0
    if want_state:
        out_specs.append(pl.BlockSpec((None, None, 2, HPS, DH, DH),
                                      lambda b, h: (b, layer, 0, h, 0, 0)))
        out_shape.append(jax.ShapeDtypeStruct((n_seq, DEPTH, 2, NH, DH, DH), f32))
        if state_buf is not None:
            in_specs.append(pl.BlockSpec(memory_space=pl.ANY))
            args.append(state_buf)
            aliases[len(args) - 1] = 1
    scratch = [pltpu.VMEM((2, T, wh), f32)] * 3
    return pl.pallas_call(
        functools.partial(_hgrn_kernel, has_state=has_state, want_state=want_state,
                          n_aliased=len(aliases)),
        grid=(n_seq, NH // HPS),
        in_specs=in_specs, out_specs=out_specs, out_shape=out_shape,
        scratch_shapes=scratch,
        input_output_aliases=aliases,
        compiler_params=_cparams("parallel", "parallel"),
        name=name,
    )(*args)


def _split3(x):
    x0 = x.astype(bf16)
    r = x - x0.astype(f32)
    x1 = r.astype(bf16)
    x2 = (r - x1.astype(f32)).astype(bf16)
    return x0, x1, x2


def _tri_matmul(tri, x):
    return sum(jnp.dot(tri, p, preferred_element_type=f32) for p in _split3(x))


def _matmul_tri(x, tri):
    return sum(jnp.dot(p, tri, preferred_element_type=f32) for p in _split3(x))


def _mlstm_kernel(*refs, grid_rows, has_state, want_state, n_aliased):
    x_ref, v_ref, og_ref, gc_ref, gr_ref, cw_ref, cb_ref, wq_ref, wk_ref, fbr_ref, fbc_ref, gain_ref = refs[:12]
    pos = 12
    if has_state:
        c0_ref, m0_ref = refs[pos:pos + 2]
        pos += 2
    pos += n_aliased
    y_ref = refs[pos]
    pos += 1
    if want_state:
        ct_ref, nt_ref, mt_ref = refs[pos:pos + 3]

    head0 = pl.program_id(1) * HPS
    T = x_ref.shape[0]
    L = min(CH_B, T)
    nchunk = T // L

    t_idx = lax.broadcasted_iota(jnp.int32, (T, DH), 0)
    lane = lax.broadcasted_iota(jnp.int32, (T, DH), 1)
    ones_col = jnp.where(lane == 0, 1.0, 0.0)
    if grid_rows is None:
        taps = [(0, dc) for dc in (-1, 0, 1)]
    else:
        taps = [(dr, dc) for dr in (-1, 0, 1) for dc in (-1, 0, 1)]
        c_idx = t_idx % GRID_W
    qs, ks, vaugs = [], [], []
    for hh in range(HPS):
        ls = slice(hh * DH, (hh + 1) * DH)
        x = x_ref[:, ls].astype(f32)
        acc = jnp.zeros((T, DH), f32) + cb_ref[:, ls]
        for dr, dc in taps:
            off = dr * GRID_W + dc
            tap = (dr + 1) * 3 + (dc + 1)
            w = cw_ref[tap:tap + 1, ls]
            xs = x if off == 0 else pltpu.roll(x, (-off) % T, axis=0)
            ok = (t_idx + off >= 0) & (t_idx + off < T)
            if grid_rows is not None:
                ok = ok & (c_idx + dc >= 0) & (c_idx + dc < GRID_W)
            acc = acc + jnp.where(ok, xs, 0.0) * w
        ca = (acc * _sigmoid(acc)).astype(bf16)
        qs.append(jnp.dot(ca, wq_ref[hh], preferred_element_type=f32).astype(bf16))
        ks.append((jnp.dot(ca, wk_ref[hh], preferred_element_type=f32) * (DH ** -0.5)).astype(bf16))
        vaugs.append(jnp.concatenate([v_ref[:, ls].astype(f32), ones_col], axis=1))

    gate_lane = lax.broadcasted_iota(jnp.int32, (T, DH), 1) // NH
    is_f = (gate_lane == 1) | (gate_lane == 3)
    gc = gc_ref[...] + fbr_ref[...]
    gc = jnp.where(is_f, _log_sigmoid(gc), gc)

    def pick_lane(a, ln):
        sel = lax.broadcasted_iota(jnp.int32, a.shape, 1) == ln
        return jnp.sum(jnp.where(sel, a, 0.0), axis=1, keepdims=True)

    def pick_row(a, rw):
        sel = lax.broadcasted_iota(jnp.int32, a.shape, 0) == rw
        return jnp.sum(jnp.where(sel, a, 0.0), axis=0, keepdims=True)

    r_i = lax.broadcasted_iota(jnp.int32, (L, L), 0)
    c_i = lax.broadcasted_iota(jnp.int32, (L, L), 1)
    lower = r_i >= c_i
    upper = r_i <= c_i
    tri_lo = jnp.where(lower, 1.0, 0.0).astype(bf16)
    tri_up = jnp.where(upper, 1.0, 0.0).astype(bf16)

    h_parts = [[[None] * nchunk for _ in range(2)] for _ in range(HPS)]
    finals = [[None, None] for _ in range(HPS)]
    for d in range(2):
        if has_state:
            carry = [(c0_ref[d, hh], m0_ref[d, hh]) for hh in range(HPS)]
        else:
            carry = [(jnp.zeros((DH, 2 * DH), f32), jnp.zeros((1, 1), f32)) for _ in range(HPS)]
        order = range(nchunk) if d == 0 else range(nchunk - 1, -1, -1)
        mask = lower if d == 0 else upper
        tri_c = tri_lo if d == 0 else tri_up
        tri_r = tri_up if d == 0 else tri_lo
        for cidx in order:
            rs = slice(cidx * L, (cidx + 1) * L)
            gcc = gc[rs]
            F_c_all = _tri_matmul(tri_c, gcc)
            lf_r8 = _log_sigmoid(gr_ref[(2 * d + 1) * NH:(2 * d + 2) * NH, rs] + fbc_ref[d])
            F_r_all = _matmul_tri(lf_r8, tri_r)
            ig_r_all = gr_ref[2 * d * NH:(2 * d + 1) * NH, rs]
            for hh in range(HPS):
                head = head0 + hh
                caug, m_prev = carry[hh]
                ig_c = pick_lane(gcc, 2 * d * NH + head)
                F_c = pick_lane(F_c_all, (2 * d + 1) * NH + head)
                ig_r = pick_row(ig_r_all, head)
                F_r = pick_row(F_r_all, head)
                F_end = F_c[L - 1:L, :] if d == 0 else F_c[0:1, :]

                logd = jnp.where(mask, F_c - F_r + ig_r, -jnp.inf)
                log_inter = F_c + m_prev
                m_q = jnp.maximum(log_inter, jnp.max(logd, axis=-1, keepdims=True))
                w = jnp.exp(logd - m_q)
                a = jnp.exp(log_inter - m_q)
                qc, kc = qs[hh][rs], ks[hh][rs]
                va = vaugs[hh][rs]
                s = lax.dot_general(qc, kc, NT, preferred_element_type=f32) * w
                nd = jnp.dot(s.astype(bf16), va.astype(bf16), preferred_element_type=f32)
                zero_mem = (not has_state) and cidx == order[0]
                if not zero_mem:
                    nd = nd + a * jnp.dot(qc, caug.astype(bf16), preferred_element_type=f32)
                num = nd[:, :DH]
                den = nd[:, DH:DH + 1]
                h_parts[hh][d][cidx] = num / jnp.maximum(jnp.abs(den), jnp.exp(-m_q))
                log_w_c = F_end - F_c + ig_c
                log_w_r = F_end - F_r + ig_r
                m_new = jnp.maximum(F_end + m_prev, jnp.max(log_w_r, axis=-1, keepdims=True))
                ws_c = jnp.exp(log_w_c - m_new)
                d0 = jnp.exp(F_end + m_prev - m_new)
                upd = lax.dot_general(kc, (ws_c * va).astype(bf16), TN, preferred_element_type=f32)
                carry[hh] = (upd if zero_mem else d0 * caug + upd, m_new)
        for hh in range(HPS):
            finals[hh][d] = carry[hh]

    for hh in range(HPS):
        ls = slice(hh * DH, (hh + 1) * DH)
        hsum = jnp.concatenate([h_parts[hh][0][c] + h_parts[hh][1][c] for c in range(nchunk)], axis=0)
        mu = jnp.mean(hsum, axis=-1, keepdims=True)
        dv = hsum - mu
        var = jnp.mean(dv * dv, axis=-1, keepdims=True)
        og = og_ref[:, ls].astype(f32)
        y_ref[:, ls] = (dv * lax.rsqrt(var + 1e-6) * gain_ref[...] * _sigmoid(og)).astype(y_ref.dtype)
        if want_state:
            for d in range(2):
                caug, m_fin = finals[hh][d]
                ct_ref[d, hh] = caug[:, :DH]
                nt_ref[d, hh] = caug[:, DH:2 * DH].T[0:1, :]
                mt_ref[d, hh] = jnp.broadcast_to(m_fin, (1, DH))


def mlstm_scan(bb, gates, gates_t, conv_w9, conv_b, wq, wk, fb_row, fb_col, gain, state, y_buf, c_buf,
               layer, *, n_seq, T, row_blk0, grid_rows, name):
    has_state = state is not None
    want_state = layer is not None

    wh = HPS * DH

    def col(off):
        return lambda b, h: (row_blk0 + b, off // HPS + h)

    in_specs = [pl.BlockSpec((T, wh), col(0)), pl.BlockSpec((T, wh), col(NH)),
                pl.BlockSpec((T, wh), col(2 * NH)),
                pl.BlockSpec((T, DH), lambda b, h: (row_blk0 + b, 0)),
                pl.BlockSpec((DH, T), lambda b, h: (0, row_blk0 + b)),
                pl.BlockSpec((9, wh), lambda b, h: (0, h)),
                pl.BlockSpec((1, wh), lambda b, h: (0, h)),
                pl.BlockSpec((HPS, DH, DH), lambda b, h: (h, 0, 0)),
                pl.BlockSpec((HPS, DH, DH), lambda b, h: (h, 0, 0)),
                pl.BlockSpec((1, DH), lambda b, h: (0, 0)),
                pl.BlockSpec((2, NH, 1), lambda b, h: (0, 0, 0)),
                pl.BlockSpec((1, DH), lambda b, h: (0, 0))]
    args = [bb, bb, bb, gates, gates_t, conv_w9, conv_b, wq, wk, fb_row, fb_col, gain]
    if has_state:
        c0aug, m0 = state
        in_specs += [pl.BlockSpec((None, 2, HPS, DH, 2 * DH), lambda b, h: (b, 0, h, 0, 0)),
                     pl.BlockSpec((None, 2, HPS, 1, 1), lambda b, h: (b, 0, h, 0, 0))]
        args += [c0aug, m0]
    out_specs = [pl.BlockSpec((T, wh), col(0))]
    out_shape = [jax.ShapeDtypeStruct((bb.shape[0], D), bf16)]
    aliases = {}
    if y_buf is not None:
        in_specs.append(pl.BlockSpec(memory_space=pl.ANY))
        args.append(y_buf)
        aliases[len(args) - 1] = 0
    if want_state:
        out_specs += [pl.BlockSpec((None, None, 2, HPS, DH, DH), lambda b, h: (b, layer, 0, h, 0, 0)),
                      pl.BlockSpec((None, 2, HPS, 1, DH), lambda b, h: (b, 0, h, 0, 0)),
                      pl.BlockSpec((None, 2, HPS, 1, DH), lambda b, h: (b, 0, h, 0, 0))]
        out_shape += [jax.ShapeDtypeStruct((n_seq, DEPTH, 2, NH, DH, DH), f32),
                      jax.ShapeDtypeStruct((n_seq, 2, NH, 1, DH), f32),
                      jax.ShapeDtypeStruct((n_seq, 2, NH, 1, DH), f32)]
        if c_buf is not None:
            in_specs.append(pl.BlockSpec(memory_space=pl.ANY))
            args.append(c_buf)
            aliases[len(args) - 1] = 1
    return pl.pallas_call(
        functools.partial(_mlstm_kernel, grid_rows=grid_rows, has_state=has_state,
                          want_state=want_state, n_aliased=len(aliases)),
        grid=(n_seq, NH // HPS),
        in_specs=in_specs, out_specs=out_specs, out_shape=out_shape,
        input_output_aliases=aliases,
        compiler_params=_cparams("parallel", "parallel"),
        name=name,
    )(*args)


def _merge_kernel(ya_ref, yb_ref, gm_ref, x_ref, mod_ref, wa_ref, wb_ref, wo_ref, lg_ref, lb_ref,
                  x1_ref, hh_ref):
    pa = jnp.dot(ya_ref[...], wa_ref[...], preferred_element_type=f32)
    pb = jnp.dot(yb_ref[...], wb_ref[...], preferred_element_type=f32)
    gma = gm_ref[:, :D].astype(f32)
    gmb = gm_ref[:, D:].astype(f32)
    merged = (_sigmoid(gma) * pa + _sigmoid(gmb) * pb).astype(bf16)
    y = jnp.dot(merged, wo_ref[...], preferred_element_type=f32)
    x1 = _layer_norm(ALPHA * x_ref[...] + mod_ref[2:3, :] * y, lg_ref[...], lb_ref[...])
    x1_ref[...] = x1
    hh_ref[...] = (x1 * (1.0 + mod_ref[4:5, :]) + mod_ref[3:4, :]).astype(hh_ref.dtype)


def merge(ya, yb, gm, x, mod_l, wa, wb, wo, lg, lb):
    n = x.shape[0]
    tm = 512
    grp = _group_of_tile(tm)
    row = lambda i: (i, 0)
    const = lambda i: (0, 0)
    return pl.pallas_call(
        _merge_kernel,
        grid=(n // tm,),
        in_specs=[pl.BlockSpec((tm, D), row), pl.BlockSpec((tm, D), row),
                  pl.BlockSpec((tm, 2 * D), row), pl.BlockSpec((tm, D), row),
                  pl.BlockSpec((None, 8, D), lambda i: (grp(i), 0, 0)),
                  pl.BlockSpec((D, D), const), pl.BlockSpec((D, D), const), pl.BlockSpec((D, D), const),
                  pl.BlockSpec((1, D), const), pl.BlockSpec((1, D), const)],
        out_specs=[pl.BlockSpec((tm, D), row), pl.BlockSpec((tm, D), row)],
        out_shape=[jax.ShapeDtypeStruct((n, D), f32), jax.ShapeDtypeStruct((n, D), bf16)],
        compiler_params=_cparams("parallel"),
        name="merge",
    )(ya, yb, gm, x, mod_l, wa, wb, wo, lg.reshape(1, D), lb.reshape(1, D))


def _post_ffn(f, x1, mod, mod_next, lg, lb, x2_ref, hn_ref):
    x2 = _layer_norm(ALPHA * x1 + mod[5:6, :] * f, lg, lb)
    x2_ref[...] = x2
    hn_ref[...] = (x2 * (1.0 + mod_next[1:2, :]) + mod_next[0:1, :]).astype(hn_ref.dtype)


def _ffn_kernel(hh_ref, w1_ref, w3_ref, w2_ref, x1_ref, mod_ref, modn_ref, lg_ref, lb_ref,
                x2_ref, hn_ref, acc_ref):
    j = pl.program_id(1)
    hh = hh_ref[...]
    a = jnp.dot(hh, w1_ref[...], preferred_element_type=f32)
    b = jnp.dot(hh, w3_ref[...], preferred_element_type=f32)
    part = jnp.dot((a * _sigmoid(a) * b).astype(bf16), w2_ref[...], preferred_element_type=f32)

    @pl.when(j == 0)
    def _():
        acc_ref[...] = part

    @pl.when(j != 0)
    def _():
        acc_ref[...] += part

    @pl.when(j == pl.num_programs(1) - 1)
    def _():
        _post_ffn(acc_ref[...], x1_ref[...], mod_ref, modn_ref, lg_ref[...], lb_ref[...],
                  x2_ref, hn_ref)


def ffn(hh, w1, w3, w2, x1, mod_l, mod_next, lg, lb):
    n = hh.shape[0]
    tm, tf = 512, 1408
    grp = _group_of_tile(tm)
    row = lambda i, j: (i, 0)
    const = lambda i, j: (0, 0)
    return pl.pallas_call(
        _ffn_kernel,
        grid=(n // tm, D_FF // tf),
        in_specs=[pl.BlockSpec((tm, D), row),
                  pl.BlockSpec((D, tf), lambda i, j: (0, j)),
                  pl.BlockSpec((D, tf), lambda i, j: (0, j)),
                  pl.BlockSpec((tf, D), lambda i, j: (j, 0)),
                  pl.BlockSpec((tm, D), row),
                  pl.BlockSpec((None, 8, D), lambda i, j: (grp(i), 0, 0)),
                  pl.BlockSpec((None, 8, D), lambda i, j: (grp(i), 0, 0)),
                  pl.BlockSpec((1, D), const), pl.BlockSpec((1, D), const)],
        out_specs=[pl.BlockSpec((tm, D), row), pl.BlockSpec((tm, D), row)],
        out_shape=[jax.ShapeDtypeStruct((n, D), f32), jax.ShapeDtypeStruct((n, D), bf16)],
        scratch_shapes=[pltpu.VMEM((tm, D), f32)],
        compiler_params=_cparams("parallel", "arbitrary"),
        name="ffn",
    )(hh, w1, w3, w2, x1, mod_l, mod_next, lg.reshape(1, D), lb.reshape(1, D))


def _router_kernel(hh_ref, wr_ref, br_ref, o_ref, cnt_ref, run_ref):
    logits = jnp.dot(hh_ref[...].astype(f32), wr_ref[...], preferred_element_type=f32,
                     precision=lax.Precision.HIGHEST) + br_ref[...]
    lane = lax.broadcasted_iota(jnp.int32, logits.shape, 1).astype(f32)
    m1 = jnp.max(logits, axis=-1, keepdims=True)
    i1 = jnp.min(jnp.where(logits == m1, lane, float(DH)), axis=-1, keepdims=True)
    rest = jnp.where(lane == i1, -jnp.inf, logits)
    m2 = jnp.max(rest, axis=-1, keepdims=True)
    i2 = jnp.min(jnp.where(rest == m2, lane, float(DH)), axis=-1, keepdims=True)
    e21 = jnp.exp(m2 - m1)
    p1 = 1.0 / (1.0 + e21)
    p2 = e21 * p1

    @pl.when(pl.program_id(0) == 0)
    def _():
        run_ref[...] = jnp.zeros_like(run_ref)

    tm = logits.shape[0]
    r_i = lax.broadcasted_iota(jnp.int32, (tm, tm), 0)
    c_i = lax.broadcasted_iota(jnp.int32, (tm, tm), 1)
    earlier = jnp.where(c_i < r_i, 1.0, 0.0).astype(bf16)
    oh1 = jnp.where(lane == i1, 1.0, 0.0)
    oh2 = jnp.where(lane == i2, 1.0, 0.0)
    pre1 = jnp.dot(earlier, oh1.astype(bf16), preferred_element_type=f32)
    pre2 = jnp.dot(earlier, oh2.astype(bf16), preferred_element_type=f32)
    cnt1 = jnp.sum(oh1, axis=0, keepdims=True)
    cnt2 = jnp.sum(oh2, axis=0, keepdims=True)
    run = run_ref[...]
    rank1 = jnp.sum(oh1 * (run + pre1), axis=-1, keepdims=True)
    rank2 = jnp.sum(oh2 * (run + cnt1 + pre2), axis=-1, keepdims=True)
    run_ref[...] = run + cnt1 + cnt2
    cnt_ref[...] = jnp.broadcast_to(run + cnt1 + cnt2, cnt_ref.shape)
    o_ref[...] = (jnp.where(lane == 0.0, i1, 0.0) + jnp.where(lane == 1.0, i2, 0.0)
                  + jnp.where(lane == 2.0, p1, 0.0) + jnp.where(lane == 3.0, p2, 0.0)
                  + jnp.where(lane == 4.0, rank1, 0.0) + jnp.where(lane == 5.0, rank2, 0.0))


def router(hh, wr_pad, br_pad):
    n = hh.shape[0]
    tm = 1024
    return pl.pallas_call(
        _router_kernel,
        grid=(n // tm,),
        in_specs=[pl.BlockSpec((tm, D), lambda i: (i, 0)),
                  pl.BlockSpec((D, DH), lambda i: (0, 0)),
                  pl.BlockSpec((1, DH), lambda i: (0, 0))],
        out_specs=[pl.BlockSpec((tm, DH), lambda i: (i, 0)),
                   pl.BlockSpec((8, DH), lambda i: (0, 0))],
        out_shape=[jax.ShapeDtypeStruct((n, DH), f32), jax.ShapeDtypeStruct((8, DH), f32)],
        scratch_shapes=[pltpu.VMEM((1, DH), f32)],
        compiler_params=_cparams("arbitrary"),
        name="router",
    )(hh, wr_pad, br_pad)


def _moe_up_kernel(te_ref, x_ref, w1_ref, w3_ref, a_ref):
    del te_ref
    x = x_ref[...]
    a = jnp.dot(x, w1_ref[...].astype(bf16), preferred_element_type=f32)
    b = jnp.dot(x, w3_ref[...].astype(bf16), preferred_element_type=f32)
    a_ref[...] = (a * _sigmoid(a) * b).astype(a_ref.dtype)


def _moe_down_kernel(te_ref, a_ref, w2_ref, y_ref):
    del te_ref
    y_ref[...] = jnp.dot(a_ref[...], w2_ref[...].astype(bf16),
                         preferred_element_type=f32).astype(y_ref.dtype)


def moe_experts(tile_expert, xs, w1, w3, w2, lyr):
    r = xs.shape[0]
    nt = r // MOE_TM
    tf = 1408
    up = pl.pallas_call(
        _moe_up_kernel,
        grid_spec=pltpu.PrefetchScalarGridSpec(
            num_scalar_prefetch=1,
            grid=(D_FF // tf, nt),
            in_specs=[pl.BlockSpec((MOE_TM, D), lambda j, i, te: (i, 0)),
                      pl.BlockSpec((None, None, D, tf), lambda j, i, te: (lyr, te[i], 0, j)),
                      pl.BlockSpec((None, None, D, tf), lambda j, i, te: (lyr, te[i], 0, j))],
            out_specs=pl.BlockSpec((MOE_TM, tf), lambda j, i, te: (i, j))),
        out_shape=jax.ShapeDtypeStruct((r, D_FF), bf16),
        compiler_params=_cparams("arbitrary", "arbitrary"),
        name="moe_up",
    )(tile_expert, xs, w1, w3)
    return pl.pallas_call(
        _moe_down_kernel,
        grid_spec=pltpu.PrefetchScalarGridSpec(
            num_scalar_prefetch=1,
            grid=(nt,),
            in_specs=[pl.BlockSpec((MOE_TM, D_FF), lambda i, te: (i, 0)),
                      pl.BlockSpec((None, None, D_FF, D), lambda i, te: (lyr, te[i], 0, 0))],
            out_specs=pl.BlockSpec((MOE_TM, D), lambda i, te: (i, 0))),
        out_shape=jax.ShapeDtypeStruct((r, D), bf16),
        compiler_params=_cparams("arbitrary"),
        name="moe_down",
    )(tile_expert, up, w2)


def _combine_kernel(ya_ref, yb_ref, rt_ref, x1_ref, mod_ref, modn_ref, lg_ref, lb_ref, x2_ref, hn_ref):
    f = rt_ref[:, 2:3] * ya_ref[...].astype(f32) + rt_ref[:, 3:4] * yb_ref[...].astype(f32)
    _post_ffn(f, x1_ref[...], mod_ref, modn_ref, lg_ref[...], lb_ref[...], x2_ref, hn_ref)


def combine(ya, yb, rt, x1, mod_l, mod_next, lg, lb):
    n = x1.shape[0]
    tm = 1024
    grp = _group_of_tile(tm)
    row = lambda i: (i, 0)
    const = lambda i: (0, 0)
    return pl.pallas_call(
        _combine_kernel,
        grid=(n // tm,),
        in_specs=[pl.BlockSpec((tm, D), row), pl.BlockSpec((tm, D), row),
                  pl.BlockSpec((tm, DH), row), pl.BlockSpec((tm, D), row),
                  pl.BlockSpec((None, 8, D), lambda i: (grp(i), 0, 0)),
                  pl.BlockSpec((None, 8, D), lambda i: (grp(i), 0, 0)),
                  pl.BlockSpec((1, D), const), pl.BlockSpec((1, D), const)],
        out_specs=[pl.BlockSpec((tm, D), row), pl.BlockSpec((tm, D), row)],
        out_shape=[jax.ShapeDtypeStruct((n, D), f32), jax.ShapeDtypeStruct((n, D), bf16)],
        compiler_params=_cparams("parallel"),
        name="combine",
    )(ya, yb, rt, x1, mod_l, mod_next, lg.reshape(1, D), lb.reshape(1, D))


def moe_dispatch(rt, counts, n):
    r_total = 2 * n + N_EXPERTS * MOE_TM
    nt = r_total // MOE_TM
    cnt = counts[0, :N_EXPERTS].astype(jnp.int32)
    padded = (cnt + MOE_TM - 1) // MOE_TM * MOE_TM
    ends = jnp.cumsum(padded)
    starts = ends - padded
    e_idx = rt[:, 0:2].astype(jnp.int32)
    rank = rt[:, 4:6].astype(jnp.int32)
    onehot = e_idx[..., None] == jnp.arange(N_EXPERTS)[None, None, :]
    dest = rank + jnp.sum(jnp.where(onehot, starts[None, None, :], 0), axis=-1)
    tok = jnp.broadcast_to(jnp.arange(n, dtype=jnp.int32)[:, None], (n, 2))
    src_tok = jnp.zeros((r_total,), jnp.int32).at[dest.reshape(-1)].set(
        tok.reshape(-1), unique_indices=True, mode="promise_in_bounds")
    tile_start = jnp.arange(nt, dtype=jnp.int32) * MOE_TM
    tile_expert = jnp.minimum(jnp.sum((tile_start[:, None] >= ends[None, :]).astype(jnp.int32), axis=1),
                              N_EXPERTS - 1)
    return src_tok, dest, tile_expert


def _take_rows(a, idx):
    return a.at[idx].get(mode="promise_in_bounds")


def kernel(x_prompt, x_sample, c, state_hgrn, state_mlstm_C, state_mlstm_n, state_mlstm_m, c_ctx,
           w_ada, b_ada, w_in, b_in, hgrn_lb_raw, hgrn_norm_g, conv_w, conv_b, w_mq, w_mk,
           mlstm_fbias, mlstm_norm_g, w_branch_a, w_branch_b, w_out, ln1_g, ln1_b, ln2_g, ln2_b,
           ffn_w1, ffn_w3, ffn_w2, moe_router_w, moe_router_b, moe_w1, moe_w3, moe_w2):
    nb, seq, _ = x_prompt.shape
    db, dseq, _ = x_sample.shape
    n_ctx, n_lat = nb * seq, db * dseq
    n = n_ctx + n_lat
    assert n_ctx == N_CTX and dseq == SEQ_GROUP and db == 2

    lb = jnp.cumsum(jax.nn.softmax(hgrn_lb_raw.astype(f32), axis=1), axis=1)
    lb = lb - lb[:, :1]

    cond8 = jnp.concatenate([c_ctx[None, :], c, jnp.zeros((5, D), f32)], axis=0)
    mod = modulation(cond8, w_ada, b_ada)[:, :3, :].reshape(DEPTH, 3, 6, D)
    mod = jnp.pad(mod, ((0, 0), (0, 0), (0, 2), (0, 0)))

    x = jnp.concatenate([x_prompt.reshape(n_ctx, D), x_sample.reshape(n_lat, D)], axis=0)
    h = modulate(x, mod[0])

    sizes = (D,) * 8 + (4 * NH, D, D)
    offs = [0]
    for s in sizes:
        offs.append(offs[-1] + s)

    def cols(w, idxs):
        return jnp.concatenate([w[..., offs[i]:offs[i + 1]] for i in idxs], axis=-1)

    out_n, out_m = [], []
    buf_a = buf_c = None
    for l in range(DEPTH):
        wl, bl = w_in[l], b_in[l]
        qig = linear(h, cols(wl, (0, 3, 4)).astype(bf16), cols(bl, (0, 3, 4)), bf16, "proj_qig")
        zz = linear(h, cols(wl, (1, 2)).astype(bf16), cols(bl, (1, 2)), f32, "proj_z")
        bb = linear(h, cols(wl, (5, 6, 7)).astype(bf16), cols(bl, (5, 6, 7)), bf16, "proj_b")
        gm = linear(h, cols(wl, (9, 10)).astype(bf16), cols(bl, (9, 10)), bf16, "proj_gm")
        w_g = jnp.pad(cols(wl, (8,)), ((0, 0), (0, DH - 4 * NH))).astype(bf16)
        gates = linear(h, w_g, jnp.pad(cols(bl, (8,)), (0, DH - 4 * NH)), f32, "proj_gates")
        gates_t = gates.T

        lbl = lb[:, l]
        lbp = jnp.concatenate([lbl, 1.0 - lbl, jnp.zeros((4, D), f32)], axis=0)
        gain_a = hgrn_norm_g[l].reshape(1, DH)
        ya, buf_a = hgrn_scan(qig, zz, lbp, gain_a, None, None, buf_a, l, n_seq=nb, T=seq,
                              row_blk0=0, name="hgrn_ctx")
        (ya,) = hgrn_scan(qig, zz, lbp, gain_a, state_hgrn[:, l], ya, None, None, n_seq=db, T=dseq,
                          row_blk0=n_ctx // dseq, name="hgrn_lat")

        cw9 = conv_w[l].reshape(9, D)
        cb = conv_b[l].reshape(1, D)
        wq = w_mq[l].astype(bf16)
        wk = w_mk[l].astype(bf16)
        fb = mlstm_fbias[l]
        zero8 = jnp.zeros((NH,), f32)
        fb_row = jnp.concatenate([zero8, fb[0], zero8, fb[1], jnp.zeros((DH - 4 * NH,), f32)]).reshape(1, DH)
        fb_col = fb.reshape(2, NH, 1)
        gain_b = mlstm_norm_g[l].reshape(1, DH)
        yb, buf_c, s_n, s_m = mlstm_scan(bb, gates, gates_t, cw9, cb, wq, wk, fb_row, fb_col, gain_b,
                                         None, None, buf_c, l, n_seq=nb, T=seq, row_blk0=0,
                                         grid_rows=None, name="mlstm_ctx")
        c0aug = jnp.concatenate([state_mlstm_C[:, l], state_mlstm_n[:, l][..., None],
                                 jnp.zeros((db, 2, NH, DH, DH - 1), f32)], axis=-1)
        m0 = state_mlstm_m[:, l].reshape(db, 2, NH, 1, 1)
        (yb,) = mlstm_scan(bb, gates, gates_t, cw9, cb, wq, wk, fb_row, fb_col, gain_b,
                           (c0aug, m0), yb, None, None, n_seq=db, T=dseq, row_blk0=n_ctx // dseq,
                           grid_rows=dseq // GRID_W, name="mlstm_lat")

        x1, hh = merge(ya, yb, gm, x, mod[l], w_branch_a[l].astype(bf16), w_branch_b[l].astype(bf16),
                       w_out[l].astype(bf16), ln1_g[l], ln1_b[l])

        mod_next = mod[(l + 1) % DEPTH]
        j = l // 2
        if l % 2 == 0:
            x, h = ffn(hh, ffn_w1[j].astype(bf16), ffn_w3[j].astype(bf16), ffn_w2[j].astype(bf16),
                       x1, mod[l], mod_next, ln2_g[l], ln2_b[l])
        else:
            wr = jnp.pad(moe_router_w[j], ((0, 0), (0, DH - N_EXPERTS)))
            br = jnp.concatenate([moe_router_b[j], jnp.full((DH - N_EXPERTS,), -1e30, f32)]).reshape(1, DH)
            rt, counts = router(hh, wr, br)
            src_tok, dest, tile_expert = moe_dispatch(rt, counts, n)
            ys = moe_experts(tile_expert, _take_rows(hh, src_tok), moe_w1, moe_w3, moe_w2, j)
            x, h = combine(_take_rows(ys, dest[:, 0]), _take_rows(ys, dest[:, 1]), rt,
                           x1, mod[l], mod_next, ln2_g[l], ln2_b[l])

        out_n.append(s_n.reshape(nb, 2, NH, DH))
        out_m.append(s_m[..., 0, 0])

    y_prompt = x[:n_ctx].reshape(nb, seq, D)
    y_sample = x[n_ctx:].reshape(db, dseq, D)
    return (y_prompt, y_sample, buf_a, buf_c, jnp.stack(out_n, axis=1), jnp.stack(out_m, axis=1))
```

```python
import functools

import jax
import jax.numpy as jnp
from jax import lax
from jax.experimental import pallas as pl
from jax.experimental.pallas import tpu as pltpu

f32 = jnp.float32
bf16 = jnp.bfloat16

D = 1024
DEPTH = 4
GRID_W = 64
NH = 8
DH = 128
CH_A = 16
HALF_A = 8
SC_A = 128
HPS = 4
CH_B = 256
N_EXPERTS = 8
D_FF = 2816
ALPHA = (2 * DEPTH) ** 0.25
SEQ_GROUP = 1024
LOG2E = 1.4426950408889634
LOG2_F_FLOOR = -1e30
MOE_TM = 512

VMEM_LIMIT = 56 * 1024 * 1024
NT = (((1,), (1,)), ((), ()))
TN = (((0,), (0,)), ((), ()))


def _cparams(*sem):
    return pltpu.CompilerParams(dimension_semantics=sem, vmem_limit_bytes=VMEM_LIMIT)


def _sigmoid(x):
    return 1.0 / (1.0 + jnp.exp(-x))


def _log_sigmoid(t):
    return jnp.minimum(t, 0.0) - jnp.log1p(jnp.exp(-jnp.abs(t)))


def _layer_norm(r, g, b):
    mu = jnp.mean(r, axis=-1, keepdims=True)
    d = r - mu
    var = jnp.mean(d * d, axis=-1, keepdims=True)
    return d * lax.rsqrt(var + 1e-5) * g + b


def _mod_kernel(c_ref, w_ref, b_ref, o_ref):
    c = c_ref[...]
    act = (c * _sigmoid(c)).astype(bf16)
    o_ref[...] = jnp.dot(act, w_ref[...].astype(bf16), preferred_element_type=f32) + b_ref[...]


def modulation(cond8, w_ada, b_ada):
    L, _, n = w_ada.shape
    tn = 1536
    return pl.pallas_call(
        _mod_kernel,
        grid=(L, n // tn),
        in_specs=[pl.BlockSpec((8, D), lambda l, j: (0, 0)),
                  pl.BlockSpec((None, D, tn), lambda l, j: (l, 0, j)),
                  pl.BlockSpec((None, 1, tn), lambda l, j: (l, 0, j))],
        out_specs=pl.BlockSpec((None, 8, tn), lambda l, j: (l, 0, j)),
        out_shape=jax.ShapeDtypeStruct((L, 8, n), f32),
        compiler_params=_cparams("parallel", "parallel"),
        name="modulation",
    )(cond8, w_ada, b_ada.reshape(L, 1, n))


N_CTX = 32 * 256


def _group_of_tile(tm):
    def idx(i):
        return jnp.maximum(i * tm // SEQ_GROUP - (N_CTX // SEQ_GROUP - 1), 0)
    return idx


def _modulate_kernel(x_ref, mod_ref, h_ref):
    h_ref[...] = (x_ref[...] * (1.0 + mod_ref[1:2, :]) + mod_ref[0:1, :]).astype(h_ref.dtype)


def modulate(x, mod_l):
    n = x.shape[0]
    tm = 1024
    grp = _group_of_tile(tm)
    return pl.pallas_call(
        _modulate_kernel,
        grid=(n // tm,),
        in_specs=[pl.BlockSpec((tm, D), lambda i: (i, 0)),
                  pl.BlockSpec((None, 8, D), lambda i: (grp(i), 0, 0))],
        out_specs=pl.BlockSpec((tm, D), lambda i: (i, 0)),
        out_shape=jax.ShapeDtypeStruct((n, D), bf16),
        compiler_params=_cparams("parallel"),
        name="modulate",
    )(x, mod_l)


def _linear_kernel(x_ref, w_ref, b_ref, o_ref):
    acc = jnp.dot(x_ref[...], w_ref[...], preferred_element_type=f32)
    o_ref[...] = (acc + b_ref[...]).astype(o_ref.dtype)


def linear(x, w, b, out_dtype, name):
    m, k = x.shape
    n = w.shape[1]
    tm = 2048
    tn = 1024 if n % 1024 == 0 else n
    return pl.pallas_call(
        _linear_kernel,
        grid=(m // tm, n // tn),
        in_specs=[pl.BlockSpec((tm, k), lambda i, j: (i, 0)),
                  pl.BlockSpec((k, tn), lambda i, j: (0, j)),
                  pl.BlockSpec((1, tn), lambda i, j: (0, j))],
        out_specs=pl.BlockSpec((tm, tn), lambda i, j: (i, j)),
        out_shape=jax.ShapeDtypeStruct((m, n), out_dtype),
        compiler_params=_cparams("parallel", "parallel"),
        name=name,
    )(x, w, b.reshape(1, n))


def _hgrn_kernel(*refs, has_state, want_state, n_aliased):
    q_ref, v_ref, g_ref, zf_ref, zb_ref, lbp_ref, gain_ref = refs[:7]
    pos = 7
    s0_ref = None
    if has_state:
        s0_ref = refs[pos]
        pos += 1
    pos += n_aliased
    y_ref = refs[pos]
    pos += 1
    st_ref = None
    if want_state:
        st_ref = refs[pos]
        pos += 1
    b_ref, kk_ref, o_ref = refs[pos:]

    T = q_ref.shape[0]
    nsc = T // SC_A

    row = lax.broadcasted_iota(jnp.int32, (SC_A, SC_A), 0)
    col = lax.broadcasted_iota(jnp.int32, (SC_A, SC_A), 1)
    tris = tuple(jnp.where(m, 1.0, 0.0).astype(bf16) for m in (row >= col, row <= col))
    nblk = SC_A // CH_A
    half = SC_A // 2
    r_sq = lax.broadcasted_iota(jnp.int32, (half, half), 0)
    c_sq = lax.broadcasted_iota(jnp.int32, (half, half), 1)
    r_w = lax.broadcasted_iota(jnp.int32, (half, SC_A), 0)
    c_w = lax.broadcasted_iota(jnp.int32, (half, SC_A), 1)
    masks_lo = (r_sq >= c_sq, r_w <= c_w)
    masks_hi = (r_w + half >= c_w, r_sq <= c_sq)

    for d, z_ref in enumerate((zf_ref, zb_ref)):
        z = z_ref[...]
        lb = lbp_ref[d:d + 1, :]
        one_m_lb = lbp_ref[2 + d:3 + d, :]
        e = jnp.exp(-jnp.abs(z))
        inv = 1.0 / (1.0 + e)
        sig = jnp.where(z >= 0.0, 1.0, e) * inv
        kk = one_m_lb * (jnp.where(z >= 0.0, e, 1.0) * inv)
        lf2 = jnp.maximum(jnp.log(lb + one_m_lb * sig) * LOG2E, LOG2_F_FLOOR)
        for sc in range(nsc):
            rows = slice(sc * SC_A, (sc + 1) * SC_A)
            b_ref[d, rows, :] = _tri_matmul(tris[d], lf2[rows])
        kk_ref[d] = kk

    if has_state:
        st0 = tuple(s0_ref[d, hh].T for hh in range(HPS) for d in range(2))
    else:
        st0 = tuple(jnp.zeros((DH, DH), f32) for _ in range(2 * HPS))

    def body(c, carry):
        new = []
        for hd in range(2 * HPS):
            hh, d = divmod(hd, 2)
            ls = slice(hh * DH, (hh + 1) * DH)
            cidx = c if d == 0 else nsc - 1 - c
            r = pl.ds(pl.multiple_of(cidx * SC_A, SC_A), SC_A)
            st = carry[hd]
            q = q_ref[r, ls].astype(f32)
            v = v_ref[r, ls]
            kk = kk_ref[d, r, ls]
            b = b_ref[d, r, ls]
            b_end = b[SC_A - 1:SC_A, :] if d == 0 else b[0:1, :]
            att_rows = ([], [])
            for i in range(nblk):
                mid = i * CH_A + (HALF_A - 1 if d == 0 else HALF_A)
                r_i = b[mid:mid + 1, :]
                blk = slice(i * CH_A, (i + 1) * CH_A)
                qi = (q[blk] * jnp.exp2(b[blk] - r_i)).astype(bf16)
                if d == 0 and i < nblk // 2:
                    ksl = slice(0, half)
                elif d == 1 and i >= nblk // 2:
                    ksl = slice(half, SC_A)
                else:
                    ksl = slice(0, SC_A)
                ks = (kk[ksl] * jnp.exp2(r_i - b[ksl])).astype(bf16)
                att_rows[i >= nblk // 2].append(lax.dot_general(qi, ks, NT, preferred_element_type=f32))
            a_lo = jnp.where(masks_lo[d], jnp.concatenate(att_rows[0], axis=0), 0.0).astype(bf16)
            a_hi = jnp.where(masks_hi[d], jnp.concatenate(att_rows[1], axis=0), 0.0).astype(bf16)
            o_lo = jnp.dot(a_lo, v[:half] if d == 0 else v, preferred_element_type=f32)
            o_hi = jnp.dot(a_hi, v if d == 0 else v[half:], preferred_element_type=f32)
            qb = (q * jnp.exp2(b)).astype(bf16)
            o_ref[d, r, ls] = (jnp.concatenate([o_lo, o_hi], axis=0)
                               + lax.dot_general(qb, st.astype(bf16), NT, preferred_element_type=f32))
            kl = (kk * jnp.exp2(b_end - b)).astype(bf16)
            new.append(jnp.exp2(b_end) * st + lax.dot_general(v, kl, TN, preferred_element_type=f32))
        return tuple(new)

    st_fin = lax.fori_loop(0, nsc, body, st0, unroll=2)

    for hh in range(HPS):
        ls = slice(hh * DH, (hh + 1) * DH)
        o = o_ref[0, :, ls] + o_ref[1, :, ls]
        ms = jnp.mean(o * o, axis=-1, keepdims=True)
        g = g_ref[:, ls].astype(f32)
        y_ref[:, ls] = (o * lax.rsqrt(ms + 1e-6) * gain_ref[...] * (g * _sigmoid(g))).astype(y_ref.dtype)
        if want_state:
            for d in range(2):
                st_ref[d, hh] = st_fin[2 * hh + d].T


def hgrn_scan(qig, zz, lbp, gain, s0, y_buf, state_buf, layer, *, n_seq, T, row_blk0, name):
    has_state = s0 is not None
    want_state = layer is not None

    wh = HPS * DH

    def col(off):
        return lambda b, h: (row_blk0 + b, off // HPS + h)

    in_specs = [pl.BlockSpec((T, wh), col(0)), pl.BlockSpec((T, wh), col(NH)),
                pl.BlockSpec((T, wh), col(2 * NH)),
                pl.BlockSpec((T, wh), col(0)), pl.BlockSpec((T, wh), col(NH)),
                pl.BlockSpec((8, wh), lambda b, h: (0, h)),
                pl.BlockSpec((1, DH), lambda b, h: (0, 0))]
    args = [qig, qig, qig, zz, zz, lbp, gain]
    if has_state:
        in_specs.append(pl.BlockSpec((None, 2, HPS, DH, DH), lambda b, h: (b, 0, h, 0, 0)))
        args.append(s0)
    out_specs = [pl.BlockSpec((T, wh), col(0))]
    out_shape = [jax.ShapeDtypeStruct((qig.shape[0], D), bf16)]
    aliases = {}
    if y_buf is not None:
        in_specs.append(pl.BlockSpec(memory_space=pl.ANY))
        args.append(y_buf)
        aliases[len(args) - 1] = 0
    if want_state:
        out_specs.append(pl.BlockSpec((None, None, 2, HPS, DH, DH),
                                      lambda b, h: (b, layer, 0, h, 0, 0)))
        out_shape.append(jax.ShapeDtypeStruct((n_seq, DEPTH, 2, NH, DH, DH), f32))
        if state_buf is not None:
            in_specs.append(pl.BlockSpec(memory_space=pl.ANY))
            args.append(state_buf)
            aliases[len(args) - 1] = 1
    scratch = [pltpu.VMEM((2, T, wh), f32)] * 3
    return pl.pallas_call(
        functools.partial(_hgrn_kernel, has_state=has_state, want_state=want_state,
                          n_aliased=len(aliases)),
        grid=(n_seq, NH // HPS),
        in_specs=in_specs, out_specs=out_specs, out_shape=out_shape,
        scratch_shapes=scratch,
        input_output_aliases=aliases,
        compiler_params=_cparams("parallel", "parallel"),
        name=name,
    )(*args)


def _split3(x):
    x0 = x.astype(bf16)
    r = x - x0.astype(f32)
    x1 = r.astype(bf16)
    x2 = (r - x1.astype(f32)).astype(bf16)
    return x0, x1, x2


def _tri_matmul(tri, x):
    return sum(jnp.dot(tri, p, preferred_element_type=f32) for p in _split3(x))


def _matmul_tri(x, tri):
    return sum(jnp.dot(p, tri, preferred_element_type=f32) for p in _split3(x))


def _mlstm_kernel(*refs, grid_rows, has_state, want_state, n_aliased):
    x_ref, v_ref, og_ref, gc_ref, gr_ref, cw_ref, cb_ref, wq_ref, wk_ref, fbr_ref, fbc_ref, gain_ref = refs[:12]
    pos = 12
    if has_state:
        c0_ref, m0_ref = refs[pos:pos + 2]
        pos += 2
    pos += n_aliased
    y_ref = refs[pos]
    pos += 1
    if want_state:
        ct_ref, nt_ref, mt_ref = refs[pos:pos + 3]

    head0 = pl.program_id(1) * HPS
    T = x_ref.shape[0]
    L = min(CH_B, T)
    nchunk = T // L

    t_idx = lax.broadcasted_iota(jnp.int32, (T, DH), 0)
    lane = lax.broadcasted_iota(jnp.int32, (T, DH), 1)
    ones_col = jnp.where(lane == 0, 1.0, 0.0)
    if grid_rows is None:
        taps = [(0, dc) for dc in (-1, 0, 1)]
    else:
        taps = [(dr, dc) for dr in (-1, 0, 1) for dc in (-1, 0, 1)]
        c_idx = t_idx % GRID_W
    qs, ks, vaugs = [], [], []
    for hh in range(HPS):
        ls = slice(hh * DH, (hh + 1) * DH)
        x = x_ref[:, ls].astype(f32)
        acc = jnp.zeros((T, DH), f32) + cb_ref[:, ls]
        for dr, dc in taps:
            off = dr * GRID_W + dc
            tap = (dr + 1) * 3 + (dc + 1)
            w = cw_ref[tap:tap + 1, ls]
            xs = x if off == 0 else pltpu.roll(x, (-off) % T, axis=0)
            ok = (t_idx + off >= 0) & (t_idx + off < T)
            if grid_rows is not None:
                ok = ok & (c_idx + dc >= 0) & (c_idx + dc < GRID_W)
            acc = acc + jnp.where(ok, xs, 0.0) * w
        ca = (acc * _sigmoid(acc)).astype(bf16)
        qs.append(jnp.dot(ca, wq_ref[hh], preferred_element_type=f32).astype(bf16))
        ks.append((jnp.dot(ca, wk_ref[hh], preferred_element_type=f32) * (DH ** -0.5)).astype(bf16))
        vaugs.append(jnp.concatenate([v_ref[:, ls].astype(f32), ones_col], axis=1))

    gate_lane = lax.broadcasted_iota(jnp.int32, (T, DH), 1) // NH
    is_f = (gate_lane == 1) | (gate_lane == 3)
    gc = gc_ref[...] + fbr_ref[...]
    gc = jnp.where(is_f, _log_sigmoid(gc), gc)

    def pick_lane(a, ln):
        sel = lax.broadcasted_iota(jnp.int32, a.shape, 1) == ln
        return jnp.sum(jnp.where(sel, a, 0.0), axis=1, keepdims=True)

    def pick_row(a, rw):
        sel = lax.broadcasted_iota(jnp.int32, a.shape, 0) == rw
        return jnp.sum(jnp.where(sel, a, 0.0), axis=0, keepdims=True)

    r_i = lax.broadcasted_iota(jnp.int32, (L, L), 0)
    c_i = lax.broadcasted_iota(jnp.int32, (L, L), 1)
    lower = r_i >= c_i
    upper = r_i <= c_i
    tri_lo = jnp.where(lower, 1.0, 0.0).astype(bf16)
    tri_up = jnp.where(upper, 1.0, 0.0).astype(bf16)

    h_parts = [[[None] * nchunk for _ in range(2)] for _ in range(HPS)]
    finals = [[None, None] for _ in range(HPS)]
    for d in range(2):
        if has_state:
            carry = [(c0_ref[d, hh], m0_ref[d, hh]) for hh in range(HPS)]
        else:
            carry = [(jnp.zeros((DH, 2 * DH), f32), jnp.zeros((1, 1), f32)) for _ in range(HPS)]
        order = range(nchunk) if d == 0 else range(nchunk - 1, -1, -1)
        mask = lower if d == 0 else upper
        tri_c = tri_lo if d == 0 else tri_up
        tri_r = tri_up if d == 0 else tri_lo
        for cidx in order:
            rs = slice(cidx * L, (cidx + 1) * L)
            gcc = gc[rs]
            F_c_all = _tri_matmul(tri_c, gcc)
            lf_r8 = _log_sigmoid(gr_ref[(2 * d + 1) * NH:(2 * d + 2) * NH, rs] + fbc_ref[d])
            F_r_all = _matmul_tri(lf_r8, tri_r)
            ig_r_all = gr_ref[2 * d * NH:(2 * d + 1) * NH, rs]
            for hh in range(HPS):
                head = head0 + hh
                caug, m_prev = carry[hh]
                ig_c = pick_lane(gcc, 2 * d * NH + head)
                F_c = pick_lane(F_c_all, (2 * d + 1) * NH + head)
                ig_r = pick_row(ig_r_all, head)
                F_r = pick_row(F_r_all, head)
                F_end = F_c[L - 1:L, :] if d == 0 else F_c[0:1, :]

                logd = jnp.where(mask, F_c - F_r + ig_r, -jnp.inf)
                log_inter = F_c + m_prev
                m_q = jnp.maximum(log_inter, jnp.max(logd, axis=-1, keepdims=True))
                w = jnp.exp(logd - m_q)
                a = jnp.exp(log_inter - m_q)
                qc, kc = qs[hh][rs], ks[hh][rs]
                va = vaugs[hh][rs]
                s = lax.dot_general(qc, kc, NT, preferred_element_type=f32) * w
                nd = (a * jnp.dot(qc, caug.astype(bf16), preferred_element_type=f32)
                      + jnp.dot(s.astype(bf16), va.astype(bf16), preferred_element_type=f32))
                num = nd[:, :DH]
                den = nd[:, DH:DH + 1]
                h_parts[hh][d][cidx] = num / jnp.maximum(jnp.abs(den), jnp.exp(-m_q))
                log_w_c = F_end - F_c + ig_c
                log_w_r = F_end - F_r + ig_r
                m_new = jnp.maximum(F_end + m_prev, jnp.max(log_w_r, axis=-1, keepdims=True))
                ws_c = jnp.exp(log_w_c - m_new)
                d0 = jnp.exp(F_end + m_prev - m_new)
                upd = lax.dot_general(kc, (ws_c * va).astype(bf16), TN, preferred_element_type=f32)
                carry[hh] = (d0 * caug + upd, m_new)
        for hh in range(HPS):
            finals[hh][d] = carry[hh]

    for hh in range(HPS):
        ls = slice(hh * DH, (hh + 1) * DH)
        hsum = jnp.concatenate([h_parts[hh][0][c] + h_parts[hh][1][c] for c in range(nchunk)], axis=0)
        mu = jnp.mean(hsum, axis=-1, keepdims=True)
        dv = hsum - mu
        var = jnp.mean(dv * dv, axis=-1, keepdims=True)
        og = og_ref[:, ls].astype(f32)
        y_ref[:, ls] = (dv * lax.rsqrt(var + 1e-6) * gain_ref[...] * _sigmoid(og)).astype(y_ref.dtype)
        if want_state:
            for d in range(2):
                caug, m_fin = finals[hh][d]
                ct_ref[d, hh] = caug[:, :DH]
                nt_ref[d, hh] = caug[:, DH:2 * DH].T[0:1, :]
                mt_ref[d, hh] = jnp.broadcast_to(m_fin, (1, DH))


def mlstm_scan(bb, gates, gates_t, conv_w9, conv_b, wq, wk, fb_row, fb_col, gain, state, y_buf, c_buf,
               layer, *, n_seq, T, row_blk0, grid_rows, name):
    has_state = state is not None
    want_state = layer is not None

    wh = HPS * DH

    def col(off):
        return lambda b, h: (row_blk0 + b, off // HPS + h)

    in_specs = [pl.BlockSpec((T, wh), col(0)), pl.BlockSpec((T, wh), col(NH)),
                pl.BlockSpec((T, wh), col(2 * NH)),
                pl.BlockSpec((T, DH), lambda b, h: (row_blk0 + b, 0)),
                pl.BlockSpec((DH, T), lambda b, h: (0, row_blk0 + b)),
                pl.BlockSpec((9, wh), lambda b, h: (0, h)),
                pl.BlockSpec((1, wh), lambda b, h: (0, h)),
                pl.BlockSpec((HPS, DH, DH), lambda b, h: (h, 0, 0)),
                pl.BlockSpec((HPS, DH, DH), lambda b, h: (h, 0, 0)),
                pl.BlockSpec((1, DH), lambda b, h: (0, 0)),
                pl.BlockSpec((2, NH, 1), lambda b, h: (0, 0, 0)),
                pl.BlockSpec((1, DH), lambda b, h: (0, 0))]
    args = [bb, bb, bb, gates, gates_t, conv_w9, conv_b, wq, wk, fb_row, fb_col, gain]
    if has_state:
        c0aug, m0 = state
        in_specs += [pl.BlockSpec((None, 2, HPS, DH, 2 * DH), lambda b, h: (b, 0, h, 0, 0)),
                     pl.BlockSpec((None, 2, HPS, 1, 1), lambda b, h: (b, 0, h, 0, 0))]
        args += [c0aug, m0]
    out_specs = [pl.BlockSpec((T, wh), col(0))]
    out_shape = [jax.ShapeDtypeStruct((bb.shape[0], D), bf16)]
    aliases = {}
    if y_buf is not None:
        in_specs.append(pl.BlockSpec(memory_space=pl.ANY))
        args.append(y_buf)
        aliases[len(args) - 1] = 0
    if want_state:
        out_specs += [pl.BlockSpec((None, None, 2, HPS, DH, DH), lambda b, h: (b, layer, 0, h, 0, 0)),
                      pl.BlockSpec((None, 2, HPS, 1, DH), lambda b, h: (b, 0, h, 0, 0)),
                      pl.BlockSpec((None, 2, HPS, 1, DH), lambda b, h: (b, 0, h, 0, 0))]
        out_shape += [jax.ShapeDtypeStruct((n_seq, DEPTH, 2, NH, DH, DH), f32),
                      jax.ShapeDtypeStruct((n_seq, 2, NH, 1, DH), f32),
                      jax.ShapeDtypeStruct((n_seq, 2, NH, 1, DH), f32)]
        if c_buf is not None:
            in_specs.append(pl.BlockSpec(memory_space=pl.ANY))
            args.append(c_buf)
            aliases[len(args) - 1] = 1
    return pl.pallas_call(
        functools.partial(_mlstm_kernel, grid_rows=grid_rows, has_state=has_state,
                          want_state=want_state, n_aliased=len(aliases)),
        grid=(n_seq, NH // HPS),
        in_specs=in_specs, out_specs=out_specs, out_shape=out_shape,
        input_output_aliases=aliases,
        compiler_params=_cparams("parallel", "parallel"),
        name=name,
    )(*args)


def _merge_kernel(ya_ref, yb_ref, gm_ref, x_ref, mod_ref, wa_ref, wb_ref, wo_ref, lg_ref, lb_ref,
                  x1_ref, hh_ref):
    pa = jnp.dot(ya_ref[...], wa_ref[...], preferred_element_type=f32)
    pb = jnp.dot(yb_ref[...], wb_ref[...], preferred_element_type=f32)
    gma = gm_ref[:, :D].astype(f32)
    gmb = gm_ref[:, D:].astype(f32)
    merged = (_sigmoid(gma) * pa + _sigmoid(gmb) * pb).astype(bf16)
    y = jnp.dot(merged, wo_ref[...], preferred_element_type=f32)
    x1 = _layer_norm(ALPHA * x_ref[...] + mod_ref[2:3, :] * y, lg_ref[...], lb_ref[...])
    x1_ref[...] = x1
    hh_ref[...] = (x1 * (1.0 + mod_ref[4:5, :]) + mod_ref[3:4, :]).astype(hh_ref.dtype)


def merge(ya, yb, gm, x, mod_l, wa, wb, wo, lg, lb):
    n = x.shape[0]
    tm = 512
    grp = _group_of_tile(tm)
    row = lambda i: (i, 0)
    const = lambda i: (0, 0)
    return pl.pallas_call(
        _merge_kernel,
        grid=(n // tm,),
        in_specs=[pl.BlockSpec((tm, D), row), pl.BlockSpec((tm, D), row),
                  pl.BlockSpec((tm, 2 * D), row), pl.BlockSpec((tm, D), row),
                  pl.BlockSpec((None, 8, D), lambda i: (grp(i), 0, 0)),
                  pl.BlockSpec((D, D), const), pl.BlockSpec((D, D), const), pl.BlockSpec((D, D), const),
                  pl.BlockSpec((1, D), const), pl.BlockSpec((1, D), const)],
        out_specs=[pl.BlockSpec((tm, D), row), pl.BlockSpec((tm, D), row)],
        out_shape=[jax.ShapeDtypeStruct((n, D), f32), jax.ShapeDtypeStruct((n, D), bf16)],
        compiler_params=_cparams("parallel"),
        name="merge",
    )(ya, yb, gm, x, mod_l, wa, wb, wo, lg.reshape(1, D), lb.reshape(1, D))


def _post_ffn(f, x1, mod, mod_next, lg, lb, x2_ref, hn_ref):
    x2 = _layer_norm(ALPHA * x1 + mod[5:6, :] * f, lg, lb)
    x2_ref[...] = x2
    hn_ref[...] = (x2 * (1.0 + mod_next[1:2, :]) + mod_next[0:1, :]).astype(hn_ref.dtype)


def _ffn_kernel(hh_ref, w1_ref, w3_ref, w2_ref, x1_ref, mod_ref, modn_ref, lg_ref, lb_ref,
                x2_ref, hn_ref, acc_ref):
    j = pl.program_id(1)
    hh = hh_ref[...]
    a = jnp.dot(hh, w1_ref[...], preferred_element_type=f32)
    b = jnp.dot(hh, w3_ref[...], preferred_element_type=f32)
    part = jnp.dot((a * _sigmoid(a) * b).astype(bf16), w2_ref[...], preferred_element_type=f32)

    @pl.when(j == 0)
    def _():
        acc_ref[...] = part

    @pl.when(j != 0)
    def _():
        acc_ref[...] += part

    @pl.when(j == pl.num_programs(1) - 1)
    def _():
        _post_ffn(acc_ref[...], x1_ref[...], mod_ref, modn_ref, lg_ref[...], lb_ref[...],
                  x2_ref, hn_ref)


def ffn(hh, w1, w3, w2, x1, mod_l, mod_next, lg, lb):
    n = hh.shape[0]
    tm, tf = 512, 1408
    grp = _group_of_tile(tm)
    row = lambda i, j: (i, 0)
    const = lambda i, j: (0, 0)
    return pl.pallas_call(
        _ffn_kernel,
        grid=(n // tm, D_FF // tf),
        in_specs=[pl.BlockSpec((tm, D), row),
                  pl.BlockSpec((D, tf), lambda i, j: (0, j)),
                  pl.BlockSpec((D, tf), lambda i, j: (0, j)),
                  pl.BlockSpec((tf, D), lambda i, j: (j, 0)),
                  pl.BlockSpec((tm, D), row),
                  pl.BlockSpec((None, 8, D), lambda i, j: (grp(i), 0, 0)),
                  pl.BlockSpec((None, 8, D), lambda i, j: (grp(i), 0, 0)),
                  pl.BlockSpec((1, D), const), pl.BlockSpec((1, D), const)],
        out_specs=[pl.BlockSpec((tm, D), row), pl.BlockSpec((tm, D), row)],
        out_shape=[jax.ShapeDtypeStruct((n, D), f32), jax.ShapeDtypeStruct((n, D), bf16)],
        scratch_shapes=[pltpu.VMEM((tm, D), f32)],
        compiler_params=_cparams("parallel", "arbitrary"),
        name="ffn",
    )(hh, w1, w3, w2, x1, mod_l, mod_next, lg.reshape(1, D), lb.reshape(1, D))


def _router_kernel(hh_ref, wr_ref, br_ref, o_ref, cnt_ref, run_ref):
    logits = jnp.dot(hh_ref[...].astype(f32), wr_ref[...], preferred_element_type=f32,
                     precision=lax.Precision.HIGHEST) + br_ref[...]
    lane = lax.broadcasted_iota(jnp.int32, logits.shape, 1).astype(f32)
    m1 = jnp.max(logits, axis=-1, keepdims=True)
    i1 = jnp.min(jnp.where(logits == m1, lane, float(DH)), axis=-1, keepdims=True)
    rest = jnp.where(lane == i1, -jnp.inf, logits)
    m2 = jnp.max(rest, axis=-1, keepdims=True)
    i2 = jnp.min(jnp.where(rest == m2, lane, float(DH)), axis=-1, keepdims=True)
    e21 = jnp.exp(m2 - m1)
    p1 = 1.0 / (1.0 + e21)
    p2 = e21 * p1

    @pl.when(pl.program_id(0) == 0)
    def _():
        run_ref[...] = jnp.zeros_like(run_ref)

    tm = logits.shape[0]
    r_i = lax.broadcasted_iota(jnp.int32, (tm, tm), 0)
    c_i = lax.broadcasted_iota(jnp.int32, (tm, tm), 1)
    earlier = jnp.where(c_i < r_i, 1.0, 0.0).astype(bf16)
    oh1 = jnp.where(lane == i1, 1.0, 0.0)
    oh2 = jnp.where(lane == i2, 1.0, 0.0)
    pre1 = jnp.dot(earlier, oh1.astype(bf16), preferred_element_type=f32)
    pre2 = jnp.dot(earlier, oh2.astype(bf16), preferred_element_type=f32)
    cnt1 = jnp.sum(oh1, axis=0, keepdims=True)
    cnt2 = jnp.sum(oh2, axis=0, keepdims=True)
    run = run_ref[...]
    rank1 = jnp.sum(oh1 * (run + pre1), axis=-1, keepdims=True)
    rank2 = jnp.sum(oh2 * (run + cnt1 + pre2), axis=-1, keepdims=True)
    run_ref[...] = run + cnt1 + cnt2
    cnt_ref[...] = jnp.broadcast_to(run + cnt1 + cnt2, cnt_ref.shape)
    o_ref[...] = (jnp.where(lane == 0.0, i1, 0.0) + jnp.where(lane == 1.0, i2, 0.0)
                  + jnp.where(lane == 2.0, p1, 0.0) + jnp.where(lane == 3.0, p2, 0.0)
                  + jnp.where(lane == 4.0, rank1, 0.0) + jnp.where(lane == 5.0, rank2, 0.0))


def router(hh, wr_pad, br_pad):
    n = hh.shape[0]
    tm = 1024
    return pl.pallas_call(
        _router_kernel,
        grid=(n // tm,),
        in_specs=[pl.BlockSpec((tm, D), lambda i: (i, 0)),
                  pl.BlockSpec((D, DH), lambda i: (0, 0)),
                  pl.BlockSpec((1, DH), lambda i: (0, 0))],
        out_specs=[pl.BlockSpec((tm, DH), lambda i: (i, 0)),
                   pl.BlockSpec((8, DH), lambda i: (0, 0))],
        out_shape=[jax.ShapeDtypeStruct((n, DH), f32), jax.ShapeDtypeStruct((8, DH), f32)],
        scratch_shapes=[pltpu.VMEM((1, DH), f32)],
        compiler_params=_cparams("arbitrary"),
        name="router",
    )(hh, wr_pad, br_pad)


def _moe_up_kernel(te_ref, x_ref, w1_ref, w3_ref, a_ref):
    del te_ref
    x = x_ref[...]
    a = jnp.dot(x, w1_ref[...].astype(bf16), preferred_element_type=f32)
    b = jnp.dot(x, w3_ref[...].astype(bf16), preferred_element_type=f32)
    a_ref[...] = (a * _sigmoid(a) * b).astype(a_ref.dtype)


def _moe_down_kernel(te_ref, a_ref, w2_ref, y_ref):
    del te_ref
    y_ref[...] = jnp.dot(a_ref[...], w2_ref[...].astype(bf16),
                         preferred_element_type=f32).astype(y_ref.dtype)


def moe_experts(tile_expert, xs, w1, w3, w2, lyr):
    r = xs.shape[0]
    nt = r // MOE_TM
    tf = 1408
    up = pl.pallas_call(
        _moe_up_kernel,
        grid_spec=pltpu.PrefetchScalarGridSpec(
            num_scalar_prefetch=1,
            grid=(D_FF // tf, nt),
            in_specs=[pl.BlockSpec((MOE_TM, D), lambda j, i, te: (i, 0)),
                      pl.BlockSpec((None, None, D, tf), lambda j, i, te: (lyr, te[i], 0, j)),
                      pl.BlockSpec((None, None, D, tf), lambda j, i, te: (lyr, te[i], 0, j))],
            out_specs=pl.BlockSpec((MOE_TM, tf), lambda j, i, te: (i, j))),
        out_shape=jax.ShapeDtypeStruct((r, D_FF), bf16),
        compiler_params=_cparams("arbitrary", "arbitrary"),
        name="moe_up",
    )(tile_expert, xs, w1, w3)
    return pl.pallas_call(
        _moe_down_kernel,
        grid_spec=pltpu.PrefetchScalarGridSpec(
            num_scalar_prefetch=1,
            grid=(nt,),
            in_specs=[pl.BlockSpec((MOE_TM, D_FF), lambda i, te: (i, 0)),
                      pl.BlockSpec((None, None, D_FF, D), lambda i, te: (lyr, te[i], 0, 0))],
            out_specs=pl.BlockSpec((MOE_TM, D), lambda i, te: (i, 0))),
        out_shape=jax.ShapeDtypeStruct((r, D), bf16),
        compiler_params=_cparams("arbitrary"),
        name="moe_down",
    )(tile_expert, up, w2)


def _combine_kernel(ya_ref, yb_ref, rt_ref, x1_ref, mod_ref, modn_ref, lg_ref, lb_ref, x2_ref, hn_ref):
    f = rt_ref[:, 2:3] * ya_ref[...].astype(f32) + rt_ref[:, 3:4] * yb_ref[...].astype(f32)
    _post_ffn(f, x1_ref[...], mod_ref, modn_ref, lg_ref[...], lb_ref[...], x2_ref, hn_ref)


def combine(ya, yb, rt, x1, mod_l, mod_next, lg, lb):
    n = x1.shape[0]
    tm = 1024
    grp = _group_of_tile(tm)
    row = lambda i: (i, 0)
    const = lambda i: (0, 0)
    return pl.pallas_call(
        _combine_kernel,
        grid=(n // tm,),
        in_specs=[pl.BlockSpec((tm, D), row), pl.BlockSpec((tm, D), row),
                  pl.BlockSpec((tm, DH), row), pl.BlockSpec((tm, D), row),
                  pl.BlockSpec((None, 8, D), lambda i: (grp(i), 0, 0)),
                  pl.BlockSpec((None, 8, D), lambda i: (grp(i), 0, 0)),
                  pl.BlockSpec((1, D), const), pl.BlockSpec((1, D), const)],
        out_specs=[pl.BlockSpec((tm, D), row), pl.BlockSpec((tm, D), row)],
        out_shape=[jax.ShapeDtypeStruct((n, D), f32), jax.ShapeDtypeStruct((n, D), bf16)],
        compiler_params=_cparams("parallel"),
        name="combine",
    )(ya, yb, rt, x1, mod_l, mod_next, lg.reshape(1, D), lb.reshape(1, D))


def moe_dispatch(rt, counts, n):
    r_total = 2 * n + N_EXPERTS * MOE_TM
    nt = r_total // MOE_TM
    cnt = counts[0, :N_EXPERTS].astype(jnp.int32)
    padded = (cnt + MOE_TM - 1) // MOE_TM * MOE_TM
    ends = jnp.cumsum(padded)
    starts = ends - padded
    e_idx = rt[:, 0:2].astype(jnp.int32)
    rank = rt[:, 4:6].astype(jnp.int32)
    onehot = e_idx[..., None] == jnp.arange(N_EXPERTS)[None, None, :]
    dest = rank + jnp.sum(jnp.where(onehot, starts[None, None, :], 0), axis=-1)
    tok = jnp.broadcast_to(jnp.arange(n, dtype=jnp.int32)[:, None], (n, 2))
    src_tok = jnp.zeros((r_total,), jnp.int32).at[dest.reshape(-1)].set(
        tok.reshape(-1), unique_indices=True, mode="promise_in_bounds")
    tile_start = jnp.arange(nt, dtype=jnp.int32) * MOE_TM
    tile_expert = jnp.minimum(jnp.sum((tile_start[:, None] >= ends[None, :]).astype(jnp.int32), axis=1),
                              N_EXPERTS - 1)
    return src_tok, dest, tile_expert


def _take_rows(a, idx):
    r, c = a.shape
    words = lax.bitcast_convert_type(a.reshape(r, c // 2, 2), jnp.uint32)
    out = words.at[idx].get(mode="promise_in_bounds")
    return lax.bitcast_convert_type(out, bf16).reshape(idx.shape[0], c)


def kernel(x_prompt, x_sample, c, state_hgrn, state_mlstm_C, state_mlstm_n, state_mlstm_m, c_ctx,
           w_ada, b_ada, w_in, b_in, hgrn_lb_raw, hgrn_norm_g, conv_w, conv_b, w_mq, w_mk,
           mlstm_fbias, mlstm_norm_g, w_branch_a, w_branch_b, w_out, ln1_g, ln1_b, ln2_g, ln2_b,
           ffn_w1, ffn_w3, ffn_w2, moe_router_w, moe_router_b, moe_w1, moe_w3, moe_w2):
    nb, seq, _ = x_prompt.shape
    db, dseq, _ = x_sample.shape
    n_ctx, n_lat = nb * seq, db * dseq
    n = n_ctx + n_lat
    assert n_ctx == N_CTX and dseq == SEQ_GROUP and db == 2

    lb = jnp.cumsum(jax.nn.softmax(hgrn_lb_raw.astype(f32), axis=1), axis=1)
    lb = lb - lb[:, :1]

    cond8 = jnp.concatenate([c_ctx[None, :], c, jnp.zeros((5, D), f32)], axis=0)
    mod = modulation(cond8, w_ada, b_ada)[:, :3, :].reshape(DEPTH, 3, 6, D)
    mod = jnp.pad(mod, ((0, 0), (0, 0), (0, 2), (0, 0)))

    x = jnp.concatenate([x_prompt.reshape(n_ctx, D), x_sample.reshape(n_lat, D)], axis=0)
    h = modulate(x, mod[0])

    sizes = (D,) * 8 + (4 * NH, D, D)
    offs = [0]
    for s in sizes:
        offs.append(offs[-1] + s)

    def cols(w, idxs):
        return jnp.concatenate([w[..., offs[i]:offs[i + 1]] for i in idxs], axis=-1)

    out_n, out_m = [], []
    buf_a = buf_c = None
    for l in range(DEPTH):
        wl, bl = w_in[l], b_in[l]
        qig = linear(h, cols(wl, (0, 3, 4)).astype(bf16), cols(bl, (0, 3, 4)), bf16, "proj_qig")
        zz = linear(h, cols(wl, (1, 2)).astype(bf16), cols(bl, (1, 2)), f32, "proj_z")
        bb = linear(h, cols(wl, (5, 6, 7)).astype(bf16), cols(bl, (5, 6, 7)), bf16, "proj_b")
        gm = linear(h, cols(wl, (9, 10)).astype(bf16), cols(bl, (9, 10)), bf16, "proj_gm")
        w_g = jnp.pad(cols(wl, (8,)), ((0, 0), (0, DH - 4 * NH))).astype(bf16)
        gates = linear(h, w_g, jnp.pad(cols(bl, (8,)), (0, DH - 4 * NH)), f32, "proj_gates")
        gates_t = gates.T

        lbl = lb[:, l]
        lbp = jnp.concatenate([lbl, 1.0 - lbl, jnp.zeros((4, D), f32)], axis=0)
        gain_a = hgrn_norm_g[l].reshape(1, DH)
        ya, buf_a = hgrn_scan(qig, zz, lbp, gain_a, None, None, buf_a, l, n_seq=nb, T=seq,
                              row_blk0=0, name="hgrn_ctx")
        (ya,) = hgrn_scan(qig, zz, lbp, gain_a, state_hgrn[:, l], ya, None, None, n_seq=db, T=dseq,
                          row_blk0=n_ctx // dseq, name="hgrn_lat")

        cw9 = conv_w[l].reshape(9, D)
        cb = conv_b[l].reshape(1, D)
        wq = w_mq[l].astype(bf16)
        wk = w_mk[l].astype(bf16)
        fb = mlstm_fbias[l]
        zero8 = jnp.zeros((NH,), f32)
        fb_row = jnp.concatenate([zero8, fb[0], zero8, fb[1], jnp.zeros((DH - 4 * NH,), f32)]).reshape(1, DH)
        fb_col = fb.reshape(2, NH, 1)
        gain_b = mlstm_norm_g[l].reshape(1, DH)
        yb, buf_c, s_n, s_m = mlstm_scan(bb, gates, gates_t, cw9, cb, wq, wk, fb_row, fb_col, gain_b,
                                         None, None, buf_c, l, n_seq=nb, T=seq, row_blk0=0,
                                         grid_rows=None, name="mlstm_ctx")
        c0aug = jnp.concatenate([state_mlstm_C[:, l], state_mlstm_n[:, l][..., None],
                                 jnp.zeros((db, 2, NH, DH, DH - 1), f32)], axis=-1)
        m0 = state_mlstm_m[:, l].reshape(db, 2, NH, 1, 1)
        (yb,) = mlstm_scan(bb, gates, gates_t, cw9, cb, wq, wk, fb_row, fb_col, gain_b,
                           (c0aug, m0), yb, None, None, n_seq=db, T=dseq, row_blk0=n_ctx // dseq,
                           grid_rows=dseq // GRID_W, name="mlstm_lat")

        x1, hh = merge(ya, yb, gm, x, mod[l], w_branch_a[l].astype(bf16), w_branch_b[l].astype(bf16),
                       w_out[l].astype(bf16), ln1_g[l], ln1_b[l])

        mod_next = mod[(l + 1) % DEPTH]
        j = l // 2
        if l % 2 == 0:
            x, h = ffn(hh, ffn_w1[j].astype(bf16), ffn_w3[j].astype(bf16), ffn_w2[j].astype(bf16),
                       x1, mod[l], mod_next, ln2_g[l], ln2_b[l])
        else:
            wr = jnp.pad(moe_router_w[j], ((0, 0), (0, DH - N_EXPERTS)))
            br = jnp.concatenate([moe_router_b[j], jnp.full((DH - N_EXPERTS,), -1e30, f32)]).reshape(1, DH)
            rt, counts = router(hh, wr, br)
            src_tok, dest, tile_expert = moe_dispatch(rt, counts, n)
            ys = moe_experts(tile_expert, _take_rows(hh, src_tok), moe_w1, moe_w3, moe_w2, j)
            x, h = combine(_take_rows(ys, dest[:, 0]), _take_rows(ys, dest[:, 1]), rt,
                           x1, mod[l], mod_next, ln2_g[l], ln2_b[l])

        out_n.append(s_n.reshape(nb, 2, NH, DH))
        out_m.append(s_m[..., 0, 0])

    y_prompt = x[:n_ctx].reshape(nb, seq, D)
    y_sample = x[n_ctx:].reshape(db, dseq, D)
    return (y_prompt, y_sample, buf_a, buf_c, jnp.stack(out_n, axis=1), jnp.stack(out_m, axis=1))
```

```python
import functools

import jax
import jax.numpy as jnp
from jax import lax
from jax.experimental import pallas as pl
from jax.experimental.pallas import tpu as pltpu

f32 = jnp.float32
bf16 = jnp.bfloat16

D = 1024
DEPTH = 4
GRID_W = 64
NH = 8
DH = 128
CH_A = 16
HALF_A = 8
SC_A = 128
HPS = 4
CH_B = 256
N_EXPERTS = 8
D_FF = 2816
ALPHA = (2 * DEPTH) ** 0.25
SEQ_GROUP = 1024
LOG2E = 1.4426950408889634
LOG2_F_FLOOR = -1e30
MOE_TM = 512

VMEM_LIMIT = 56 * 1024 * 1024
NT = (((1,), (1,)), ((), ()))
TN = (((0,), (0,)), ((), ()))


def _cparams(*sem):
    return pltpu.CompilerParams(dimension_semantics=sem, vmem_limit_bytes=VMEM_LIMIT)


def _sigmoid(x):
    return 1.0 / (1.0 + jnp.exp(-x))


def _log_sigmoid(t):
    return jnp.minimum(t, 0.0) - jnp.log1p(jnp.exp(-jnp.abs(t)))


def _layer_norm(r, g, b):
    mu = jnp.mean(r, axis=-1, keepdims=True)
    d = r - mu
    var = jnp.mean(d * d, axis=-1, keepdims=True)
    return d * lax.rsqrt(var + 1e-5) * g + b


def _mod_kernel(c_ref, w_ref, b_ref, o_ref):
    c = c_ref[...]
    act = (c * _sigmoid(c)).astype(bf16)
    o_ref[...] = jnp.dot(act, w_ref[...].astype(bf16), preferred_element_type=f32) + b_ref[...]


def modulation(cond8, w_ada, b_ada):
    L, _, n = w_ada.shape
    tn = 1536
    return pl.pallas_call(
        _mod_kernel,
        grid=(L, n // tn),
        in_specs=[pl.BlockSpec((8, D), lambda l, j: (0, 0)),
                  pl.BlockSpec((None, D, tn), lambda l, j: (l, 0, j)),
                  pl.BlockSpec((None, 1, tn), lambda l, j: (l, 0, j))],
        out_specs=pl.BlockSpec((None, 8, tn), lambda l, j: (l, 0, j)),
        out_shape=jax.ShapeDtypeStruct((L, 8, n), f32),
        compiler_params=_cparams("parallel", "parallel"),
        name="modulation",
    )(cond8, w_ada, b_ada.reshape(L, 1, n))


N_CTX = 32 * 256


def _group_of_tile(tm):
    def idx(i):
        return jnp.maximum(i * tm // SEQ_GROUP - (N_CTX // SEQ_GROUP - 1), 0)
    return idx


def _modulate_kernel(x_ref, mod_ref, h_ref):
    h_ref[...] = (x_ref[...] * (1.0 + mod_ref[1:2, :]) + mod_ref[0:1, :]).astype(h_ref.dtype)


def modulate(x, mod_l):
    n = x.shape[0]
    tm = 1024
    grp = _group_of_tile(tm)
    return pl.pallas_call(
        _modulate_kernel,
        grid=(n // tm,),
        in_specs=[pl.BlockSpec((tm, D), lambda i: (i, 0)),
                  pl.BlockSpec((None, 8, D), lambda i: (grp(i), 0, 0))],
        out_specs=pl.BlockSpec((tm, D), lambda i: (i, 0)),
        out_shape=jax.ShapeDtypeStruct((n, D), bf16),
        compiler_params=_cparams("parallel"),
        name="modulate",
    )(x, mod_l)


def _linear_kernel(x_ref, w_ref, b_ref, o_ref):
    acc = jnp.dot(x_ref[...], w_ref[...], preferred_element_type=f32)
    o_ref[...] = (acc + b_ref[...]).astype(o_ref.dtype)


def linear(x, w, b, out_dtype, name):
    m, k = x.shape
    n = w.shape[1]
    tm = 2048
    tn = 1024 if n % 1024 == 0 else n
    return pl.pallas_call(
        _linear_kernel,
        grid=(m // tm, n // tn),
        in_specs=[pl.BlockSpec((tm, k), lambda i, j: (i, 0)),
                  pl.BlockSpec((k, tn), lambda i, j: (0, j)),
                  pl.BlockSpec((1, tn), lambda i, j: (0, j))],
        out_specs=pl.BlockSpec((tm, tn), lambda i, j: (i, j)),
        out_shape=jax.ShapeDtypeStruct((m, n), out_dtype),
        compiler_params=_cparams("parallel", "parallel"),
        name=name,
    )(x, w, b.reshape(1, n))


def _hgrn_kernel(*refs, has_state, want_state, n_aliased):
    q_ref, v_ref, g_ref, zf_ref, zb_ref, lbp_ref, gain_ref = refs[:7]
    pos = 7
    s0_ref = None
    if has_state:
        s0_ref = refs[pos]
        pos += 1
    pos += n_aliased
    y_ref = refs[pos]
    pos += 1
    st_ref = None
    if want_state:
        st_ref = refs[pos]
        pos += 1
    b_ref, kk_ref, o_ref = refs[pos:]

    T = q_ref.shape[0]
    nsc = T // SC_A

    row = lax.broadcasted_iota(jnp.int32, (SC_A, SC_A), 0)
    col = lax.broadcasted_iota(jnp.int32, (SC_A, SC_A), 1)
    tris = tuple(jnp.where(m, 1.0, 0.0).astype(bf16) for m in (row >= col, row <= col))
    nblk = SC_A // CH_A
    half = SC_A // 2
    r_sq = lax.broadcasted_iota(jnp.int32, (half, half), 0)
    c_sq = lax.broadcasted_iota(jnp.int32, (half, half), 1)
    r_w = lax.broadcasted_iota(jnp.int32, (half, SC_A), 0)
    c_w = lax.broadcasted_iota(jnp.int32, (half, SC_A), 1)
    masks_lo = (r_sq >= c_sq, r_w <= c_w)
    masks_hi = (r_w + half >= c_w, r_sq <= c_sq)

    for d, z_ref in enumerate((zf_ref, zb_ref)):
        z = z_ref[...]
        lb = lbp_ref[d:d + 1, :]
        one_m_lb = lbp_ref[2 + d:3 + d, :]
        e = jnp.exp(-jnp.abs(z))
        inv = 1.0 / (1.0 + e)
        sig = jnp.where(z >= 0.0, 1.0, e) * inv
        kk = one_m_lb * (jnp.where(z >= 0.0, e, 1.0) * inv)
        lf2 = jnp.maximum(jnp.log(lb + one_m_lb * sig) * LOG2E, LOG2_F_FLOOR)
        for sc in range(nsc):
            rows = slice(sc * SC_A, (sc + 1) * SC_A)
            b_ref[d, rows, :] = _tri_matmul(tris[d], lf2[rows])
        kk_ref[d] = kk

    if has_state:
        st0 = tuple(s0_ref[d, hh].T for hh in range(HPS) for d in range(2))
    else:
        st0 = tuple(jnp.zeros((DH, DH), f32) for _ in range(2 * HPS))

    def body(c, carry):
        new = []
        for hd in range(2 * HPS):
            hh, d = divmod(hd, 2)
            ls = slice(hh * DH, (hh + 1) * DH)
            cidx = c if d == 0 else nsc - 1 - c
            r = pl.ds(pl.multiple_of(cidx * SC_A, SC_A), SC_A)
            st = carry[hd]
            q = q_ref[r, ls].astype(f32)
            v = v_ref[r, ls]
            kk = kk_ref[d, r, ls]
            b = b_ref[d, r, ls]
            b_end = b[SC_A - 1:SC_A, :] if d == 0 else b[0:1, :]
            att_rows = ([], [])
            for i in range(nblk):
                mid = i * CH_A + (HALF_A - 1 if d == 0 else HALF_A)
                r_i = b[mid:mid + 1, :]
                blk = slice(i * CH_A, (i + 1) * CH_A)
                qi = (q[blk] * jnp.exp2(b[blk] - r_i)).astype(bf16)
                if d == 0 and i < nblk // 2:
                    ksl = slice(0, half)
                elif d == 1 and i >= nblk // 2:
                    ksl = slice(half, SC_A)
                else:
                    ksl = slice(0, SC_A)
                ks = (kk[ksl] * jnp.exp2(r_i - b[ksl])).astype(bf16)
                att_rows[i >= nblk // 2].append(lax.dot_general(qi, ks, NT, preferred_element_type=f32))
            a_lo = jnp.where(masks_lo[d], jnp.concatenate(att_rows[0], axis=0), 0.0).astype(bf16)
            a_hi = jnp.where(masks_hi[d], jnp.concatenate(att_rows[1], axis=0), 0.0).astype(bf16)
            o_lo = jnp.dot(a_lo, v[:half] if d == 0 else v, preferred_element_type=f32)
            o_hi = jnp.dot(a_hi, v if d == 0 else v[half:], preferred_element_type=f32)
            qb = (q * jnp.exp2(b)).astype(bf16)
            o_ref[d, r, ls] = (jnp.concatenate([o_lo, o_hi], axis=0)
                               + lax.dot_general(qb, st.astype(bf16), NT, preferred_element_type=f32))
            kl = (kk * jnp.exp2(b_end - b)).astype(bf16)
            new.append(jnp.exp2(b_end) * st + lax.dot_general(v, kl, TN, preferred_element_type=f32))
        return tuple(new)

    st_fin = lax.fori_loop(0, nsc, body, st0, unroll=2)

    for hh in range(HPS):
        ls = slice(hh * DH, (hh + 1) * DH)
        o = o_ref[0, :, ls] + o_ref[1, :, ls]
        ms = jnp.mean(o * o, axis=-1, keepdims=True)
        g = g_ref[:, ls].astype(f32)
        y_ref[:, ls] = (o * lax.rsqrt(ms + 1e-6) * gain_ref[...] * (g * _sigmoid(g))).astype(y_ref.dtype)
        if want_state:
            for d in range(2):
                st_ref[d, hh] = st_fin[2 * hh + d].T


def hgrn_scan(qig, zz, lbp, gain, s0, y_buf, state_buf, layer, *, n_seq, T, row_blk0, name):
    has_state = s0 is not None
    want_state = layer is not None

    wh = HPS * DH

    def col(off):
        return lambda b, h: (row_blk0 + b, off // HPS + h)

    in_specs = [pl.BlockSpec((T, wh), col(0)), pl.BlockSpec((T, wh), col(NH)),
                pl.BlockSpec((T, wh), col(2 * NH)),
                pl.BlockSpec((T, wh), col(0)), pl.BlockSpec((T, wh), col(NH)),
                pl.BlockSpec((8, wh), lambda b, h: (0, h)),
                pl.BlockSpec((1, DH), lambda b, h: (0, 0))]
    args = [qig, qig, qig, zz, zz, lbp, gain]
    if has_state:
        in_specs.append(pl.BlockSpec((None, 2, HPS, DH, DH), lambda b, h: (b, 0, h, 0, 0)))
        args.append(s0)
    out_specs = [pl.BlockSpec((T, wh), col(0))]
    out_shape = [jax.ShapeDtypeStruct((qig.shape[0], D), bf16)]
    aliases = {}
    if y_buf is not None:
        in_specs.append(pl.BlockSpec(memory_space=pl.ANY))
        args.append(y_buf)
        aliases[len(args) - 1] = 0
    if want_state:
        out_specs.append(pl.BlockSpec((None, None, 2, HPS, DH, DH),
                                      lambda b, h: (b, layer, 0, h, 0, 0)))
        out_shape.append(jax.ShapeDtypeStruct((n_seq, DEPTH, 2, NH, DH, DH), f32))
        if state_buf is not None:
            in_specs.append(pl.BlockSpec(memory_space=pl.ANY))
            args.append(state_buf)
            aliases[len(args) - 1] = 1
    scratch = [pltpu.VMEM((2, T, wh), f32)] * 3
    return pl.pallas_call(
        functools.partial(_hgrn_kernel, has_state=has_state, want_state=want_state,
                          n_aliased=len(aliases)),
        grid=(n_seq, NH // HPS),
        in_specs=in_specs, out_specs=out_specs, out_shape=out_shape,
        scratch_shapes=scratch,
        input_output_aliases=aliases,
        compiler_params=_cparams("parallel", "parallel"),
        name=name,
    )(*args)


def _split3(x):
    x0 = x.astype(bf16)
    r = x - x0.astype(f32)
    x1 = r.astype(bf16)
    x2 = (r - x1.astype(f32)).astype(bf16)
    return x0, x1, x2


def _tri_matmul(tri, x):
    return sum(jnp.dot(tri, p, preferred_element_type=f32) for p in _split3(x))


def _matmul_tri(x, tri):
    return sum(jnp.dot(p, tri, preferred_element_type=f32) for p in _split3(x))


def _mlstm_kernel(*refs, grid_rows, has_state, want_state, n_aliased):
    x_ref, v_ref, og_ref, gc_ref, gr_ref, cw_ref, cb_ref, wq_ref, wk_ref, fbr_ref, fbc_ref, gain_ref = refs[:12]
    pos = 12
    if has_state:
        c0_ref, m0_ref = refs[pos:pos + 2]
        pos += 2
    pos += n_aliased
    y_ref = refs[pos]
    pos += 1
    if want_state:
        ct_ref, nt_ref, mt_ref = refs[pos:pos + 3]

    head0 = pl.program_id(1) * HPS
    T = x_ref.shape[0]
    L = min(CH_B, T)
    nchunk = T // L

    t_idx = lax.broadcasted_iota(jnp.int32, (T, DH), 0)
    lane = lax.broadcasted_iota(jnp.int32, (T, DH), 1)
    ones_col = jnp.where(lane == 0, 1.0, 0.0)
    if grid_rows is None:
        taps = [(0, dc) for dc in (-1, 0, 1)]
    else:
        taps = [(dr, dc) for dr in (-1, 0, 1) for dc in (-1, 0, 1)]
        c_idx = t_idx % GRID_W
    qs, ks, vaugs = [], [], []
    for hh in range(HPS):
        ls = slice(hh * DH, (hh + 1) * DH)
        x = x_ref[:, ls].astype(f32)
        acc = jnp.zeros((T, DH), f32) + cb_ref[:, ls]
        for dr, dc in taps:
            off = dr * GRID_W + dc
            tap = (dr + 1) * 3 + (dc + 1)
            w = cw_ref[tap:tap + 1, ls]
            xs = x if off == 0 else pltpu.roll(x, (-off) % T, axis=0)
            ok = (t_idx + off >= 0) & (t_idx + off < T)
            if grid_rows is not None:
                ok = ok & (c_idx + dc >= 0) & (c_idx + dc < GRID_W)
            acc = acc + jnp.where(ok, xs, 0.0) * w
        ca = (acc * _sigmoid(acc)).astype(bf16)
        qs.append(jnp.dot(ca, wq_ref[hh], preferred_element_type=f32).astype(bf16))
        ks.append((jnp.dot(ca, wk_ref[hh], preferred_element_type=f32) * (DH ** -0.5)).astype(bf16))
        vaugs.append(jnp.concatenate([v_ref[:, ls].astype(f32), ones_col], axis=1))

    gate_lane = lax.broadcasted_iota(jnp.int32, (T, DH), 1) // NH
    is_f = (gate_lane == 1) | (gate_lane == 3)
    gc = gc_ref[...] + fbr_ref[...]
    gc = jnp.where(is_f, _log_sigmoid(gc), gc)

    def pick_lane(a, ln):
        sel = lax.broadcasted_iota(jnp.int32, a.shape, 1) == ln
        return jnp.sum(jnp.where(sel, a, 0.0), axis=1, keepdims=True)

    def pick_row(a, rw):
        sel = lax.broadcasted_iota(jnp.int32, a.shape, 0) == rw
        return jnp.sum(jnp.where(sel, a, 0.0), axis=0, keepdims=True)

    r_i = lax.broadcasted_iota(jnp.int32, (L, L), 0)
    c_i = lax.broadcasted_iota(jnp.int32, (L, L), 1)
    lower = r_i >= c_i
    upper = r_i <= c_i
    tri_lo = jnp.where(lower, 1.0, 0.0).astype(bf16)
    tri_up = jnp.where(upper, 1.0, 0.0).astype(bf16)

    h_parts = [[[None] * nchunk for _ in range(2)] for _ in range(HPS)]
    finals = [[None, None] for _ in range(HPS)]
    for d in range(2):
        if has_state:
            carry = [(c0_ref[d, hh], m0_ref[d, hh]) for hh in range(HPS)]
        else:
            carry = [(jnp.zeros((DH, 2 * DH), f32), jnp.zeros((1, 1), f32)) for _ in range(HPS)]
        order = range(nchunk) if d == 0 else range(nchunk - 1, -1, -1)
        mask = lower if d == 0 else upper
        tri_c = tri_lo if d == 0 else tri_up
        tri_r = tri_up if d == 0 else tri_lo
        for cidx in order:
            rs = slice(cidx * L, (cidx + 1) * L)
            gcc = gc[rs]
            F_c_all = _tri_matmul(tri_c, gcc)
            lf_r8 = _log_sigmoid(gr_ref[(2 * d + 1) * NH:(2 * d + 2) * NH, rs] + fbc_ref[d])
            F_r_all = _matmul_tri(lf_r8, tri_r)
            ig_r_all = gr_ref[2 * d * NH:(2 * d + 1) * NH, rs]
            for hh in range(HPS):
                head = head0 + hh
                caug, m_prev = carry[hh]
                ig_c = pick_lane(gcc, 2 * d * NH + head)
                F_c = pick_lane(F_c_all, (2 * d + 1) * NH + head)
                ig_r = pick_row(ig_r_all, head)
                F_r = pick_row(F_r_all, head)
                F_end = F_c[L - 1:L, :] if d == 0 else F_c[0:1, :]

                logd = jnp.where(mask, F_c - F_r + ig_r, -jnp.inf)
                log_inter = F_c + m_prev
                m_q = jnp.maximum(log_inter, jnp.max(logd, axis=-1, keepdims=True))
                w = jnp.exp(logd - m_q)
                a = jnp.exp(log_inter - m_q)
                qc, kc = qs[hh][rs], ks[hh][rs]
                va = vaugs[hh][rs]
                s = lax.dot_general(qc, kc, NT, preferred_element_type=f32) * w
                nd = (a * jnp.dot(qc, caug.astype(bf16), preferred_element_type=f32)
                      + jnp.dot(s.astype(bf16), va.astype(bf16), preferred_element_type=f32))
                num = nd[:, :DH]
                den = nd[:, DH:DH + 1]
                h_parts[hh][d][cidx] = num / jnp.maximum(jnp.abs(den), jnp.exp(-m_q))
                log_w_c = F_end - F_c + ig_c
                log_w_r = F_end - F_r + ig_r
                m_new = jnp.maximum(F_end + m_prev, jnp.max(log_w_r, axis=-1, keepdims=True))
                ws_c = jnp.exp(log_w_c - m_new)
                d0 = jnp.exp(F_end + m_prev - m_new)
                upd = lax.dot_general(kc, (ws_c * va).astype(bf16), TN, preferred_element_type=f32)
                carry[hh] = (d0 * caug + upd, m_new)
        for hh in range(HPS):
            finals[hh][d] = carry[hh]

    for hh in range(HPS):
        ls = slice(hh * DH, (hh + 1) * DH)
        hsum = jnp.concatenate([h_parts[hh][0][c] + h_parts[hh][1][c] for c in range(nchunk)], axis=0)
        mu = jnp.mean(hsum, axis=-1, keepdims=True)
        dv = hsum - mu
        var = jnp.mean(dv * dv, axis=-1, keepdims=True)
        og = og_ref[:, ls].astype(f32)
        y_ref[:, ls] = (dv * lax.rsqrt(var + 1e-6) * gain_ref[...] * _sigmoid(og)).astype(y_ref.dtype)
        if want_state:
            for d in range(2):
                caug, m_fin = finals[hh][d]
                ct_ref[d, hh] = caug[:, :DH]
                nt_ref[d, hh] = caug[:, DH:2 * DH].T[0:1, :]
                mt_ref[d, hh] = jnp.broadcast_to(m_fin, (1, DH))


def mlstm_scan(bb, gates, gates_t, conv_w9, conv_b, wq, wk, fb_row, fb_col, gain, state, y_buf, c_buf,
               layer, *, n_seq, T, row_blk0, grid_rows, name):
    has_state = state is not None
    want_state = layer is not None

    wh = HPS * DH

    def col(off):
        return lambda b, h: (row_blk0 + b, off // HPS + h)

    in_specs = [pl.BlockSpec((T, wh), col(0)), pl.BlockSpec((T, wh), col(NH)),
                pl.BlockSpec((T, wh), col(2 * NH)),
                pl.BlockSpec((T, DH), lambda b, h: (row_blk0 + b, 0)),
                pl.BlockSpec((DH, T), lambda b, h: (0, row_blk0 + b)),
                pl.BlockSpec((9, wh), lambda b, h: (0, h)),
                pl.BlockSpec((1, wh), lambda b, h: (0, h)),
                pl.BlockSpec((HPS, DH, DH), lambda b, h: (h, 0, 0)),
                pl.BlockSpec((HPS, DH, DH), lambda b, h: (h, 0, 0)),
                pl.BlockSpec((1, DH), lambda b, h: (0, 0)),
                pl.BlockSpec((2, NH, 1), lambda b, h: (0, 0, 0)),
                pl.BlockSpec((1, DH), lambda b, h: (0, 0))]
    args = [bb, bb, bb, gates, gates_t, conv_w9, conv_b, wq, wk, fb_row, fb_col, gain]
    if has_state:
        c0aug, m0 = state
        in_specs += [pl.BlockSpec((None, 2, HPS, DH, 2 * DH), lambda b, h: (b, 0, h, 0, 0)),
                     pl.BlockSpec((None, 2, HPS, 1, 1), lambda b, h: (b, 0, h, 0, 0))]
        args += [c0aug, m0]
    out_specs = [pl.BlockSpec((T, wh), col(0))]
    out_shape = [jax.ShapeDtypeStruct((bb.shape[0], D), bf16)]
    aliases = {}
    if y_buf is not None:
        in_specs.append(pl.BlockSpec(memory_space=pl.ANY))
        args.append(y_buf)
        aliases[len(args) - 1] = 0
    if want_state:
        out_specs += [pl.BlockSpec((None, None, 2, HPS, DH, DH), lambda b, h: (b, layer, 0, h, 0, 0)),
                      pl.BlockSpec((None, 2, HPS, 1, DH), lambda b, h: (b, 0, h, 0, 0)),
                      pl.BlockSpec((None, 2, HPS, 1, DH), lambda b, h: (b, 0, h, 0, 0))]
        out_shape += [jax.ShapeDtypeStruct((n_seq, DEPTH, 2, NH, DH, DH), f32),
                      jax.ShapeDtypeStruct((n_seq, 2, NH, 1, DH), f32),
                      jax.ShapeDtypeStruct((n_seq, 2, NH, 1, DH), f32)]
        if c_buf is not None:
            in_specs.append(pl.BlockSpec(memory_space=pl.ANY))
            args.append(c_buf)
            aliases[len(args) - 1] = 1
    return pl.pallas_call(
        functools.partial(_mlstm_kernel, grid_rows=grid_rows, has_state=has_state,
                          want_state=want_state, n_aliased=len(aliases)),
        grid=(n_seq, NH // HPS),
        in_specs=in_specs, out_specs=out_specs, out_shape=out_shape,
        input_output_aliases=aliases,
        compiler_params=_cparams("parallel", "parallel"),
        name=name,
    )(*args)


def _merge_kernel(ya_ref, yb_ref, gm_ref, x_ref, mod_ref, wa_ref, wb_ref, wo_ref, lg_ref, lb_ref,
                  x1_ref, hh_ref):
    pa = jnp.dot(ya_ref[...], wa_ref[...], preferred_element_type=f32)
    pb = jnp.dot(yb_ref[...], wb_ref[...], preferred_element_type=f32)
    gma = gm_ref[:, :D].astype(f32)
    gmb = gm_ref[:, D:].astype(f32)
    merged = (_sigmoid(gma) * pa + _sigmoid(gmb) * pb).astype(bf16)
    y = jnp.dot(merged, wo_ref[...], preferred_element_type=f32)
    x1 = _layer_norm(ALPHA * x_ref[...] + mod_ref[2:3, :] * y, lg_ref[...], lb_ref[...])
    x1_ref[...] = x1
    hh_ref[...] = (x1 * (1.0 + mod_ref[4:5, :]) + mod_ref[3:4, :]).astype(hh_ref.dtype)


def merge(ya, yb, gm, x, mod_l, wa, wb, wo, lg, lb, hh_dtype):
    n = x.shape[0]
    tm = 512
    grp = _group_of_tile(tm)
    row = lambda i: (i, 0)
    const = lambda i: (0, 0)
    return pl.pallas_call(
        _merge_kernel,
        grid=(n // tm,),
        in_specs=[pl.BlockSpec((tm, D), row), pl.BlockSpec((tm, D), row),
                  pl.BlockSpec((tm, 2 * D), row), pl.BlockSpec((tm, D), row),
                  pl.BlockSpec((None, 8, D), lambda i: (grp(i), 0, 0)),
                  pl.BlockSpec((D, D), const), pl.BlockSpec((D, D), const), pl.BlockSpec((D, D), const),
                  pl.BlockSpec((1, D), const), pl.BlockSpec((1, D), const)],
        out_specs=[pl.BlockSpec((tm, D), row), pl.BlockSpec((tm, D), row)],
        out_shape=[jax.ShapeDtypeStruct((n, D), f32), jax.ShapeDtypeStruct((n, D), hh_dtype)],
        compiler_params=_cparams("parallel"),
        name="merge",
    )(ya, yb, gm, x, mod_l, wa, wb, wo, lg.reshape(1, D), lb.reshape(1, D))


def _post_ffn(f, x1, mod, mod_next, lg, lb, x2_ref, hn_ref):
    x2 = _layer_norm(ALPHA * x1 + mod[5:6, :] * f, lg, lb)
    x2_ref[...] = x2
    hn_ref[...] = (x2 * (1.0 + mod_next[1:2, :]) + mod_next[0:1, :]).astype(hn_ref.dtype)


def _ffn_kernel(hh_ref, w1_ref, w3_ref, w2_ref, x1_ref, mod_ref, modn_ref, lg_ref, lb_ref,
                x2_ref, hn_ref, acc_ref):
    j = pl.program_id(1)
    hh = hh_ref[...]
    a = jnp.dot(hh, w1_ref[...], preferred_element_type=f32)
    b = jnp.dot(hh, w3_ref[...], preferred_element_type=f32)
    part = jnp.dot((a * _sigmoid(a) * b).astype(bf16), w2_ref[...], preferred_element_type=f32)

    @pl.when(j == 0)
    def _():
        acc_ref[...] = part

    @pl.when(j != 0)
    def _():
        acc_ref[...] += part

    @pl.when(j == pl.num_programs(1) - 1)
    def _():
        _post_ffn(acc_ref[...], x1_ref[...], mod_ref, modn_ref, lg_ref[...], lb_ref[...],
                  x2_ref, hn_ref)


def ffn(hh, w1, w3, w2, x1, mod_l, mod_next, lg, lb):
    n = hh.shape[0]
    tm, tf = 512, 1408
    grp = _group_of_tile(tm)
    row = lambda i, j: (i, 0)
    const = lambda i, j: (0, 0)
    return pl.pallas_call(
        _ffn_kernel,
        grid=(n // tm, D_FF // tf),
        in_specs=[pl.BlockSpec((tm, D), row),
                  pl.BlockSpec((D, tf), lambda i, j: (0, j)),
                  pl.BlockSpec((D, tf), lambda i, j: (0, j)),
                  pl.BlockSpec((tf, D), lambda i, j: (j, 0)),
                  pl.BlockSpec((tm, D), row),
                  pl.BlockSpec((None, 8, D), lambda i, j: (grp(i), 0, 0)),
                  pl.BlockSpec((None, 8, D), lambda i, j: (grp(i), 0, 0)),
                  pl.BlockSpec((1, D), const), pl.BlockSpec((1, D), const)],
        out_specs=[pl.BlockSpec((tm, D), row), pl.BlockSpec((tm, D), row)],
        out_shape=[jax.ShapeDtypeStruct((n, D), f32), jax.ShapeDtypeStruct((n, D), bf16)],
        scratch_shapes=[pltpu.VMEM((tm, D), f32)],
        compiler_params=_cparams("parallel", "arbitrary"),
        name="ffn",
    )(hh, w1, w3, w2, x1, mod_l, mod_next, lg.reshape(1, D), lb.reshape(1, D))


def _router_kernel(hh_ref, wr_ref, br_ref, o_ref, cnt_ref, run_ref):
    logits = jnp.dot(hh_ref[...].astype(f32), wr_ref[...], preferred_element_type=f32,
                     precision=lax.Precision.HIGHEST) + br_ref[...]
    lane = lax.broadcasted_iota(jnp.int32, logits.shape, 1).astype(f32)
    m1 = jnp.max(logits, axis=-1, keepdims=True)
    i1 = jnp.min(jnp.where(logits == m1, lane, float(DH)), axis=-1, keepdims=True)
    rest = jnp.where(lane == i1, -jnp.inf, logits)
    m2 = jnp.max(rest, axis=-1, keepdims=True)
    i2 = jnp.min(jnp.where(rest == m2, lane, float(DH)), axis=-1, keepdims=True)
    e21 = jnp.exp(m2 - m1)
    p1 = 1.0 / (1.0 + e21)
    p2 = e21 * p1

    @pl.when(pl.program_id(0) == 0)
    def _():
        run_ref[...] = jnp.zeros_like(run_ref)

    tm = logits.shape[0]
    r_i = lax.broadcasted_iota(jnp.int32, (tm, tm), 0)
    c_i = lax.broadcasted_iota(jnp.int32, (tm, tm), 1)
    earlier = jnp.where(c_i < r_i, 1.0, 0.0).astype(bf16)
    oh1 = jnp.where(lane == i1, 1.0, 0.0)
    oh2 = jnp.where(lane == i2, 1.0, 0.0)
    pre1 = jnp.dot(earlier, oh1.astype(bf16), preferred_element_type=f32)
    pre2 = jnp.dot(earlier, oh2.astype(bf16), preferred_element_type=f32)
    cnt1 = jnp.sum(oh1, axis=0, keepdims=True)
    cnt2 = jnp.sum(oh2, axis=0, keepdims=True)
    run = run_ref[...]
    rank1 = jnp.sum(oh1 * (run + pre1), axis=-1, keepdims=True)
    rank2 = jnp.sum(oh2 * (run + cnt1 + pre2), axis=-1, keepdims=True)
    run_ref[...] = run + cnt1 + cnt2
    cnt_ref[...] = jnp.broadcast_to(run + cnt1 + cnt2, cnt_ref.shape)
    o_ref[...] = (jnp.where(lane == 0.0, i1, 0.0) + jnp.where(lane == 1.0, i2, 0.0)
                  + jnp.where(lane == 2.0, p1, 0.0) + jnp.where(lane == 3.0, p2, 0.0)
                  + jnp.where(lane == 4.0, rank1, 0.0) + jnp.where(lane == 5.0, rank2, 0.0))


def router(hh, wr_pad, br_pad):
    n = hh.shape[0]
    tm = 1024
    return pl.pallas_call(
        _router_kernel,
        grid=(n // tm,),
        in_specs=[pl.BlockSpec((tm, D), lambda i: (i, 0)),
                  pl.BlockSpec((D, DH), lambda i: (0, 0)),
                  pl.BlockSpec((1, DH), lambda i: (0, 0))],
        out_specs=[pl.BlockSpec((tm, DH), lambda i: (i, 0)),
                   pl.BlockSpec((8, DH), lambda i: (0, 0))],
        out_shape=[jax.ShapeDtypeStruct((n, DH), f32), jax.ShapeDtypeStruct((8, DH), f32)],
        scratch_shapes=[pltpu.VMEM((1, DH), f32)],
        compiler_params=_cparams("arbitrary"),
        name="router",
    )(hh, wr_pad, br_pad)


def _moe_up_kernel(te_ref, x_ref, w1_ref, w3_ref, a_ref):
    del te_ref
    x = x_ref[...].astype(bf16)
    a = jnp.dot(x, w1_ref[...].astype(bf16), preferred_element_type=f32)
    b = jnp.dot(x, w3_ref[...].astype(bf16), preferred_element_type=f32)
    a_ref[...] = (a * _sigmoid(a) * b).astype(a_ref.dtype)


def _moe_down_kernel(te_ref, a_ref, w2_ref, y_ref):
    del te_ref
    y_ref[...] = jnp.dot(a_ref[...], w2_ref[...].astype(bf16), preferred_element_type=f32)


def moe_experts(tile_expert, xs, w1, w3, w2, lyr):
    r = xs.shape[0]
    nt = r // MOE_TM
    tf = 1408
    up = pl.pallas_call(
        _moe_up_kernel,
        grid_spec=pltpu.PrefetchScalarGridSpec(
            num_scalar_prefetch=1,
            grid=(D_FF // tf, nt),
            in_specs=[pl.BlockSpec((MOE_TM, D), lambda j, i, te: (i, 0)),
                      pl.BlockSpec((None, None, D, tf), lambda j, i, te: (lyr, te[i], 0, j)),
                      pl.BlockSpec((None, None, D, tf), lambda j, i, te: (lyr, te[i], 0, j))],
            out_specs=pl.BlockSpec((MOE_TM, tf), lambda j, i, te: (i, j))),
        out_shape=jax.ShapeDtypeStruct((r, D_FF), bf16),
        compiler_params=_cparams("arbitrary", "arbitrary"),
        name="moe_up",
    )(tile_expert, xs, w1, w3)
    return pl.pallas_call(
        _moe_down_kernel,
        grid_spec=pltpu.PrefetchScalarGridSpec(
            num_scalar_prefetch=1,
            grid=(nt,),
            in_specs=[pl.BlockSpec((MOE_TM, D_FF), lambda i, te: (i, 0)),
                      pl.BlockSpec((None, None, D_FF, D), lambda i, te: (lyr, te[i], 0, 0))],
            out_specs=pl.BlockSpec((MOE_TM, D), lambda i, te: (i, 0))),
        out_shape=jax.ShapeDtypeStruct((r, D), f32),
        compiler_params=_cparams("arbitrary"),
        name="moe_down",
    )(tile_expert, up, w2)


def _combine_kernel(ya_ref, yb_ref, rt_ref, x1_ref, mod_ref, modn_ref, lg_ref, lb_ref, x2_ref, hn_ref):
    f = rt_ref[:, 2:3] * ya_ref[...] + rt_ref[:, 3:4] * yb_ref[...]
    _post_ffn(f, x1_ref[...], mod_ref, modn_ref, lg_ref[...], lb_ref[...], x2_ref, hn_ref)


def combine(ya, yb, rt, x1, mod_l, mod_next, lg, lb):
    n = x1.shape[0]
    tm = 1024
    grp = _group_of_tile(tm)
    row = lambda i: (i, 0)
    const = lambda i: (0, 0)
    return pl.pallas_call(
        _combine_kernel,
        grid=(n // tm,),
        in_specs=[pl.BlockSpec((tm, D), row), pl.BlockSpec((tm, D), row),
                  pl.BlockSpec((tm, DH), row), pl.BlockSpec((tm, D), row),
                  pl.BlockSpec((None, 8, D), lambda i: (grp(i), 0, 0)),
                  pl.BlockSpec((None, 8, D), lambda i: (grp(i), 0, 0)),
                  pl.BlockSpec((1, D), const), pl.BlockSpec((1, D), const)],
        out_specs=[pl.BlockSpec((tm, D), row), pl.BlockSpec((tm, D), row)],
        out_shape=[jax.ShapeDtypeStruct((n, D), f32), jax.ShapeDtypeStruct((n, D), bf16)],
        compiler_params=_cparams("parallel"),
        name="combine",
    )(ya, yb, rt, x1, mod_l, mod_next, lg.reshape(1, D), lb.reshape(1, D))


def moe_dispatch(rt, counts, n):
    r_total = 2 * n + N_EXPERTS * MOE_TM
    nt = r_total // MOE_TM
    cnt = counts[0, :N_EXPERTS].astype(jnp.int32)
    padded = (cnt + MOE_TM - 1) // MOE_TM * MOE_TM
    ends = jnp.cumsum(padded)
    starts = ends - padded
    e_idx = rt[:, 0:2].astype(jnp.int32)
    rank = rt[:, 4:6].astype(jnp.int32)
    onehot = e_idx[..., None] == jnp.arange(N_EXPERTS)[None, None, :]
    dest = rank + jnp.sum(jnp.where(onehot, starts[None, None, :], 0), axis=-1)
    tok = jnp.broadcast_to(jnp.arange(n, dtype=jnp.int32)[:, None], (n, 2))
    src_tok = jnp.zeros((r_total,), jnp.int32).at[dest.reshape(-1)].set(
        tok.reshape(-1), unique_indices=True, mode="promise_in_bounds")
    tile_start = jnp.arange(nt, dtype=jnp.int32) * MOE_TM
    tile_expert = jnp.minimum(jnp.sum((tile_start[:, None] >= ends[None, :]).astype(jnp.int32), axis=1),
                              N_EXPERTS - 1)
    return src_tok, dest, tile_expert


def _take_rows(a, idx):
    return a.at[idx].get(mode="promise_in_bounds")


def kernel(x_prompt, x_sample, c, state_hgrn, state_mlstm_C, state_mlstm_n, state_mlstm_m, c_ctx,
           w_ada, b_ada, w_in, b_in, hgrn_lb_raw, hgrn_norm_g, conv_w, conv_b, w_mq, w_mk,
           mlstm_fbias, mlstm_norm_g, w_branch_a, w_branch_b, w_out, ln1_g, ln1_b, ln2_g, ln2_b,
           ffn_w1, ffn_w3, ffn_w2, moe_router_w, moe_router_b, moe_w1, moe_w3, moe_w2):
    nb, seq, _ = x_prompt.shape
    db, dseq, _ = x_sample.shape
    n_ctx, n_lat = nb * seq, db * dseq
    n = n_ctx + n_lat
    assert n_ctx == N_CTX and dseq == SEQ_GROUP and db == 2

    lb = jnp.cumsum(jax.nn.softmax(hgrn_lb_raw.astype(f32), axis=1), axis=1)
    lb = lb - lb[:, :1]

    cond8 = jnp.concatenate([c_ctx[None, :], c, jnp.zeros((5, D), f32)], axis=0)
    mod = modulation(cond8, w_ada, b_ada)[:, :3, :].reshape(DEPTH, 3, 6, D)
    mod = jnp.pad(mod, ((0, 0), (0, 0), (0, 2), (0, 0)))

    x = jnp.concatenate([x_prompt.reshape(n_ctx, D), x_sample.reshape(n_lat, D)], axis=0)
    h = modulate(x, mod[0])

    sizes = (D,) * 8 + (4 * NH, D, D)
    offs = [0]
    for s in sizes:
        offs.append(offs[-1] + s)

    def cols(w, idxs):
        return jnp.concatenate([w[..., offs[i]:offs[i + 1]] for i in idxs], axis=-1)

    out_n, out_m = [], []
    buf_a = buf_c = None
    for l in range(DEPTH):
        wl, bl = w_in[l], b_in[l]
        qig = linear(h, cols(wl, (0, 3, 4)).astype(bf16), cols(bl, (0, 3, 4)), bf16, "proj_qig")
        zz = linear(h, cols(wl, (1, 2)).astype(bf16), cols(bl, (1, 2)), f32, "proj_z")
        bb = linear(h, cols(wl, (5, 6, 7)).astype(bf16), cols(bl, (5, 6, 7)), bf16, "proj_b")
        gm = linear(h, cols(wl, (9, 10)).astype(bf16), cols(bl, (9, 10)), bf16, "proj_gm")
        w_g = jnp.pad(cols(wl, (8,)), ((0, 0), (0, DH - 4 * NH))).astype(bf16)
        gates = linear(h, w_g, jnp.pad(cols(bl, (8,)), (0, DH - 4 * NH)), f32, "proj_gates")
        gates_t = gates.T

        lbl = lb[:, l]
        lbp = jnp.concatenate([lbl, 1.0 - lbl, jnp.zeros((4, D), f32)], axis=0)
        gain_a = hgrn_norm_g[l].reshape(1, DH)
        ya, buf_a = hgrn_scan(qig, zz, lbp, gain_a, None, None, buf_a, l, n_seq=nb, T=seq,
                              row_blk0=0, name="hgrn_ctx")
        (ya,) = hgrn_scan(qig, zz, lbp, gain_a, state_hgrn[:, l], ya, None, None, n_seq=db, T=dseq,
                          row_blk0=n_ctx // dseq, name="hgrn_lat")

        cw9 = conv_w[l].reshape(9, D)
        cb = conv_b[l].reshape(1, D)
        wq = w_mq[l].astype(bf16)
        wk = w_mk[l].astype(bf16)
        fb = mlstm_fbias[l]
        zero8 = jnp.zeros((NH,), f32)
        fb_row = jnp.concatenate([zero8, fb[0], zero8, fb[1], jnp.zeros((DH - 4 * NH,), f32)]).reshape(1, DH)
        fb_col = fb.reshape(2, NH, 1)
        gain_b = mlstm_norm_g[l].reshape(1, DH)
        yb, buf_c, s_n, s_m = mlstm_scan(bb, gates, gates_t, cw9, cb, wq, wk, fb_row, fb_col, gain_b,
                                         None, None, buf_c, l, n_seq=nb, T=seq, row_blk0=0,
                                         grid_rows=None, name="mlstm_ctx")
        c0aug = jnp.concatenate([state_mlstm_C[:, l], state_mlstm_n[:, l][..., None],
                                 jnp.zeros((db, 2, NH, DH, DH - 1), f32)], axis=-1)
        m0 = state_mlstm_m[:, l].reshape(db, 2, NH, 1, 1)
        (yb,) = mlstm_scan(bb, gates, gates_t, cw9, cb, wq, wk, fb_row, fb_col, gain_b,
                           (c0aug, m0), yb, None, None, n_seq=db, T=dseq, row_blk0=n_ctx // dseq,
                           grid_rows=dseq // GRID_W, name="mlstm_lat")

        x1, hh = merge(ya, yb, gm, x, mod[l], w_branch_a[l].astype(bf16), w_branch_b[l].astype(bf16),
                       w_out[l].astype(bf16), ln1_g[l], ln1_b[l], bf16 if l % 2 == 0 else f32)

        mod_next = mod[(l + 1) % DEPTH]
        j = l // 2
        if l % 2 == 0:
            x, h = ffn(hh, ffn_w1[j].astype(bf16), ffn_w3[j].astype(bf16), ffn_w2[j].astype(bf16),
                       x1, mod[l], mod_next, ln2_g[l], ln2_b[l])
        else:
            wr = jnp.pad(moe_router_w[j], ((0, 0), (0, DH - N_EXPERTS)))
            br = jnp.concatenate([moe_router_b[j], jnp.full((DH - N_EXPERTS,), -1e30, f32)]).reshape(1, DH)
            rt, counts = router(hh, wr, br)
            src_tok, dest, tile_expert = moe_dispatch(rt, counts, n)
            ys = moe_experts(tile_expert, _take_rows(hh, src_tok), moe_w1, moe_w3, moe_w2, j)
            x, h = combine(_take_rows(ys, dest[:, 0]), _take_rows(ys, dest[:, 1]), rt,
                           x1, mod[l], mod_next, ln2_g[l], ln2_b[l])

        out_n.append(s_n.reshape(nb, 2, NH, DH))
        out_m.append(s_m[..., 0, 0])

    y_prompt = x[:n_ctx].reshape(nb, seq, D)
    y_sample = x[n_ctx:].reshape(db, dseq, D)
    return (y_prompt, y_sample, buf_a, buf_c, jnp.stack(out_n, axis=1), jnp.stack(out_m, axis=1))
```

```python
import functools

import jax
import jax.numpy as jnp
from jax import lax
from jax.experimental import pallas as pl
from jax.experimental.pallas import tpu as pltpu

f32 = jnp.float32
bf16 = jnp.bfloat16

D = 1024
DEPTH = 4
GRID_W = 64
NH = 8
DH = 128
CH_A = 16
HALF_A = 8
SC_A = 128
HPS = 4
CH_B = 256
N_EXPERTS = 8
D_FF = 2816
ALPHA = (2 * DEPTH) ** 0.25
SEQ_GROUP = 1024
LOG2E = 1.4426950408889634
LOG2_F_FLOOR = -1e30
MOE_TM = 512

VMEM_LIMIT = 56 * 1024 * 1024
NT = (((1,), (1,)), ((), ()))
TN = (((0,), (0,)), ((), ()))


def _cparams(*sem):
    return pltpu.CompilerParams(dimension_semantics=sem, vmem_limit_bytes=VMEM_LIMIT)


def _sigmoid(x):
    return 1.0 / (1.0 + jnp.exp(-x))


def _log_sigmoid(t):
    return jnp.minimum(t, 0.0) - jnp.log1p(jnp.exp(-jnp.abs(t)))


def _layer_norm(r, g, b):
    mu = jnp.mean(r, axis=-1, keepdims=True)
    d = r - mu
    var = jnp.mean(d * d, axis=-1, keepdims=True)
    return d * lax.rsqrt(var + 1e-5) * g + b


def _mod_kernel(c_ref, w_ref, b_ref, o_ref):
    c = c_ref[...]
    act = (c * _sigmoid(c)).astype(bf16)
    o_ref[...] = jnp.dot(act, w_ref[...].astype(bf16), preferred_element_type=f32) + b_ref[...]


def modulation(cond8, w_ada, b_ada):
    L, _, n = w_ada.shape
    tn = 1536
    return pl.pallas_call(
        _mod_kernel,
        grid=(L, n // tn),
        in_specs=[pl.BlockSpec((8, D), lambda l, j: (0, 0)),
                  pl.BlockSpec((None, D, tn), lambda l, j: (l, 0, j)),
                  pl.BlockSpec((None, 1, tn), lambda l, j: (l, 0, j))],
        out_specs=pl.BlockSpec((None, 8, tn), lambda l, j: (l, 0, j)),
        out_shape=jax.ShapeDtypeStruct((L, 8, n), f32),
        compiler_params=_cparams("parallel", "parallel"),
        name="modulation",
    )(cond8, w_ada, b_ada.reshape(L, 1, n))


N_CTX = 32 * 256


def _group_of_tile(tm):
    def idx(i):
        return jnp.maximum(i * tm // SEQ_GROUP - (N_CTX // SEQ_GROUP - 1), 0)
    return idx


def _modulate_kernel(x_ref, mod_ref, h_ref):
    h_ref[...] = (x_ref[...] * (1.0 + mod_ref[1:2, :]) + mod_ref[0:1, :]).astype(h_ref.dtype)


def modulate(x, mod_l):
    n = x.shape[0]
    tm = 1024
    grp = _group_of_tile(tm)
    return pl.pallas_call(
        _modulate_kernel,
        grid=(n // tm,),
        in_specs=[pl.BlockSpec((tm, D), lambda i: (i, 0)),
                  pl.BlockSpec((None, 8, D), lambda i: (grp(i), 0, 0))],
        out_specs=pl.BlockSpec((tm, D), lambda i: (i, 0)),
        out_shape=jax.ShapeDtypeStruct((n, D), bf16),
        compiler_params=_cparams("parallel"),
        name="modulate",
    )(x, mod_l)


def _linear_kernel(x_ref, w_ref, b_ref, o_ref):
    acc = jnp.dot(x_ref[...], w_ref[...], preferred_element_type=f32)
    o_ref[...] = (acc + b_ref[...]).astype(o_ref.dtype)


def linear(x, w, b, out_dtype, name):
    m, k = x.shape
    n = w.shape[1]
    tm = 2048
    tn = 1024 if n % 1024 == 0 else n
    return pl.pallas_call(
        _linear_kernel,
        grid=(m // tm, n // tn),
        in_specs=[pl.BlockSpec((tm, k), lambda i, j: (i, 0)),
                  pl.BlockSpec((k, tn), lambda i, j: (0, j)),
                  pl.BlockSpec((1, tn), lambda i, j: (0, j))],
        out_specs=pl.BlockSpec((tm, tn), lambda i, j: (i, j)),
        out_shape=jax.ShapeDtypeStruct((m, n), out_dtype),
        compiler_params=_cparams("parallel", "parallel"),
        name=name,
    )(x, w, b.reshape(1, n))


def _hgrn_kernel(*refs, has_state, want_state, n_aliased):
    q_ref, v_ref, g_ref, zf_ref, zb_ref, lbp_ref, gain_ref = refs[:7]
    pos = 7
    s0_ref = None
    if has_state:
        s0_ref = refs[pos]
        pos += 1
    pos += n_aliased
    y_ref = refs[pos]
    pos += 1
    st_ref = None
    if want_state:
        st_ref = refs[pos]
        pos += 1
    b_ref, kk_ref, o_ref = refs[pos:]

    T = q_ref.shape[0]
    nsc = T // SC_A

    row = lax.broadcasted_iota(jnp.int32, (SC_A, SC_A), 0)
    col = lax.broadcasted_iota(jnp.int32, (SC_A, SC_A), 1)
    tris = tuple(jnp.where(m, 1.0, 0.0).astype(bf16) for m in (row >= col, row <= col))
    nblk = SC_A // CH_A
    half = SC_A // 2
    r_sq = lax.broadcasted_iota(jnp.int32, (half, half), 0)
    c_sq = lax.broadcasted_iota(jnp.int32, (half, half), 1)
    r_w = lax.broadcasted_iota(jnp.int32, (half, SC_A), 0)
    c_w = lax.broadcasted_iota(jnp.int32, (half, SC_A), 1)
    masks_lo = (r_sq >= c_sq, r_w <= c_w)
    masks_hi = (r_w + half >= c_w, r_sq <= c_sq)

    for d, z_ref in enumerate((zf_ref, zb_ref)):
        z = z_ref[...]
        lb = lbp_ref[d:d + 1, :]
        one_m_lb = lbp_ref[2 + d:3 + d, :]
        e = jnp.exp(-jnp.abs(z))
        inv = 1.0 / (1.0 + e)
        sig = jnp.where(z >= 0.0, 1.0, e) * inv
        kk = one_m_lb * (jnp.where(z >= 0.0, e, 1.0) * inv)
        lf2 = jnp.maximum(jnp.log(lb + one_m_lb * sig) * LOG2E, LOG2_F_FLOOR)
        for sc in range(nsc):
            rows = slice(sc * SC_A, (sc + 1) * SC_A)
            b_ref[d, rows, :] = _tri_matmul(tris[d], lf2[rows])
        kk_ref[d] = kk

    if has_state:
        st0 = tuple(s0_ref[d, hh].T for hh in range(HPS) for d in range(2))
    else:
        st0 = tuple(jnp.zeros((DH, DH), f32) for _ in range(2 * HPS))

    def body(c, carry):
        units = []
        for hd in range(2 * HPS):
            hh, d = divmod(hd, 2)
            ls = slice(hh * DH, (hh + 1) * DH)
            cidx = c if d == 0 else nsc - 1 - c
            r = pl.ds(pl.multiple_of(cidx * SC_A, SC_A), SC_A)
            units.append((d, ls, r))

        scores = []
        for d, ls, r in units:
            q = q_ref[r, ls].astype(f32)
            kk = kk_ref[d, r, ls]
            b = b_ref[d, r, ls]
            att_rows = ([], [])
            for i in range(nblk):
                mid = i * CH_A + (HALF_A - 1 if d == 0 else HALF_A)
                r_i = b[mid:mid + 1, :]
                blk = slice(i * CH_A, (i + 1) * CH_A)
                qi = (q[blk] * jnp.exp2(b[blk] - r_i)).astype(bf16)
                if d == 0 and i < nblk // 2:
                    ksl = slice(0, half)
                elif d == 1 and i >= nblk // 2:
                    ksl = slice(half, SC_A)
                else:
                    ksl = slice(0, SC_A)
                ks = (kk[ksl] * jnp.exp2(r_i - b[ksl])).astype(bf16)
                att_rows[i >= nblk // 2].append(lax.dot_general(qi, ks, NT, preferred_element_type=f32))
            scores.append(att_rows)

        for hd, (d, ls, r) in enumerate(units):
            v = v_ref[r, ls]
            a_lo = jnp.where(masks_lo[d], jnp.concatenate(scores[hd][0], axis=0), 0.0).astype(bf16)
            a_hi = jnp.where(masks_hi[d], jnp.concatenate(scores[hd][1], axis=0), 0.0).astype(bf16)
            o_lo = jnp.dot(a_lo, v[:half] if d == 0 else v, preferred_element_type=f32)
            o_hi = jnp.dot(a_hi, v if d == 0 else v[half:], preferred_element_type=f32)
            qb = (q_ref[r, ls].astype(f32) * jnp.exp2(b_ref[d, r, ls])).astype(bf16)
            o_ref[d, r, ls] = (jnp.concatenate([o_lo, o_hi], axis=0)
                               + lax.dot_general(qb, carry[hd].astype(bf16), NT,
                                                 preferred_element_type=f32))

        new = []
        for hd, (d, ls, r) in enumerate(units):
            b = b_ref[d, r, ls]
            b_end = b[SC_A - 1:SC_A, :] if d == 0 else b[0:1, :]
            kl = (kk_ref[d, r, ls] * jnp.exp2(b_end - b)).astype(bf16)
            new.append(jnp.exp2(b_end) * carry[hd]
                       + lax.dot_general(v_ref[r, ls], kl, TN, preferred_element_type=f32))
        return tuple(new)

    st_fin = lax.fori_loop(0, nsc, body, st0, unroll=2)

    for hh in range(HPS):
        ls = slice(hh * DH, (hh + 1) * DH)
        o = o_ref[0, :, ls] + o_ref[1, :, ls]
        ms = jnp.mean(o * o, axis=-1, keepdims=True)
        g = g_ref[:, ls].astype(f32)
        y_ref[:, ls] = (o * lax.rsqrt(ms + 1e-6) * gain_ref[...] * (g * _sigmoid(g))).astype(y_ref.dtype)
        if want_state:
            for d in range(2):
                st_ref[d, hh] = st_fin[2 * hh + d].T


def hgrn_scan(qig, zz, lbp, gain, s0, y_buf, state_buf, layer, *, n_seq, T, row_blk0, name):
    has_state = s0 is not None
    want_state = layer is not None

    wh = HPS * DH

    def col(off):
        return lambda b, h: (row_blk0 + b, off // HPS + h)

    in_specs = [pl.BlockSpec((T, wh), col(0)), pl.BlockSpec((T, wh), col(NH)),
                pl.BlockSpec((T, wh), col(2 * NH)),
                pl.BlockSpec((T, wh), col(0)), pl.BlockSpec((T, wh), col(NH)),
                pl.BlockSpec((8, wh), lambda b, h: (0, h)),
                pl.BlockSpec((1, DH), lambda b, h: (0, 0))]
    args = [qig, qig, qig, zz, zz, lbp, gain]
    if has_state:
        in_specs.append(pl.BlockSpec((None, 2, HPS, DH, DH), lambda b, h: (b, 0, h, 0, 0)))
        args.append(s0)
    out_specs = [pl.BlockSpec((T, wh), col(0))]
    out_shape = [jax.ShapeDtypeStruct((qig.shape[0], D), bf16)]
    aliases = {}
    if y_buf is not None:
        in_specs.append(pl.BlockSpec(memory_space=pl.ANY))
        args.append(y_buf)
        aliases[len(args) - 1] = 0
    if want_state:
        out_specs.append(pl.BlockSpec((None, None, 2, HPS, DH, DH),
                                      lambda b, h: (b, layer, 0, h, 0, 0)))
        out_shape.append(jax.ShapeDtypeStruct((n_seq, DEPTH, 2, NH, DH, DH), f32))
        if state_buf is not None:
            in_specs.append(pl.BlockSpec(memory_space=pl.ANY))
            args.append(state_buf)
            aliases[len(args) - 1] = 1
    scratch = [pltpu.VMEM((2, T, wh), f32)] * 3
    return pl.pallas_call(
        functools.partial(_hgrn_kernel, has_state=has_state, want_state=want_state,
                          n_aliased=len(aliases)),
        grid=(n_seq, NH // HPS),
        in_specs=in_specs, out_specs=out_specs, out_shape=out_shape,
        scratch_shapes=scratch,
        input_output_aliases=aliases,
        compiler_params=_cparams("parallel", "parallel"),
        name=name,
    )(*args)


def _split3(x):
    x0 = x.astype(bf16)
    r = x - x0.astype(f32)
    x1 = r.astype(bf16)
    x2 = (r - x1.astype(f32)).astype(bf16)
    return x0, x1, x2


def _tri_matmul(tri, x):
    return sum(jnp.dot(tri, p, preferred_element_type=f32) for p in _split3(x))


def _matmul_tri(x, tri):
    return sum(jnp.dot(p, tri, preferred_element_type=f32) for p in _split3(x))


def _mlstm_kernel(*refs, grid_rows, has_state, want_state, n_aliased):
    x_ref, v_ref, og_ref, gc_ref, gr_ref, cw_ref, cb_ref, wq_ref, wk_ref, fbr_ref, fbc_ref, gain_ref = refs[:12]
    pos = 12
    if has_state:
        c0_ref, m0_ref = refs[pos:pos + 2]
        pos += 2
    pos += n_aliased
    y_ref = refs[pos]
    pos += 1
    if want_state:
        ct_ref, nt_ref, mt_ref = refs[pos:pos + 3]

    head0 = pl.program_id(1) * HPS
    T = x_ref.shape[0]
    L = min(CH_B, T)
    nchunk = T // L

    t_idx = lax.broadcasted_iota(jnp.int32, (T, DH), 0)
    lane = lax.broadcasted_iota(jnp.int32, (T, DH), 1)
    ones_col = jnp.where(lane == 0, 1.0, 0.0)
    if grid_rows is None:
        taps = [(0, dc) for dc in (-1, 0, 1)]
    else:
        taps = [(dr, dc) for dr in (-1, 0, 1) for dc in (-1, 0, 1)]
        c_idx = t_idx % GRID_W
    cas, vaugs = [], []
    for hh in range(HPS):
        ls = slice(hh * DH, (hh + 1) * DH)
        x = x_ref[:, ls].astype(f32)
        acc = jnp.zeros((T, DH), f32) + cb_ref[:, ls]
        for dr, dc in taps:
            off = dr * GRID_W + dc
            tap = (dr + 1) * 3 + (dc + 1)
            w = cw_ref[tap:tap + 1, ls]
            xs = x if off == 0 else pltpu.roll(x, (-off) % T, axis=0)
            ok = (t_idx + off >= 0) & (t_idx + off < T)
            if grid_rows is not None:
                ok = ok & (c_idx + dc >= 0) & (c_idx + dc < GRID_W)
            acc = acc + jnp.where(ok, xs, 0.0) * w
        cas.append((acc * _sigmoid(acc)).astype(bf16))
        vaugs.append(jnp.concatenate([v_ref[:, ls].astype(f32), ones_col], axis=1))
    qs = [jnp.dot(cas[hh], wq_ref[hh], preferred_element_type=f32).astype(bf16) for hh in range(HPS)]
    ks = [(jnp.dot(cas[hh], wk_ref[hh], preferred_element_type=f32) * (DH ** -0.5)).astype(bf16)
          for hh in range(HPS)]

    gate_lane = lax.broadcasted_iota(jnp.int32, (T, DH), 1) // NH
    is_f = (gate_lane == 1) | (gate_lane == 3)
    gc = gc_ref[...] + fbr_ref[...]
    gc = jnp.where(is_f, _log_sigmoid(gc), gc)

    def pick_lane(a, ln):
        sel = lax.broadcasted_iota(jnp.int32, a.shape, 1) == ln
        return jnp.sum(jnp.where(sel, a, 0.0), axis=1, keepdims=True)

    def pick_row(a, rw):
        sel = lax.broadcasted_iota(jnp.int32, a.shape, 0) == rw
        return jnp.sum(jnp.where(sel, a, 0.0), axis=0, keepdims=True)

    r_i = lax.broadcasted_iota(jnp.int32, (L, L), 0)
    c_i = lax.broadcasted_iota(jnp.int32, (L, L), 1)
    lower = r_i >= c_i
    upper = r_i <= c_i
    tri_lo = jnp.where(lower, 1.0, 0.0).astype(bf16)
    tri_up = jnp.where(upper, 1.0, 0.0).astype(bf16)

    h_parts = [[[None] * nchunk for _ in range(2)] for _ in range(HPS)]
    units = [(hh, d) for hh in range(HPS) for d in range(2)]
    if has_state:
        carry = {u: (c0_ref[u[1], u[0]], m0_ref[u[1], u[0]]) for u in units}
    else:
        carry = {u: (jnp.zeros((DH, 2 * DH), f32), jnp.zeros((1, 1), f32)) for u in units}
    masks = (lower, upper)
    tri_cs = (tri_lo, tri_up)
    tri_rs = (tri_up, tri_lo)

    for step in range(nchunk):
        cidx = (step, nchunk - 1 - step)
        rsl = tuple(slice(c * L, (c + 1) * L) for c in cidx)

        g = {}
        for d in range(2):
            gcc = gc[rsl[d]]
            lf_r8 = _log_sigmoid(gr_ref[(2 * d + 1) * NH:(2 * d + 2) * NH, rsl[d]] + fbc_ref[d])
            g[d] = (gcc, _tri_matmul(tri_cs[d], gcc), _matmul_tri(lf_r8, tri_rs[d]),
                    gr_ref[2 * d * NH:(2 * d + 1) * NH, rsl[d]])
        picks = {}
        for hh, d in units:
            head = head0 + hh
            gcc, F_c_all, F_r_all, ig_r_all = g[d]
            ig_c = pick_lane(gcc, 2 * d * NH + head)
            F_c = pick_lane(F_c_all, (2 * d + 1) * NH + head)
            ig_r = pick_row(ig_r_all, head)
            F_r = pick_row(F_r_all, head)
            F_end = F_c[L - 1:L, :] if d == 0 else F_c[0:1, :]
            picks[hh, d] = (ig_c, F_c, ig_r, F_r, F_end)

        decay = {}
        for u in units:
            ig_c, F_c, ig_r, F_r, F_end = picks[u]
            m_prev = carry[u][1]
            logd = jnp.where(masks[u[1]], F_c - F_r + ig_r, -jnp.inf)
            log_inter = F_c + m_prev
            m_q = jnp.maximum(log_inter, jnp.max(logd, axis=-1, keepdims=True))
            decay[u] = (jnp.exp(logd - m_q), jnp.exp(log_inter - m_q), m_q)

        scores = {}
        for hh, d in units:
            qc, kc = qs[hh][rsl[d]], ks[hh][rsl[d]]
            scores[hh, d] = (lax.dot_general(qc, kc, NT, preferred_element_type=f32)
                             * decay[hh, d][0]).astype(bf16)

        for hh, d in units:
            _, a, m_q = decay[hh, d]
            qc = qs[hh][rsl[d]]
            va = vaugs[hh][rsl[d]]
            nd = (a * jnp.dot(qc, carry[hh, d][0].astype(bf16), preferred_element_type=f32)
                  + jnp.dot(scores[hh, d], va.astype(bf16), preferred_element_type=f32))
            num = nd[:, :DH]
            den = nd[:, DH:DH + 1]
            h_parts[hh][d][cidx[d]] = num / jnp.maximum(jnp.abs(den), jnp.exp(-m_q))

        for hh, d in units:
            ig_c, F_c, ig_r, F_r, F_end = picks[hh, d]
            caug, m_prev = carry[hh, d]
            log_w_c = F_end - F_c + ig_c
            log_w_r = F_end - F_r + ig_r
            m_new = jnp.maximum(F_end + m_prev, jnp.max(log_w_r, axis=-1, keepdims=True))
            ws_c = jnp.exp(log_w_c - m_new)
            d0 = jnp.exp(F_end + m_prev - m_new)
            upd = lax.dot_general(ks[hh][rsl[d]], (ws_c * vaugs[hh][rsl[d]]).astype(bf16), TN,
                                  preferred_element_type=f32)
            carry[hh, d] = (d0 * caug + upd, m_new)
    finals = [[carry[hh, d] for d in range(2)] for hh in range(HPS)]

    for hh in range(HPS):
        ls = slice(hh * DH, (hh + 1) * DH)
        hsum = jnp.concatenate([h_parts[hh][0][c] + h_parts[hh][1][c] for c in range(nchunk)], axis=0)
        mu = jnp.mean(hsum, axis=-1, keepdims=True)
        dv = hsum - mu
        var = jnp.mean(dv * dv, axis=-1, keepdims=True)
        og = og_ref[:, ls].astype(f32)
        y_ref[:, ls] = (dv * lax.rsqrt(var + 1e-6) * gain_ref[...] * _sigmoid(og)).astype(y_ref.dtype)
        if want_state:
            for d in range(2):
                caug, m_fin = finals[hh][d]
                ct_ref[d, hh] = caug[:, :DH]
                nt_ref[d, hh] = caug[:, DH:2 * DH].T[0:1, :]
                mt_ref[d, hh] = jnp.broadcast_to(m_fin, (1, DH))


def mlstm_scan(bb, gates, gates_t, conv_w9, conv_b, wq, wk, fb_row, fb_col, gain, state, y_buf, c_buf,
               layer, *, n_seq, T, row_blk0, grid_rows, name):
    has_state = state is not None
    want_state = layer is not None

    wh = HPS * DH

    def col(off):
        return lambda b, h: (row_blk0 + b, off // HPS + h)

    in_specs = [pl.BlockSpec((T, wh), col(0)), pl.BlockSpec((T, wh), col(NH)),
                pl.BlockSpec((T, wh), col(2 * NH)),
                pl.BlockSpec((T, DH), lambda b, h: (row_blk0 + b, 0)),
                pl.BlockSpec((DH, T), lambda b, h: (0, row_blk0 + b)),
                pl.BlockSpec((9, wh), lambda b, h: (0, h)),
                pl.BlockSpec((1, wh), lambda b, h: (0, h)),
                pl.BlockSpec((HPS, DH, DH), lambda b, h: (h, 0, 0)),
                pl.BlockSpec((HPS, DH, DH), lambda b, h: (h, 0, 0)),
                pl.BlockSpec((1, DH), lambda b, h: (0, 0)),
                pl.BlockSpec((2, NH, 1), lambda b, h: (0, 0, 0)),
                pl.BlockSpec((1, DH), lambda b, h: (0, 0))]
    args = [bb, bb, bb, gates, gates_t, conv_w9, conv_b, wq, wk, fb_row, fb_col, gain]
    if has_state:
        c0aug, m0 = state
        in_specs += [pl.BlockSpec((None, 2, HPS, DH, 2 * DH), lambda b, h: (b, 0, h, 0, 0)),
                     pl.BlockSpec((None, 2, HPS, 1, 1), lambda b, h: (b, 0, h, 0, 0))]
        args += [c0aug, m0]
    out_specs = [pl.BlockSpec((T, wh), col(0))]
    out_shape = [jax.ShapeDtypeStruct((bb.shape[0], D), bf16)]
    aliases = {}
    if y_buf is not None:
        in_specs.append(pl.BlockSpec(memory_space=pl.ANY))
        args.append(y_buf)
        aliases[len(args) - 1] = 0
    if want_state:
        out_specs += [pl.BlockSpec((None, None, 2, HPS, DH, DH), lambda b, h: (b, layer, 0, h, 0, 0)),
                      pl.BlockSpec((None, 2, HPS, 1, DH), lambda b, h: (b, 0, h, 0, 0)),
                      pl.BlockSpec((None, 2, HPS, 1, DH), lambda b, h: (b, 0, h, 0, 0))]
        out_shape += [jax.ShapeDtypeStruct((n_seq, DEPTH, 2, NH, DH, DH), f32),
                      jax.ShapeDtypeStruct((n_seq, 2, NH, 1, DH), f32),
                      jax.ShapeDtypeStruct((n_seq, 2, NH, 1, DH), f32)]
        if c_buf is not None:
            in_specs.append(pl.BlockSpec(memory_space=pl.ANY))
            args.append(c_buf)
            aliases[len(args) - 1] = 1
    return pl.pallas_call(
        functools.partial(_mlstm_kernel, grid_rows=grid_rows, has_state=has_state,
                          want_state=want_state, n_aliased=len(aliases)),
        grid=(n_seq, NH // HPS),
        in_specs=in_specs, out_specs=out_specs, out_shape=out_shape,
        input_output_aliases=aliases,
        compiler_params=_cparams("parallel", "parallel"),
        name=name,
    )(*args)


def _merge_kernel(ya_ref, yb_ref, gm_ref, x_ref, mod_ref, wa_ref, wb_ref, wo_ref, lg_ref, lb_ref,
                  x1_ref, hh_ref):
    pa = jnp.dot(ya_ref[...], wa_ref[...], preferred_element_type=f32)
    pb = jnp.dot(yb_ref[...], wb_ref[...], preferred_element_type=f32)
    gma = gm_ref[:, :D].astype(f32)
    gmb = gm_ref[:, D:].astype(f32)
    merged = (_sigmoid(gma) * pa + _sigmoid(gmb) * pb).astype(bf16)
    y = jnp.dot(merged, wo_ref[...], preferred_element_type=f32)
    x1 = _layer_norm(ALPHA * x_ref[...] + mod_ref[2:3, :] * y, lg_ref[...], lb_ref[...])
    x1_ref[...] = x1
    hh_ref[...] = (x1 * (1.0 + mod_ref[4:5, :]) + mod_ref[3:4, :]).astype(hh_ref.dtype)


def merge(ya, yb, gm, x, mod_l, wa, wb, wo, lg, lb, hh_dtype):
    n = x.shape[0]
    tm = 512
    grp = _group_of_tile(tm)
    row = lambda i: (i, 0)
    const = lambda i: (0, 0)
    return pl.pallas_call(
        _merge_kernel,
        grid=(n // tm,),
        in_specs=[pl.BlockSpec((tm, D), row), pl.BlockSpec((tm, D), row),
                  pl.BlockSpec((tm, 2 * D), row), pl.BlockSpec((tm, D), row),
                  pl.BlockSpec((None, 8, D), lambda i: (grp(i), 0, 0)),
                  pl.BlockSpec((D, D), const), pl.BlockSpec((D, D), const), pl.BlockSpec((D, D), const),
                  pl.BlockSpec((1, D), const), pl.BlockSpec((1, D), const)],
        out_specs=[pl.BlockSpec((tm, D), row), pl.BlockSpec((tm, D), row)],
        out_shape=[jax.ShapeDtypeStruct((n, D), f32), jax.ShapeDtypeStruct((n, D), hh_dtype)],
        compiler_params=_cparams("parallel"),
        name="merge",
    )(ya, yb, gm, x, mod_l, wa, wb, wo, lg.reshape(1, D), lb.reshape(1, D))


def _post_ffn(f, x1, mod, mod_next, lg, lb, x2_ref, hn_ref):
    x2 = _layer_norm(ALPHA * x1 + mod[5:6, :] * f, lg, lb)
    x2_ref[...] = x2
    hn_ref[...] = (x2 * (1.0 + mod_next[1:2, :]) + mod_next[0:1, :]).astype(hn_ref.dtype)


def _ffn_kernel(hh_ref, w1_ref, w3_ref, w2_ref, x1_ref, mod_ref, modn_ref, lg_ref, lb_ref,
                x2_ref, hn_ref, acc_ref):
    j = pl.program_id(1)
    hh = hh_ref[...]
    a = jnp.dot(hh, w1_ref[...], preferred_element_type=f32)
    b = jnp.dot(hh, w3_ref[...], preferred_element_type=f32)
    part = jnp.dot((a * _sigmoid(a) * b).astype(bf16), w2_ref[...], preferred_element_type=f32)

    @pl.when(j == 0)
    def _():
        acc_ref[...] = part

    @pl.when(j != 0)
    def _():
        acc_ref[...] += part

    @pl.when(j == pl.num_programs(1) - 1)
    def _():
        _post_ffn(acc_ref[...], x1_ref[...], mod_ref, modn_ref, lg_ref[...], lb_ref[...],
                  x2_ref, hn_ref)


def ffn(hh, w1, w3, w2, x1, mod_l, mod_next, lg, lb):
    n = hh.shape[0]
    tm, tf = 512, 1408
    grp = _group_of_tile(tm)
    row = lambda i, j: (i, 0)
    const = lambda i, j: (0, 0)
    return pl.pallas_call(
        _ffn_kernel,
        grid=(n // tm, D_FF // tf),
        in_specs=[pl.BlockSpec((tm, D), row),
                  pl.BlockSpec((D, tf), lambda i, j: (0, j)),
                  pl.BlockSpec((D, tf), lambda i, j: (0, j)),
                  pl.BlockSpec((tf, D), lambda i, j: (j, 0)),
                  pl.BlockSpec((tm, D), row),
                  pl.BlockSpec((None, 8, D), lambda i, j: (grp(i), 0, 0)),
                  pl.BlockSpec((None, 8, D), lambda i, j: (grp(i), 0, 0)),
                  pl.BlockSpec((1, D), const), pl.BlockSpec((1, D), const)],
        out_specs=[pl.BlockSpec((tm, D), row), pl.BlockSpec((tm, D), row)],
        out_shape=[jax.ShapeDtypeStruct((n, D), f32), jax.ShapeDtypeStruct((n, D), bf16)],
        scratch_shapes=[pltpu.VMEM((tm, D), f32)],
        compiler_params=_cparams("parallel", "arbitrary"),
        name="ffn",
    )(hh, w1, w3, w2, x1, mod_l, mod_next, lg.reshape(1, D), lb.reshape(1, D))


def _router_kernel(hh_ref, wr_ref, br_ref, o_ref, cnt_ref, run_ref):
    logits = jnp.dot(hh_ref[...].astype(f32), wr_ref[...], preferred_element_type=f32,
                     precision=lax.Precision.HIGHEST) + br_ref[...]
    lane = lax.broadcasted_iota(jnp.int32, logits.shape, 1).astype(f32)
    m1 = jnp.max(logits, axis=-1, keepdims=True)
    i1 = jnp.min(jnp.where(logits == m1, lane, float(DH)), axis=-1, keepdims=True)
    rest = jnp.where(lane == i1, -jnp.inf, logits)
    m2 = jnp.max(rest, axis=-1, keepdims=True)
    i2 = jnp.min(jnp.where(rest == m2, lane, float(DH)), axis=-1, keepdims=True)
    e21 = jnp.exp(m2 - m1)
    p1 = 1.0 / (1.0 + e21)
    p2 = e21 * p1

    @pl.when(pl.program_id(0) == 0)
    def _():
        run_ref[...] = jnp.zeros_like(run_ref)

    tm = logits.shape[0]
    r_i = lax.broadcasted_iota(jnp.int32, (tm, tm), 0)
    c_i = lax.broadcasted_iota(jnp.int32, (tm, tm), 1)
    earlier = jnp.where(c_i < r_i, 1.0, 0.0).astype(bf16)
    oh1 = jnp.where(lane == i1, 1.0, 0.0)
    oh2 = jnp.where(lane == i2, 1.0, 0.0)
    pre1 = jnp.dot(earlier, oh1.astype(bf16), preferred_element_type=f32)
    pre2 = jnp.dot(earlier, oh2.astype(bf16), preferred_element_type=f32)
    cnt1 = jnp.sum(oh1, axis=0, keepdims=True)
    cnt2 = jnp.sum(oh2, axis=0, keepdims=True)
    run = run_ref[...]
    rank1 = jnp.sum(oh1 * (run + pre1), axis=-1, keepdims=True)
    rank2 = jnp.sum(oh2 * (run + cnt1 + pre2), axis=-1, keepdims=True)
    run_ref[...] = run + cnt1 + cnt2
    cnt_ref[...] = jnp.broadcast_to(run + cnt1 + cnt2, cnt_ref.shape)
    o_ref[...] = (jnp.where(lane == 0.0, i1, 0.0) + jnp.where(lane == 1.0, i2, 0.0)
                  + jnp.where(lane == 2.0, p1, 0.0) + jnp.where(lane == 3.0, p2, 0.0)
                  + jnp.where(lane == 4.0, rank1, 0.0) + jnp.where(lane == 5.0, rank2, 0.0))


def router(hh, wr_pad, br_pad):
    n = hh.shape[0]
    tm = 1024
    return pl.pallas_call(
        _router_kernel,
        grid=(n // tm,),
        in_specs=[pl.BlockSpec((tm, D), lambda i: (i, 0)),
                  pl.BlockSpec((D, DH), lambda i: (0, 0)),
                  pl.BlockSpec((1, DH), lambda i: (0, 0))],
        out_specs=[pl.BlockSpec((tm, DH), lambda i: (i, 0)),
                   pl.BlockSpec((8, DH), lambda i: (0, 0))],
        out_shape=[jax.ShapeDtypeStruct((n, DH), f32), jax.ShapeDtypeStruct((8, DH), f32)],
        scratch_shapes=[pltpu.VMEM((1, DH), f32)],
        compiler_params=_cparams("arbitrary"),
        name="router",
    )(hh, wr_pad, br_pad)


def _moe_up_kernel(te_ref, x_ref, w1_ref, w3_ref, a_ref):
    del te_ref
    x = x_ref[...].astype(bf16)
    a = jnp.dot(x, w1_ref[...].astype(bf16), preferred_element_type=f32)
    b = jnp.dot(x, w3_ref[...].astype(bf16), preferred_element_type=f32)
    a_ref[...] = (a * _sigmoid(a) * b).astype(a_ref.dtype)


def _moe_down_kernel(te_ref, a_ref, w2_ref, y_ref):
    del te_ref
    y_ref[...] = jnp.dot(a_ref[...], w2_ref[...].astype(bf16), preferred_element_type=f32)


def moe_experts(tile_expert, xs, w1, w3, w2, lyr):
    r = xs.shape[0]
    nt = r // MOE_TM
    tf = 1408
    up = pl.pallas_call(
        _moe_up_kernel,
        grid_spec=pltpu.PrefetchScalarGridSpec(
            num_scalar_prefetch=1,
            grid=(D_FF // tf, nt),
            in_specs=[pl.BlockSpec((MOE_TM, D), lambda j, i, te: (i, 0)),
                      pl.BlockSpec((None, None, D, tf), lambda j, i, te: (lyr, te[i], 0, j)),
                      pl.BlockSpec((None, None, D, tf), lambda j, i, te: (lyr, te[i], 0, j))],
            out_specs=pl.BlockSpec((MOE_TM, tf), lambda j, i, te: (i, j))),
        out_shape=jax.ShapeDtypeStruct((r, D_FF), bf16),
        compiler_params=_cparams("arbitrary", "arbitrary"),
        name="moe_up",
    )(tile_expert, xs, w1, w3)
    return pl.pallas_call(
        _moe_down_kernel,
        grid_spec=pltpu.PrefetchScalarGridSpec(
            num_scalar_prefetch=1,
            grid=(nt,),
            in_specs=[pl.BlockSpec((MOE_TM, D_FF), lambda i, te: (i, 0)),
                      pl.BlockSpec((None, None, D_FF, D), lambda i, te: (lyr, te[i], 0, 0))],
            out_specs=pl.BlockSpec((MOE_TM, D), lambda i, te: (i, 0))),
        out_shape=jax.ShapeDtypeStruct((r, D), f32),
        compiler_params=_cparams("arbitrary"),
        name="moe_down",
    )(tile_expert, up, w2)


def _combine_kernel(ya_ref, yb_ref, rt_ref, x1_ref, mod_ref, modn_ref, lg_ref, lb_ref, x2_ref, hn_ref):
    f = rt_ref[:, 2:3] * ya_ref[...] + rt_ref[:, 3:4] * yb_ref[...]
    _post_ffn(f, x1_ref[...], mod_ref, modn_ref, lg_ref[...], lb_ref[...], x2_ref, hn_ref)


def combine(ya, yb, rt, x1, mod_l, mod_next, lg, lb):
    n = x1.shape[0]
    tm = 1024
    grp = _group_of_tile(tm)
    row = lambda i: (i, 0)
    const = lambda i: (0, 0)
    return pl.pallas_call(
        _combine_kernel,
        grid=(n // tm,),
        in_specs=[pl.BlockSpec((tm, D), row), pl.BlockSpec((tm, D), row),
                  pl.BlockSpec((tm, DH), row), pl.BlockSpec((tm, D), row),
                  pl.BlockSpec((None, 8, D), lambda i: (grp(i), 0, 0)),
                  pl.BlockSpec((None, 8, D), lambda i: (grp(i), 0, 0)),
                  pl.BlockSpec((1, D), const), pl.BlockSpec((1, D), const)],
        out_specs=[pl.BlockSpec((tm, D), row), pl.BlockSpec((tm, D), row)],
        out_shape=[jax.ShapeDtypeStruct((n, D), f32), jax.ShapeDtypeStruct((n, D), bf16)],
        compiler_params=_cparams("parallel"),
        name="combine",
    )(ya, yb, rt, x1, mod_l, mod_next, lg.reshape(1, D), lb.reshape(1, D))


def moe_dispatch(rt, counts, n):
    r_total = 2 * n + N_EXPERTS * MOE_TM
    nt = r_total // MOE_TM
    cnt = counts[0, :N_EXPERTS].astype(jnp.int32)
    padded = (cnt + MOE_TM - 1) // MOE_TM * MOE_TM
    ends = jnp.cumsum(padded)
    starts = ends - padded
    e_idx = rt[:, 0:2].astype(jnp.int32)
    rank = rt[:, 4:6].astype(jnp.int32)
    onehot = e_idx[..., None] == jnp.arange(N_EXPERTS)[None, None, :]
    dest = rank + jnp.sum(jnp.where(onehot, starts[None, None, :], 0), axis=-1)
    tok = jnp.broadcast_to(jnp.arange(n, dtype=jnp.int32)[:, None], (n, 2))
    src_tok = jnp.zeros((r_total,), jnp.int32).at[dest.reshape(-1)].set(
        tok.reshape(-1), unique_indices=True, mode="promise_in_bounds")
    tile_start = jnp.arange(nt, dtype=jnp.int32) * MOE_TM
    tile_expert = jnp.minimum(jnp.sum((tile_start[:, None] >= ends[None, :]).astype(jnp.int32), axis=1),
                              N_EXPERTS - 1)
    return src_tok, dest, tile_expert


def _take_rows(a, idx):
    return a.at[idx].get(mode="promise_in_bounds")


def kernel(x_prompt, x_sample, c, state_hgrn, state_mlstm_C, state_mlstm_n, state_mlstm_m, c_ctx,
           w_ada, b_ada, w_in, b_in, hgrn_lb_raw, hgrn_norm_g, conv_w, conv_b, w_mq, w_mk,
           mlstm_fbias, mlstm_norm_g, w_branch_a, w_branch_b, w_out, ln1_g, ln1_b, ln2_g, ln2_b,
           ffn_w1, ffn_w3, ffn_w2, moe_router_w, moe_router_b, moe_w1, moe_w3, moe_w2):
    nb, seq, _ = x_prompt.shape
    db, dseq, _ = x_sample.shape
    n_ctx, n_lat = nb * seq, db * dseq
    n = n_ctx + n_lat
    assert n_ctx == N_CTX and dseq == SEQ_GROUP and db == 2

    lb = jnp.cumsum(jax.nn.softmax(hgrn_lb_raw.astype(f32), axis=1), axis=1)
    lb = lb - lb[:, :1]

    cond8 = jnp.concatenate([c_ctx[None, :], c, jnp.zeros((5, D), f32)], axis=0)
    mod = modulation(cond8, w_ada, b_ada)[:, :3, :].reshape(DEPTH, 3, 6, D)
    mod = jnp.pad(mod, ((0, 0), (0, 0), (0, 2), (0, 0)))

    x = jnp.concatenate([x_prompt.reshape(n_ctx, D), x_sample.reshape(n_lat, D)], axis=0)
    h = modulate(x, mod[0])

    sizes = (D,) * 8 + (4 * NH, D, D)
    offs = [0]
    for s in sizes:
        offs.append(offs[-1] + s)

    def cols(w, idxs):
        return jnp.concatenate([w[..., offs[i]:offs[i + 1]] for i in idxs], axis=-1)

    out_n, out_m = [], []
    buf_a = buf_c = None
    for l in range(DEPTH):
        wl, bl = w_in[l], b_in[l]
        qig = linear(h, cols(wl, (0, 3, 4)).astype(bf16), cols(bl, (0, 3, 4)), bf16, "proj_qig")
        zz = linear(h, cols(wl, (1, 2)).astype(bf16), cols(bl, (1, 2)), f32, "proj_z")
        bb = linear(h, cols(wl, (5, 6, 7)).astype(bf16), cols(bl, (5, 6, 7)), bf16, "proj_b")
        gm = linear(h, cols(wl, (9, 10)).astype(bf16), cols(bl, (9, 10)), bf16, "proj_gm")
        w_g = jnp.pad(cols(wl, (8,)), ((0, 0), (0, DH - 4 * NH))).astype(bf16)
        gates = linear(h, w_g, jnp.pad(cols(bl, (8,)), (0, DH - 4 * NH)), f32, "proj_gates")
        gates_t = gates.T

        lbl = lb[:, l]
        lbp = jnp.concatenate([lbl, 1.0 - lbl, jnp.zeros((4, D), f32)], axis=0)
        gain_a = hgrn_norm_g[l].reshape(1, DH)
        ya, buf_a = hgrn_scan(qig, zz, lbp, gain_a, None, None, buf_a, l, n_seq=nb, T=seq,
                              row_blk0=0, name="hgrn_ctx")
        (ya,) = hgrn_scan(qig, zz, lbp, gain_a, state_hgrn[:, l], ya, None, None, n_seq=db, T=dseq,
                          row_blk0=n_ctx // dseq, name="hgrn_lat")

        cw9 = conv_w[l].reshape(9, D)
        cb = conv_b[l].reshape(1, D)
        wq = w_mq[l].astype(bf16)
        wk = w_mk[l].astype(bf16)
        fb = mlstm_fbias[l]
        zero8 = jnp.zeros((NH,), f32)
        fb_row = jnp.concatenate([zero8, fb[0], zero8, fb[1], jnp.zeros((DH - 4 * NH,), f32)]).reshape(1, DH)
        fb_col = fb.reshape(2, NH, 1)
        gain_b = mlstm_norm_g[l].reshape(1, DH)
        yb, buf_c, s_n, s_m = mlstm_scan(bb, gates, gates_t, cw9, cb, wq, wk, fb_row, fb_col, gain_b,
                                         None, None, buf_c, l, n_seq=nb, T=seq, row_blk0=0,
                                         grid_rows=None, name="mlstm_ctx")
        c0aug = jnp.concatenate([state_mlstm_C[:, l], state_mlstm_n[:, l][..., None],
                                 jnp.zeros((db, 2, NH, DH, DH - 1), f32)], axis=-1)
        m0 = state_mlstm_m[:, l].reshape(db, 2, NH, 1, 1)
        (yb,) = mlstm_scan(bb, gates, gates_t, cw9, cb, wq, wk, fb_row, fb_col, gain_b,
                           (c0aug, m0), yb, None, None, n_seq=db, T=dseq, row_blk0=n_ctx // dseq,
                           grid_rows=dseq // GRID_W, name="mlstm_lat")

        x1, hh = merge(ya, yb, gm, x, mod[l], w_branch_a[l].astype(bf16), w_branch_b[l].astype(bf16),
                       w_out[l].astype(bf16), ln1_g[l], ln1_b[l], bf16 if l % 2 == 0 else f32)

        mod_next = mod[(l + 1) % DEPTH]
        j = l // 2
        if l % 2 == 0:
            x, h = ffn(hh, ffn_w1[j].astype(bf16), ffn_w3[j].astype(bf16), ffn_w2[j].astype(bf16),
                       x1, mod[l], mod_next, ln2_g[l], ln2_b[l])
        else:
            wr = jnp.pad(moe_router_w[j], ((0, 0), (0, DH - N_EXPERTS)))
            br = jnp.concatenate([moe_router_b[j], jnp.full((DH - N_EXPERTS,), -1e30, f32)]).reshape(1, DH)
            rt, counts = router(hh, wr, br)
            src_tok, dest, tile_expert = moe_dispatch(rt, counts, n)
            ys = moe_experts(tile_expert, _take_rows(hh, src_tok), moe_w1, moe_w3, moe_w2, j)
            x, h = combine(_take_rows(ys, dest[:, 0]), _take_rows(ys, dest[:, 1]), rt,
                           x1, mod[l], mod_next, ln2_g[l], ln2_b[l])

        out_n.append(s_n.reshape(nb, 2, NH, DH))
        out_m.append(s_m[..., 0, 0])

    y_prompt = x[:n_ctx].reshape(nb, seq, D)
    y_sample = x[n_ctx:].reshape(db, dseq, D)
    return (y_prompt, y_sample, buf_a, buf_c, jnp.stack(out_n, axis=1), jnp.stack(out_m, axis=1))
```

```python
import functools

import jax
import jax.numpy as jnp
from jax import lax
from jax.experimental import pallas as pl
from jax.experimental.pallas import tpu as pltpu

f32 = jnp.float32
bf16 = jnp.bfloat16

D = 1024
DEPTH = 4
GRID_W = 64
NH = 8
DH = 128
CH_A = 16
HALF_A = 8
SC_A = 128
HPS = 4
CH_B = 256
N_EXPERTS = 8
D_FF = 2816
ALPHA = (2 * DEPTH) ** 0.25
SEQ_GROUP = 1024
LOG2E = 1.4426950408889634
LOG2_F_FLOOR = -1e30
MOE_TM = 512
MOE_PARTS = 3

VMEM_LIMIT = 56 * 1024 * 1024
NT = (((1,), (1,)), ((), ()))
TN = (((0,), (0,)), ((), ()))


def _cparams(*sem):
    return pltpu.CompilerParams(dimension_semantics=sem, vmem_limit_bytes=VMEM_LIMIT)


def _sigmoid(x):
    return 1.0 / (1.0 + jnp.exp(-x))


def _log_sigmoid(t):
    return jnp.minimum(t, 0.0) - jnp.log1p(jnp.exp(-jnp.abs(t)))


def _layer_norm(r, g, b):
    mu = jnp.mean(r, axis=-1, keepdims=True)
    d = r - mu
    var = jnp.mean(d * d, axis=-1, keepdims=True)
    return d * lax.rsqrt(var + 1e-5) * g + b


def _mod_kernel(c_ref, w_ref, b_ref, o_ref):
    c = c_ref[...]
    act = (c * _sigmoid(c)).astype(bf16)
    o_ref[...] = jnp.dot(act, w_ref[...].astype(bf16), preferred_element_type=f32) + b_ref[...]


def modulation(cond8, w_ada, b_ada):
    L, _, n = w_ada.shape
    tn = 1536
    return pl.pallas_call(
        _mod_kernel,
        grid=(L, n // tn),
        in_specs=[pl.BlockSpec((8, D), lambda l, j: (0, 0)),
                  pl.BlockSpec((None, D, tn), lambda l, j: (l, 0, j)),
                  pl.BlockSpec((None, 1, tn), lambda l, j: (l, 0, j))],
        out_specs=pl.BlockSpec((None, 8, tn), lambda l, j: (l, 0, j)),
        out_shape=jax.ShapeDtypeStruct((L, 8, n), f32),
        compiler_params=_cparams("parallel", "parallel"),
        name="modulation",
    )(cond8, w_ada, b_ada.reshape(L, 1, n))


N_CTX = 32 * 256


def _group_of_tile(tm):
    def idx(i):
        return jnp.maximum(i * tm // SEQ_GROUP - (N_CTX // SEQ_GROUP - 1), 0)
    return idx


def _modulate_kernel(x_ref, mod_ref, h_ref):
    h_ref[...] = (x_ref[...] * (1.0 + mod_ref[1:2, :]) + mod_ref[0:1, :]).astype(h_ref.dtype)


def modulate(x, mod_l):
    n = x.shape[0]
    tm = 1024
    grp = _group_of_tile(tm)
    return pl.pallas_call(
        _modulate_kernel,
        grid=(n // tm,),
        in_specs=[pl.BlockSpec((tm, D), lambda i: (i, 0)),
                  pl.BlockSpec((None, 8, D), lambda i: (grp(i), 0, 0))],
        out_specs=pl.BlockSpec((tm, D), lambda i: (i, 0)),
        out_shape=jax.ShapeDtypeStruct((n, D), bf16),
        compiler_params=_cparams("parallel"),
        name="modulate",
    )(x, mod_l)


def _linear_kernel(x_ref, w_ref, b_ref, o_ref):
    acc = jnp.dot(x_ref[...], w_ref[...], preferred_element_type=f32)
    o_ref[...] = (acc + b_ref[...]).astype(o_ref.dtype)


def linear(x, w, b, out_dtype, name):
    m, k = x.shape
    n = w.shape[1]
    tm = 2048
    tn = 1024 if n % 1024 == 0 else n
    return pl.pallas_call(
        _linear_kernel,
        grid=(m // tm, n // tn),
        in_specs=[pl.BlockSpec((tm, k), lambda i, j: (i, 0)),
                  pl.BlockSpec((k, tn), lambda i, j: (0, j)),
                  pl.BlockSpec((1, tn), lambda i, j: (0, j))],
        out_specs=pl.BlockSpec((tm, tn), lambda i, j: (i, j)),
        out_shape=jax.ShapeDtypeStruct((m, n), out_dtype),
        compiler_params=_cparams("parallel", "parallel"),
        name=name,
    )(x, w, b.reshape(1, n))


def _hgrn_kernel(*refs, has_state, want_state, n_aliased):
    q_ref, v_ref, g_ref, zf_ref, zb_ref, lbp_ref, gain_ref = refs[:7]
    pos = 7
    s0_ref = None
    if has_state:
        s0_ref = refs[pos]
        pos += 1
    pos += n_aliased
    y_ref = refs[pos]
    pos += 1
    st_ref = None
    if want_state:
        st_ref = refs[pos]
        pos += 1
    b_ref, kk_ref, o_ref = refs[pos:]

    T = q_ref.shape[0]
    nsc = T // SC_A

    row = lax.broadcasted_iota(jnp.int32, (SC_A, SC_A), 0)
    col = lax.broadcasted_iota(jnp.int32, (SC_A, SC_A), 1)
    tris = tuple(jnp.where(m, 1.0, 0.0).astype(bf16) for m in (row >= col, row <= col))
    nblk = SC_A // CH_A
    half = SC_A // 2
    r_sq = lax.broadcasted_iota(jnp.int32, (half, half), 0)
    c_sq = lax.broadcasted_iota(jnp.int32, (half, half), 1)
    r_w = lax.broadcasted_iota(jnp.int32, (half, SC_A), 0)
    c_w = lax.broadcasted_iota(jnp.int32, (half, SC_A), 1)
    masks_lo = (r_sq >= c_sq, r_w <= c_w)
    masks_hi = (r_w + half >= c_w, r_sq <= c_sq)

    for d, z_ref in enumerate((zf_ref, zb_ref)):
        z = z_ref[...]
        lb = lbp_ref[d:d + 1, :]
        one_m_lb = lbp_ref[2 + d:3 + d, :]
        e = jnp.exp(-jnp.abs(z))
        inv = 1.0 / (1.0 + e)
        sig = jnp.where(z >= 0.0, 1.0, e) * inv
        kk = one_m_lb * (jnp.where(z >= 0.0, e, 1.0) * inv)
        lf2 = jnp.maximum(jnp.log(lb + one_m_lb * sig) * LOG2E, LOG2_F_FLOOR)
        for sc in range(nsc):
            rows = slice(sc * SC_A, (sc + 1) * SC_A)
            b_ref[d, rows, :] = _tri_matmul(tris[d], lf2[rows])
        kk_ref[d] = kk

    if has_state:
        st0 = tuple(s0_ref[d, hh].T for hh in range(HPS) for d in range(2))
    else:
        st0 = tuple(jnp.zeros((DH, DH), f32) for _ in range(2 * HPS))

    def body(c, carry):
        units = []
        for hd in range(2 * HPS):
            hh, d = divmod(hd, 2)
            ls = slice(hh * DH, (hh + 1) * DH)
            cidx = c if d == 0 else nsc - 1 - c
            r = pl.ds(pl.multiple_of(cidx * SC_A, SC_A), SC_A)
            units.append((d, ls, r))

        scores = []
        for d, ls, r in units:
            q = q_ref[r, ls].astype(f32)
            kk = kk_ref[d, r, ls]
            b = b_ref[d, r, ls]
            att_rows = ([], [])
            for i in range(nblk):
                mid = i * CH_A + (HALF_A - 1 if d == 0 else HALF_A)
                r_i = b[mid:mid + 1, :]
                blk = slice(i * CH_A, (i + 1) * CH_A)
                qi = (q[blk] * jnp.exp2(b[blk] - r_i)).astype(bf16)
                if d == 0 and i < nblk // 2:
                    ksl = slice(0, half)
                elif d == 1 and i >= nblk // 2:
                    ksl = slice(half, SC_A)
                else:
                    ksl = slice(0, SC_A)
                ks = (kk[ksl] * jnp.exp2(r_i - b[ksl])).astype(bf16)
                att_rows[i >= nblk // 2].append(lax.dot_general(qi, ks, NT, preferred_element_type=f32))
            scores.append(att_rows)

        for hd, (d, ls, r) in enumerate(units):
            v = v_ref[r, ls]
            a_lo = jnp.where(masks_lo[d], jnp.concatenate(scores[hd][0], axis=0), 0.0).astype(bf16)
            a_hi = jnp.where(masks_hi[d], jnp.concatenate(scores[hd][1], axis=0), 0.0).astype(bf16)
            o_lo = jnp.dot(a_lo, v[:half] if d == 0 else v, preferred_element_type=f32)
            o_hi = jnp.dot(a_hi, v if d == 0 else v[half:], preferred_element_type=f32)
            qb = (q_ref[r, ls].astype(f32) * jnp.exp2(b_ref[d, r, ls])).astype(bf16)
            o_ref[d, r, ls] = (jnp.concatenate([o_lo, o_hi], axis=0)
                               + lax.dot_general(qb, carry[hd].astype(bf16), NT,
                                                 preferred_element_type=f32))

        new = []
        for hd, (d, ls, r) in enumerate(units):
            b = b_ref[d, r, ls]
            b_end = b[SC_A - 1:SC_A, :] if d == 0 else b[0:1, :]
            kl = (kk_ref[d, r, ls] * jnp.exp2(b_end - b)).astype(bf16)
            new.append(jnp.exp2(b_end) * carry[hd]
                       + lax.dot_general(v_ref[r, ls], kl, TN, preferred_element_type=f32))
        return tuple(new)

    st_fin = lax.fori_loop(0, nsc, body, st0, unroll=2)

    for hh in range(HPS):
        ls = slice(hh * DH, (hh + 1) * DH)
        o = o_ref[0, :, ls] + o_ref[1, :, ls]
        ms = jnp.mean(o * o, axis=-1, keepdims=True)
        g = g_ref[:, ls].astype(f32)
        y_ref[:, ls] = (o * lax.rsqrt(ms + 1e-6) * gain_ref[...] * (g * _sigmoid(g))).astype(y_ref.dtype)
        if want_state:
            for d in range(2):
                st_ref[d, hh] = st_fin[2 * hh + d].T


def hgrn_scan(qig, zz, lbp, gain, s0, y_buf, state_buf, layer, *, n_seq, T, row_blk0, name):
    has_state = s0 is not None
    want_state = layer is not None

    wh = HPS * DH

    def col(off):
        return lambda b, h: (row_blk0 + b, off // HPS + h)

    in_specs = [pl.BlockSpec((T, wh), col(0)), pl.BlockSpec((T, wh), col(NH)),
                pl.BlockSpec((T, wh), col(2 * NH)),
                pl.BlockSpec((T, wh), col(0)), pl.BlockSpec((T, wh), col(NH)),
                pl.BlockSpec((8, wh), lambda b, h: (0, h)),
                pl.BlockSpec((1, DH), lambda b, h: (0, 0))]
    args = [qig, qig, qig, zz, zz, lbp, gain]
    if has_state:
        in_specs.append(pl.BlockSpec((None, 2, HPS, DH, DH), lambda b, h: (b, 0, h, 0, 0)))
        args.append(s0)
    out_specs = [pl.BlockSpec((T, wh), col(0))]
    out_shape = [jax.ShapeDtypeStruct((qig.shape[0], D), bf16)]
    aliases = {}
    if y_buf is not None:
        in_specs.append(pl.BlockSpec(memory_space=pl.ANY))
        args.append(y_buf)
        aliases[len(args) - 1] = 0
    if want_state:
        out_specs.append(pl.BlockSpec((None, None, 2, HPS, DH, DH),
                                      lambda b, h: (b, layer, 0, h, 0, 0)))
        out_shape.append(jax.ShapeDtypeStruct((n_seq, DEPTH, 2, NH, DH, DH), f32))
        if state_buf is not None:
            in_specs.append(pl.BlockSpec(memory_space=pl.ANY))
            args.append(state_buf)
            aliases[len(args) - 1] = 1
    scratch = [pltpu.VMEM((2, T, wh), f32)] * 3
    return pl.pallas_call(
        functools.partial(_hgrn_kernel, has_state=has_state, want_state=want_state,
                          n_aliased=len(aliases)),
        grid=(n_seq, NH // HPS),
        in_specs=in_specs, out_specs=out_specs, out_shape=out_shape,
        scratch_shapes=scratch,
        input_output_aliases=aliases,
        compiler_params=_cparams("parallel", "parallel"),
        name=name,
    )(*args)


def _split3(x):
    x0 = x.astype(bf16)
    r = x - x0.astype(f32)
    x1 = r.astype(bf16)
    x2 = (r - x1.astype(f32)).astype(bf16)
    return x0, x1, x2


def _tri_matmul(tri, x):
    return sum(jnp.dot(tri, p, preferred_element_type=f32) for p in _split3(x))


def _matmul_tri(x, tri):
    return sum(jnp.dot(p, tri, preferred_element_type=f32) for p in _split3(x))


def _mlstm_kernel(*refs, grid_rows, has_state, want_state, n_aliased):
    x_ref, v_ref, og_ref, gc_ref, gr_ref, cw_ref, cb_ref, wq_ref, wk_ref, fbr_ref, fbc_ref, gain_ref = refs[:12]
    pos = 12
    if has_state:
        c0_ref, m0_ref = refs[pos:pos + 2]
        pos += 2
    pos += n_aliased
    y_ref = refs[pos]
    pos += 1
    if want_state:
        ct_ref, nt_ref, mt_ref = refs[pos:pos + 3]

    head0 = pl.program_id(1) * HPS
    T = x_ref.shape[0]
    L = min(CH_B, T)
    nchunk = T // L

    t_idx = lax.broadcasted_iota(jnp.int32, (T, DH), 0)
    lane = lax.broadcasted_iota(jnp.int32, (T, DH), 1)
    ones_col = jnp.where(lane == 0, 1.0, 0.0)
    if grid_rows is None:
        taps = [(0, dc) for dc in (-1, 0, 1)]
    else:
        taps = [(dr, dc) for dr in (-1, 0, 1) for dc in (-1, 0, 1)]
        c_idx = t_idx % GRID_W
    cas, vaugs = [], []
    for hh in range(HPS):
        ls = slice(hh * DH, (hh + 1) * DH)
        x = x_ref[:, ls].astype(f32)
        acc = jnp.zeros((T, DH), f32) + cb_ref[:, ls]
        for dr, dc in taps:
            off = dr * GRID_W + dc
            tap = (dr + 1) * 3 + (dc + 1)
            w = cw_ref[tap:tap + 1, ls]
            xs = x if off == 0 else pltpu.roll(x, (-off) % T, axis=0)
            ok = (t_idx + off >= 0) & (t_idx + off < T)
            if grid_rows is not None:
                ok = ok & (c_idx + dc >= 0) & (c_idx + dc < GRID_W)
            acc = acc + jnp.where(ok, xs, 0.0) * w
        cas.append((acc * _sigmoid(acc)).astype(bf16))
        vaugs.append(jnp.concatenate([v_ref[:, ls].astype(f32), ones_col], axis=1))
    qs = [jnp.dot(cas[hh], wq_ref[hh], preferred_element_type=f32).astype(bf16) for hh in range(HPS)]
    ks = [(jnp.dot(cas[hh], wk_ref[hh], preferred_element_type=f32) * (DH ** -0.5)).astype(bf16)
          for hh in range(HPS)]

    gate_lane = lax.broadcasted_iota(jnp.int32, (T, DH), 1) // NH
    is_f = (gate_lane == 1) | (gate_lane == 3)
    gc = gc_ref[...] + fbr_ref[...]
    gc = jnp.where(is_f, _log_sigmoid(gc), gc)

    def pick_lane(a, ln):
        sel = lax.broadcasted_iota(jnp.int32, a.shape, 1) == ln
        return jnp.sum(jnp.where(sel, a, 0.0), axis=1, keepdims=True)

    def pick_row(a, rw):
        sel = lax.broadcasted_iota(jnp.int32, a.shape, 0) == rw
        return jnp.sum(jnp.where(sel, a, 0.0), axis=0, keepdims=True)

    r_i = lax.broadcasted_iota(jnp.int32, (L, L), 0)
    c_i = lax.broadcasted_iota(jnp.int32, (L, L), 1)
    lower = r_i >= c_i
    upper = r_i <= c_i
    tri_lo = jnp.where(lower, 1.0, 0.0).astype(bf16)
    tri_up = jnp.where(upper, 1.0, 0.0).astype(bf16)

    h_parts = [[[None] * nchunk for _ in range(2)] for _ in range(HPS)]
    units = [(hh, d) for hh in range(HPS) for d in range(2)]
    if has_state:
        carry = {u: (c0_ref[u[1], u[0]], m0_ref[u[1], u[0]]) for u in units}
    else:
        carry = {u: (jnp.zeros((DH, 2 * DH), f32), jnp.zeros((1, 1), f32)) for u in units}
    masks = (lower, upper)
    tri_cs = (tri_lo, tri_up)
    tri_rs = (tri_up, tri_lo)

    for step in range(nchunk):
        cidx = (step, nchunk - 1 - step)
        rsl = tuple(slice(c * L, (c + 1) * L) for c in cidx)

        g = {}
        for d in range(2):
            gcc = gc[rsl[d]]
            lf_r8 = _log_sigmoid(gr_ref[(2 * d + 1) * NH:(2 * d + 2) * NH, rsl[d]] + fbc_ref[d])
            g[d] = (gcc, _tri_matmul(tri_cs[d], gcc), _matmul_tri(lf_r8, tri_rs[d]),
                    gr_ref[2 * d * NH:(2 * d + 1) * NH, rsl[d]])
        picks = {}
        for hh, d in units:
            head = head0 + hh
            gcc, F_c_all, F_r_all, ig_r_all = g[d]
            ig_c = pick_lane(gcc, 2 * d * NH + head)
            F_c = pick_lane(F_c_all, (2 * d + 1) * NH + head)
            ig_r = pick_row(ig_r_all, head)
            F_r = pick_row(F_r_all, head)
            F_end = F_c[L - 1:L, :] if d == 0 else F_c[0:1, :]
            picks[hh, d] = (ig_c, F_c, ig_r, F_r, F_end)

        decay = {}
        for u in units:
            ig_c, F_c, ig_r, F_r, F_end = picks[u]
            m_prev = carry[u][1]
            logd = jnp.where(masks[u[1]], F_c - F_r + ig_r, -jnp.inf)
            log_inter = F_c + m_prev
            m_q = jnp.maximum(log_inter, jnp.max(logd, axis=-1, keepdims=True))
            decay[u] = (jnp.exp(logd - m_q), jnp.exp(log_inter - m_q), m_q)

        scores = {}
        for hh, d in units:
            qc, kc = qs[hh][rsl[d]], ks[hh][rsl[d]]
            scores[hh, d] = (lax.dot_general(qc, kc, NT, preferred_element_type=f32)
                             * decay[hh, d][0]).astype(bf16)

        for hh, d in units:
            _, a, m_q = decay[hh, d]
            qc = qs[hh][rsl[d]]
            va = vaugs[hh][rsl[d]]
            nd = (a * jnp.dot(qc, carry[hh, d][0].astype(bf16), preferred_element_type=f32)
                  + jnp.dot(scores[hh, d], va.astype(bf16), preferred_element_type=f32))
            num = nd[:, :DH]
            den = nd[:, DH:DH + 1]
            h_parts[hh][d][cidx[d]] = num / jnp.maximum(jnp.abs(den), jnp.exp(-m_q))

        for hh, d in units:
            ig_c, F_c, ig_r, F_r, F_end = picks[hh, d]
            caug, m_prev = carry[hh, d]
            log_w_c = F_end - F_c + ig_c
            log_w_r = F_end - F_r + ig_r
            m_new = jnp.maximum(F_end + m_prev, jnp.max(log_w_r, axis=-1, keepdims=True))
            ws_c = jnp.exp(log_w_c - m_new)
            d0 = jnp.exp(F_end + m_prev - m_new)
            upd = lax.dot_general(ks[hh][rsl[d]], (ws_c * vaugs[hh][rsl[d]]).astype(bf16), TN,
                                  preferred_element_type=f32)
            carry[hh, d] = (d0 * caug + upd, m_new)
    finals = [[carry[hh, d] for d in range(2)] for hh in range(HPS)]

    for hh in range(HPS):
        ls = slice(hh * DH, (hh + 1) * DH)
        hsum = jnp.concatenate([h_parts[hh][0][c] + h_parts[hh][1][c] for c in range(nchunk)], axis=0)
        mu = jnp.mean(hsum, axis=-1, keepdims=True)
        dv = hsum - mu
        var = jnp.mean(dv * dv, axis=-1, keepdims=True)
        og = og_ref[:, ls].astype(f32)
        y_ref[:, ls] = (dv * lax.rsqrt(var + 1e-6) * gain_ref[...] * _sigmoid(og)).astype(y_ref.dtype)
        if want_state:
            for d in range(2):
                caug, m_fin = finals[hh][d]
                ct_ref[d, hh] = caug[:, :DH]
                nt_ref[d, hh] = caug[:, DH:2 * DH].T[0:1, :]
                mt_ref[d, hh] = jnp.broadcast_to(m_fin, (1, DH))


def mlstm_scan(bb, gates, gates_t, conv_w9, conv_b, wq, wk, fb_row, fb_col, gain, state, y_buf, c_buf,
               layer, *, n_seq, T, row_blk0, grid_rows, name):
    has_state = state is not None
    want_state = layer is not None

    wh = HPS * DH

    def col(off):
        return lambda b, h: (row_blk0 + b, off // HPS + h)

    in_specs = [pl.BlockSpec((T, wh), col(0)), pl.BlockSpec((T, wh), col(NH)),
                pl.BlockSpec((T, wh), col(2 * NH)),
                pl.BlockSpec((T, DH), lambda b, h: (row_blk0 + b, 0)),
                pl.BlockSpec((DH, T), lambda b, h: (0, row_blk0 + b)),
                pl.BlockSpec((9, wh), lambda b, h: (0, h)),
                pl.BlockSpec((1, wh), lambda b, h: (0, h)),
                pl.BlockSpec((HPS, DH, DH), lambda b, h: (h, 0, 0)),
                pl.BlockSpec((HPS, DH, DH), lambda b, h: (h, 0, 0)),
                pl.BlockSpec((1, DH), lambda b, h: (0, 0)),
                pl.BlockSpec((2, NH, 1), lambda b, h: (0, 0, 0)),
                pl.BlockSpec((1, DH), lambda b, h: (0, 0))]
    args = [bb, bb, bb, gates, gates_t, conv_w9, conv_b, wq, wk, fb_row, fb_col, gain]
    if has_state:
        c0aug, m0 = state
        in_specs += [pl.BlockSpec((None, 2, HPS, DH, 2 * DH), lambda b, h: (b, 0, h, 0, 0)),
                     pl.BlockSpec((None, 2, HPS, 1, 1), lambda b, h: (b, 0, h, 0, 0))]
        args += [c0aug, m0]
    out_specs = [pl.BlockSpec((T, wh), col(0))]
    out_shape = [jax.ShapeDtypeStruct((bb.shape[0], D), bf16)]
    aliases = {}
    if y_buf is not None:
        in_specs.append(pl.BlockSpec(memory_space=pl.ANY))
        args.append(y_buf)
        aliases[len(args) - 1] = 0
    if want_state:
        out_specs += [pl.BlockSpec((None, None, 2, HPS, DH, DH), lambda b, h: (b, layer, 0, h, 0, 0)),
                      pl.BlockSpec((None, 2, HPS, 1, DH), lambda b, h: (b, 0, h, 0, 0)),
                      pl.BlockSpec((None, 2, HPS, 1, DH), lambda b, h: (b, 0, h, 0, 0))]
        out_shape += [jax.ShapeDtypeStruct((n_seq, DEPTH, 2, NH, DH, DH), f32),
                      jax.ShapeDtypeStruct((n_seq, 2, NH, 1, DH), f32),
                      jax.ShapeDtypeStruct((n_seq, 2, NH, 1, DH), f32)]
        if c_buf is not None:
            in_specs.append(pl.BlockSpec(memory_space=pl.ANY))
            args.append(c_buf)
            aliases[len(args) - 1] = 1
    return pl.pallas_call(
        functools.partial(_mlstm_kernel, grid_rows=grid_rows, has_state=has_state,
                          want_state=want_state, n_aliased=len(aliases)),
        grid=(n_seq, NH // HPS),
        in_specs=in_specs, out_specs=out_specs, out_shape=out_shape,
        input_output_aliases=aliases,
        compiler_params=_cparams("parallel", "parallel"),
        name=name,
    )(*args)


def _merge_kernel(ya_ref, yb_ref, gm_ref, x_ref, mod_ref, wa_ref, wb_ref, wo_ref, lg_ref, lb_ref,
                  x1_ref, hh_ref):
    pa = jnp.dot(ya_ref[...], wa_ref[...], preferred_element_type=f32)
    pb = jnp.dot(yb_ref[...], wb_ref[...], preferred_element_type=f32)
    gma = gm_ref[:, :D].astype(f32)
    gmb = gm_ref[:, D:].astype(f32)
    merged = (_sigmoid(gma) * pa + _sigmoid(gmb) * pb).astype(bf16)
    y = jnp.dot(merged, wo_ref[...], preferred_element_type=f32)
    x1 = _layer_norm(ALPHA * x_ref[...] + mod_ref[2:3, :] * y, lg_ref[...], lb_ref[...])
    x1_ref[...] = x1
    hh_ref[...] = (x1 * (1.0 + mod_ref[4:5, :]) + mod_ref[3:4, :]).astype(hh_ref.dtype)


def merge(ya, yb, gm, x, mod_l, wa, wb, wo, lg, lb, hh_dtype):
    n = x.shape[0]
    tm = 512
    grp = _group_of_tile(tm)
    row = lambda i: (i, 0)
    const = lambda i: (0, 0)
    return pl.pallas_call(
        _merge_kernel,
        grid=(n // tm,),
        in_specs=[pl.BlockSpec((tm, D), row), pl.BlockSpec((tm, D), row),
                  pl.BlockSpec((tm, 2 * D), row), pl.BlockSpec((tm, D), row),
                  pl.BlockSpec((None, 8, D), lambda i: (grp(i), 0, 0)),
                  pl.BlockSpec((D, D), const), pl.BlockSpec((D, D), const), pl.BlockSpec((D, D), const),
                  pl.BlockSpec((1, D), const), pl.BlockSpec((1, D), const)],
        out_specs=[pl.BlockSpec((tm, D), row), pl.BlockSpec((tm, D), row)],
        out_shape=[jax.ShapeDtypeStruct((n, D), f32), jax.ShapeDtypeStruct((n, D), hh_dtype)],
        compiler_params=_cparams("parallel"),
        name="merge",
    )(ya, yb, gm, x, mod_l, wa, wb, wo, lg.reshape(1, D), lb.reshape(1, D))


def _post_ffn(f, x1, mod, mod_next, lg, lb, x2_ref, hn_ref):
    x2 = _layer_norm(ALPHA * x1 + mod[5:6, :] * f, lg, lb)
    x2_ref[...] = x2
    hn_ref[...] = (x2 * (1.0 + mod_next[1:2, :]) + mod_next[0:1, :]).astype(hn_ref.dtype)


def _ffn_kernel(hh_ref, w1_ref, w3_ref, w2_ref, x1_ref, mod_ref, modn_ref, lg_ref, lb_ref,
                x2_ref, hn_ref, acc_ref):
    j = pl.program_id(1)
    hh = hh_ref[...]
    a = jnp.dot(hh, w1_ref[...], preferred_element_type=f32)
    b = jnp.dot(hh, w3_ref[...], preferred_element_type=f32)
    part = jnp.dot((a * _sigmoid(a) * b).astype(bf16), w2_ref[...], preferred_element_type=f32)

    @pl.when(j == 0)
    def _():
        acc_ref[...] = part

    @pl.when(j != 0)
    def _():
        acc_ref[...] += part

    @pl.when(j == pl.num_programs(1) - 1)
    def _():
        _post_ffn(acc_ref[...], x1_ref[...], mod_ref, modn_ref, lg_ref[...], lb_ref[...],
                  x2_ref, hn_ref)


def ffn(hh, w1, w3, w2, x1, mod_l, mod_next, lg, lb):
    n = hh.shape[0]
    tm, tf = 512, 1408
    grp = _group_of_tile(tm)
    row = lambda i, j: (i, 0)
    const = lambda i, j: (0, 0)
    return pl.pallas_call(
        _ffn_kernel,
        grid=(n // tm, D_FF // tf),
        in_specs=[pl.BlockSpec((tm, D), row),
                  pl.BlockSpec((D, tf), lambda i, j: (0, j)),
                  pl.BlockSpec((D, tf), lambda i, j: (0, j)),
                  pl.BlockSpec((tf, D), lambda i, j: (j, 0)),
                  pl.BlockSpec((tm, D), row),
                  pl.BlockSpec((None, 8, D), lambda i, j: (grp(i), 0, 0)),
                  pl.BlockSpec((None, 8, D), lambda i, j: (grp(i), 0, 0)),
                  pl.BlockSpec((1, D), const), pl.BlockSpec((1, D), const)],
        out_specs=[pl.BlockSpec((tm, D), row), pl.BlockSpec((tm, D), row)],
        out_shape=[jax.ShapeDtypeStruct((n, D), f32), jax.ShapeDtypeStruct((n, D), bf16)],
        scratch_shapes=[pltpu.VMEM((tm, D), f32)],
        compiler_params=_cparams("parallel", "arbitrary"),
        name="ffn",
    )(hh, w1, w3, w2, x1, mod_l, mod_next, lg.reshape(1, D), lb.reshape(1, D))


def _router_kernel(hh_ref, wr_ref, br_ref, o_ref, cnt_ref, run_ref):
    logits = jnp.dot(hh_ref[...].astype(f32), wr_ref[...], preferred_element_type=f32,
                     precision=lax.Precision.HIGHEST) + br_ref[...]
    lane = lax.broadcasted_iota(jnp.int32, logits.shape, 1).astype(f32)
    m1 = jnp.max(logits, axis=-1, keepdims=True)
    i1 = jnp.min(jnp.where(logits == m1, lane, float(DH)), axis=-1, keepdims=True)
    rest = jnp.where(lane == i1, -jnp.inf, logits)
    m2 = jnp.max(rest, axis=-1, keepdims=True)
    i2 = jnp.min(jnp.where(rest == m2, lane, float(DH)), axis=-1, keepdims=True)
    e21 = jnp.exp(m2 - m1)
    p1 = 1.0 / (1.0 + e21)
    p2 = e21 * p1

    @pl.when(pl.program_id(0) == 0)
    def _():
        run_ref[...] = jnp.zeros_like(run_ref)

    tm = logits.shape[0]
    r_i = lax.broadcasted_iota(jnp.int32, (tm, tm), 0)
    c_i = lax.broadcasted_iota(jnp.int32, (tm, tm), 1)
    earlier = jnp.where(c_i < r_i, 1.0, 0.0).astype(bf16)
    oh1 = jnp.where(lane == i1, 1.0, 0.0)
    oh2 = jnp.where(lane == i2, 1.0, 0.0)
    pre1 = jnp.dot(earlier, oh1.astype(bf16), preferred_element_type=f32)
    pre2 = jnp.dot(earlier, oh2.astype(bf16), preferred_element_type=f32)
    cnt1 = jnp.sum(oh1, axis=0, keepdims=True)
    cnt2 = jnp.sum(oh2, axis=0, keepdims=True)
    run = run_ref[...]
    rank1 = jnp.sum(oh1 * (run + pre1), axis=-1, keepdims=True)
    rank2 = jnp.sum(oh2 * (run + cnt1 + pre2), axis=-1, keepdims=True)
    run_ref[...] = run + cnt1 + cnt2
    cnt_ref[...] = jnp.broadcast_to(run + cnt1 + cnt2, cnt_ref.shape)
    o_ref[...] = (jnp.where(lane == 0.0, i1, 0.0) + jnp.where(lane == 1.0, i2, 0.0)
                  + jnp.where(lane == 2.0, p1, 0.0) + jnp.where(lane == 3.0, p2, 0.0)
                  + jnp.where(lane == 4.0, rank1, 0.0) + jnp.where(lane == 5.0, rank2, 0.0))


def router(hh, wr_pad, br_pad):
    n = hh.shape[0]
    tm = 1024
    return pl.pallas_call(
        _router_kernel,
        grid=(n // tm,),
        in_specs=[pl.BlockSpec((tm, D), lambda i: (i, 0)),
                  pl.BlockSpec((D, DH), lambda i: (0, 0)),
                  pl.BlockSpec((1, DH), lambda i: (0, 0))],
        out_specs=[pl.BlockSpec((tm, DH), lambda i: (i, 0)),
                   pl.BlockSpec((8, DH), lambda i: (0, 0))],
        out_shape=[jax.ShapeDtypeStruct((n, DH), f32), jax.ShapeDtypeStruct((8, DH), f32)],
        scratch_shapes=[pltpu.VMEM((1, DH), f32)],
        compiler_params=_cparams("arbitrary"),
        name="router",
    )(hh, wr_pad, br_pad)


def _moe_up_kernel(te_ref, x_ref, w1_ref, w3_ref, a_ref):
    del te_ref
    x = x_ref[...].astype(bf16)
    a = jnp.dot(x, w1_ref[...].astype(bf16), preferred_element_type=f32)
    b = jnp.dot(x, w3_ref[...].astype(bf16), preferred_element_type=f32)
    a_ref[...] = (a * _sigmoid(a) * b).astype(a_ref.dtype)


def _moe_down_kernel(te_ref, a_ref, w2_ref, *rest):
    del te_ref
    y_ref = rest[-1]
    y_ref[...] = jnp.dot(a_ref[...], w2_ref[...].astype(bf16), preferred_element_type=f32)


def moe_experts(tile_expert, xs, w1, w3, w2, lyr, ys_buf, tile0, r_total):
    r = xs.shape[0]
    nt = r // MOE_TM
    tf = 1408
    up = pl.pallas_call(
        _moe_up_kernel,
        grid_spec=pltpu.PrefetchScalarGridSpec(
            num_scalar_prefetch=1,
            grid=(D_FF // tf, nt),
            in_specs=[pl.BlockSpec((MOE_TM, D), lambda j, i, te: (i, 0)),
                      pl.BlockSpec((None, None, D, tf), lambda j, i, te: (lyr, te[i], 0, j)),
                      pl.BlockSpec((None, None, D, tf), lambda j, i, te: (lyr, te[i], 0, j))],
            out_specs=pl.BlockSpec((MOE_TM, tf), lambda j, i, te: (i, j))),
        out_shape=jax.ShapeDtypeStruct((r, D_FF), bf16),
        compiler_params=_cparams("arbitrary", "arbitrary"),
        name="moe_up",
    )(tile_expert, xs, w1, w3)
    in_specs = [pl.BlockSpec((MOE_TM, D_FF), lambda i, te: (i, 0)),
                pl.BlockSpec((None, None, D_FF, D), lambda i, te: (lyr, te[i], 0, 0))]
    args = [tile_expert, up, w2]
    aliases = {}
    if ys_buf is not None:
        in_specs.append(pl.BlockSpec(memory_space=pl.ANY))
        args.append(ys_buf)
        aliases = {len(args) - 1: 0}
    return pl.pallas_call(
        _moe_down_kernel,
        grid_spec=pltpu.PrefetchScalarGridSpec(
            num_scalar_prefetch=1,
            grid=(nt,),
            in_specs=in_specs,
            out_specs=pl.BlockSpec((MOE_TM, D), lambda i, te: (tile0 + i, 0))),
        out_shape=jax.ShapeDtypeStruct((r_total, D), f32),
        input_output_aliases=aliases,
        compiler_params=_cparams("arbitrary"),
        name="moe_down",
    )(*args)


def _combine_kernel(ya_ref, yb_ref, rt_ref, x1_ref, mod_ref, modn_ref, lg_ref, lb_ref, x2_ref, hn_ref):
    f = rt_ref[:, 2:3] * ya_ref[...] + rt_ref[:, 3:4] * yb_ref[...]
    _post_ffn(f, x1_ref[...], mod_ref, modn_ref, lg_ref[...], lb_ref[...], x2_ref, hn_ref)


def combine(ya, yb, rt, x1, mod_l, mod_next, lg, lb):
    n = x1.shape[0]
    tm = 1024
    grp = _group_of_tile(tm)
    row = lambda i: (i, 0)
    const = lambda i: (0, 0)
    return pl.pallas_call(
        _combine_kernel,
        grid=(n // tm,),
        in_specs=[pl.BlockSpec((tm, D), row), pl.BlockSpec((tm, D), row),
                  pl.BlockSpec((tm, DH), row), pl.BlockSpec((tm, D), row),
                  pl.BlockSpec((None, 8, D), lambda i: (grp(i), 0, 0)),
                  pl.BlockSpec((None, 8, D), lambda i: (grp(i), 0, 0)),
                  pl.BlockSpec((1, D), const), pl.BlockSpec((1, D), const)],
        out_specs=[pl.BlockSpec((tm, D), row), pl.BlockSpec((tm, D), row)],
        out_shape=[jax.ShapeDtypeStruct((n, D), f32), jax.ShapeDtypeStruct((n, D), bf16)],
        compiler_params=_cparams("parallel"),
        name="combine",
    )(ya, yb, rt, x1, mod_l, mod_next, lg.reshape(1, D), lb.reshape(1, D))


def moe_dispatch(rt, counts, n):
    r_total = 2 * n + N_EXPERTS * MOE_TM
    nt = r_total // MOE_TM
    cnt = counts[0, :N_EXPERTS].astype(jnp.int32)
    padded = (cnt + MOE_TM - 1) // MOE_TM * MOE_TM
    ends = jnp.cumsum(padded)
    starts = ends - padded
    e_idx = rt[:, 0:2].astype(jnp.int32)
    rank = rt[:, 4:6].astype(jnp.int32)
    onehot = e_idx[..., None] == jnp.arange(N_EXPERTS)[None, None, :]
    dest = rank + jnp.sum(jnp.where(onehot, starts[None, None, :], 0), axis=-1)
    tok = jnp.broadcast_to(jnp.arange(n, dtype=jnp.int32)[:, None], (n, 2))
    src_tok = jnp.zeros((r_total,), jnp.int32).at[dest.reshape(-1)].set(
        tok.reshape(-1), unique_indices=True, mode="promise_in_bounds")
    tile_start = jnp.arange(nt, dtype=jnp.int32) * MOE_TM
    tile_expert = jnp.minimum(jnp.sum((tile_start[:, None] >= ends[None, :]).astype(jnp.int32), axis=1),
                              N_EXPERTS - 1)
    return src_tok, dest, tile_expert


def _take_rows(a, idx):
    return a.at[idx].get(mode="promise_in_bounds")


def kernel(x_prompt, x_sample, c, state_hgrn, state_mlstm_C, state_mlstm_n, state_mlstm_m, c_ctx,
           w_ada, b_ada, w_in, b_in, hgrn_lb_raw, hgrn_norm_g, conv_w, conv_b, w_mq, w_mk,
           mlstm_fbias, mlstm_norm_g, w_branch_a, w_branch_b, w_out, ln1_g, ln1_b, ln2_g, ln2_b,
           ffn_w1, ffn_w3, ffn_w2, moe_router_w, moe_router_b, moe_w1, moe_w3, moe_w2):
    nb, seq, _ = x_prompt.shape
    db, dseq, _ = x_sample.shape
    n_ctx, n_lat = nb * seq, db * dseq
    n = n_ctx + n_lat
    assert n_ctx == N_CTX and dseq == SEQ_GROUP and db == 2

    lb = jnp.cumsum(jax.nn.softmax(hgrn_lb_raw.astype(f32), axis=1), axis=1)
    lb = lb - lb[:, :1]

    cond8 = jnp.concatenate([c_ctx[None, :], c, jnp.zeros((5, D), f32)], axis=0)
    mod = modulation(cond8, w_ada, b_ada)[:, :3, :].reshape(DEPTH, 3, 6, D)
    mod = jnp.pad(mod, ((0, 0), (0, 0), (0, 2), (0, 0)))

    x = jnp.concatenate([x_prompt.reshape(n_ctx, D), x_sample.reshape(n_lat, D)], axis=0)
    h = modulate(x, mod[0])

    sizes = (D,) * 8 + (4 * NH, D, D)
    offs = [0]
    for s in sizes:
        offs.append(offs[-1] + s)

    def cols(w, idxs):
        return jnp.concatenate([w[..., offs[i]:offs[i + 1]] for i in idxs], axis=-1)

    out_n, out_m = [], []
    buf_a = buf_c = None
    for l in range(DEPTH):
        wl, bl = w_in[l], b_in[l]
        qig = linear(h, cols(wl, (0, 3, 4)).astype(bf16), cols(bl, (0, 3, 4)), bf16, "proj_qig")
        zz = linear(h, cols(wl, (1, 2)).astype(bf16), cols(bl, (1, 2)), f32, "proj_z")
        bb = linear(h, cols(wl, (5, 6, 7)).astype(bf16), cols(bl, (5, 6, 7)), bf16, "proj_b")
        gm = linear(h, cols(wl, (9, 10)).astype(bf16), cols(bl, (9, 10)), bf16, "proj_gm")
        w_g = jnp.pad(cols(wl, (8,)), ((0, 0), (0, DH - 4 * NH))).astype(bf16)
        gates = linear(h, w_g, jnp.pad(cols(bl, (8,)), (0, DH - 4 * NH)), f32, "proj_gates")
        gates_t = gates.T

        lbl = lb[:, l]
        lbp = jnp.concatenate([lbl, 1.0 - lbl, jnp.zeros((4, D), f32)], axis=0)
        gain_a = hgrn_norm_g[l].reshape(1, DH)
        ya, buf_a = hgrn_scan(qig, zz, lbp, gain_a, None, None, buf_a, l, n_seq=nb, T=seq,
                              row_blk0=0, name="hgrn_ctx")
        (ya,) = hgrn_scan(qig, zz, lbp, gain_a, state_hgrn[:, l], ya, None, None, n_seq=db, T=dseq,
                          row_blk0=n_ctx // dseq, name="hgrn_lat")

        cw9 = conv_w[l].reshape(9, D)
        cb = conv_b[l].reshape(1, D)
        wq = w_mq[l].astype(bf16)
        wk = w_mk[l].astype(bf16)
        fb = mlstm_fbias[l]
        zero8 = jnp.zeros((NH,), f32)
        fb_row = jnp.concatenate([zero8, fb[0], zero8, fb[1], jnp.zeros((DH - 4 * NH,), f32)]).reshape(1, DH)
        fb_col = fb.reshape(2, NH, 1)
        gain_b = mlstm_norm_g[l].reshape(1, DH)
        yb, buf_c, s_n, s_m = mlstm_scan(bb, gates, gates_t, cw9, cb, wq, wk, fb_row, fb_col, gain_b,
                                         None, None, buf_c, l, n_seq=nb, T=seq, row_blk0=0,
                                         grid_rows=None, name="mlstm_ctx")
        c0aug = jnp.concatenate([state_mlstm_C[:, l], state_mlstm_n[:, l][..., None],
                                 jnp.zeros((db, 2, NH, DH, DH - 1), f32)], axis=-1)
        m0 = state_mlstm_m[:, l].reshape(db, 2, NH, 1, 1)
        (yb,) = mlstm_scan(bb, gates, gates_t, cw9, cb, wq, wk, fb_row, fb_col, gain_b,
                           (c0aug, m0), yb, None, None, n_seq=db, T=dseq, row_blk0=n_ctx // dseq,
                           grid_rows=dseq // GRID_W, name="mlstm_lat")

        x1, hh = merge(ya, yb, gm, x, mod[l], w_branch_a[l].astype(bf16), w_branch_b[l].astype(bf16),
                       w_out[l].astype(bf16), ln1_g[l], ln1_b[l], bf16 if l % 2 == 0 else f32)

        mod_next = mod[(l + 1) % DEPTH]
        j = l // 2
        if l % 2 == 0:
            x, h = ffn(hh, ffn_w1[j].astype(bf16), ffn_w3[j].astype(bf16), ffn_w2[j].astype(bf16),
                       x1, mod[l], mod_next, ln2_g[l], ln2_b[l])
        else:
            wr = jnp.pad(moe_router_w[j], ((0, 0), (0, DH - N_EXPERTS)))
            br = jnp.concatenate([moe_router_b[j], jnp.full((DH - N_EXPERTS,), -1e30, f32)]).reshape(1, DH)
            rt, counts = router(hh, wr, br)
            src_tok, dest, tile_expert = moe_dispatch(rt, counts, n)
            r_total = src_tok.shape[0]
            part = r_total // MOE_PARTS
            tiles = part // MOE_TM
            ys = None
            for k in range(MOE_PARTS):
                ys = moe_experts(tile_expert[k * tiles:(k + 1) * tiles],
                                 _take_rows(hh, src_tok[k * part:(k + 1) * part]),
                                 moe_w1, moe_w3, moe_w2, j, ys, k * tiles, r_total)
            x, h = combine(_take_rows(ys, dest[:, 0]), _take_rows(ys, dest[:, 1]), rt,
                           x1, mod[l], mod_next, ln2_g[l], ln2_b[l])

        out_n.append(s_n.reshape(nb, 2, NH, DH))
        out_m.append(s_m[..., 0, 0])

    y_prompt = x[:n_ctx].reshape(nb, seq, D)
    y_sample = x[n_ctx:].reshape(db, dseq, D)
    return (y_prompt, y_sample, buf_a, buf_c, jnp.stack(out_n, axis=1), jnp.stack(out_m, axis=1))
```

```python
import functools

import jax
import jax.numpy as jnp
from jax import lax
from jax.experimental import pallas as pl
from jax.experimental.pallas import tpu as pltpu

f32 = jnp.float32
bf16 = jnp.bfloat16

D = 1024
DEPTH = 4
GRID_W = 64
NH = 8
DH = 128
CH_A = 16
HALF_A = 8
SC_A = 128
HPS = 4
CH_B = 256
N_EXPERTS = 8
D_FF = 2816
ALPHA = (2 * DEPTH) ** 0.25
SEQ_GROUP = 1024
LOG2E = 1.4426950408889634
LOG2_F_FLOOR = -1e30
MOE_TM = 512
MOE_PARTS = 1

VMEM_LIMIT = 56 * 1024 * 1024
NT = (((1,), (1,)), ((), ()))
TN = (((0,), (0,)), ((), ()))


def _cparams(*sem):
    return pltpu.CompilerParams(dimension_semantics=sem, vmem_limit_bytes=VMEM_LIMIT)


def _sigmoid(x):
    return 1.0 / (1.0 + jnp.exp(-x))


def _log_sigmoid(t):
    return jnp.minimum(t, 0.0) - jnp.log1p(jnp.exp(-jnp.abs(t)))


def _layer_norm(r, g, b):
    mu = jnp.mean(r, axis=-1, keepdims=True)
    d = r - mu
    var = jnp.mean(d * d, axis=-1, keepdims=True)
    return d * lax.rsqrt(var + 1e-5) * g + b


def _mod_kernel(c_ref, w_ref, b_ref, o_ref):
    c = c_ref[...]
    act = (c * _sigmoid(c)).astype(bf16)
    o_ref[...] = jnp.dot(act, w_ref[...].astype(bf16), preferred_element_type=f32) + b_ref[...]


def modulation(cond8, w_ada, b_ada):
    L, _, n = w_ada.shape
    tn = 1536
    return pl.pallas_call(
        _mod_kernel,
        grid=(L, n // tn),
        in_specs=[pl.BlockSpec((8, D), lambda l, j: (0, 0)),
                  pl.BlockSpec((None, D, tn), lambda l, j: (l, 0, j)),
                  pl.BlockSpec((None, 1, tn), lambda l, j: (l, 0, j))],
        out_specs=pl.BlockSpec((None, 8, tn), lambda l, j: (l, 0, j)),
        out_shape=jax.ShapeDtypeStruct((L, 8, n), f32),
        compiler_params=_cparams("parallel", "parallel"),
        name="modulation",
    )(cond8, w_ada, b_ada.reshape(L, 1, n))


N_CTX = 32 * 256


def _group_of_tile(tm):
    def idx(i):
        return jnp.maximum(i * tm // SEQ_GROUP - (N_CTX // SEQ_GROUP - 1), 0)
    return idx


def _modulate_kernel(x_ref, mod_ref, h_ref):
    h_ref[...] = (x_ref[...] * (1.0 + mod_ref[1:2, :]) + mod_ref[0:1, :]).astype(h_ref.dtype)


def modulate(x, mod_l):
    n = x.shape[0]
    tm = 1024
    grp = _group_of_tile(tm)
    return pl.pallas_call(
        _modulate_kernel,
        grid=(n // tm,),
        in_specs=[pl.BlockSpec((tm, D), lambda i: (i, 0)),
                  pl.BlockSpec((None, 8, D), lambda i: (grp(i), 0, 0))],
        out_specs=pl.BlockSpec((tm, D), lambda i: (i, 0)),
        out_shape=jax.ShapeDtypeStruct((n, D), bf16),
        compiler_params=_cparams("parallel"),
        name="modulate",
    )(x, mod_l)


def _linear_kernel(x_ref, w_ref, b_ref, o_ref):
    acc = jnp.dot(x_ref[...], w_ref[...], preferred_element_type=f32)
    o_ref[...] = (acc + b_ref[...]).astype(o_ref.dtype)


def linear(x, w, b, out_dtype, name):
    m, k = x.shape
    n = w.shape[1]
    tm = 2048
    tn = 1024 if n % 1024 == 0 else n
    return pl.pallas_call(
        _linear_kernel,
        grid=(m // tm, n // tn),
        in_specs=[pl.BlockSpec((tm, k), lambda i, j: (i, 0)),
                  pl.BlockSpec((k, tn), lambda i, j: (0, j)),
                  pl.BlockSpec((1, tn), lambda i, j: (0, j))],
        out_specs=pl.BlockSpec((tm, tn), lambda i, j: (i, j)),
        out_shape=jax.ShapeDtypeStruct((m, n), out_dtype),
        compiler_params=_cparams("parallel", "parallel"),
        name=name,
    )(x, w, b.reshape(1, n))


def _hgrn_gate_kernel(x_ref, w_ref, b_ref, lbp_ref, lf_ref, kk_ref):
    z = jnp.dot(x_ref[...], w_ref[...], preferred_element_type=f32) + b_ref[...]
    lb = lbp_ref[0:1, :]
    one_m_lb = lbp_ref[1:2, :]
    e = jnp.exp(-jnp.abs(z))
    inv = 1.0 / (1.0 + e)
    sig = jnp.where(z >= 0.0, 1.0, e) * inv
    kk_ref[...] = one_m_lb * (jnp.where(z >= 0.0, e, 1.0) * inv)
    lf_ref[...] = jnp.maximum(jnp.log(lb + one_m_lb * sig) * LOG2E, LOG2_F_FLOOR)


def hgrn_gates(x, w, b, lbp):
    m, k = x.shape
    n = w.shape[1]
    tm, tn = 1024, 1024
    return pl.pallas_call(
        _hgrn_gate_kernel,
        grid=(m // tm, n // tn),
        in_specs=[pl.BlockSpec((tm, k), lambda i, j: (i, 0)),
                  pl.BlockSpec((k, tn), lambda i, j: (0, j)),
                  pl.BlockSpec((1, tn), lambda i, j: (0, j)),
                  pl.BlockSpec((8, tn), lambda i, j: (0, j))],
        out_specs=[pl.BlockSpec((tm, tn), lambda i, j: (i, j)),
                   pl.BlockSpec((tm, tn), lambda i, j: (i, j))],
        out_shape=[jax.ShapeDtypeStruct((m, n), f32), jax.ShapeDtypeStruct((m, n), f32)],
        compiler_params=_cparams("parallel", "parallel"),
        name="proj_z",
    )(x, w, b.reshape(1, n), lbp)


def _hgrn_kernel(*refs, has_state, want_state, n_aliased):
    q_ref, v_ref, g_ref, lff_ref, lfb_ref, kkf_ref, kkb_ref, gain_ref = refs[:8]
    lf_refs = (lff_ref, lfb_ref)
    kk_refs = (kkf_ref, kkb_ref)
    pos = 8
    s0_ref = None
    if has_state:
        s0_ref = refs[pos]
        pos += 1
    pos += n_aliased
    y_ref = refs[pos]
    pos += 1
    st_ref = None
    if want_state:
        st_ref = refs[pos]
        pos += 1
    b_ref, o_ref = refs[pos:]

    T = q_ref.shape[0]
    nsc = T // SC_A

    row = lax.broadcasted_iota(jnp.int32, (SC_A, SC_A), 0)
    col = lax.broadcasted_iota(jnp.int32, (SC_A, SC_A), 1)
    tris = tuple(jnp.where(m, 1.0, 0.0).astype(bf16) for m in (row >= col, row <= col))
    nblk = SC_A // CH_A
    half = SC_A // 2
    r_sq = lax.broadcasted_iota(jnp.int32, (half, half), 0)
    c_sq = lax.broadcasted_iota(jnp.int32, (half, half), 1)
    r_w = lax.broadcasted_iota(jnp.int32, (half, SC_A), 0)
    c_w = lax.broadcasted_iota(jnp.int32, (half, SC_A), 1)
    masks_lo = (r_sq >= c_sq, r_w <= c_w)
    masks_hi = (r_w + half >= c_w, r_sq <= c_sq)

    for d in range(2):
        for sc in range(nsc):
            rows = slice(sc * SC_A, (sc + 1) * SC_A)
            b_ref[d, rows, :] = _tri_matmul(tris[d], lf_refs[d][rows, :])

    if has_state:
        st0 = tuple(s0_ref[d, hh].T for hh in range(HPS) for d in range(2))
    else:
        st0 = tuple(jnp.zeros((DH, DH), f32) for _ in range(2 * HPS))

    def body(c, carry):
        units = []
        for hd in range(2 * HPS):
            hh, d = divmod(hd, 2)
            ls = slice(hh * DH, (hh + 1) * DH)
            cidx = c if d == 0 else nsc - 1 - c
            r = pl.ds(pl.multiple_of(cidx * SC_A, SC_A), SC_A)
            units.append((d, ls, r))

        scores = []
        for d, ls, r in units:
            q = q_ref[r, ls].astype(f32)
            kk = kk_refs[d][r, ls]
            b = b_ref[d, r, ls]
            att_rows = ([], [])
            for i in range(nblk):
                mid = i * CH_A + (HALF_A - 1 if d == 0 else HALF_A)
                r_i = b[mid:mid + 1, :]
                blk = slice(i * CH_A, (i + 1) * CH_A)
                qi = (q[blk] * jnp.exp2(b[blk] - r_i)).astype(bf16)
                if d == 0 and i < nblk // 2:
                    ksl = slice(0, half)
                elif d == 1 and i >= nblk // 2:
                    ksl = slice(half, SC_A)
                else:
                    ksl = slice(0, SC_A)
                ks = (kk[ksl] * jnp.exp2(r_i - b[ksl])).astype(bf16)
                att_rows[i >= nblk // 2].append(lax.dot_general(qi, ks, NT, preferred_element_type=f32))
            scores.append(att_rows)

        for hd, (d, ls, r) in enumerate(units):
            v = v_ref[r, ls]
            a_lo = jnp.where(masks_lo[d], jnp.concatenate(scores[hd][0], axis=0), 0.0).astype(bf16)
            a_hi = jnp.where(masks_hi[d], jnp.concatenate(scores[hd][1], axis=0), 0.0).astype(bf16)
            o_lo = jnp.dot(a_lo, v[:half] if d == 0 else v, preferred_element_type=f32)
            o_hi = jnp.dot(a_hi, v if d == 0 else v[half:], preferred_element_type=f32)
            qb = (q_ref[r, ls].astype(f32) * jnp.exp2(b_ref[d, r, ls])).astype(bf16)
            o_ref[d, r, ls] = (jnp.concatenate([o_lo, o_hi], axis=0)
                               + lax.dot_general(qb, carry[hd].astype(bf16), NT,
                                                 preferred_element_type=f32))

        new = []
        for hd, (d, ls, r) in enumerate(units):
            b = b_ref[d, r, ls]
            b_end = b[SC_A - 1:SC_A, :] if d == 0 else b[0:1, :]
            kl = (kk_refs[d][r, ls] * jnp.exp2(b_end - b)).astype(bf16)
            new.append(jnp.exp2(b_end) * carry[hd]
                       + lax.dot_general(v_ref[r, ls], kl, TN, preferred_element_type=f32))
        return tuple(new)

    st_fin = lax.fori_loop(0, nsc, body, st0, unroll=2)

    for hh in range(HPS):
        ls = slice(hh * DH, (hh + 1) * DH)
        o = o_ref[0, :, ls] + o_ref[1, :, ls]
        ms = jnp.mean(o * o, axis=-1, keepdims=True)
        g = g_ref[:, ls].astype(f32)
        y_ref[:, ls] = (o * lax.rsqrt(ms + 1e-6) * gain_ref[...] * (g * _sigmoid(g))).astype(y_ref.dtype)
        if want_state:
            for d in range(2):
                st_ref[d, hh] = st_fin[2 * hh + d].T


def hgrn_scan(qig, lf, kk, gain, s0, y_buf, state_buf, layer, *, n_seq, T, row_blk0, name):
    has_state = s0 is not None
    want_state = layer is not None

    wh = HPS * DH

    def col(off):
        return lambda b, h: (row_blk0 + b, off // HPS + h)

    in_specs = [pl.BlockSpec((T, wh), col(0)), pl.BlockSpec((T, wh), col(NH)),
                pl.BlockSpec((T, wh), col(2 * NH)),
                pl.BlockSpec((T, wh), col(0)), pl.BlockSpec((T, wh), col(NH)),
                pl.BlockSpec((T, wh), col(0)), pl.BlockSpec((T, wh), col(NH)),
                pl.BlockSpec((1, DH), lambda b, h: (0, 0))]
    args = [qig, qig, qig, lf, lf, kk, kk, gain]
    if has_state:
        in_specs.append(pl.BlockSpec((None, 2, HPS, DH, DH), lambda b, h: (b, 0, h, 0, 0)))
        args.append(s0)
    out_specs = [pl.BlockSpec((T, wh), col(0))]
    out_shape = [jax.ShapeDtypeStruct((qig.shape[0], D), bf16)]
    aliases = {}
    if y_buf is not None:
        in_specs.append(pl.BlockSpec(memory_space=pl.ANY))
        args.append(y_buf)
        aliases[len(args) - 1] = 0
    if want_state:
        out_specs.append(pl.BlockSpec((None, None, 2, HPS, DH, DH),
                                      lambda b, h: (b, layer, 0, h, 0, 0)))
        out_shape.append(jax.ShapeDtypeStruct((n_seq, DEPTH, 2, NH, DH, DH), f32))
        if state_buf is not None:
            in_specs.append(pl.BlockSpec(memory_space=pl.ANY))
            args.append(state_buf)
            aliases[len(args) - 1] = 1
    scratch = [pltpu.VMEM((2, T, wh), f32)] * 2
    return pl.pallas_call(
        functools.partial(_hgrn_kernel, has_state=has_state, want_state=want_state,
                          n_aliased=len(aliases)),
        grid=(n_seq, NH // HPS),
        in_specs=in_specs, out_specs=out_specs, out_shape=out_shape,
        scratch_shapes=scratch,
        input_output_aliases=aliases,
        compiler_params=_cparams("parallel", "parallel"),
        name=name,
    )(*args)


def _split3(x):
    x0 = x.astype(bf16)
    r = x - x0.astype(f32)
    x1 = r.astype(bf16)
    x2 = (r - x1.astype(f32)).astype(bf16)
    return x0, x1, x2


def _tri_matmul(tri, x):
    return sum(jnp.dot(tri, p, preferred_element_type=f32) for p in _split3(x))


def _matmul_tri(x, tri):
    return sum(jnp.dot(p, tri, preferred_element_type=f32) for p in _split3(x))


def _mlstm_kernel(*refs, grid_rows, has_state, want_state, n_aliased):
    x_ref, v_ref, og_ref, gc_ref, gr_ref, cw_ref, cb_ref, wq_ref, wk_ref, fbr_ref, fbc_ref, gain_ref = refs[:12]
    pos = 12
    if has_state:
        c0_ref, m0_ref = refs[pos:pos + 2]
        pos += 2
    pos += n_aliased
    y_ref = refs[pos]
    pos += 1
    if want_state:
        ct_ref, nt_ref, mt_ref = refs[pos:pos + 3]

    head0 = pl.program_id(1) * HPS
    T = x_ref.shape[0]
    L = min(CH_B, T)
    nchunk = T // L

    t_idx = lax.broadcasted_iota(jnp.int32, (T, DH), 0)
    lane = lax.broadcasted_iota(jnp.int32, (T, DH), 1)
    ones_col = jnp.where(lane == 0, 1.0, 0.0)
    if grid_rows is None:
        taps = [(0, dc) for dc in (-1, 0, 1)]
    else:
        taps = [(dr, dc) for dr in (-1, 0, 1) for dc in (-1, 0, 1)]
        c_idx = t_idx % GRID_W
    cas, vaugs = [], []
    for hh in range(HPS):
        ls = slice(hh * DH, (hh + 1) * DH)
        x = x_ref[:, ls].astype(f32)
        acc = jnp.zeros((T, DH), f32) + cb_ref[:, ls]
        for dr, dc in taps:
            off = dr * GRID_W + dc
            tap = (dr + 1) * 3 + (dc + 1)
            w = cw_ref[tap:tap + 1, ls]
            xs = x if off == 0 else pltpu.roll(x, (-off) % T, axis=0)
            ok = (t_idx + off >= 0) & (t_idx + off < T)
            if grid_rows is not None:
                ok = ok & (c_idx + dc >= 0) & (c_idx + dc < GRID_W)
            acc = acc + jnp.where(ok, xs, 0.0) * w
        cas.append((acc * _sigmoid(acc)).astype(bf16))
        vaugs.append(jnp.concatenate([v_ref[:, ls].astype(f32), ones_col], axis=1))
    qs = [jnp.dot(cas[hh], wq_ref[hh], preferred_element_type=f32).astype(bf16) for hh in range(HPS)]
    ks = [(jnp.dot(cas[hh], wk_ref[hh], preferred_element_type=f32) * (DH ** -0.5)).astype(bf16)
          for hh in range(HPS)]

    gate_lane = lax.broadcasted_iota(jnp.int32, (T, DH), 1) // NH
    is_f = (gate_lane == 1) | (gate_lane == 3)
    gc = gc_ref[...] + fbr_ref[...]
    gc = jnp.where(is_f, _log_sigmoid(gc), gc)

    def pick_lane(a, ln):
        sel = lax.broadcasted_iota(jnp.int32, a.shape, 1) == ln
        return jnp.sum(jnp.where(sel, a, 0.0), axis=1, keepdims=True)

    def pick_row(a, rw):
        sel = lax.broadcasted_iota(jnp.int32, a.shape, 0) == rw
        return jnp.sum(jnp.where(sel, a, 0.0), axis=0, keepdims=True)

    r_i = lax.broadcasted_iota(jnp.int32, (L, L), 0)
    c_i = lax.broadcasted_iota(jnp.int32, (L, L), 1)
    lower = r_i >= c_i
    upper = r_i <= c_i
    tri_lo = jnp.where(lower, 1.0, 0.0).astype(bf16)
    tri_up = jnp.where(upper, 1.0, 0.0).astype(bf16)

    h_parts = [[[None] * nchunk for _ in range(2)] for _ in range(HPS)]
    units = [(hh, d) for hh in range(HPS) for d in range(2)]
    if has_state:
        carry = {u: (c0_ref[u[1], u[0]], m0_ref[u[1], u[0]]) for u in units}
    else:
        carry = {u: (jnp.zeros((DH, 2 * DH), f32), jnp.zeros((1, 1), f32)) for u in units}
    masks = (lower, upper)
    tri_cs = (tri_lo, tri_up)
    tri_rs = (tri_up, tri_lo)

    for step in range(nchunk):
        cidx = (step, nchunk - 1 - step)
        rsl = tuple(slice(c * L, (c + 1) * L) for c in cidx)

        g = {}
        for d in range(2):
            gcc = gc[rsl[d]]
            lf_r8 = _log_sigmoid(gr_ref[(2 * d + 1) * NH:(2 * d + 2) * NH, rsl[d]] + fbc_ref[d])
            g[d] = (gcc, _tri_matmul(tri_cs[d], gcc), _matmul_tri(lf_r8, tri_rs[d]),
                    gr_ref[2 * d * NH:(2 * d + 1) * NH, rsl[d]])
        picks = {}
        for hh, d in units:
            head = head0 + hh
            gcc, F_c_all, F_r_all, ig_r_all = g[d]
            ig_c = pick_lane(gcc, 2 * d * NH + head)
            F_c = pick_lane(F_c_all, (2 * d + 1) * NH + head)
            ig_r = pick_row(ig_r_all, head)
            F_r = pick_row(F_r_all, head)
            F_end = F_c[L - 1:L, :] if d == 0 else F_c[0:1, :]
            picks[hh, d] = (ig_c, F_c, ig_r, F_r, F_end)

        decay = {}
        for u in units:
            ig_c, F_c, ig_r, F_r, F_end = picks[u]
            m_prev = carry[u][1]
            logd = jnp.where(masks[u[1]], F_c - F_r + ig_r, -jnp.inf)
            log_inter = F_c + m_prev
            m_q = jnp.maximum(log_inter, jnp.max(logd, axis=-1, keepdims=True))
            decay[u] = (jnp.exp(logd - m_q), jnp.exp(log_inter - m_q), m_q)

        scores = {}
        for hh, d in units:
            qc, kc = qs[hh][rsl[d]], ks[hh][rsl[d]]
            scores[hh, d] = (lax.dot_general(qc, kc, NT, preferred_element_type=f32)
                             * decay[hh, d][0]).astype(bf16)

        for hh, d in units:
            _, a, m_q = decay[hh, d]
            qc = qs[hh][rsl[d]]
            va = vaugs[hh][rsl[d]]
            nd = (a * jnp.dot(qc, carry[hh, d][0].astype(bf16), preferred_element_type=f32)
                  + jnp.dot(scores[hh, d], va.astype(bf16), preferred_element_type=f32))
            num = nd[:, :DH]
            den = nd[:, DH:DH + 1]
            h_parts[hh][d][cidx[d]] = num / jnp.maximum(jnp.abs(den), jnp.exp(-m_q))

        for hh, d in units:
            ig_c, F_c, ig_r, F_r, F_end = picks[hh, d]
            caug, m_prev = carry[hh, d]
            log_w_c = F_end - F_c + ig_c
            log_w_r = F_end - F_r + ig_r
            m_new = jnp.maximum(F_end + m_prev, jnp.max(log_w_r, axis=-1, keepdims=True))
            ws_c = jnp.exp(log_w_c - m_new)
            d0 = jnp.exp(F_end + m_prev - m_new)
            upd = lax.dot_general(ks[hh][rsl[d]], (ws_c * vaugs[hh][rsl[d]]).astype(bf16), TN,
                                  preferred_element_type=f32)
            carry[hh, d] = (d0 * caug + upd, m_new)
    finals = [[carry[hh, d] for d in range(2)] for hh in range(HPS)]

    for hh in range(HPS):
        ls = slice(hh * DH, (hh + 1) * DH)
        hsum = jnp.concatenate([h_parts[hh][0][c] + h_parts[hh][1][c] for c in range(nchunk)], axis=0)
        mu = jnp.mean(hsum, axis=-1, keepdims=True)
        dv = hsum - mu
        var = jnp.mean(dv * dv, axis=-1, keepdims=True)
        og = og_ref[:, ls].astype(f32)
        y_ref[:, ls] = (dv * lax.rsqrt(var + 1e-6) * gain_ref[...] * _sigmoid(og)).astype(y_ref.dtype)
        if want_state:
            for d in range(2):
                caug, m_fin = finals[hh][d]
                ct_ref[d, hh] = caug[:, :DH]
                nt_ref[d, hh] = caug[:, DH:2 * DH].T[0:1, :]
                mt_ref[d, hh] = jnp.broadcast_to(m_fin, (1, DH))


def mlstm_scan(bb, gates, gates_t, conv_w9, conv_b, wq, wk, fb_row, fb_col, gain, state, y_buf, c_buf,
               layer, *, n_seq, T, row_blk0, grid_rows, name):
    has_state = state is not None
    want_state = layer is not None

    wh = HPS * DH

    def col(off):
        return lambda b, h: (row_blk0 + b, off // HPS + h)

    in_specs = [pl.BlockSpec((T, wh), col(0)), pl.BlockSpec((T, wh), col(NH)),
                pl.BlockSpec((T, wh), col(2 * NH)),
                pl.BlockSpec((T, DH), lambda b, h: (row_blk0 + b, 0)),
                pl.BlockSpec((DH, T), lambda b, h: (0, row_blk0 + b)),
                pl.BlockSpec((9, wh), lambda b, h: (0, h)),
                pl.BlockSpec((1, wh), lambda b, h: (0, h)),
                pl.BlockSpec((HPS, DH, DH), lambda b, h: (h, 0, 0)),
                pl.BlockSpec((HPS, DH, DH), lambda b, h: (h, 0, 0)),
                pl.BlockSpec((1, DH), lambda b, h: (0, 0)),
                pl.BlockSpec((2, NH, 1), lambda b, h: (0, 0, 0)),
                pl.BlockSpec((1, DH), lambda b, h: (0, 0))]
    args = [bb, bb, bb, gates, gates_t, conv_w9, conv_b, wq, wk, fb_row, fb_col, gain]
    if has_state:
        c0aug, m0 = state
        in_specs += [pl.BlockSpec((None, 2, HPS, DH, 2 * DH), lambda b, h: (b, 0, h, 0, 0)),
                     pl.BlockSpec((None, 2, HPS, 1, 1), lambda b, h: (b, 0, h, 0, 0))]
        args += [c0aug, m0]
    out_specs = [pl.BlockSpec((T, wh), col(0))]
    out_shape = [jax.ShapeDtypeStruct((bb.shape[0], D), bf16)]
    aliases = {}
    if y_buf is not None:
        in_specs.append(pl.BlockSpec(memory_space=pl.ANY))
        args.append(y_buf)
        aliases[len(args) - 1] = 0
    if want_state:
        out_specs += [pl.BlockSpec((None, None, 2, HPS, DH, DH), lambda b, h: (b, layer, 0, h, 0, 0)),
                      pl.BlockSpec((None, 2, HPS, 1, DH), lambda b, h: (b, 0, h, 0, 0)),
                      pl.BlockSpec((None, 2, HPS, 1, DH), lambda b, h: (b, 0, h, 0, 0))]
        out_shape += [jax.ShapeDtypeStruct((n_seq, DEPTH, 2, NH, DH, DH), f32),
                      jax.ShapeDtypeStruct((n_seq, 2, NH, 1, DH), f32),
                      jax.ShapeDtypeStruct((n_seq, 2, NH, 1, DH), f32)]
        if c_buf is not None:
            in_specs.append(pl.BlockSpec(memory_space=pl.ANY))
            args.append(c_buf)
            aliases[len(args) - 1] = 1
    return pl.pallas_call(
        functools.partial(_mlstm_kernel, grid_rows=grid_rows, has_state=has_state,
                          want_state=want_state, n_aliased=len(aliases)),
        grid=(n_seq, NH // HPS),
        in_specs=in_specs, out_specs=out_specs, out_shape=out_shape,
        input_output_aliases=aliases,
        compiler_params=_cparams("parallel", "parallel"),
        name=name,
    )(*args)


def _software_pipeline(n, stages):
    vals = [None] * n
    for t in range(n + len(stages) - 1):
        for k in range(len(stages) - 1, -1, -1):
            s = t - k
            if 0 <= s < n:
                vals[s] = stages[k](s, vals[s])


SUB_ROWS = 256


def _merge_kernel(ya_ref, yb_ref, gm_ref, x_ref, mod_ref, wa_ref, wb_ref, wo_ref, lg_ref, lb_ref,
                  x1_ref, hh_ref):
    def rows(s):
        return slice(s * SUB_ROWS, (s + 1) * SUB_ROWS)

    def branches(s, _):
        return (jnp.dot(ya_ref[rows(s), :], wa_ref[...], preferred_element_type=f32),
                jnp.dot(yb_ref[rows(s), :], wb_ref[...], preferred_element_type=f32))

    def gate(s, p):
        gma = gm_ref[rows(s), :D].astype(f32)
        gmb = gm_ref[rows(s), D:].astype(f32)
        return (_sigmoid(gma) * p[0] + _sigmoid(gmb) * p[1]).astype(bf16)

    def project(s, merged):
        return jnp.dot(merged, wo_ref[...], preferred_element_type=f32)

    def norm(s, y):
        x1 = _layer_norm(ALPHA * x_ref[rows(s), :] + mod_ref[2:3, :] * y, lg_ref[...], lb_ref[...])
        x1_ref[rows(s), :] = x1
        hh_ref[rows(s), :] = (x1 * (1.0 + mod_ref[4:5, :]) + mod_ref[3:4, :]).astype(hh_ref.dtype)

    _software_pipeline(ya_ref.shape[0] // SUB_ROWS, (branches, gate, project, norm))


def merge(ya, yb, gm, x, mod_l, wa, wb, wo, lg, lb, hh_dtype):
    n = x.shape[0]
    tm = 512
    grp = _group_of_tile(tm)
    row = lambda i: (i, 0)
    const = lambda i: (0, 0)
    return pl.pallas_call(
        _merge_kernel,
        grid=(n // tm,),
        in_specs=[pl.BlockSpec((tm, D), row), pl.BlockSpec((tm, D), row),
                  pl.BlockSpec((tm, 2 * D), row), pl.BlockSpec((tm, D), row),
                  pl.BlockSpec((None, 8, D), lambda i: (grp(i), 0, 0)),
                  pl.BlockSpec((D, D), const), pl.BlockSpec((D, D), const), pl.BlockSpec((D, D), const),
                  pl.BlockSpec((1, D), const), pl.BlockSpec((1, D), const)],
        out_specs=[pl.BlockSpec((tm, D), row), pl.BlockSpec((tm, D), row)],
        out_shape=[jax.ShapeDtypeStruct((n, D), f32), jax.ShapeDtypeStruct((n, D), hh_dtype)],
        compiler_params=_cparams("parallel"),
        name="merge",
    )(ya, yb, gm, x, mod_l, wa, wb, wo, lg.reshape(1, D), lb.reshape(1, D))


def _post_ffn(f, x1, mod, mod_next, lg, lb, x2_ref, hn_ref):
    x2 = _layer_norm(ALPHA * x1 + mod[5:6, :] * f, lg, lb)
    x2_ref[...] = x2
    hn_ref[...] = (x2 * (1.0 + mod_next[1:2, :]) + mod_next[0:1, :]).astype(hn_ref.dtype)


def _ffn_kernel(hh_ref, w1_ref, w3_ref, w2_ref, x1_ref, mod_ref, modn_ref, lg_ref, lb_ref,
                x2_ref, hn_ref, acc_ref):
    j = pl.program_id(1)

    @pl.when(j == 0)
    def _():
        acc_ref[...] = jnp.zeros_like(acc_ref)

    def rows(s):
        return slice(s * SUB_ROWS, (s + 1) * SUB_ROWS)

    def up(s, _):
        hh = hh_ref[rows(s), :]
        return (jnp.dot(hh, w1_ref[...], preferred_element_type=f32),
                jnp.dot(hh, w3_ref[...], preferred_element_type=f32))

    def act(s, ab):
        return (ab[0] * _sigmoid(ab[0]) * ab[1]).astype(bf16)

    def down(s, h):
        return jnp.dot(h, w2_ref[...], preferred_element_type=f32)

    def accumulate(s, part):
        acc_ref[rows(s), :] += part

    _software_pipeline(hh_ref.shape[0] // SUB_ROWS, (up, act, down, accumulate))

    @pl.when(j == pl.num_programs(1) - 1)
    def _():
        _post_ffn(acc_ref[...], x1_ref[...], mod_ref, modn_ref, lg_ref[...], lb_ref[...],
                  x2_ref, hn_ref)


def ffn(hh, w1, w3, w2, x1, mod_l, mod_next, lg, lb):
    n = hh.shape[0]
    tm, tf = 512, 1408
    grp = _group_of_tile(tm)
    row = lambda i, j: (i, 0)
    const = lambda i, j: (0, 0)
    return pl.pallas_call(
        _ffn_kernel,
        grid=(n // tm, D_FF // tf),
        in_specs=[pl.BlockSpec((tm, D), row),
                  pl.BlockSpec((D, tf), lambda i, j: (0, j)),
                  pl.BlockSpec((D, tf), lambda i, j: (0, j)),
                  pl.BlockSpec((tf, D), lambda i, j: (j, 0)),
                  pl.BlockSpec((tm, D), row),
                  pl.BlockSpec((None, 8, D), lambda i, j: (grp(i), 0, 0)),
                  pl.BlockSpec((None, 8, D), lambda i, j: (grp(i), 0, 0)),
                  pl.BlockSpec((1, D), const), pl.BlockSpec((1, D), const)],
        out_specs=[pl.BlockSpec((tm, D), row), pl.BlockSpec((tm, D), row)],
        out_shape=[jax.ShapeDtypeStruct((n, D), f32), jax.ShapeDtypeStruct((n, D), bf16)],
        scratch_shapes=[pltpu.VMEM((tm, D), f32)],
        compiler_params=_cparams("parallel", "arbitrary"),
        name="ffn",
    )(hh, w1, w3, w2, x1, mod_l, mod_next, lg.reshape(1, D), lb.reshape(1, D))


def _router_kernel(hh_ref, wr_ref, br_ref, o_ref, cnt_ref, run_ref):
    hh = hh_ref[...].astype(f32)
    wr = wr_ref[...]
    h_hi = hh.astype(bf16)
    h_lo = (hh - h_hi.astype(f32)).astype(bf16)
    w_hi = wr.astype(bf16)
    w_lo = (wr - w_hi.astype(f32)).astype(bf16)
    logits = (jnp.dot(h_hi, w_hi, preferred_element_type=f32)
              + jnp.dot(h_hi, w_lo, preferred_element_type=f32)
              + jnp.dot(h_lo, w_hi, preferred_element_type=f32)) + br_ref[...]
    lane = lax.broadcasted_iota(jnp.int32, logits.shape, 1).astype(f32)
    m1 = jnp.max(logits, axis=-1, keepdims=True)
    i1 = jnp.min(jnp.where(logits == m1, lane, float(DH)), axis=-1, keepdims=True)
    rest = jnp.where(lane == i1, -jnp.inf, logits)
    m2 = jnp.max(rest, axis=-1, keepdims=True)
    i2 = jnp.min(jnp.where(rest == m2, lane, float(DH)), axis=-1, keepdims=True)
    e21 = jnp.exp(m2 - m1)
    p1 = 1.0 / (1.0 + e21)
    p2 = e21 * p1

    @pl.when(pl.program_id(0) == 0)
    def _():
        run_ref[...] = jnp.zeros_like(run_ref)

    tm = logits.shape[0]
    r_i = lax.broadcasted_iota(jnp.int32, (tm, tm), 0)
    c_i = lax.broadcasted_iota(jnp.int32, (tm, tm), 1)
    earlier = jnp.where(c_i < r_i, 1.0, 0.0).astype(bf16)
    oh1 = jnp.where(lane == i1, 1.0, 0.0)
    oh2 = jnp.where(lane == i2, 1.0, 0.0)
    pre1 = jnp.dot(earlier, oh1.astype(bf16), preferred_element_type=f32)
    pre2 = jnp.dot(earlier, oh2.astype(bf16), preferred_element_type=f32)
    cnt1 = jnp.sum(oh1, axis=0, keepdims=True)
    cnt2 = jnp.sum(oh2, axis=0, keepdims=True)
    run = run_ref[...]
    rank1 = jnp.sum(oh1 * (run + pre1), axis=-1, keepdims=True)
    rank2 = jnp.sum(oh2 * (run + cnt1 + pre2), axis=-1, keepdims=True)
    run_ref[...] = run + cnt1 + cnt2
    cnt_ref[...] = jnp.broadcast_to(run + cnt1 + cnt2, cnt_ref.shape)
    o_ref[...] = (jnp.where(lane == 0.0, i1, 0.0) + jnp.where(lane == 1.0, i2, 0.0)
                  + jnp.where(lane == 2.0, p1, 0.0) + jnp.where(lane == 3.0, p2, 0.0)
                  + jnp.where(lane == 4.0, rank1, 0.0) + jnp.where(lane == 5.0, rank2, 0.0))


def router(hh, wr_pad, br_pad):
    n = hh.shape[0]
    tm = 1024
    return pl.pallas_call(
        _router_kernel,
        grid=(n // tm,),
        in_specs=[pl.BlockSpec((tm, D), lambda i: (i, 0)),
                  pl.BlockSpec((D, DH), lambda i: (0, 0)),
                  pl.BlockSpec((1, DH), lambda i: (0, 0))],
        out_specs=[pl.BlockSpec((tm, DH), lambda i: (i, 0)),
                   pl.BlockSpec((8, DH), lambda i: (0, 0))],
        out_shape=[jax.ShapeDtypeStruct((n, DH), f32), jax.ShapeDtypeStruct((8, DH), f32)],
        scratch_shapes=[pltpu.VMEM((1, DH), f32)],
        compiler_params=_cparams("arbitrary"),
        name="router",
    )(hh, wr_pad, br_pad)


def _moe_up_kernel(te_ref, x_ref, w1_ref, w3_ref, a_ref):
    del te_ref
    x = x_ref[...].astype(bf16)
    a = jnp.dot(x, w1_ref[...].astype(bf16), preferred_element_type=f32)
    b = jnp.dot(x, w3_ref[...].astype(bf16), preferred_element_type=f32)
    a_ref[...] = (a * _sigmoid(a) * b).astype(a_ref.dtype)


def _moe_down_kernel(te_ref, a_ref, w2_ref, *rest):
    del te_ref
    y_ref = rest[-1]
    y_ref[...] = jnp.dot(a_ref[...], w2_ref[...].astype(bf16), preferred_element_type=f32)


def moe_experts(tile_expert, xs, w1, w3, w2, lyr, ys_buf, tile0, r_total):
    r = xs.shape[0]
    nt = r // MOE_TM
    tf = 1408
    up = pl.pallas_call(
        _moe_up_kernel,
        grid_spec=pltpu.PrefetchScalarGridSpec(
            num_scalar_prefetch=1,
            grid=(D_FF // tf, nt),
            in_specs=[pl.BlockSpec((MOE_TM, D), lambda j, i, te: (i, 0)),
                      pl.BlockSpec((None, None, D, tf), lambda j, i, te: (lyr, te[i], 0, j)),
                      pl.BlockSpec((None, None, D, tf), lambda j, i, te: (lyr, te[i], 0, j))],
            out_specs=pl.BlockSpec((MOE_TM, tf), lambda j, i, te: (i, j))),
        out_shape=jax.ShapeDtypeStruct((r, D_FF), bf16),
        compiler_params=_cparams("arbitrary", "arbitrary"),
        name="moe_up",
    )(tile_expert, xs, w1, w3)
    in_specs = [pl.BlockSpec((MOE_TM, D_FF), lambda i, te: (i, 0)),
                pl.BlockSpec((None, None, D_FF, D), lambda i, te: (lyr, te[i], 0, 0))]
    args = [tile_expert, up, w2]
    aliases = {}
    if ys_buf is not None:
        in_specs.append(pl.BlockSpec(memory_space=pl.ANY))
        args.append(ys_buf)
        aliases = {len(args) - 1: 0}
    return pl.pallas_call(
        _moe_down_kernel,
        grid_spec=pltpu.PrefetchScalarGridSpec(
            num_scalar_prefetch=1,
            grid=(nt,),
            in_specs=in_specs,
            out_specs=pl.BlockSpec((MOE_TM, D), lambda i, te: (tile0 + i, 0))),
        out_shape=jax.ShapeDtypeStruct((r_total, D), f32),
        input_output_aliases=aliases,
        compiler_params=_cparams("arbitrary"),
        name="moe_down",
    )(*args)


def _combine_kernel(ya_ref, yb_ref, rt_ref, x1_ref, mod_ref, modn_ref, lg_ref, lb_ref, x2_ref, hn_ref):
    f = rt_ref[:, 2:3] * ya_ref[...] + rt_ref[:, 3:4] * yb_ref[...]
    _post_ffn(f, x1_ref[...], mod_ref, modn_ref, lg_ref[...], lb_ref[...], x2_ref, hn_ref)


def combine(ya, yb, rt, x1, mod_l, mod_next, lg, lb):
    n = x1.shape[0]
    tm = 1024
    grp = _group_of_tile(tm)
    row = lambda i: (i, 0)
    const = lambda i: (0, 0)
    return pl.pallas_call(
        _combine_kernel,
        grid=(n // tm,),
        in_specs=[pl.BlockSpec((tm, D), row), pl.BlockSpec((tm, D), row),
                  pl.BlockSpec((tm, DH), row), pl.BlockSpec((tm, D), row),
                  pl.BlockSpec((None, 8, D), lambda i: (grp(i), 0, 0)),
                  pl.BlockSpec((None, 8, D), lambda i: (grp(i), 0, 0)),
                  pl.BlockSpec((1, D), const), pl.BlockSpec((1, D), const)],
        out_specs=[pl.BlockSpec((tm, D), row), pl.BlockSpec((tm, D), row)],
        out_shape=[jax.ShapeDtypeStruct((n, D), f32), jax.ShapeDtypeStruct((n, D), bf16)],
        compiler_params=_cparams("parallel"),
        name="combine",
    )(ya, yb, rt, x1, mod_l, mod_next, lg.reshape(1, D), lb.reshape(1, D))


def moe_dispatch(rt, counts, n):
    r_total = 2 * n + N_EXPERTS * MOE_TM
    nt = r_total // MOE_TM
    cnt = counts[0, :N_EXPERTS].astype(jnp.int32)
    padded = (cnt + MOE_TM - 1) // MOE_TM * MOE_TM
    ends = jnp.cumsum(padded)
    starts = ends - padded
    e_idx = rt[:, 0:2].astype(jnp.int32)
    rank = rt[:, 4:6].astype(jnp.int32)
    onehot = e_idx[..., None] == jnp.arange(N_EXPERTS)[None, None, :]
    dest = rank + jnp.sum(jnp.where(onehot, starts[None, None, :], 0), axis=-1)
    tok = jnp.broadcast_to(jnp.arange(n, dtype=jnp.int32)[:, None], (n, 2))
    src_tok = jnp.zeros((r_total,), jnp.int32).at[dest.reshape(-1)].set(
        tok.reshape(-1), unique_indices=True, mode="promise_in_bounds")
    tile_start = jnp.arange(nt, dtype=jnp.int32) * MOE_TM
    tile_expert = jnp.minimum(jnp.sum((tile_start[:, None] >= ends[None, :]).astype(jnp.int32), axis=1),
                              N_EXPERTS - 1)
    return src_tok, dest, tile_expert


def _take_rows(a, idx):
    return a.at[idx].get(mode="promise_in_bounds")


def kernel(x_prompt, x_sample, c, state_hgrn, state_mlstm_C, state_mlstm_n, state_mlstm_m, c_ctx,
           w_ada, b_ada, w_in, b_in, hgrn_lb_raw, hgrn_norm_g, conv_w, conv_b, w_mq, w_mk,
           mlstm_fbias, mlstm_norm_g, w_branch_a, w_branch_b, w_out, ln1_g, ln1_b, ln2_g, ln2_b,
           ffn_w1, ffn_w3, ffn_w2, moe_router_w, moe_router_b, moe_w1, moe_w3, moe_w2):
    nb, seq, _ = x_prompt.shape
    db, dseq, _ = x_sample.shape
    n_ctx, n_lat = nb * seq, db * dseq
    n = n_ctx + n_lat
    assert n_ctx == N_CTX and dseq == SEQ_GROUP and db == 2

    lb = jnp.cumsum(jax.nn.softmax(hgrn_lb_raw.astype(f32), axis=1), axis=1)
    lb = lb - lb[:, :1]

    cond8 = jnp.concatenate([c_ctx[None, :], c, jnp.zeros((5, D), f32)], axis=0)
    mod = modulation(cond8, w_ada, b_ada)[:, :3, :].reshape(DEPTH, 3, 6, D)
    mod = jnp.pad(mod, ((0, 0), (0, 0), (0, 2), (0, 0)))

    x = jnp.concatenate([x_prompt.reshape(n_ctx, D), x_sample.reshape(n_lat, D)], axis=0)
    h = modulate(x, mod[0])

    sizes = (D,) * 8 + (4 * NH, D, D)
    offs = [0]
    for s in sizes:
        offs.append(offs[-1] + s)

    def cols(w, idxs):
        return jnp.concatenate([w[..., offs[i]:offs[i + 1]] for i in idxs], axis=-1)

    out_n, out_m = [], []
    buf_a = buf_c = None
    for l in range(DEPTH):
        wl, bl = w_in[l], b_in[l]
        qig = linear(h, cols(wl, (0, 3, 4)).astype(bf16), cols(bl, (0, 3, 4)), bf16, "proj_qig")
        lbl = lb[:, l].reshape(1, 2 * D)
        lbp = jnp.concatenate([lbl, 1.0 - lbl, jnp.zeros((6, 2 * D), f32)], axis=0)
        lf, kk = hgrn_gates(h, cols(wl, (1, 2)).astype(bf16), cols(bl, (1, 2)), lbp)
        bb = linear(h, cols(wl, (5, 6, 7)).astype(bf16), cols(bl, (5, 6, 7)), bf16, "proj_b")
        gm = linear(h, cols(wl, (9, 10)).astype(bf16), cols(bl, (9, 10)), bf16, "proj_gm")
        w_g = jnp.pad(cols(wl, (8,)), ((0, 0), (0, DH - 4 * NH))).astype(bf16)
        gates = linear(h, w_g, jnp.pad(cols(bl, (8,)), (0, DH - 4 * NH)), f32, "proj_gates")
        gates_t = gates.T

        gain_a = hgrn_norm_g[l].reshape(1, DH)
        ya, buf_a = hgrn_scan(qig, lf, kk, gain_a, None, None, buf_a, l, n_seq=nb, T=seq,
                              row_blk0=0, name="hgrn_ctx")
        (ya,) = hgrn_scan(qig, lf, kk, gain_a, state_hgrn[:, l], ya, None, None, n_seq=db, T=dseq,
                          row_blk0=n_ctx // dseq, name="hgrn_lat")

        cw9 = conv_w[l].reshape(9, D)
        cb = conv_b[l].reshape(1, D)
        wq = w_mq[l].astype(bf16)
        wk = w_mk[l].astype(bf16)
        fb = mlstm_fbias[l]
        zero8 = jnp.zeros((NH,), f32)
        fb_row = jnp.concatenate([zero8, fb[0], zero8, fb[1], jnp.zeros((DH - 4 * NH,), f32)]).reshape(1, DH)
        fb_col = fb.reshape(2, NH, 1)
        gain_b = mlstm_norm_g[l].reshape(1, DH)
        yb, buf_c, s_n, s_m = mlstm_scan(bb, gates, gates_t, cw9, cb, wq, wk, fb_row, fb_col, gain_b,
                                         None, None, buf_c, l, n_seq=nb, T=seq, row_blk0=0,
                                         grid_rows=None, name="mlstm_ctx")
        c0aug = jnp.concatenate([state_mlstm_C[:, l], state_mlstm_n[:, l][..., None],
                                 jnp.zeros((db, 2, NH, DH, DH - 1), f32)], axis=-1)
        m0 = state_mlstm_m[:, l].reshape(db, 2, NH, 1, 1)
        (yb,) = mlstm_scan(bb, gates, gates_t, cw9, cb, wq, wk, fb_row, fb_col, gain_b,
                           (c0aug, m0), yb, None, None, n_seq=db, T=dseq, row_blk0=n_ctx // dseq,
                           grid_rows=dseq // GRID_W, name="mlstm_lat")

        x1, hh = merge(ya, yb, gm, x, mod[l], w_branch_a[l].astype(bf16), w_branch_b[l].astype(bf16),
                       w_out[l].astype(bf16), ln1_g[l], ln1_b[l], bf16 if l % 2 == 0 else f32)

        mod_next = mod[(l + 1) % DEPTH]
        j = l // 2
        if l % 2 == 0:
            x, h = ffn(hh, ffn_w1[j].astype(bf16), ffn_w3[j].astype(bf16), ffn_w2[j].astype(bf16),
                       x1, mod[l], mod_next, ln2_g[l], ln2_b[l])
        else:
            wr = jnp.pad(moe_router_w[j], ((0, 0), (0, DH - N_EXPERTS)))
            br = jnp.concatenate([moe_router_b[j], jnp.full((DH - N_EXPERTS,), -1e30, f32)]).reshape(1, DH)
            rt, counts = router(hh, wr, br)
            src_tok, dest, tile_expert = moe_dispatch(rt, counts, n)
            r_total = src_tok.shape[0]
            part = r_total // MOE_PARTS
            tiles = part // MOE_TM
            ys = None
            for k in range(MOE_PARTS):
                ys = moe_experts(tile_expert[k * tiles:(k + 1) * tiles],
                                 _take_rows(hh, src_tok[k * part:(k + 1) * part]),
                                 moe_w1, moe_w3, moe_w2, j, ys, k * tiles, r_total)
            x, h = combine(_take_rows(ys, dest[:, 0]), _take_rows(ys, dest[:, 1]), rt,
                           x1, mod[l], mod_next, ln2_g[l], ln2_b[l])

        out_n.append(s_n.reshape(nb, 2, NH, DH))
        out_m.append(s_m[..., 0, 0])

    y_prompt = x[:n_ctx].reshape(nb, seq, D)
    y_sample = x[n_ctx:].reshape(db, dseq, D)
    return (y_prompt, y_sample, buf_a, buf_c, jnp.stack(out_n, axis=1), jnp.stack(out_m, axis=1))
```

```python
import functools

import jax
import jax.numpy as jnp
from jax import lax
from jax.experimental import pallas as pl
from jax.experimental.pallas import tpu as pltpu

f32 = jnp.float32
bf16 = jnp.bfloat16

D = 1024
DEPTH = 4
GRID_W = 64
NH = 8
DH = 128
CH_A = 16
HALF_A = 8
SC_A = 128
SCAN_ROWS_PER_STEP = 4096
CH_B = 256
N_EXPERTS = 8
D_FF = 2816
ALPHA = (2 * DEPTH) ** 0.25
SEQ_GROUP = 1024
LOG2E = 1.4426950408889634
LOG2_F_FLOOR = -1e30
MOE_TM = 512

VMEM_LIMIT = 56 * 1024 * 1024
NT = (((1,), (1,)), ((), ()))
TN = (((0,), (0,)), ((), ()))


def _heads_per_step(seq_len):
    return max(1, min(NH, SCAN_ROWS_PER_STEP // seq_len))


def _cparams(*sem):
    return pltpu.CompilerParams(dimension_semantics=sem, vmem_limit_bytes=VMEM_LIMIT)


def _sigmoid(x):
    return 1.0 / (1.0 + jnp.exp(-x))


def _log_sigmoid(t):
    return jnp.minimum(t, 0.0) - jnp.log1p(jnp.exp(-jnp.abs(t)))


def _layer_norm(r, g, b):
    mu = jnp.mean(r, axis=-1, keepdims=True)
    d = r - mu
    var = jnp.mean(d * d, axis=-1, keepdims=True)
    return d * lax.rsqrt(var + 1e-5) * g + b


def _mod_kernel(c_ref, w_ref, b_ref, o_ref):
    c = c_ref[...]
    act = (c * _sigmoid(c)).astype(bf16)
    o_ref[...] = jnp.dot(act, w_ref[...].astype(bf16), preferred_element_type=f32) + b_ref[...]


def modulation(cond8, w_ada, b_ada):
    L, _, n = w_ada.shape
    tn = 1536
    return pl.pallas_call(
        _mod_kernel,
        grid=(L, n // tn),
        in_specs=[pl.BlockSpec((8, D), lambda l, j: (0, 0)),
                  pl.BlockSpec((None, D, tn), lambda l, j: (l, 0, j)),
                  pl.BlockSpec((None, 1, tn), lambda l, j: (l, 0, j))],
        out_specs=pl.BlockSpec((None, 8, tn), lambda l, j: (l, 0, j)),
        out_shape=jax.ShapeDtypeStruct((L, 8, n), f32),
        compiler_params=_cparams("parallel", "parallel"),
        name="modulation",
    )(cond8, w_ada, b_ada.reshape(L, 1, n))


N_CTX = 32 * 256


def _group_of_tile(tm):
    def idx(i):
        return jnp.maximum(i * tm // SEQ_GROUP - (N_CTX // SEQ_GROUP - 1), 0)
    return idx


def _modulate_kernel(x_ref, mod_ref, h_ref):
    h_ref[...] = (x_ref[...] * (1.0 + mod_ref[1:2, :]) + mod_ref[0:1, :]).astype(h_ref.dtype)


def modulate(x, mod_l):
    n = x.shape[0]
    tm = 1024
    grp = _group_of_tile(tm)
    return pl.pallas_call(
        _modulate_kernel,
        grid=(n // tm,),
        in_specs=[pl.BlockSpec((tm, D), lambda i: (i, 0)),
                  pl.BlockSpec((None, 8, D), lambda i: (grp(i), 0, 0))],
        out_specs=pl.BlockSpec((tm, D), lambda i: (i, 0)),
        out_shape=jax.ShapeDtypeStruct((n, D), bf16),
        compiler_params=_cparams("parallel"),
        name="modulate",
    )(x, mod_l)


def _linear_kernel(x_ref, w_ref, b_ref, o_ref):
    acc = jnp.dot(x_ref[...], w_ref[...], preferred_element_type=f32)
    o_ref[...] = (acc + b_ref[...]).astype(o_ref.dtype)


def linear(x, w, b, out_dtype, name):
    m, k = x.shape
    n = w.shape[1]
    tm = 2048
    tn = 1024 if n % 1024 == 0 else n
    return pl.pallas_call(
        _linear_kernel,
        grid=(m // tm, n // tn),
        in_specs=[pl.BlockSpec((tm, k), lambda i, j: (i, 0)),
                  pl.BlockSpec((k, tn), lambda i, j: (0, j)),
                  pl.BlockSpec((1, tn), lambda i, j: (0, j))],
        out_specs=pl.BlockSpec((tm, tn), lambda i, j: (i, j)),
        out_shape=jax.ShapeDtypeStruct((m, n), out_dtype),
        compiler_params=_cparams("parallel", "parallel"),
        name=name,
    )(x, w, b.reshape(1, n))


def _hgrn_gate_kernel(x_ref, w_ref, b_ref, lbp_ref, lf_ref, kk_ref):
    z = jnp.dot(x_ref[...], w_ref[...], preferred_element_type=f32) + b_ref[...]
    lb = lbp_ref[0:1, :]
    one_m_lb = lbp_ref[1:2, :]
    e = jnp.exp(-jnp.abs(z))
    t = one_m_lb / (1.0 + e)
    kk_ref[...] = t * jnp.where(z >= 0.0, e, 1.0)
    f = lb + t * jnp.where(z >= 0.0, 1.0, e)
    lf_ref[...] = jnp.maximum(jnp.log(f) * LOG2E, LOG2_F_FLOOR)


def hgrn_gates(x, w, b, lbp):
    m, k = x.shape
    n = w.shape[1]
    tm, tn = 1024, 1024
    return pl.pallas_call(
        _hgrn_gate_kernel,
        grid=(m // tm, n // tn),
        in_specs=[pl.BlockSpec((tm, k), lambda i, j: (i, 0)),
                  pl.BlockSpec((k, tn), lambda i, j: (0, j)),
                  pl.BlockSpec((1, tn), lambda i, j: (0, j)),
                  pl.BlockSpec((8, tn), lambda i, j: (0, j))],
        out_specs=[pl.BlockSpec((tm, tn), lambda i, j: (i, j)),
                   pl.BlockSpec((tm, tn), lambda i, j: (i, j))],
        out_shape=[jax.ShapeDtypeStruct((m, n), f32), jax.ShapeDtypeStruct((m, n), f32)],
        compiler_params=_cparams("parallel", "parallel"),
        name="proj_z",
    )(x, w, b.reshape(1, n), lbp)


def _hgrn_kernel(*refs, has_state, want_state, n_aliased):
    q_ref, v_ref, g_ref, lff_ref, lfb_ref, kkf_ref, kkb_ref, gain_ref = refs[:8]
    lf_refs = (lff_ref, lfb_ref)
    kk_refs = (kkf_ref, kkb_ref)
    pos = 8
    s0_ref = None
    if has_state:
        s0_ref = refs[pos]
        pos += 1
    pos += n_aliased
    y_ref = refs[pos]
    pos += 1
    st_ref = None
    if want_state:
        st_ref = refs[pos]
        pos += 1
    b_ref, o_ref = refs[pos:]

    T = q_ref.shape[0]
    hps = q_ref.shape[1] // DH
    nsc = T // SC_A

    row = lax.broadcasted_iota(jnp.int32, (SC_A, SC_A), 0)
    col = lax.broadcasted_iota(jnp.int32, (SC_A, SC_A), 1)
    tris = tuple(jnp.where(m, 1.0, 0.0).astype(bf16) for m in (row >= col, row <= col))
    nblk = SC_A // CH_A
    half = SC_A // 2
    r_sq = lax.broadcasted_iota(jnp.int32, (half, half), 0)
    c_sq = lax.broadcasted_iota(jnp.int32, (half, half), 1)
    r_w = lax.broadcasted_iota(jnp.int32, (half, SC_A), 0)
    c_w = lax.broadcasted_iota(jnp.int32, (half, SC_A), 1)
    masks_lo = (r_sq >= c_sq, r_w <= c_w)
    masks_hi = (r_w + half >= c_w, r_sq <= c_sq)

    for d in range(2):
        for sc in range(nsc):
            rows = slice(sc * SC_A, (sc + 1) * SC_A)
            b_ref[d, rows, :] = _tri_matmul(tris[d], lf_refs[d][rows, :])

    if has_state:
        st0 = tuple(s0_ref[d, hh].T for hh in range(hps) for d in range(2))
    else:
        st0 = tuple(jnp.zeros((DH, DH), f32) for _ in range(2 * hps))

    def body(c, carry):
        units = []
        for hd in range(2 * hps):
            hh, d = divmod(hd, 2)
            ls = slice(hh * DH, (hh + 1) * DH)
            cidx = c if d == 0 else nsc - 1 - c
            r = pl.ds(pl.multiple_of(cidx * SC_A, SC_A), SC_A)
            units.append((d, ls, r))

        scores = []
        for d, ls, r in units:
            q = q_ref[r, ls].astype(f32)
            kk = kk_refs[d][r, ls]
            b = b_ref[d, r, ls]
            att_rows = ([], [])
            for i in range(nblk):
                mid = i * CH_A + (HALF_A - 1 if d == 0 else HALF_A)
                r_i = b[mid:mid + 1, :]
                blk = slice(i * CH_A, (i + 1) * CH_A)
                qi = (q[blk] * jnp.exp2(b[blk] - r_i)).astype(bf16)
                if d == 0 and i < nblk // 2:
                    ksl = slice(0, half)
                elif d == 1 and i >= nblk // 2:
                    ksl = slice(half, SC_A)
                else:
                    ksl = slice(0, SC_A)
                ks = (kk[ksl] * jnp.exp2(r_i - b[ksl])).astype(bf16)
                att_rows[i >= nblk // 2].append(lax.dot_general(qi, ks, NT, preferred_element_type=f32))
            scores.append(att_rows)

        for hd, (d, ls, r) in enumerate(units):
            v = v_ref[r, ls]
            a_lo = jnp.where(masks_lo[d], jnp.concatenate(scores[hd][0], axis=0), 0.0).astype(bf16)
            a_hi = jnp.where(masks_hi[d], jnp.concatenate(scores[hd][1], axis=0), 0.0).astype(bf16)
            o_lo = jnp.dot(a_lo, v[:half] if d == 0 else v, preferred_element_type=f32)
            o_hi = jnp.dot(a_hi, v if d == 0 else v[half:], preferred_element_type=f32)
            qb = (q_ref[r, ls].astype(f32) * jnp.exp2(b_ref[d, r, ls])).astype(bf16)
            o_ref[d, r, ls] = (jnp.concatenate([o_lo, o_hi], axis=0)
                               + lax.dot_general(qb, carry[hd].astype(bf16), NT,
                                                 preferred_element_type=f32))

        new = []
        for hd, (d, ls, r) in enumerate(units):
            b = b_ref[d, r, ls]
            b_end = b[SC_A - 1:SC_A, :] if d == 0 else b[0:1, :]
            kl = (kk_refs[d][r, ls] * jnp.exp2(b_end - b)).astype(bf16)
            new.append(jnp.exp2(b_end) * carry[hd]
                       + lax.dot_general(v_ref[r, ls], kl, TN, preferred_element_type=f32))
        return tuple(new)

    st_fin = lax.fori_loop(0, nsc, body, st0, unroll=2)

    for hh in range(hps):
        ls = slice(hh * DH, (hh + 1) * DH)
        o = o_ref[0, :, ls] + o_ref[1, :, ls]
        ms = jnp.mean(o * o, axis=-1, keepdims=True)
        g = g_ref[:, ls].astype(f32)
        y_ref[:, ls] = (o * lax.rsqrt(ms + 1e-6) * gain_ref[...] * (g * _sigmoid(g))).astype(y_ref.dtype)
        if want_state:
            for d in range(2):
                st_ref[d, hh] = st_fin[2 * hh + d].T


def hgrn_scan(qig, lf, kk, gain, s0, y_buf, state_buf, layer, *, n_seq, T, row_blk0, name):
    has_state = s0 is not None
    want_state = layer is not None

    hps = _heads_per_step(T)
    wh = hps * DH

    def col(off):
        return lambda b, h: (row_blk0 + b, off // hps + h)

    in_specs = [pl.BlockSpec((T, wh), col(0)), pl.BlockSpec((T, wh), col(NH)),
                pl.BlockSpec((T, wh), col(2 * NH)),
                pl.BlockSpec((T, wh), col(0)), pl.BlockSpec((T, wh), col(NH)),
                pl.BlockSpec((T, wh), col(0)), pl.BlockSpec((T, wh), col(NH)),
                pl.BlockSpec((1, DH), lambda b, h: (0, 0))]
    args = [qig, qig, qig, lf, lf, kk, kk, gain]
    if has_state:
        in_specs.append(pl.BlockSpec((None, 2, hps, DH, DH), lambda b, h: (b, 0, h, 0, 0)))
        args.append(s0)
    out_specs = [pl.BlockSpec((T, wh), col(0))]
    out_shape = [jax.ShapeDtypeStruct((qig.shape[0], D), bf16)]
    aliases = {}
    if y_buf is not None:
        in_specs.append(pl.BlockSpec(memory_space=pl.ANY))
        args.append(y_buf)
        aliases[len(args) - 1] = 0
    if want_state:
        out_specs.append(pl.BlockSpec((None, None, 2, hps, DH, DH),
                                      lambda b, h: (b, layer, 0, h, 0, 0)))
        out_shape.append(jax.ShapeDtypeStruct((n_seq, DEPTH, 2, NH, DH, DH), f32))
        if state_buf is not None:
            in_specs.append(pl.BlockSpec(memory_space=pl.ANY))
            args.append(state_buf)
            aliases[len(args) - 1] = 1
    scratch = [pltpu.VMEM((2, T, wh), f32)] * 2
    return pl.pallas_call(
        functools.partial(_hgrn_kernel, has_state=has_state, want_state=want_state,
                          n_aliased=len(aliases)),
        grid=(n_seq, NH // hps),
        in_specs=in_specs, out_specs=out_specs, out_shape=out_shape,
        scratch_shapes=scratch,
        input_output_aliases=aliases,
        compiler_params=_cparams("parallel", "parallel"),
        name=name,
    )(*args)


def _split3(x):
    x0 = x.astype(bf16)
    r = x - x0.astype(f32)
    x1 = r.astype(bf16)
    x2 = (r - x1.astype(f32)).astype(bf16)
    return x0, x1, x2


def _tri_matmul(tri, x):
    return sum(jnp.dot(tri, p, preferred_element_type=f32) for p in _split3(x))


def _matmul_tri(x, tri):
    return sum(jnp.dot(p, tri, preferred_element_type=f32) for p in _split3(x))


def _mlstm_kernel(*refs, grid_rows, has_state, want_state, n_aliased):
    x_ref, v_ref, og_ref, gc_ref, gr_ref, cw_ref, cb_ref, wq_ref, wk_ref, fbr_ref, fbc_ref, gain_ref = refs[:12]
    pos = 12
    if has_state:
        c0_ref, m0_ref = refs[pos:pos + 2]
        pos += 2
    pos += n_aliased
    y_ref = refs[pos]
    pos += 1
    if want_state:
        ct_ref, nt_ref, mt_ref = refs[pos:pos + 3]

    hps = x_ref.shape[1] // DH
    head0 = pl.program_id(1) * hps
    T = x_ref.shape[0]
    L = min(CH_B, T)
    nchunk = T // L

    t_idx = lax.broadcasted_iota(jnp.int32, (T, DH), 0)
    lane = lax.broadcasted_iota(jnp.int32, (T, DH), 1)
    ones_col = jnp.where(lane == 0, 1.0, 0.0)
    if grid_rows is None:
        taps = [(0, dc) for dc in (-1, 0, 1)]
    else:
        taps = [(dr, dc) for dr in (-1, 0, 1) for dc in (-1, 0, 1)]
        c_idx = t_idx % GRID_W
    cas, vaugs = [], []
    for hh in range(hps):
        ls = slice(hh * DH, (hh + 1) * DH)
        x = x_ref[:, ls].astype(f32)
        acc = jnp.zeros((T, DH), f32) + cb_ref[:, ls]
        for dr, dc in taps:
            off = dr * GRID_W + dc
            tap = (dr + 1) * 3 + (dc + 1)
            w = cw_ref[tap:tap + 1, ls]
            xs = x if off == 0 else pltpu.roll(x, (-off) % T, axis=0)
            ok = (t_idx + off >= 0) & (t_idx + off < T)
            if grid_rows is not None:
                ok = ok & (c_idx + dc >= 0) & (c_idx + dc < GRID_W)
            acc = acc + jnp.where(ok, xs, 0.0) * w
        cas.append((acc * _sigmoid(acc)).astype(bf16))
        vaugs.append(jnp.concatenate([v_ref[:, ls].astype(f32), ones_col], axis=1))
    qs = [jnp.dot(cas[hh], wq_ref[hh], preferred_element_type=f32).astype(bf16) for hh in range(hps)]
    ks = [(jnp.dot(cas[hh], wk_ref[hh], preferred_element_type=f32) * (DH ** -0.5)).astype(bf16)
          for hh in range(hps)]

    gate_lane = lax.broadcasted_iota(jnp.int32, (T, DH), 1) // NH
    is_f = (gate_lane == 1) | (gate_lane == 3)
    gc = gc_ref[...] + fbr_ref[...]
    gc = jnp.where(is_f, _log_sigmoid(gc), gc)

    def pick_lane(a, ln):
        sel = lax.broadcasted_iota(jnp.int32, a.shape, 1) == ln
        return jnp.sum(jnp.where(sel, a, 0.0), axis=1, keepdims=True)

    def pick_row(a, rw):
        sel = lax.broadcasted_iota(jnp.int32, a.shape, 0) == rw
        return jnp.sum(jnp.where(sel, a, 0.0), axis=0, keepdims=True)

    r_i = lax.broadcasted_iota(jnp.int32, (L, L), 0)
    c_i = lax.broadcasted_iota(jnp.int32, (L, L), 1)
    lower = r_i >= c_i
    upper = r_i <= c_i
    tri_lo = jnp.where(lower, 1.0, 0.0).astype(bf16)
    tri_up = jnp.where(upper, 1.0, 0.0).astype(bf16)

    h_parts = [[[None] * nchunk for _ in range(2)] for _ in range(hps)]
    units = [(hh, d) for hh in range(hps) for d in range(2)]
    if has_state:
        carry = {u: (c0_ref[u[1], u[0]], m0_ref[u[1], u[0]]) for u in units}
    else:
        carry = {u: (jnp.zeros((DH, 2 * DH), f32), jnp.zeros((1, 1), f32)) for u in units}
    masks = (lower, upper)
    tri_cs = (tri_lo, tri_up)
    tri_rs = (tri_up, tri_lo)

    for step in range(nchunk):
        cidx = (step, nchunk - 1 - step)
        rsl = tuple(slice(c * L, (c + 1) * L) for c in cidx)

        g = {}
        for d in range(2):
            gcc = gc[rsl[d]]
            lf_r8 = _log_sigmoid(gr_ref[(2 * d + 1) * NH:(2 * d + 2) * NH, rsl[d]] + fbc_ref[d])
            g[d] = (gcc, _tri_matmul(tri_cs[d], gcc), _matmul_tri(lf_r8, tri_rs[d]),
                    gr_ref[2 * d * NH:(2 * d + 1) * NH, rsl[d]])
        picks = {}
        for hh, d in units:
            head = head0 + hh
            gcc, F_c_all, F_r_all, ig_r_all = g[d]
            ig_c = pick_lane(gcc, 2 * d * NH + head)
            F_c = pick_lane(F_c_all, (2 * d + 1) * NH + head)
            ig_r = pick_row(ig_r_all, head)
            F_r = pick_row(F_r_all, head)
            F_end = F_c[L - 1:L, :] if d == 0 else F_c[0:1, :]
            picks[hh, d] = (ig_c, F_c, ig_r, F_r, F_end)

        decay = {}
        for u in units:
            ig_c, F_c, ig_r, F_r, F_end = picks[u]
            m_prev = carry[u][1]
            logd = jnp.where(masks[u[1]], F_c - F_r + ig_r, -jnp.inf)
            log_inter = F_c + m_prev
            m_q = jnp.maximum(log_inter, jnp.max(logd, axis=-1, keepdims=True))
            decay[u] = (jnp.exp(logd - m_q), jnp.exp(log_inter - m_q), m_q)

        scores = {}
        for hh, d in units:
            qc, kc = qs[hh][rsl[d]], ks[hh][rsl[d]]
            scores[hh, d] = (lax.dot_general(qc, kc, NT, preferred_element_type=f32)
                             * decay[hh, d][0]).astype(bf16)

        for hh, d in units:
            _, a, m_q = decay[hh, d]
            qc = qs[hh][rsl[d]]
            va = vaugs[hh][rsl[d]]
            nd = (a * jnp.dot(qc, carry[hh, d][0].astype(bf16), preferred_element_type=f32)
                  + jnp.dot(scores[hh, d], va.astype(bf16), preferred_element_type=f32))
            num = nd[:, :DH]
            den = nd[:, DH:DH + 1]
            h_parts[hh][d][cidx[d]] = num / jnp.maximum(jnp.abs(den), jnp.exp(-m_q))

        for hh, d in units:
            ig_c, F_c, ig_r, F_r, F_end = picks[hh, d]
            caug, m_prev = carry[hh, d]
            log_w_c = F_end - F_c + ig_c
            log_w_r = F_end - F_r + ig_r
            m_new = jnp.maximum(F_end + m_prev, jnp.max(log_w_r, axis=-1, keepdims=True))
            ws_c = jnp.exp(log_w_c - m_new)
            d0 = jnp.exp(F_end + m_prev - m_new)
            upd = lax.dot_general(ks[hh][rsl[d]], (ws_c * vaugs[hh][rsl[d]]).astype(bf16), TN,
                                  preferred_element_type=f32)
            carry[hh, d] = (d0 * caug + upd, m_new)
    finals = [[carry[hh, d] for d in range(2)] for hh in range(hps)]

    for hh in range(hps):
        ls = slice(hh * DH, (hh + 1) * DH)
        hsum = jnp.concatenate([h_parts[hh][0][c] + h_parts[hh][1][c] for c in range(nchunk)], axis=0)
        mu = jnp.mean(hsum, axis=-1, keepdims=True)
        dv = hsum - mu
        var = jnp.mean(dv * dv, axis=-1, keepdims=True)
        og = og_ref[:, ls].astype(f32)
        y_ref[:, ls] = (dv * lax.rsqrt(var + 1e-6) * gain_ref[...] * _sigmoid(og)).astype(y_ref.dtype)
        if want_state:
            for d in range(2):
                caug, m_fin = finals[hh][d]
                ct_ref[d, hh] = caug[:, :DH]
                nt_ref[d, hh] = caug[:, DH:2 * DH].T[0:1, :]
                mt_ref[d, hh] = jnp.broadcast_to(m_fin, (1, DH))


def mlstm_scan(bb, gates, gates_t, conv_w9, conv_b, wq, wk, fb_row, fb_col, gain, state, y_buf, c_buf,
               layer, *, n_seq, T, row_blk0, grid_rows, name):
    has_state = state is not None
    want_state = layer is not None

    hps = _heads_per_step(T)
    wh = hps * DH

    def col(off):
        return lambda b, h: (row_blk0 + b, off // hps + h)

    in_specs = [pl.BlockSpec((T, wh), col(0)), pl.BlockSpec((T, wh), col(NH)),
                pl.BlockSpec((T, wh), col(2 * NH)),
                pl.BlockSpec((T, DH), lambda b, h: (row_blk0 + b, 0)),
                pl.BlockSpec((DH, T), lambda b, h: (0, row_blk0 + b)),
                pl.BlockSpec((9, wh), lambda b, h: (0, h)),
                pl.BlockSpec((1, wh), lambda b, h: (0, h)),
                pl.BlockSpec((hps, DH, DH), lambda b, h: (h, 0, 0)),
                pl.BlockSpec((hps, DH, DH), lambda b, h: (h, 0, 0)),
                pl.BlockSpec((1, DH), lambda b, h: (0, 0)),
                pl.BlockSpec((2, NH, 1), lambda b, h: (0, 0, 0)),
                pl.BlockSpec((1, DH), lambda b, h: (0, 0))]
    args = [bb, bb, bb, gates, gates_t, conv_w9, conv_b, wq, wk, fb_row, fb_col, gain]
    if has_state:
        c0aug, m0 = state
        in_specs += [pl.BlockSpec((None, 2, hps, DH, 2 * DH), lambda b, h: (b, 0, h, 0, 0)),
                     pl.BlockSpec((None, 2, hps, 1, 1), lambda b, h: (b, 0, h, 0, 0))]
        args += [c0aug, m0]
    out_specs = [pl.BlockSpec((T, wh), col(0))]
    out_shape = [jax.ShapeDtypeStruct((bb.shape[0], D), bf16)]
    aliases = {}
    if y_buf is not None:
        in_specs.append(pl.BlockSpec(memory_space=pl.ANY))
        args.append(y_buf)
        aliases[len(args) - 1] = 0
    if want_state:
        out_specs += [pl.BlockSpec((None, None, 2, hps, DH, DH), lambda b, h: (b, layer, 0, h, 0, 0)),
                      pl.BlockSpec((None, 2, hps, 1, DH), lambda b, h: (b, 0, h, 0, 0)),
                      pl.BlockSpec((None, 2, hps, 1, DH), lambda b, h: (b, 0, h, 0, 0))]
        out_shape += [jax.ShapeDtypeStruct((n_seq, DEPTH, 2, NH, DH, DH), f32),
                      jax.ShapeDtypeStruct((n_seq, 2, NH, 1, DH), f32),
                      jax.ShapeDtypeStruct((n_seq, 2, NH, 1, DH), f32)]
        if c_buf is not None:
            in_specs.append(pl.BlockSpec(memory_space=pl.ANY))
            args.append(c_buf)
            aliases[len(args) - 1] = 1
    return pl.pallas_call(
        functools.partial(_mlstm_kernel, grid_rows=grid_rows, has_state=has_state,
                          want_state=want_state, n_aliased=len(aliases)),
        grid=(n_seq, NH // hps),
        in_specs=in_specs, out_specs=out_specs, out_shape=out_shape,
        input_output_aliases=aliases,
        compiler_params=_cparams("parallel", "parallel"),
        name=name,
    )(*args)


def _software_pipeline(n, stages):
    vals = [None] * n
    for t in range(n + len(stages) - 1):
        for k in range(len(stages) - 1, -1, -1):
            s = t - k
            if 0 <= s < n:
                vals[s] = stages[k](s, vals[s])


SUB_ROWS = 256


def _merge_kernel(ya_ref, yb_ref, gm_ref, x_ref, mod_ref, wa_ref, wb_ref, wo_ref, lg_ref, lb_ref,
                  x1_ref, hh_ref):
    def rows(s):
        return slice(s * SUB_ROWS, (s + 1) * SUB_ROWS)

    def branches(s, _):
        return (jnp.dot(ya_ref[rows(s), :], wa_ref[...], preferred_element_type=f32),
                jnp.dot(yb_ref[rows(s), :], wb_ref[...], preferred_element_type=f32))

    def gate(s, p):
        gma = gm_ref[rows(s), :D].astype(f32)
        gmb = gm_ref[rows(s), D:].astype(f32)
        return (_sigmoid(gma) * p[0] + _sigmoid(gmb) * p[1]).astype(bf16)

    def project(s, merged):
        return jnp.dot(merged, wo_ref[...], preferred_element_type=f32)

    def norm(s, y):
        x1 = _layer_norm(ALPHA * x_ref[rows(s), :] + mod_ref[2:3, :] * y, lg_ref[...], lb_ref[...])
        x1_ref[rows(s), :] = x1
        hh_ref[rows(s), :] = (x1 * (1.0 + mod_ref[4:5, :]) + mod_ref[3:4, :]).astype(hh_ref.dtype)

    _software_pipeline(ya_ref.shape[0] // SUB_ROWS, (branches, gate, project, norm))


def merge(ya, yb, gm, x, mod_l, wa, wb, wo, lg, lb, hh_dtype):
    n = x.shape[0]
    tm = 512
    grp = _group_of_tile(tm)
    row = lambda i: (i, 0)
    const = lambda i: (0, 0)
    return pl.pallas_call(
        _merge_kernel,
        grid=(n // tm,),
        in_specs=[pl.BlockSpec((tm, D), row), pl.BlockSpec((tm, D), row),
                  pl.BlockSpec((tm, 2 * D), row), pl.BlockSpec((tm, D), row),
                  pl.BlockSpec((None, 8, D), lambda i: (grp(i), 0, 0)),
                  pl.BlockSpec((D, D), const), pl.BlockSpec((D, D), const), pl.BlockSpec((D, D), const),
                  pl.BlockSpec((1, D), const), pl.BlockSpec((1, D), const)],
        out_specs=[pl.BlockSpec((tm, D), row), pl.BlockSpec((tm, D), row)],
        out_shape=[jax.ShapeDtypeStruct((n, D), f32), jax.ShapeDtypeStruct((n, D), hh_dtype)],
        compiler_params=_cparams("parallel"),
        name="merge",
    )(ya, yb, gm, x, mod_l, wa, wb, wo, lg.reshape(1, D), lb.reshape(1, D))


def _post_ffn(f, x1, mod, mod_next, lg, lb, x2_ref, hn_ref):
    x2 = _layer_norm(ALPHA * x1 + mod[5:6, :] * f, lg, lb)
    x2_ref[...] = x2
    hn_ref[...] = (x2 * (1.0 + mod_next[1:2, :]) + mod_next[0:1, :]).astype(hn_ref.dtype)


def _ffn_kernel(hh_ref, w1_ref, w3_ref, w2_ref, x1_ref, mod_ref, modn_ref, lg_ref, lb_ref,
                x2_ref, hn_ref, acc_ref):
    j = pl.program_id(1)

    @pl.when(j == 0)
    def _():
        acc_ref[...] = jnp.zeros_like(acc_ref)

    def rows(s):
        return slice(s * SUB_ROWS, (s + 1) * SUB_ROWS)

    def up(s, _):
        hh = hh_ref[rows(s), :]
        return (jnp.dot(hh, w1_ref[...], preferred_element_type=f32),
                jnp.dot(hh, w3_ref[...], preferred_element_type=f32))

    def act(s, ab):
        return (ab[0] * _sigmoid(ab[0]) * ab[1]).astype(bf16)

    def down(s, h):
        return jnp.dot(h, w2_ref[...], preferred_element_type=f32)

    def accumulate(s, part):
        acc_ref[rows(s), :] += part

    _software_pipeline(hh_ref.shape[0] // SUB_ROWS, (up, act, down, accumulate))

    @pl.when(j == pl.num_programs(1) - 1)
    def _():
        _post_ffn(acc_ref[...], x1_ref[...], mod_ref, modn_ref, lg_ref[...], lb_ref[...],
                  x2_ref, hn_ref)


def ffn(hh, w1, w3, w2, x1, mod_l, mod_next, lg, lb):
    n = hh.shape[0]
    tm, tf = 512, 1408
    grp = _group_of_tile(tm)
    row = lambda i, j: (i, 0)
    const = lambda i, j: (0, 0)
    return pl.pallas_call(
        _ffn_kernel,
        grid=(n // tm, D_FF // tf),
        in_specs=[pl.BlockSpec((tm, D), row),
                  pl.BlockSpec((D, tf), lambda i, j: (0, j)),
                  pl.BlockSpec((D, tf), lambda i, j: (0, j)),
                  pl.BlockSpec((tf, D), lambda i, j: (j, 0)),
                  pl.BlockSpec((tm, D), row),
                  pl.BlockSpec((None, 8, D), lambda i, j: (grp(i), 0, 0)),
                  pl.BlockSpec((None, 8, D), lambda i, j: (grp(i), 0, 0)),
                  pl.BlockSpec((1, D), const), pl.BlockSpec((1, D), const)],
        out_specs=[pl.BlockSpec((tm, D), row), pl.BlockSpec((tm, D), row)],
        out_shape=[jax.ShapeDtypeStruct((n, D), f32), jax.ShapeDtypeStruct((n, D), bf16)],
        scratch_shapes=[pltpu.VMEM((tm, D), f32)],
        compiler_params=_cparams("parallel", "arbitrary"),
        name="ffn",
    )(hh, w1, w3, w2, x1, mod_l, mod_next, lg.reshape(1, D), lb.reshape(1, D))


def _router_kernel(hh_ref, wr_ref, br_ref, o_ref, cnt_ref, run_ref):
    hh = hh_ref[...].astype(f32)
    wr = wr_ref[...]
    h_hi = hh.astype(bf16)
    h_lo = (hh - h_hi.astype(f32)).astype(bf16)
    w_hi = wr.astype(bf16)
    w_lo = (wr - w_hi.astype(f32)).astype(bf16)
    logits = (jnp.dot(h_hi, w_hi, preferred_element_type=f32)
              + jnp.dot(h_hi, w_lo, preferred_element_type=f32)
              + jnp.dot(h_lo, w_hi, preferred_element_type=f32)) + br_ref[...]
    lane = lax.broadcasted_iota(jnp.int32, logits.shape, 1).astype(f32)
    m1 = jnp.max(logits, axis=-1, keepdims=True)
    i1 = jnp.min(jnp.where(logits == m1, lane, float(DH)), axis=-1, keepdims=True)
    rest = jnp.where(lane == i1, -jnp.inf, logits)
    m2 = jnp.max(rest, axis=-1, keepdims=True)
    i2 = jnp.min(jnp.where(rest == m2, lane, float(DH)), axis=-1, keepdims=True)
    e21 = jnp.exp(m2 - m1)
    p1 = 1.0 / (1.0 + e21)
    p2 = e21 * p1

    @pl.when(pl.program_id(0) == 0)
    def _():
        run_ref[...] = jnp.zeros_like(run_ref)

    tm = logits.shape[0]
    r_i = lax.broadcasted_iota(jnp.int32, (tm, tm), 0)
    c_i = lax.broadcasted_iota(jnp.int32, (tm, tm), 1)
    earlier = jnp.where(c_i < r_i, 1.0, 0.0).astype(bf16)
    oh1 = jnp.where(lane == i1, 1.0, 0.0)
    oh2 = jnp.where(lane == i2, 1.0, 0.0)
    pre1 = jnp.dot(earlier, oh1.astype(bf16), preferred_element_type=f32)
    pre2 = jnp.dot(earlier, oh2.astype(bf16), preferred_element_type=f32)
    cnt1 = jnp.sum(oh1, axis=0, keepdims=True)
    cnt2 = jnp.sum(oh2, axis=0, keepdims=True)
    run = run_ref[...]
    rank1 = jnp.sum(oh1 * (run + pre1), axis=-1, keepdims=True)
    rank2 = jnp.sum(oh2 * (run + cnt1 + pre2), axis=-1, keepdims=True)
    run_ref[...] = run + cnt1 + cnt2
    cnt_ref[...] = jnp.broadcast_to(run + cnt1 + cnt2, cnt_ref.shape)
    o_ref[...] = (jnp.where(lane == 0.0, i1, 0.0) + jnp.where(lane == 1.0, i2, 0.0)
                  + jnp.where(lane == 2.0, p1, 0.0) + jnp.where(lane == 3.0, p2, 0.0)
                  + jnp.where(lane == 4.0, rank1, 0.0) + jnp.where(lane == 5.0, rank2, 0.0))


def router(hh, wr_pad, br_pad):
    n = hh.shape[0]
    tm = 1024
    return pl.pallas_call(
        _router_kernel,
        grid=(n // tm,),
        in_specs=[pl.BlockSpec((tm, D), lambda i: (i, 0)),
                  pl.BlockSpec((D, DH), lambda i: (0, 0)),
                  pl.BlockSpec((1, DH), lambda i: (0, 0))],
        out_specs=[pl.BlockSpec((tm, DH), lambda i: (i, 0)),
                   pl.BlockSpec((8, DH), lambda i: (0, 0))],
        out_shape=[jax.ShapeDtypeStruct((n, DH), f32), jax.ShapeDtypeStruct((8, DH), f32)],
        scratch_shapes=[pltpu.VMEM((1, DH), f32)],
        compiler_params=_cparams("arbitrary"),
        name="router",
    )(hh, wr_pad, br_pad)


def _moe_up_kernel(te_ref, x_ref, w1_ref, w3_ref, a_ref):
    del te_ref
    x = x_ref[...].astype(bf16)
    a = jnp.dot(x, w1_ref[...].astype(bf16), preferred_element_type=f32)
    b = jnp.dot(x, w3_ref[...].astype(bf16), preferred_element_type=f32)
    a_ref[...] = (a * _sigmoid(a) * b).astype(a_ref.dtype)


def _moe_down_kernel(te_ref, a_ref, w2_ref, y_ref):
    del te_ref
    y_ref[...] = jnp.dot(a_ref[...], w2_ref[...].astype(bf16), preferred_element_type=f32)


def moe_experts(tile_expert, xs, w1, w3, w2, lyr):
    r = xs.shape[0]
    nt = r // MOE_TM
    tf = 1408
    up = pl.pallas_call(
        _moe_up_kernel,
        grid_spec=pltpu.PrefetchScalarGridSpec(
            num_scalar_prefetch=1,
            grid=(D_FF // tf, nt),
            in_specs=[pl.BlockSpec((MOE_TM, D), lambda j, i, te: (i, 0)),
                      pl.BlockSpec((None, None, D, tf), lambda j, i, te: (lyr, te[i], 0, j)),
                      pl.BlockSpec((None, None, D, tf), lambda j, i, te: (lyr, te[i], 0, j))],
            out_specs=pl.BlockSpec((MOE_TM, tf), lambda j, i, te: (i, j))),
        out_shape=jax.ShapeDtypeStruct((r, D_FF), bf16),
        compiler_params=_cparams("arbitrary", "arbitrary"),
        name="moe_up",
    )(tile_expert, xs, w1, w3)
    return pl.pallas_call(
        _moe_down_kernel,
        grid_spec=pltpu.PrefetchScalarGridSpec(
            num_scalar_prefetch=1,
            grid=(nt,),
            in_specs=[pl.BlockSpec((MOE_TM, D_FF), lambda i, te: (i, 0)),
                      pl.BlockSpec((None, None, D_FF, D), lambda i, te: (lyr, te[i], 0, 0))],
            out_specs=pl.BlockSpec((MOE_TM, D), lambda i, te: (i, 0))),
        out_shape=jax.ShapeDtypeStruct((r, D), f32),
        compiler_params=_cparams("arbitrary"),
        name="moe_down",
    )(tile_expert, up, w2)


def _combine_kernel(ya_ref, yb_ref, rt_ref, x1_ref, mod_ref, modn_ref, lg_ref, lb_ref, x2_ref, hn_ref):
    f = rt_ref[:, 2:3] * ya_ref[...] + rt_ref[:, 3:4] * yb_ref[...]
    _post_ffn(f, x1_ref[...], mod_ref, modn_ref, lg_ref[...], lb_ref[...], x2_ref, hn_ref)


def combine(ya, yb, rt, x1, mod_l, mod_next, lg, lb):
    n = x1.shape[0]
    tm = 1024
    grp = _group_of_tile(tm)
    row = lambda i: (i, 0)
    const = lambda i: (0, 0)
    return pl.pallas_call(
        _combine_kernel,
        grid=(n // tm,),
        in_specs=[pl.BlockSpec((tm, D), row), pl.BlockSpec((tm, D), row),
                  pl.BlockSpec((tm, DH), row), pl.BlockSpec((tm, D), row),
                  pl.BlockSpec((None, 8, D), lambda i: (grp(i), 0, 0)),
                  pl.BlockSpec((None, 8, D), lambda i: (grp(i), 0, 0)),
                  pl.BlockSpec((1, D), const), pl.BlockSpec((1, D), const)],
        out_specs=[pl.BlockSpec((tm, D), row), pl.BlockSpec((tm, D), row)],
        out_shape=[jax.ShapeDtypeStruct((n, D), f32), jax.ShapeDtypeStruct((n, D), bf16)],
        compiler_params=_cparams("parallel"),
        name="combine",
    )(ya, yb, rt, x1, mod_l, mod_next, lg.reshape(1, D), lb.reshape(1, D))


def moe_dispatch(rt, counts, n):
    r_total = 2 * n + N_EXPERTS * MOE_TM
    nt = r_total // MOE_TM
    cnt = counts[0, :N_EXPERTS].astype(jnp.int32)
    padded = (cnt + MOE_TM - 1) // MOE_TM * MOE_TM
    ends = jnp.cumsum(padded)
    starts = ends - padded
    e_idx = rt[:, 0:2].astype(jnp.int32)
    rank = rt[:, 4:6].astype(jnp.int32)
    onehot = e_idx[..., None] == jnp.arange(N_EXPERTS)[None, None, :]
    dest = rank + jnp.sum(jnp.where(onehot, starts[None, None, :], 0), axis=-1)
    tok = jnp.broadcast_to(jnp.arange(n, dtype=jnp.int32)[:, None], (n, 2))
    src_tok = jnp.zeros((r_total,), jnp.int32).at[dest.reshape(-1)].set(
        tok.reshape(-1), unique_indices=True, mode="promise_in_bounds")
    tile_start = jnp.arange(nt, dtype=jnp.int32) * MOE_TM
    tile_expert = jnp.minimum(jnp.sum((tile_start[:, None] >= ends[None, :]).astype(jnp.int32), axis=1),
                              N_EXPERTS - 1)
    return src_tok, dest, tile_expert


def _take_rows(a, idx):
    return a.at[idx].get(mode="promise_in_bounds")


def kernel(x_prompt, x_sample, c, state_hgrn, state_mlstm_C, state_mlstm_n, state_mlstm_m, c_ctx,
           w_ada, b_ada, w_in, b_in, hgrn_lb_raw, hgrn_norm_g, conv_w, conv_b, w_mq, w_mk,
           mlstm_fbias, mlstm_norm_g, w_branch_a, w_branch_b, w_out, ln1_g, ln1_b, ln2_g, ln2_b,
           ffn_w1, ffn_w3, ffn_w2, moe_router_w, moe_router_b, moe_w1, moe_w3, moe_w2):
    nb, seq, _ = x_prompt.shape
    db, dseq, _ = x_sample.shape
    n_ctx, n_lat = nb * seq, db * dseq
    n = n_ctx + n_lat
    assert n_ctx == N_CTX and dseq == SEQ_GROUP and db == 2

    lb = jnp.cumsum(jax.nn.softmax(hgrn_lb_raw.astype(f32), axis=1), axis=1)
    lb = lb - lb[:, :1]

    cond8 = jnp.concatenate([c_ctx[None, :], c, jnp.zeros((5, D), f32)], axis=0)
    mod = modulation(cond8, w_ada, b_ada)[:, :3, :].reshape(DEPTH, 3, 6, D)
    mod = jnp.pad(mod, ((0, 0), (0, 0), (0, 2), (0, 0)))

    x = jnp.concatenate([x_prompt.reshape(n_ctx, D), x_sample.reshape(n_lat, D)], axis=0)
    h = modulate(x, mod[0])

    sizes = (D,) * 8 + (4 * NH, D, D)
    offs = [0]
    for s in sizes:
        offs.append(offs[-1] + s)

    def cols(w, idxs):
        return jnp.concatenate([w[..., offs[i]:offs[i + 1]] for i in idxs], axis=-1)

    out_n, out_m = [], []
    buf_a = buf_c = None
    for l in range(DEPTH):
        wl, bl = w_in[l], b_in[l]
        qig = linear(h, cols(wl, (0, 3, 4)).astype(bf16), cols(bl, (0, 3, 4)), bf16, "proj_qig")
        lbl = lb[:, l].reshape(1, 2 * D)
        lbp = jnp.concatenate([lbl, 1.0 - lbl, jnp.zeros((6, 2 * D), f32)], axis=0)
        lf, kk = hgrn_gates(h, cols(wl, (1, 2)).astype(bf16), cols(bl, (1, 2)), lbp)
        bb = linear(h, cols(wl, (5, 6, 7)).astype(bf16), cols(bl, (5, 6, 7)), bf16, "proj_b")
        gm = linear(h, cols(wl, (9, 10)).astype(bf16), cols(bl, (9, 10)), bf16, "proj_gm")
        w_g = jnp.pad(cols(wl, (8,)), ((0, 0), (0, DH - 4 * NH))).astype(bf16)
        gates = linear(h, w_g, jnp.pad(cols(bl, (8,)), (0, DH - 4 * NH)), f32, "proj_gates")
        gates_t = gates.T

        gain_a = hgrn_norm_g[l].reshape(1, DH)
        ya, buf_a = hgrn_scan(qig, lf, kk, gain_a, None, None, buf_a, l, n_seq=nb, T=seq,
                              row_blk0=0, name="hgrn_ctx")
        (ya,) = hgrn_scan(qig, lf, kk, gain_a, state_hgrn[:, l], ya, None, None, n_seq=db, T=dseq,
                          row_blk0=n_ctx // dseq, name="hgrn_lat")

        cw9 = conv_w[l].reshape(9, D)
        cb = conv_b[l].reshape(1, D)
        wq = w_mq[l].astype(bf16)
        wk = w_mk[l].astype(bf16)
        fb = mlstm_fbias[l]
        zero8 = jnp.zeros((NH,), f32)
        fb_row = jnp.concatenate([zero8, fb[0], zero8, fb[1], jnp.zeros((DH - 4 * NH,), f32)]).reshape(1, DH)
        fb_col = fb.reshape(2, NH, 1)
        gain_b = mlstm_norm_g[l].reshape(1, DH)
        yb, buf_c, s_n, s_m = mlstm_scan(bb, gates, gates_t, cw9, cb, wq, wk, fb_row, fb_col, gain_b,
                                         None, None, buf_c, l, n_seq=nb, T=seq, row_blk0=0,
                                         grid_rows=None, name="mlstm_ctx")
        c0aug = jnp.concatenate([state_mlstm_C[:, l], state_mlstm_n[:, l][..., None],
                                 jnp.zeros((db, 2, NH, DH, DH - 1), f32)], axis=-1)
        m0 = state_mlstm_m[:, l].reshape(db, 2, NH, 1, 1)
        (yb,) = mlstm_scan(bb, gates, gates_t, cw9, cb, wq, wk, fb_row, fb_col, gain_b,
                           (c0aug, m0), yb, None, None, n_seq=db, T=dseq, row_blk0=n_ctx // dseq,
                           grid_rows=dseq // GRID_W, name="mlstm_lat")

        x1, hh = merge(ya, yb, gm, x, mod[l], w_branch_a[l].astype(bf16), w_branch_b[l].astype(bf16),
                       w_out[l].astype(bf16), ln1_g[l], ln1_b[l], bf16 if l % 2 == 0 else f32)

        mod_next = mod[(l + 1) % DEPTH]
        j = l // 2
        if l % 2 == 0:
            x, h = ffn(hh, ffn_w1[j].astype(bf16), ffn_w3[j].astype(bf16), ffn_w2[j].astype(bf16),
                       x1, mod[l], mod_next, ln2_g[l], ln2_b[l])
        else:
            wr = jnp.pad(moe_router_w[j], ((0, 0), (0, DH - N_EXPERTS)))
            br = jnp.concatenate([moe_router_b[j], jnp.full((DH - N_EXPERTS,), -1e30, f32)]).reshape(1, DH)
            rt, counts = router(hh, wr, br)
            src_tok, dest, tile_expert = moe_dispatch(rt, counts, n)
            ys = moe_experts(tile_expert, _take_rows(hh, src_tok), moe_w1, moe_w3, moe_w2, j)
            x, h = combine(_take_rows(ys, dest[:, 0]), _take_rows(ys, dest[:, 1]), rt,
                           x1, mod[l], mod_next, ln2_g[l], ln2_b[l])

        out_n.append(s_n.reshape(nb, 2, NH, DH))
        out_m.append(s_m[..., 0, 0])

    y_prompt = x[:n_ctx].reshape(nb, seq, D)
    y_sample = x[n_ctx:].reshape(db, dseq, D)
    return (y_prompt, y_sample, buf_a, buf_c, jnp.stack(out_n, axis=1), jnp.stack(out_m, axis=1))
```

```python
import functools

import jax
import jax.numpy as jnp
from jax import lax
from jax.experimental import pallas as pl
from jax.experimental.pallas import tpu as pltpu

f32 = jnp.float32
bf16 = jnp.bfloat16

D = 1024
DEPTH = 4
GRID_W = 64
NH = 8
DH = 128
CH_A = 16
HALF_A = 8
SC_A = 128
SCAN_ROWS_PER_STEP = 4096
CH_B = 256
N_EXPERTS = 8
D_FF = 2816
ALPHA = (2 * DEPTH) ** 0.25
SEQ_GROUP = 1024
LOG2E = 1.4426950408889634
LOG2_F_FLOOR = -1e30
MOE_TM = 512

VMEM_LIMIT = 56 * 1024 * 1024
NT = (((1,), (1,)), ((), ()))
TN = (((0,), (0,)), ((), ()))


def _heads_per_step(seq_len):
    return max(1, min(NH, SCAN_ROWS_PER_STEP // seq_len))


def _cparams(*sem):
    return pltpu.CompilerParams(dimension_semantics=sem, vmem_limit_bytes=VMEM_LIMIT)


def _sigmoid(x):
    return 1.0 / (1.0 + jnp.exp(-x))


def _log_sigmoid(t):
    return jnp.minimum(t, 0.0) - jnp.log1p(jnp.exp(-jnp.abs(t)))


def _layer_norm(r, g, b):
    mu = jnp.mean(r, axis=-1, keepdims=True)
    d = r - mu
    var = jnp.mean(d * d, axis=-1, keepdims=True)
    return d * lax.rsqrt(var + 1e-5) * g + b


def _mod_kernel(c_ref, wt_ref, wb_ref, b_ref, o_ref):
    c = c_ref[...]
    act = (c * _sigmoid(c)).astype(bf16)
    half = D // 2
    o_ref[...] = (jnp.dot(act[:, :half], wt_ref[...].astype(bf16), preferred_element_type=f32)
                  + jnp.dot(act[:, half:], wb_ref[...].astype(bf16), preferred_element_type=f32)
                  + b_ref[...])


def modulation(cond8, w_ada, b_ada):
    L, _, n = w_ada.shape
    tn = 1536
    half = D // 2
    return pl.pallas_call(
        _mod_kernel,
        grid=(L, n // tn),
        in_specs=[pl.BlockSpec((8, D), lambda l, j: (0, 0)),
                  pl.BlockSpec((None, half, tn), lambda l, j: (l, 0, j)),
                  pl.BlockSpec((None, half, tn), lambda l, j: (l, 1, j)),
                  pl.BlockSpec((None, 1, tn), lambda l, j: (l, 0, j))],
        out_specs=pl.BlockSpec((None, 8, tn), lambda l, j: (l, 0, j)),
        out_shape=jax.ShapeDtypeStruct((L, 8, n), f32),
        compiler_params=_cparams("parallel", "parallel"),
        name="modulation",
    )(cond8, w_ada, w_ada, b_ada.reshape(L, 1, n))


N_CTX = 32 * 256


def _group_of_tile(tm):
    def idx(i):
        return jnp.maximum(i * tm // SEQ_GROUP - (N_CTX // SEQ_GROUP - 1), 0)
    return idx


def _modulate_kernel(x_ref, mod_ref, h_ref):
    h_ref[...] = (x_ref[...] * (1.0 + mod_ref[1:2, :]) + mod_ref[0:1, :]).astype(h_ref.dtype)


def modulate(x, mod_l):
    n = x.shape[0]
    tm = 1024
    grp = _group_of_tile(tm)
    return pl.pallas_call(
        _modulate_kernel,
        grid=(n // tm,),
        in_specs=[pl.BlockSpec((tm, D), lambda i: (i, 0)),
                  pl.BlockSpec((None, 8, D), lambda i: (grp(i), 0, 0))],
        out_specs=pl.BlockSpec((tm, D), lambda i: (i, 0)),
        out_shape=jax.ShapeDtypeStruct((n, D), bf16),
        compiler_params=_cparams("parallel"),
        name="modulate",
    )(x, mod_l)


def _linear_kernel(x_ref, w_ref, b_ref, o_ref):
    acc = jnp.dot(x_ref[...], w_ref[...], preferred_element_type=f32)
    o_ref[...] = (acc + b_ref[...]).astype(o_ref.dtype)


def linear(x, w, b, out_dtype, name):
    m, k = x.shape
    n = w.shape[1]
    tm = 2048
    tn = 1024 if n % 1024 == 0 else n
    return pl.pallas_call(
        _linear_kernel,
        grid=(m // tm, n // tn),
        in_specs=[pl.BlockSpec((tm, k), lambda i, j: (i, 0)),
                  pl.BlockSpec((k, tn), lambda i, j: (0, j)),
                  pl.BlockSpec((1, tn), lambda i, j: (0, j))],
        out_specs=pl.BlockSpec((tm, tn), lambda i, j: (i, j)),
        out_shape=jax.ShapeDtypeStruct((m, n), out_dtype),
        compiler_params=_cparams("parallel", "parallel"),
        name=name,
    )(x, w, b.reshape(1, n))


def _hgrn_gate_kernel(x_ref, w_ref, b_ref, lbp_ref, lf_ref, kk_ref):
    z = jnp.dot(x_ref[...], w_ref[...], preferred_element_type=f32) + b_ref[...]
    lb = lbp_ref[0:1, :]
    one_m_lb = lbp_ref[1:2, :]
    e = jnp.exp(-jnp.abs(z))
    t = one_m_lb / (1.0 + e)
    kk_ref[...] = t * jnp.where(z >= 0.0, e, 1.0)
    f = lb + t * jnp.where(z >= 0.0, 1.0, e)
    lf_ref[...] = jnp.maximum(jnp.log(f) * LOG2E, LOG2_F_FLOOR)


def hgrn_gates(x, w, b, lbp):
    m, k = x.shape
    n = w.shape[1]
    tm, tn = 1024, 1024
    return pl.pallas_call(
        _hgrn_gate_kernel,
        grid=(m // tm, n // tn),
        in_specs=[pl.BlockSpec((tm, k), lambda i, j: (i, 0)),
                  pl.BlockSpec((k, tn), lambda i, j: (0, j)),
                  pl.BlockSpec((1, tn), lambda i, j: (0, j)),
                  pl.BlockSpec((8, tn), lambda i, j: (0, j))],
        out_specs=[pl.BlockSpec((tm, tn), lambda i, j: (i, j)),
                   pl.BlockSpec((tm, tn), lambda i, j: (i, j))],
        out_shape=[jax.ShapeDtypeStruct((m, n), f32), jax.ShapeDtypeStruct((m, n), f32)],
        compiler_params=_cparams("parallel", "parallel"),
        name="proj_z",
    )(x, w, b.reshape(1, n), lbp)


def _hgrn_kernel(*refs, has_state, want_state, n_aliased):
    q_ref, v_ref, g_ref, lff_ref, lfb_ref, kkf_ref, kkb_ref, gain_ref = refs[:8]
    lf_refs = (lff_ref, lfb_ref)
    kk_refs = (kkf_ref, kkb_ref)
    pos = 8
    s0_ref = None
    if has_state:
        s0_ref = refs[pos]
        pos += 1
    pos += n_aliased
    y_ref = refs[pos]
    pos += 1
    st_ref = None
    if want_state:
        st_ref = refs[pos]
        pos += 1
    b_ref, o_ref = refs[pos:]

    T = q_ref.shape[0]
    hps = q_ref.shape[1] // DH
    nsc = T // SC_A

    row = lax.broadcasted_iota(jnp.int32, (SC_A, SC_A), 0)
    col = lax.broadcasted_iota(jnp.int32, (SC_A, SC_A), 1)
    tris = tuple(jnp.where(m, 1.0, 0.0).astype(bf16) for m in (row >= col, row <= col))
    nblk = SC_A // CH_A
    half = SC_A // 2
    r_sq = lax.broadcasted_iota(jnp.int32, (half, half), 0)
    c_sq = lax.broadcasted_iota(jnp.int32, (half, half), 1)
    r_w = lax.broadcasted_iota(jnp.int32, (half, SC_A), 0)
    c_w = lax.broadcasted_iota(jnp.int32, (half, SC_A), 1)
    masks_lo = (r_sq >= c_sq, r_w <= c_w)
    masks_hi = (r_w + half >= c_w, r_sq <= c_sq)

    for d in range(2):
        for sc in range(nsc):
            rows = slice(sc * SC_A, (sc + 1) * SC_A)
            b_ref[d, rows, :] = _tri_matmul(tris[d], lf_refs[d][rows, :])

    if has_state:
        st0 = tuple(s0_ref[d, hh].T for hh in range(hps) for d in range(2))
    else:
        st0 = tuple(jnp.zeros((DH, DH), f32) for _ in range(2 * hps))

    def body(c, carry):
        units = []
        for hd in range(2 * hps):
            hh, d = divmod(hd, 2)
            ls = slice(hh * DH, (hh + 1) * DH)
            cidx = c if d == 0 else nsc - 1 - c
            r = pl.ds(pl.multiple_of(cidx * SC_A, SC_A), SC_A)
            units.append((d, ls, r))

        scores = []
        for d, ls, r in units:
            q = q_ref[r, ls].astype(f32)
            kk = kk_refs[d][r, ls]
            b = b_ref[d, r, ls]
            att_rows = ([], [])
            for i in range(nblk):
                mid = i * CH_A + (HALF_A - 1 if d == 0 else HALF_A)
                r_i = b[mid:mid + 1, :]
                blk = slice(i * CH_A, (i + 1) * CH_A)
                qi = (q[blk] * jnp.exp2(b[blk] - r_i)).astype(bf16)
                if d == 0 and i < nblk // 2:
                    ksl = slice(0, half)
                elif d == 1 and i >= nblk // 2:
                    ksl = slice(half, SC_A)
                else:
                    ksl = slice(0, SC_A)
                ks = (kk[ksl] * jnp.exp2(r_i - b[ksl])).astype(bf16)
                att_rows[i >= nblk // 2].append(lax.dot_general(qi, ks, NT, preferred_element_type=f32))
            scores.append(att_rows)

        for hd, (d, ls, r) in enumerate(units):
            v = v_ref[r, ls]
            a_lo = jnp.where(masks_lo[d], jnp.concatenate(scores[hd][0], axis=0), 0.0).astype(bf16)
            a_hi = jnp.where(masks_hi[d], jnp.concatenate(scores[hd][1], axis=0), 0.0).astype(bf16)
            o_lo = jnp.dot(a_lo, v[:half] if d == 0 else v, preferred_element_type=f32)
            o_hi = jnp.dot(a_hi, v if d == 0 else v[half:], preferred_element_type=f32)
            qb = (q_ref[r, ls].astype(f32) * jnp.exp2(b_ref[d, r, ls])).astype(bf16)
            o_ref[d, r, ls] = (jnp.concatenate([o_lo, o_hi], axis=0)
                               + lax.dot_general(qb, carry[hd].astype(bf16), NT,
                                                 preferred_element_type=f32))

        new = []
        for hd, (d, ls, r) in enumerate(units):
            b = b_ref[d, r, ls]
            b_end = b[SC_A - 1:SC_A, :] if d == 0 else b[0:1, :]
            kl = (kk_refs[d][r, ls] * jnp.exp2(b_end - b)).astype(bf16)
            new.append(jnp.exp2(b_end) * carry[hd]
                       + lax.dot_general(v_ref[r, ls], kl, TN, preferred_element_type=f32))
        return tuple(new)

    st_fin = lax.fori_loop(0, nsc, body, st0, unroll=2)

    for hh in range(hps):
        ls = slice(hh * DH, (hh + 1) * DH)
        o = o_ref[0, :, ls] + o_ref[1, :, ls]
        ms = jnp.mean(o * o, axis=-1, keepdims=True)
        g = g_ref[:, ls].astype(f32)
        y_ref[:, ls] = (o * lax.rsqrt(ms + 1e-6) * gain_ref[...] * (g * _sigmoid(g))).astype(y_ref.dtype)
        if want_state:
            for d in range(2):
                st_ref[d, hh] = st_fin[2 * hh + d].T


def hgrn_scan(qig, lf, kk, gain, s0, y_buf, state_buf, layer, *, n_seq, T, row_blk0, name):
    has_state = s0 is not None
    want_state = layer is not None

    hps = _heads_per_step(T)
    wh = hps * DH

    def col(off):
        return lambda b, h: (row_blk0 + b, off // hps + h)

    in_specs = [pl.BlockSpec((T, wh), col(0)), pl.BlockSpec((T, wh), col(NH)),
                pl.BlockSpec((T, wh), col(2 * NH)),
                pl.BlockSpec((T, wh), col(0)), pl.BlockSpec((T, wh), col(NH)),
                pl.BlockSpec((T, wh), col(0)), pl.BlockSpec((T, wh), col(NH)),
                pl.BlockSpec((1, DH), lambda b, h: (0, 0))]
    args = [qig, qig, qig, lf, lf, kk, kk, gain]
    if has_state:
        in_specs.append(pl.BlockSpec((None, 2, hps, DH, DH), lambda b, h: (b, 0, h, 0, 0)))
        args.append(s0)
    out_specs = [pl.BlockSpec((T, wh), col(0))]
    out_shape = [jax.ShapeDtypeStruct((qig.shape[0], D), bf16)]
    aliases = {}
    if y_buf is not None:
        in_specs.append(pl.BlockSpec(memory_space=pl.ANY))
        args.append(y_buf)
        aliases[len(args) - 1] = 0
    if want_state:
        out_specs.append(pl.BlockSpec((None, None, 2, hps, DH, DH),
                                      lambda b, h: (b, layer, 0, h, 0, 0)))
        out_shape.append(jax.ShapeDtypeStruct((n_seq, DEPTH, 2, NH, DH, DH), f32))
        if state_buf is not None:
            in_specs.append(pl.BlockSpec(memory_space=pl.ANY))
            args.append(state_buf)
            aliases[len(args) - 1] = 1
    scratch = [pltpu.VMEM((2, T, wh), f32)] * 2
    return pl.pallas_call(
        functools.partial(_hgrn_kernel, has_state=has_state, want_state=want_state,
                          n_aliased=len(aliases)),
        grid=(n_seq, NH // hps),
        in_specs=in_specs, out_specs=out_specs, out_shape=out_shape,
        scratch_shapes=scratch,
        input_output_aliases=aliases,
        compiler_params=_cparams("parallel", "parallel"),
        name=name,
    )(*args)


def _split3(x):
    x0 = x.astype(bf16)
    r = x - x0.astype(f32)
    x1 = r.astype(bf16)
    x2 = (r - x1.astype(f32)).astype(bf16)
    return x0, x1, x2


def _tri_matmul(tri, x):
    return sum(jnp.dot(tri, p, preferred_element_type=f32) for p in _split3(x))


def _matmul_tri(x, tri):
    return sum(jnp.dot(p, tri, preferred_element_type=f32) for p in _split3(x))


def _mlstm_kernel(*refs, grid_rows, has_state, want_state, n_aliased):
    x_ref, v_ref, og_ref, gc_ref, gr_ref, cw_ref, cb_ref, wq_ref, wk_ref, fbr_ref, fbc_ref, gain_ref = refs[:12]
    pos = 12
    if has_state:
        c0_ref, m0_ref = refs[pos:pos + 2]
        pos += 2
    pos += n_aliased
    y_ref = refs[pos]
    pos += 1
    if want_state:
        ct_ref, nt_ref, mt_ref = refs[pos:pos + 3]

    hps = x_ref.shape[1] // DH
    head0 = pl.program_id(1) * hps
    T = x_ref.shape[0]
    L = min(CH_B, T)
    nchunk = T // L

    t_idx = lax.broadcasted_iota(jnp.int32, (T, DH), 0)
    lane = lax.broadcasted_iota(jnp.int32, (T, DH), 1)
    ones_col = jnp.where(lane == 0, 1.0, 0.0)
    if grid_rows is None:
        taps = [(0, dc) for dc in (-1, 0, 1)]
    else:
        taps = [(dr, dc) for dr in (-1, 0, 1) for dc in (-1, 0, 1)]
        c_idx = t_idx % GRID_W
    cas, vaugs = [], []
    for hh in range(hps):
        ls = slice(hh * DH, (hh + 1) * DH)
        x = x_ref[:, ls].astype(f32)
        acc = jnp.zeros((T, DH), f32) + cb_ref[:, ls]
        for dr, dc in taps:
            off = dr * GRID_W + dc
            tap = (dr + 1) * 3 + (dc + 1)
            w = cw_ref[tap:tap + 1, ls]
            xs = x if off == 0 else pltpu.roll(x, (-off) % T, axis=0)
            ok = (t_idx + off >= 0) & (t_idx + off < T)
            if grid_rows is not None:
                ok = ok & (c_idx + dc >= 0) & (c_idx + dc < GRID_W)
            acc = acc + jnp.where(ok, xs, 0.0) * w
        cas.append((acc * _sigmoid(acc)).astype(bf16))
        vaugs.append(jnp.concatenate([v_ref[:, ls].astype(f32), ones_col], axis=1))
    qs = [jnp.dot(cas[hh], wq_ref[hh], preferred_element_type=f32).astype(bf16) for hh in range(hps)]
    ks = [(jnp.dot(cas[hh], wk_ref[hh], preferred_element_type=f32) * (DH ** -0.5)).astype(bf16)
          for hh in range(hps)]

    gate_lane = lax.broadcasted_iota(jnp.int32, (T, DH), 1) // NH
    is_f = (gate_lane == 1) | (gate_lane == 3)
    gc = gc_ref[...] + fbr_ref[...]
    gc = jnp.where(is_f, _log_sigmoid(gc), gc)

    def pick_lane(a, ln):
        sel = lax.broadcasted_iota(jnp.int32, a.shape, 1) == ln
        return jnp.sum(jnp.where(sel, a, 0.0), axis=1, keepdims=True)

    def pick_row(a, rw):
        sel = lax.broadcasted_iota(jnp.int32, a.shape, 0) == rw
        return jnp.sum(jnp.where(sel, a, 0.0), axis=0, keepdims=True)

    r_i = lax.broadcasted_iota(jnp.int32, (L, L), 0)
    c_i = lax.broadcasted_iota(jnp.int32, (L, L), 1)
    lower = r_i >= c_i
    upper = r_i <= c_i
    tri_lo = jnp.where(lower, 1.0, 0.0).astype(bf16)
    tri_up = jnp.where(upper, 1.0, 0.0).astype(bf16)

    h_parts = [[[None] * nchunk for _ in range(2)] for _ in range(hps)]
    units = [(hh, d) for hh in range(hps) for d in range(2)]
    if has_state:
        carry = {u: (c0_ref[u[1], u[0]], m0_ref[u[1], u[0]]) for u in units}
    else:
        carry = {u: (jnp.zeros((DH, 2 * DH), f32), jnp.zeros((1, 1), f32)) for u in units}
    masks = (lower, upper)
    tri_cs = (tri_lo, tri_up)
    tri_rs = (tri_up, tri_lo)

    for step in range(nchunk):
        cidx = (step, nchunk - 1 - step)
        rsl = tuple(slice(c * L, (c + 1) * L) for c in cidx)

        g = {}
        for d in range(2):
            gcc = gc[rsl[d]]
            lf_r8 = _log_sigmoid(gr_ref[(2 * d + 1) * NH:(2 * d + 2) * NH, rsl[d]] + fbc_ref[d])
            g[d] = (gcc, _tri_matmul(tri_cs[d], gcc), _matmul_tri(lf_r8, tri_rs[d]),
                    gr_ref[2 * d * NH:(2 * d + 1) * NH, rsl[d]])
        picks = {}
        for hh, d in units:
            head = head0 + hh
            gcc, F_c_all, F_r_all, ig_r_all = g[d]
            ig_c = pick_lane(gcc, 2 * d * NH + head)
            F_c = pick_lane(F_c_all, (2 * d + 1) * NH + head)
            ig_r = pick_row(ig_r_all, head)
            F_r = pick_row(F_r_all, head)
            F_end = F_c[L - 1:L, :] if d == 0 else F_c[0:1, :]
            picks[hh, d] = (ig_c, F_c, ig_r, F_r, F_end)

        decay = {}
        for u in units:
            ig_c, F_c, ig_r, F_r, F_end = picks[u]
            m_prev = carry[u][1]
            logd = jnp.where(masks[u[1]], F_c - F_r + ig_r, -jnp.inf)
            log_inter = F_c + m_prev
            m_q = jnp.maximum(log_inter, jnp.max(logd, axis=-1, keepdims=True))
            decay[u] = (jnp.exp(logd - m_q), jnp.exp(log_inter - m_q), m_q)

        scores = {}
        for hh, d in units:
            qc, kc = qs[hh][rsl[d]], ks[hh][rsl[d]]
            scores[hh, d] = (lax.dot_general(qc, kc, NT, preferred_element_type=f32)
                             * decay[hh, d][0]).astype(bf16)

        for hh, d in units:
            _, a, m_q = decay[hh, d]
            qc = qs[hh][rsl[d]]
            va = vaugs[hh][rsl[d]]
            nd = (a * jnp.dot(qc, carry[hh, d][0].astype(bf16), preferred_element_type=f32)
                  + jnp.dot(scores[hh, d], va.astype(bf16), preferred_element_type=f32))
            num = nd[:, :DH]
            den = nd[:, DH:DH + 1]
            h_parts[hh][d][cidx[d]] = num / jnp.maximum(jnp.abs(den), jnp.exp(-m_q))

        for hh, d in units:
            ig_c, F_c, ig_r, F_r, F_end = picks[hh, d]
            caug, m_prev = carry[hh, d]
            log_w_c = F_end - F_c + ig_c
            log_w_r = F_end - F_r + ig_r
            m_new = jnp.maximum(F_end + m_prev, jnp.max(log_w_r, axis=-1, keepdims=True))
            ws_c = jnp.exp(log_w_c - m_new)
            d0 = jnp.exp(F_end + m_prev - m_new)
            upd = lax.dot_general(ks[hh][rsl[d]], (ws_c * vaugs[hh][rsl[d]]).astype(bf16), TN,
                                  preferred_element_type=f32)
            carry[hh, d] = (d0 * caug + upd, m_new)
    finals = [[carry[hh, d] for d in range(2)] for hh in range(hps)]

    for hh in range(hps):
        ls = slice(hh * DH, (hh + 1) * DH)
        hsum = jnp.concatenate([h_parts[hh][0][c] + h_parts[hh][1][c] for c in range(nchunk)], axis=0)
        mu = jnp.mean(hsum, axis=-1, keepdims=True)
        dv = hsum - mu
        var = jnp.mean(dv * dv, axis=-1, keepdims=True)
        og = og_ref[:, ls].astype(f32)
        y_ref[:, ls] = (dv * lax.rsqrt(var + 1e-6) * gain_ref[...] * _sigmoid(og)).astype(y_ref.dtype)
        if want_state:
            for d in range(2):
                caug, m_fin = finals[hh][d]
                ct_ref[d, hh] = caug[:, :DH]
                nt_ref[d, hh] = caug[:, DH:2 * DH].T[0:1, :]
                mt_ref[d, hh] = jnp.broadcast_to(m_fin, (1, DH))


def mlstm_scan(bb, gates, gates_t, conv_w9, conv_b, wq, wk, fb_row, fb_col, gain, state, y_buf, c_buf,
               layer, *, n_seq, T, row_blk0, grid_rows, name):
    has_state = state is not None
    want_state = layer is not None

    hps = _heads_per_step(T)
    wh = hps * DH

    def col(off):
        return lambda b, h: (row_blk0 + b, off // hps + h)

    in_specs = [pl.BlockSpec((T, wh), col(0)), pl.BlockSpec((T, wh), col(NH)),
                pl.BlockSpec((T, wh), col(2 * NH)),
                pl.BlockSpec((T, DH), lambda b, h: (row_blk0 + b, 0)),
                pl.BlockSpec((DH, T), lambda b, h: (0, row_blk0 + b)),
                pl.BlockSpec((9, wh), lambda b, h: (0, h)),
                pl.BlockSpec((1, wh), lambda b, h: (0, h)),
                pl.BlockSpec((hps, DH, DH), lambda b, h: (h, 0, 0)),
                pl.BlockSpec((hps, DH, DH), lambda b, h: (h, 0, 0)),
                pl.BlockSpec((1, DH), lambda b, h: (0, 0)),
                pl.BlockSpec((2, NH, 1), lambda b, h: (0, 0, 0)),
                pl.BlockSpec((1, DH), lambda b, h: (0, 0))]
    args = [bb, bb, bb, gates, gates_t, conv_w9, conv_b, wq, wk, fb_row, fb_col, gain]
    if has_state:
        c0aug, m0 = state
        in_specs += [pl.BlockSpec((None, 2, hps, DH, 2 * DH), lambda b, h: (b, 0, h, 0, 0)),
                     pl.BlockSpec((None, 2, hps, 1, 1), lambda b, h: (b, 0, h, 0, 0))]
        args += [c0aug, m0]
    out_specs = [pl.BlockSpec((T, wh), col(0))]
    out_shape = [jax.ShapeDtypeStruct((bb.shape[0], D), bf16)]
    aliases = {}
    if y_buf is not None:
        in_specs.append(pl.BlockSpec(memory_space=pl.ANY))
        args.append(y_buf)
        aliases[len(args) - 1] = 0
    if want_state:
        out_specs += [pl.BlockSpec((None, None, 2, hps, DH, DH), lambda b, h: (b, layer, 0, h, 0, 0)),
                      pl.BlockSpec((None, 2, hps, 1, DH), lambda b, h: (b, 0, h, 0, 0)),
                      pl.BlockSpec((None, 2, hps, 1, DH), lambda b, h: (b, 0, h, 0, 0))]
        out_shape += [jax.ShapeDtypeStruct((n_seq, DEPTH, 2, NH, DH, DH), f32),
                      jax.ShapeDtypeStruct((n_seq, 2, NH, 1, DH), f32),
                      jax.ShapeDtypeStruct((n_seq, 2, NH, 1, DH), f32)]
        if c_buf is not None:
            in_specs.append(pl.BlockSpec(memory_space=pl.ANY))
            args.append(c_buf)
            aliases[len(args) - 1] = 1
    return pl.pallas_call(
        functools.partial(_mlstm_kernel, grid_rows=grid_rows, has_state=has_state,
                          want_state=want_state, n_aliased=len(aliases)),
        grid=(n_seq, NH // hps),
        in_specs=in_specs, out_specs=out_specs, out_shape=out_shape,
        input_output_aliases=aliases,
        compiler_params=_cparams("parallel", "parallel"),
        name=name,
    )(*args)


def _software_pipeline(n, stages):
    vals = [None] * n
    for t in range(n + len(stages) - 1):
        for k in range(len(stages) - 1, -1, -1):
            s = t - k
            if 0 <= s < n:
                vals[s] = stages[k](s, vals[s])


SUB_ROWS = 256


def _merge_kernel(ya_ref, yb_ref, gm_ref, x_ref, mod_ref, wa_ref, wb_ref, wo_ref, lg_ref, lb_ref,
                  x1_ref, hh_ref):
    def rows(s):
        return slice(s * SUB_ROWS, (s + 1) * SUB_ROWS)

    def branches(s, _):
        return (jnp.dot(ya_ref[rows(s), :], wa_ref[...], preferred_element_type=f32),
                jnp.dot(yb_ref[rows(s), :], wb_ref[...], preferred_element_type=f32))

    def gate(s, p):
        gma = gm_ref[rows(s), :D].astype(f32)
        gmb = gm_ref[rows(s), D:].astype(f32)
        return (_sigmoid(gma) * p[0] + _sigmoid(gmb) * p[1]).astype(bf16)

    def project(s, merged):
        return jnp.dot(merged, wo_ref[...], preferred_element_type=f32)

    def norm(s, y):
        x1 = _layer_norm(ALPHA * x_ref[rows(s), :] + mod_ref[2:3, :] * y, lg_ref[...], lb_ref[...])
        x1_ref[rows(s), :] = x1
        hh_ref[rows(s), :] = (x1 * (1.0 + mod_ref[4:5, :]) + mod_ref[3:4, :]).astype(hh_ref.dtype)

    _software_pipeline(ya_ref.shape[0] // SUB_ROWS, (branches, gate, project, norm))


def merge(ya, yb, gm, x, mod_l, wa, wb, wo, lg, lb, hh_dtype):
    n = x.shape[0]
    tm = 512
    grp = _group_of_tile(tm)
    row = lambda i: (i, 0)
    const = lambda i: (0, 0)
    return pl.pallas_call(
        _merge_kernel,
        grid=(n // tm,),
        in_specs=[pl.BlockSpec((tm, D), row), pl.BlockSpec((tm, D), row),
                  pl.BlockSpec((tm, 2 * D), row), pl.BlockSpec((tm, D), row),
                  pl.BlockSpec((None, 8, D), lambda i: (grp(i), 0, 0)),
                  pl.BlockSpec((D, D), const), pl.BlockSpec((D, D), const), pl.BlockSpec((D, D), const),
                  pl.BlockSpec((1, D), const), pl.BlockSpec((1, D), const)],
        out_specs=[pl.BlockSpec((tm, D), row), pl.BlockSpec((tm, D), row)],
        out_shape=[jax.ShapeDtypeStruct((n, D), f32), jax.ShapeDtypeStruct((n, D), hh_dtype)],
        compiler_params=_cparams("parallel"),
        name="merge",
    )(ya, yb, gm, x, mod_l, wa, wb, wo, lg.reshape(1, D), lb.reshape(1, D))


def _post_ffn(f, x1, mod, mod_next, lg, lb, x2_ref, hn_ref):
    x2 = _layer_norm(ALPHA * x1 + mod[5:6, :] * f, lg, lb)
    x2_ref[...] = x2
    hn_ref[...] = (x2 * (1.0 + mod_next[1:2, :]) + mod_next[0:1, :]).astype(hn_ref.dtype)


def _ffn_kernel(hh_ref, w1_ref, w3_ref, w2_ref, x1_ref, mod_ref, modn_ref, lg_ref, lb_ref,
                x2_ref, hn_ref, acc_ref):
    j = pl.program_id(1)

    @pl.when(j == 0)
    def _():
        acc_ref[...] = jnp.zeros_like(acc_ref)

    def rows(s):
        return slice(s * SUB_ROWS, (s + 1) * SUB_ROWS)

    def up(s, _):
        hh = hh_ref[rows(s), :]
        return (jnp.dot(hh, w1_ref[...], preferred_element_type=f32),
                jnp.dot(hh, w3_ref[...], preferred_element_type=f32))

    def act(s, ab):
        return (ab[0] * _sigmoid(ab[0]) * ab[1]).astype(bf16)

    def down(s, h):
        return jnp.dot(h, w2_ref[...], preferred_element_type=f32)

    def accumulate(s, part):
        acc_ref[rows(s), :] += part

    _software_pipeline(hh_ref.shape[0] // SUB_ROWS, (up, act, down, accumulate))

    @pl.when(j == pl.num_programs(1) - 1)
    def _():
        _post_ffn(acc_ref[...], x1_ref[...], mod_ref, modn_ref, lg_ref[...], lb_ref[...],
                  x2_ref, hn_ref)


def ffn(hh, w1, w3, w2, x1, mod_l, mod_next, lg, lb):
    n = hh.shape[0]
    tm, tf = 512, 1408
    grp = _group_of_tile(tm)
    row = lambda i, j: (i, 0)
    const = lambda i, j: (0, 0)
    return pl.pallas_call(
        _ffn_kernel,
        grid=(n // tm, D_FF // tf),
        in_specs=[pl.BlockSpec((tm, D), row),
                  pl.BlockSpec((D, tf), lambda i, j: (0, j)),
                  pl.BlockSpec((D, tf), lambda i, j: (0, j)),
                  pl.BlockSpec((tf, D), lambda i, j: (j, 0)),
                  pl.BlockSpec((tm, D), row),
                  pl.BlockSpec((None, 8, D), lambda i, j: (grp(i), 0, 0)),
                  pl.BlockSpec((None, 8, D), lambda i, j: (grp(i), 0, 0)),
                  pl.BlockSpec((1, D), const), pl.BlockSpec((1, D), const)],
        out_specs=[pl.BlockSpec((tm, D), row), pl.BlockSpec((tm, D), row)],
        out_shape=[jax.ShapeDtypeStruct((n, D), f32), jax.ShapeDtypeStruct((n, D), bf16)],
        scratch_shapes=[pltpu.VMEM((tm, D), f32)],
        compiler_params=_cparams("parallel", "arbitrary"),
        name="ffn",
    )(hh, w1, w3, w2, x1, mod_l, mod_next, lg.reshape(1, D), lb.reshape(1, D))


def _router_kernel(hh_ref, wr_ref, br_ref, o_ref, cnt_ref, run_ref):
    hh = hh_ref[...].astype(f32)
    wr = wr_ref[...]
    h_hi = hh.astype(bf16)
    h_lo = (hh - h_hi.astype(f32)).astype(bf16)
    w_hi = wr.astype(bf16)
    w_lo = (wr - w_hi.astype(f32)).astype(bf16)
    logits = (jnp.dot(h_hi, w_hi, preferred_element_type=f32)
              + jnp.dot(h_hi, w_lo, preferred_element_type=f32)
              + jnp.dot(h_lo, w_hi, preferred_element_type=f32)) + br_ref[...]
    lane = lax.broadcasted_iota(jnp.int32, logits.shape, 1).astype(f32)
    m1 = jnp.max(logits, axis=-1, keepdims=True)
    i1 = jnp.min(jnp.where(logits == m1, lane, float(DH)), axis=-1, keepdims=True)
    rest = jnp.where(lane == i1, -jnp.inf, logits)
    m2 = jnp.max(rest, axis=-1, keepdims=True)
    i2 = jnp.min(jnp.where(rest == m2, lane, float(DH)), axis=-1, keepdims=True)
    e21 = jnp.exp(m2 - m1)
    p1 = 1.0 / (1.0 + e21)
    p2 = e21 * p1

    @pl.when(pl.program_id(0) == 0)
    def _():
        run_ref[...] = jnp.zeros_like(run_ref)

    tm = logits.shape[0]
    r_i = lax.broadcasted_iota(jnp.int32, (tm, tm), 0)
    c_i = lax.broadcasted_iota(jnp.int32, (tm, tm), 1)
    earlier = jnp.where(c_i < r_i, 1.0, 0.0).astype(bf16)
    oh1 = jnp.where(lane == i1, 1.0, 0.0)
    oh2 = jnp.where(lane == i2, 1.0, 0.0)
    pre1 = jnp.dot(earlier, oh1.astype(bf16), preferred_element_type=f32)
    pre2 = jnp.dot(earlier, oh2.astype(bf16), preferred_element_type=f32)
    cnt1 = jnp.sum(oh1, axis=0, keepdims=True)
    cnt2 = jnp.sum(oh2, axis=0, keepdims=True)
    run = run_ref[...]
    rank1 = jnp.sum(oh1 * (run + pre1), axis=-1, keepdims=True)
    rank2 = jnp.sum(oh2 * (run + cnt1 + pre2), axis=-1, keepdims=True)
    run_ref[...] = run + cnt1 + cnt2
    cnt_ref[...] = jnp.broadcast_to(run + cnt1 + cnt2, cnt_ref.shape)
    o_ref[...] = (jnp.where(lane == 0.0, i1, 0.0) + jnp.where(lane == 1.0, i2, 0.0)
                  + jnp.where(lane == 2.0, p1, 0.0) + jnp.where(lane == 3.0, p2, 0.0)
                  + jnp.where(lane == 4.0, rank1, 0.0) + jnp.where(lane == 5.0, rank2, 0.0))


def router(hh, wr_pad, br_pad):
    n = hh.shape[0]
    tm = 1024
    return pl.pallas_call(
        _router_kernel,
        grid=(n // tm,),
        in_specs=[pl.BlockSpec((tm, D), lambda i: (i, 0)),
                  pl.BlockSpec((D, DH), lambda i: (0, 0)),
                  pl.BlockSpec((1, DH), lambda i: (0, 0))],
        out_specs=[pl.BlockSpec((tm, DH), lambda i: (i, 0)),
                   pl.BlockSpec((8, DH), lambda i: (0, 0))],
        out_shape=[jax.ShapeDtypeStruct((n, DH), f32), jax.ShapeDtypeStruct((8, DH), f32)],
        scratch_shapes=[pltpu.VMEM((1, DH), f32)],
        compiler_params=_cparams("arbitrary"),
        name="router",
    )(hh, wr_pad, br_pad)


def _moe_up_kernel(te_ref, x_ref, w1_ref, w3_ref, a_ref):
    del te_ref
    x = x_ref[...].astype(bf16)
    a = jnp.dot(x, w1_ref[...].astype(bf16), preferred_element_type=f32)
    b = jnp.dot(x, w3_ref[...].astype(bf16), preferred_element_type=f32)
    a_ref[...] = (a * _sigmoid(a) * b).astype(a_ref.dtype)


def _moe_down_kernel(te_ref, a_ref, w2_ref, y_ref):
    del te_ref
    y_ref[...] = jnp.dot(a_ref[...], w2_ref[...].astype(bf16), preferred_element_type=f32)


def moe_experts(tile_expert, xs, w1, w3, w2, lyr):
    r = xs.shape[0]
    nt = r // MOE_TM
    tf = 1408
    up = pl.pallas_call(
        _moe_up_kernel,
        grid_spec=pltpu.PrefetchScalarGridSpec(
            num_scalar_prefetch=1,
            grid=(D_FF // tf, nt),
            in_specs=[pl.BlockSpec((MOE_TM, D), lambda j, i, te: (i, 0)),
                      pl.BlockSpec((None, None, D, tf), lambda j, i, te: (lyr, te[i], 0, j)),
                      pl.BlockSpec((None, None, D, tf), lambda j, i, te: (lyr, te[i], 0, j))],
            out_specs=pl.BlockSpec((MOE_TM, tf), lambda j, i, te: (i, j))),
        out_shape=jax.ShapeDtypeStruct((r, D_FF), bf16),
        compiler_params=_cparams("arbitrary", "arbitrary"),
        name="moe_up",
    )(tile_expert, xs, w1, w3)
    return pl.pallas_call(
        _moe_down_kernel,
        grid_spec=pltpu.PrefetchScalarGridSpec(
            num_scalar_prefetch=1,
            grid=(nt,),
            in_specs=[pl.BlockSpec((MOE_TM, D_FF), lambda i, te: (i, 0)),
                      pl.BlockSpec((None, None, D_FF, D), lambda i, te: (lyr, te[i], 0, 0))],
            out_specs=pl.BlockSpec((MOE_TM, D), lambda i, te: (i, 0))),
        out_shape=jax.ShapeDtypeStruct((r, D), f32),
        compiler_params=_cparams("arbitrary"),
        name="moe_down",
    )(tile_expert, up, w2)


def _combine_kernel(ya_ref, yb_ref, rt_ref, x1_ref, mod_ref, modn_ref, lg_ref, lb_ref, x2_ref, hn_ref):
    f = rt_ref[:, 2:3] * ya_ref[...] + rt_ref[:, 3:4] * yb_ref[...]
    _post_ffn(f, x1_ref[...], mod_ref, modn_ref, lg_ref[...], lb_ref[...], x2_ref, hn_ref)


def combine(ya, yb, rt, x1, mod_l, mod_next, lg, lb):
    n = x1.shape[0]
    tm = 1024
    grp = _group_of_tile(tm)
    row = lambda i: (i, 0)
    const = lambda i: (0, 0)
    return pl.pallas_call(
        _combine_kernel,
        grid=(n // tm,),
        in_specs=[pl.BlockSpec((tm, D), row), pl.BlockSpec((tm, D), row),
                  pl.BlockSpec((tm, DH), row), pl.BlockSpec((tm, D), row),
                  pl.BlockSpec((None, 8, D), lambda i: (grp(i), 0, 0)),
                  pl.BlockSpec((None, 8, D), lambda i: (grp(i), 0, 0)),
                  pl.BlockSpec((1, D), const), pl.BlockSpec((1, D), const)],
        out_specs=[pl.BlockSpec((tm, D), row), pl.BlockSpec((tm, D), row)],
        out_shape=[jax.ShapeDtypeStruct((n, D), f32), jax.ShapeDtypeStruct((n, D), bf16)],
        compiler_params=_cparams("parallel"),
        name="combine",
    )(ya, yb, rt, x1, mod_l, mod_next, lg.reshape(1, D), lb.reshape(1, D))


def moe_dispatch(rt, counts, n):
    r_total = 2 * n + N_EXPERTS * MOE_TM
    nt = r_total // MOE_TM
    cnt = counts[0, :N_EXPERTS].astype(jnp.int32)
    padded = (cnt + MOE_TM - 1) // MOE_TM * MOE_TM
    ends = jnp.cumsum(padded)
    starts = ends - padded
    e_idx = rt[:, 0:2].astype(jnp.int32)
    rank = rt[:, 4:6].astype(jnp.int32)
    onehot = e_idx[..., None] == jnp.arange(N_EXPERTS)[None, None, :]
    dest = rank + jnp.sum(jnp.where(onehot, starts[None, None, :], 0), axis=-1)
    tok = jnp.broadcast_to(jnp.arange(n, dtype=jnp.int32)[:, None], (n, 2))
    src_tok = jnp.zeros((r_total,), jnp.int32).at[dest.reshape(-1)].set(
        tok.reshape(-1), unique_indices=True, mode="promise_in_bounds")
    tile_start = jnp.arange(nt, dtype=jnp.int32) * MOE_TM
    tile_expert = jnp.minimum(jnp.sum((tile_start[:, None] >= ends[None, :]).astype(jnp.int32), axis=1),
                              N_EXPERTS - 1)
    return src_tok, dest, tile_expert


def _take_rows(a, idx):
    return a.at[idx].get(mode="promise_in_bounds")


def kernel(x_prompt, x_sample, c, state_hgrn, state_mlstm_C, state_mlstm_n, state_mlstm_m, c_ctx,
           w_ada, b_ada, w_in, b_in, hgrn_lb_raw, hgrn_norm_g, conv_w, conv_b, w_mq, w_mk,
           mlstm_fbias, mlstm_norm_g, w_branch_a, w_branch_b, w_out, ln1_g, ln1_b, ln2_g, ln2_b,
           ffn_w1, ffn_w3, ffn_w2, moe_router_w, moe_router_b, moe_w1, moe_w3, moe_w2):
    nb, seq, _ = x_prompt.shape
    db, dseq, _ = x_sample.shape
    n_ctx, n_lat = nb * seq, db * dseq
    n = n_ctx + n_lat
    assert n_ctx == N_CTX and dseq == SEQ_GROUP and db == 2

    lb = jnp.cumsum(jax.nn.softmax(hgrn_lb_raw.astype(f32), axis=1), axis=1)
    lb = lb - lb[:, :1]

    cond8 = jnp.concatenate([c_ctx[None, :], c, jnp.zeros((5, D), f32)], axis=0)
    mod = modulation(cond8, w_ada, b_ada)[:, :3, :].reshape(DEPTH, 3, 6, D)
    mod = jnp.pad(mod, ((0, 0), (0, 0), (0, 2), (0, 0)))

    x = jnp.concatenate([x_prompt.reshape(n_ctx, D), x_sample.reshape(n_lat, D)], axis=0)
    h = modulate(x, mod[0])

    sizes = (D,) * 8 + (4 * NH, D, D)
    offs = [0]
    for s in sizes:
        offs.append(offs[-1] + s)

    def cols(w, idxs):
        return jnp.concatenate([w[..., offs[i]:offs[i + 1]] for i in idxs], axis=-1)

    out_n, out_m = [], []
    buf_a = buf_c = None
    for l in range(DEPTH):
        wl, bl = w_in[l], b_in[l]
        qig = linear(h, cols(wl, (0, 3, 4)).astype(bf16), cols(bl, (0, 3, 4)), bf16, "proj_qig")
        lbl = lb[:, l].reshape(1, 2 * D)
        lbp = jnp.concatenate([lbl, 1.0 - lbl, jnp.zeros((6, 2 * D), f32)], axis=0)
        lf, kk = hgrn_gates(h, cols(wl, (1, 2)).astype(bf16), cols(bl, (1, 2)), lbp)
        bb = linear(h, cols(wl, (5, 6, 7)).astype(bf16), cols(bl, (5, 6, 7)), bf16, "proj_b")
        gm = linear(h, cols(wl, (9, 10)).astype(bf16), cols(bl, (9, 10)), bf16, "proj_gm")
        w_g = jnp.pad(cols(wl, (8,)), ((0, 0), (0, DH - 4 * NH))).astype(bf16)
        gates = linear(h, w_g, jnp.pad(cols(bl, (8,)), (0, DH - 4 * NH)), f32, "proj_gates")
        gates_t = gates.T

        gain_a = hgrn_norm_g[l].reshape(1, DH)
        ya, buf_a = hgrn_scan(qig, lf, kk, gain_a, None, None, buf_a, l, n_seq=nb, T=seq,
                              row_blk0=0, name="hgrn_ctx")
        (ya,) = hgrn_scan(qig, lf, kk, gain_a, state_hgrn[:, l], ya, None, None, n_seq=db, T=dseq,
                          row_blk0=n_ctx // dseq, name="hgrn_lat")

        cw9 = conv_w[l].reshape(9, D)
        cb = conv_b[l].reshape(1, D)
        wq = w_mq[l].astype(bf16)
        wk = w_mk[l].astype(bf16)
        fb = mlstm_fbias[l]
        zero8 = jnp.zeros((NH,), f32)
        fb_row = jnp.concatenate([zero8, fb[0], zero8, fb[1], jnp.zeros((DH - 4 * NH,), f32)]).reshape(1, DH)
        fb_col = fb.reshape(2, NH, 1)
        gain_b = mlstm_norm_g[l].reshape(1, DH)
        yb, buf_c, s_n, s_m = mlstm_scan(bb, gates, gates_t, cw9, cb, wq, wk, fb_row, fb_col, gain_b,
                                         None, None, buf_c, l, n_seq=nb, T=seq, row_blk0=0,
                                         grid_rows=None, name="mlstm_ctx")
        c0aug = jnp.concatenate([state_mlstm_C[:, l], state_mlstm_n[:, l][..., None],
                                 jnp.zeros((db, 2, NH, DH, DH - 1), f32)], axis=-1)
        m0 = state_mlstm_m[:, l].reshape(db, 2, NH, 1, 1)
        (yb,) = mlstm_scan(bb, gates, gates_t, cw9, cb, wq, wk, fb_row, fb_col, gain_b,
                           (c0aug, m0), yb, None, None, n_seq=db, T=dseq, row_blk0=n_ctx // dseq,
                           grid_rows=dseq // GRID_W, name="mlstm_lat")

        x1, hh = merge(ya, yb, gm, x, mod[l], w_branch_a[l].astype(bf16), w_branch_b[l].astype(bf16),
                       w_out[l].astype(bf16), ln1_g[l], ln1_b[l], bf16 if l % 2 == 0 else f32)

        mod_next = mod[(l + 1) % DEPTH]
        j = l // 2
        if l % 2 == 0:
            x, h = ffn(hh, ffn_w1[j].astype(bf16), ffn_w3[j].astype(bf16), ffn_w2[j].astype(bf16),
                       x1, mod[l], mod_next, ln2_g[l], ln2_b[l])
        else:
            wr = jnp.pad(moe_router_w[j], ((0, 0), (0, DH - N_EXPERTS)))
            br = jnp.concatenate([moe_router_b[j], jnp.full((DH - N_EXPERTS,), -1e30, f32)]).reshape(1, DH)
            rt, counts = router(hh, wr, br)
            src_tok, dest, tile_expert = moe_dispatch(rt, counts, n)
            ys = moe_experts(tile_expert, _take_rows(hh, src_tok), moe_w1, moe_w3, moe_w2, j)
            x, h = combine(_take_rows(ys, dest[:, 0]), _take_rows(ys, dest[:, 1]), rt,
                           x1, mod[l], mod_next, ln2_g[l], ln2_b[l])

        out_n.append(s_n.reshape(nb, 2, NH, DH))
        out_m.append(s_m[..., 0, 0])

    y_prompt = x[:n_ctx].reshape(nb, seq, D)
    y_sample = x[n_ctx:].reshape(db, dseq, D)
    return (y_prompt, y_sample, buf_a, buf_c, jnp.stack(out_n, axis=1), jnp.stack(out_m, axis=1))
```

```python
import functools

import jax
import jax.numpy as jnp
from jax import lax
from jax.experimental import pallas as pl
from jax.experimental.pallas import tpu as pltpu

f32 = jnp.float32
bf16 = jnp.bfloat16

D = 1024
DEPTH = 4
GRID_W = 64
NH = 8
DH = 128
CH_A = 16
HALF_A = 8
SC_A = 128
SCAN_ROWS_PER_STEP = 4096
CH_B = 256
N_EXPERTS = 8
D_FF = 2816
ALPHA = (2 * DEPTH) ** 0.25
SEQ_GROUP = 1024
LOG2E = 1.4426950408889634
LOG2_F_FLOOR = -1e30
MOE_TM = 512

VMEM_LIMIT = 56 * 1024 * 1024
NT = (((1,), (1,)), ((), ()))
TN = (((0,), (0,)), ((), ()))


def _heads_per_step(seq_len):
    return max(1, min(NH, SCAN_ROWS_PER_STEP // seq_len))


def _cparams(*sem):
    return pltpu.CompilerParams(dimension_semantics=sem, vmem_limit_bytes=VMEM_LIMIT)


def _sigmoid(x):
    return 1.0 / (1.0 + jnp.exp(-x))


def _log_sigmoid(t):
    return jnp.minimum(t, 0.0) - jnp.log1p(jnp.exp(-jnp.abs(t)))


def _layer_norm(r, g, b):
    mu = jnp.mean(r, axis=-1, keepdims=True)
    d = r - mu
    var = jnp.mean(d * d, axis=-1, keepdims=True)
    return d * lax.rsqrt(var + 1e-5) * g + b


def _mod_kernel(c_ref, wt_ref, wb_ref, b_ref, o_ref):
    c = c_ref[...]
    act = (c * _sigmoid(c)).astype(bf16)
    half = D // 2
    o_ref[...] = (jnp.dot(act[:, :half], wt_ref[...].astype(bf16), preferred_element_type=f32)
                  + jnp.dot(act[:, half:], wb_ref[...].astype(bf16), preferred_element_type=f32)
                  + b_ref[...])


def modulation(cond8, w_ada, b_ada):
    L, _, n = w_ada.shape
    tn = 1536
    half = D // 2
    return pl.pallas_call(
        _mod_kernel,
        grid=(L, n // tn),
        in_specs=[pl.BlockSpec((8, D), lambda l, j: (0, 0)),
                  pl.BlockSpec((None, half, tn), lambda l, j: (l, 0, j)),
                  pl.BlockSpec((None, half, tn), lambda l, j: (l, 1, j)),
                  pl.BlockSpec((None, 1, tn), lambda l, j: (l, 0, j))],
        out_specs=pl.BlockSpec((None, 8, tn), lambda l, j: (l, 0, j)),
        out_shape=jax.ShapeDtypeStruct((L, 8, n), f32),
        compiler_params=_cparams("parallel", "parallel"),
        name="modulation",
    )(cond8, w_ada, w_ada, b_ada.reshape(L, 1, n))


N_CTX = 32 * 256


def _group_of_tile(tm):
    def idx(i):
        return jnp.maximum(i * tm // SEQ_GROUP - (N_CTX // SEQ_GROUP - 1), 0)
    return idx


def _modulate_kernel(x_ref, mod_ref, h_ref):
    h_ref[...] = (x_ref[...] * (1.0 + mod_ref[1:2, :]) + mod_ref[0:1, :]).astype(h_ref.dtype)


def modulate(x, mod_l):
    n = x.shape[0]
    tm = 1024
    grp = _group_of_tile(tm)
    return pl.pallas_call(
        _modulate_kernel,
        grid=(n // tm,),
        in_specs=[pl.BlockSpec((tm, D), lambda i: (i, 0)),
                  pl.BlockSpec((None, 8, D), lambda i: (grp(i), 0, 0))],
        out_specs=pl.BlockSpec((tm, D), lambda i: (i, 0)),
        out_shape=jax.ShapeDtypeStruct((n, D), bf16),
        compiler_params=_cparams("parallel"),
        name="modulate",
    )(x, mod_l)


def _linear_kernel(x_ref, w_ref, b_ref, o_ref):
    acc = jnp.dot(x_ref[...], w_ref[...], preferred_element_type=f32)
    o_ref[...] = (acc + b_ref[...]).astype(o_ref.dtype)


def linear(x, w, b, out_dtype, name):
    m, k = x.shape
    n = w.shape[1]
    tm = 2048
    tn = 1024 if n % 1024 == 0 else n
    return pl.pallas_call(
        _linear_kernel,
        grid=(m // tm, n // tn),
        in_specs=[pl.BlockSpec((tm, k), lambda i, j: (i, 0)),
                  pl.BlockSpec((k, tn), lambda i, j: (0, j)),
                  pl.BlockSpec((1, tn), lambda i, j: (0, j))],
        out_specs=pl.BlockSpec((tm, tn), lambda i, j: (i, j)),
        out_shape=jax.ShapeDtypeStruct((m, n), out_dtype),
        compiler_params=_cparams("parallel", "parallel"),
        name=name,
    )(x, w, b.reshape(1, n))


def _hgrn_gate_kernel(x_ref, w_ref, b_ref, lbp_ref, lf_ref, kk_ref):
    z = jnp.dot(x_ref[...], w_ref[...], preferred_element_type=f32) + b_ref[...]
    lb = lbp_ref[0:1, :]
    one_m_lb = lbp_ref[1:2, :]
    e = jnp.exp(-jnp.abs(z))
    t = one_m_lb / (1.0 + e)
    kk_ref[...] = t * jnp.where(z >= 0.0, e, 1.0)
    f = lb + t * jnp.where(z >= 0.0, 1.0, e)
    lf_ref[...] = jnp.maximum(jnp.log(f) * LOG2E, LOG2_F_FLOOR)


def hgrn_gates(x, w, b, lbp):
    m, k = x.shape
    n = w.shape[1]
    tm, tn = 1024, 1024
    return pl.pallas_call(
        _hgrn_gate_kernel,
        grid=(m // tm, n // tn),
        in_specs=[pl.BlockSpec((tm, k), lambda i, j: (i, 0)),
                  pl.BlockSpec((k, tn), lambda i, j: (0, j)),
                  pl.BlockSpec((1, tn), lambda i, j: (0, j)),
                  pl.BlockSpec((8, tn), lambda i, j: (0, j))],
        out_specs=[pl.BlockSpec((tm, tn), lambda i, j: (i, j)),
                   pl.BlockSpec((tm, tn), lambda i, j: (i, j))],
        out_shape=[jax.ShapeDtypeStruct((m, n), f32), jax.ShapeDtypeStruct((m, n), f32)],
        compiler_params=_cparams("parallel", "parallel"),
        name="proj_z",
    )(x, w, b.reshape(1, n), lbp)


def _hgrn_kernel(*refs, has_state, want_state, n_aliased):
    q_ref, v_ref, g_ref, lff_ref, lfb_ref, kkf_ref, kkb_ref, gain_ref = refs[:8]
    lf_refs = (lff_ref, lfb_ref)
    kk_refs = (kkf_ref, kkb_ref)
    pos = 8
    s0_ref = None
    if has_state:
        s0_ref = refs[pos]
        pos += 1
    pos += n_aliased
    y_ref = refs[pos]
    pos += 1
    st_ref = None
    if want_state:
        st_ref = refs[pos]
        pos += 1
    b_ref, o_ref = refs[pos:]

    T = q_ref.shape[0]
    hps = q_ref.shape[1] // DH
    nsc = T // SC_A

    row = lax.broadcasted_iota(jnp.int32, (SC_A, SC_A), 0)
    col = lax.broadcasted_iota(jnp.int32, (SC_A, SC_A), 1)
    tris = tuple(jnp.where(m, 1.0, 0.0).astype(bf16) for m in (row >= col, row <= col))
    nblk = SC_A // CH_A
    half = SC_A // 2
    r_sq = lax.broadcasted_iota(jnp.int32, (half, half), 0)
    c_sq = lax.broadcasted_iota(jnp.int32, (half, half), 1)
    r_w = lax.broadcasted_iota(jnp.int32, (half, SC_A), 0)
    c_w = lax.broadcasted_iota(jnp.int32, (half, SC_A), 1)
    masks_lo = (r_sq >= c_sq, r_w <= c_w)
    masks_hi = (r_w + half >= c_w, r_sq <= c_sq)

    for d in range(2):
        for sc in range(nsc):
            rows = slice(sc * SC_A, (sc + 1) * SC_A)
            b_ref[d, rows, :] = _tri_matmul(tris[d], lf_refs[d][rows, :])

    if has_state:
        st0 = tuple(s0_ref[d, hh].T for hh in range(hps) for d in range(2))
    else:
        st0 = tuple(jnp.zeros((DH, DH), f32) for _ in range(2 * hps))

    def body(c, carry):
        units = []
        for hd in range(2 * hps):
            hh, d = divmod(hd, 2)
            ls = slice(hh * DH, (hh + 1) * DH)
            cidx = c if d == 0 else nsc - 1 - c
            r = pl.ds(pl.multiple_of(cidx * SC_A, SC_A), SC_A)
            units.append((d, ls, r))

        scores = []
        for d, ls, r in units:
            q = q_ref[r, ls].astype(f32)
            kk = kk_refs[d][r, ls]
            b = b_ref[d, r, ls]
            att_rows = ([], [])
            for i in range(nblk):
                mid = i * CH_A + (HALF_A - 1 if d == 0 else HALF_A)
                r_i = b[mid:mid + 1, :]
                blk = slice(i * CH_A, (i + 1) * CH_A)
                qi = (q[blk] * jnp.exp2(b[blk] - r_i)).astype(bf16)
                if d == 0 and i < nblk // 2:
                    ksl = slice(0, half)
                elif d == 1 and i >= nblk // 2:
                    ksl = slice(half, SC_A)
                else:
                    ksl = slice(0, SC_A)
                ks = (kk[ksl] * jnp.exp2(r_i - b[ksl])).astype(bf16)
                att_rows[i >= nblk // 2].append(lax.dot_general(qi, ks, NT, preferred_element_type=f32))
            scores.append(att_rows)

        for hd, (d, ls, r) in enumerate(units):
            v = v_ref[r, ls]
            a_lo = jnp.where(masks_lo[d], jnp.concatenate(scores[hd][0], axis=0), 0.0).astype(bf16)
            a_hi = jnp.where(masks_hi[d], jnp.concatenate(scores[hd][1], axis=0), 0.0).astype(bf16)
            o_lo = jnp.dot(a_lo, v[:half] if d == 0 else v, preferred_element_type=f32)
            o_hi = jnp.dot(a_hi, v if d == 0 else v[half:], preferred_element_type=f32)
            qb = (q_ref[r, ls].astype(f32) * jnp.exp2(b_ref[d, r, ls])).astype(bf16)
            o_ref[d, r, ls] = (jnp.concatenate([o_lo, o_hi], axis=0)
                               + lax.dot_general(qb, carry[hd].astype(bf16), NT,
                                                 preferred_element_type=f32))

        new = []
        for hd, (d, ls, r) in enumerate(units):
            b = b_ref[d, r, ls]
            b_end = b[SC_A - 1:SC_A, :] if d == 0 else b[0:1, :]
            kl = (kk_refs[d][r, ls] * jnp.exp2(b_end - b)).astype(bf16)
            new.append(jnp.exp2(b_end) * carry[hd]
                       + lax.dot_general(v_ref[r, ls], kl, TN, preferred_element_type=f32))
        return tuple(new)

    st_fin = lax.fori_loop(0, nsc, body, st0, unroll=2)

    for hh in range(hps):
        ls = slice(hh * DH, (hh + 1) * DH)
        o = o_ref[0, :, ls] + o_ref[1, :, ls]
        ms = jnp.mean(o * o, axis=-1, keepdims=True)
        g = g_ref[:, ls].astype(f32)
        y_ref[:, ls] = (o * lax.rsqrt(ms + 1e-6) * gain_ref[...] * (g * _sigmoid(g))).astype(y_ref.dtype)
        if want_state:
            for d in range(2):
                st_ref[d, hh] = st_fin[2 * hh + d].T


def hgrn_scan(qig, lf, kk, gain, s0, y_buf, state_buf, layer, *, n_seq, T, row_blk0, name):
    has_state = s0 is not None
    want_state = layer is not None

    hps = _heads_per_step(T)
    wh = hps * DH

    def col(off):
        return lambda b, h: (row_blk0 + b, off // hps + h)

    in_specs = [pl.BlockSpec((T, wh), col(0)), pl.BlockSpec((T, wh), col(NH)),
                pl.BlockSpec((T, wh), col(2 * NH)),
                pl.BlockSpec((T, wh), col(0)), pl.BlockSpec((T, wh), col(NH)),
                pl.BlockSpec((T, wh), col(0)), pl.BlockSpec((T, wh), col(NH)),
                pl.BlockSpec((1, DH), lambda b, h: (0, 0))]
    args = [qig, qig, qig, lf, lf, kk, kk, gain]
    if has_state:
        in_specs.append(pl.BlockSpec((None, 2, hps, DH, DH), lambda b, h: (b, 0, h, 0, 0)))
        args.append(s0)
    out_specs = [pl.BlockSpec((T, wh), col(0))]
    out_shape = [jax.ShapeDtypeStruct((qig.shape[0], D), bf16)]
    aliases = {}
    if y_buf is not None:
        in_specs.append(pl.BlockSpec(memory_space=pl.ANY))
        args.append(y_buf)
        aliases[len(args) - 1] = 0
    if want_state:
        out_specs.append(pl.BlockSpec((None, None, 2, hps, DH, DH),
                                      lambda b, h: (b, layer, 0, h, 0, 0)))
        out_shape.append(jax.ShapeDtypeStruct((n_seq, DEPTH, 2, NH, DH, DH), f32))
        if state_buf is not None:
            in_specs.append(pl.BlockSpec(memory_space=pl.ANY))
            args.append(state_buf)
            aliases[len(args) - 1] = 1
    scratch = [pltpu.VMEM((2, T, wh), f32)] * 2
    return pl.pallas_call(
        functools.partial(_hgrn_kernel, has_state=has_state, want_state=want_state,
                          n_aliased=len(aliases)),
        grid=(n_seq, NH // hps),
        in_specs=in_specs, out_specs=out_specs, out_shape=out_shape,
        scratch_shapes=scratch,
        input_output_aliases=aliases,
        compiler_params=_cparams("parallel", "parallel"),
        name=name,
    )(*args)


def _split3(x):
    x0 = x.astype(bf16)
    r = x - x0.astype(f32)
    x1 = r.astype(bf16)
    x2 = (r - x1.astype(f32)).astype(bf16)
    return x0, x1, x2


def _tri_matmul(tri, x):
    return sum(jnp.dot(tri, p, preferred_element_type=f32) for p in _split3(x))


def _matmul_tri(x, tri):
    return sum(jnp.dot(p, tri, preferred_element_type=f32) for p in _split3(x))


def _mlstm_kernel(*refs, grid_rows, has_state, want_state, n_aliased):
    x_ref, v_ref, og_ref, gc_ref, gr_ref, cw_ref, cb_ref, wq_ref, wk_ref, fbr_ref, fbc_ref, gain_ref = refs[:12]
    pos = 12
    if has_state:
        c0_ref, m0_ref = refs[pos:pos + 2]
        pos += 2
    pos += n_aliased
    y_ref = refs[pos]
    pos += 1
    if want_state:
        ct_ref, nt_ref, mt_ref = refs[pos:pos + 3]

    hps = x_ref.shape[1] // DH
    head0 = pl.program_id(1) * hps
    T = x_ref.shape[0]
    L = min(CH_B, T)
    nchunk = T // L

    t_idx = lax.broadcasted_iota(jnp.int32, (T, DH), 0)
    lane = lax.broadcasted_iota(jnp.int32, (T, DH), 1)
    ones_col = jnp.where(lane == 0, 1.0, 0.0)
    if grid_rows is None:
        taps = [(0, dc) for dc in (-1, 0, 1)]
    else:
        taps = [(dr, dc) for dr in (-1, 0, 1) for dc in (-1, 0, 1)]
        c_idx = t_idx % GRID_W
    cas, vaugs = [], []
    for hh in range(hps):
        ls = slice(hh * DH, (hh + 1) * DH)
        x = x_ref[:, ls].astype(f32)
        acc = jnp.zeros((T, DH), f32) + cb_ref[:, ls]
        for dr, dc in taps:
            off = dr * GRID_W + dc
            tap = (dr + 1) * 3 + (dc + 1)
            w = cw_ref[tap:tap + 1, ls]
            xs = x if off == 0 else pltpu.roll(x, (-off) % T, axis=0)
            ok = (t_idx + off >= 0) & (t_idx + off < T)
            if grid_rows is not None:
                ok = ok & (c_idx + dc >= 0) & (c_idx + dc < GRID_W)
            acc = acc + jnp.where(ok, xs, 0.0) * w
        cas.append((acc * _sigmoid(acc)).astype(bf16))
        vaugs.append(jnp.concatenate([v_ref[:, ls].astype(f32), ones_col], axis=1))
    qs = [jnp.dot(cas[hh], wq_ref[hh], preferred_element_type=f32).astype(bf16) for hh in range(hps)]
    ks = [(jnp.dot(cas[hh], wk_ref[hh], preferred_element_type=f32) * (DH ** -0.5)).astype(bf16)
          for hh in range(hps)]

    gate_lane = lax.broadcasted_iota(jnp.int32, (T, DH), 1) // NH
    is_f = (gate_lane == 1) | (gate_lane == 3)
    gc = gc_ref[...] + fbr_ref[...]
    gc = jnp.where(is_f, _log_sigmoid(gc), gc)

    def pick_lane(a, ln):
        sel = lax.broadcasted_iota(jnp.int32, a.shape, 1) == ln
        return jnp.sum(jnp.where(sel, a, 0.0), axis=1, keepdims=True)

    def pick_row(a, rw):
        sel = lax.broadcasted_iota(jnp.int32, a.shape, 0) == rw
        return jnp.sum(jnp.where(sel, a, 0.0), axis=0, keepdims=True)

    r_i = lax.broadcasted_iota(jnp.int32, (L, L), 0)
    c_i = lax.broadcasted_iota(jnp.int32, (L, L), 1)
    lower = r_i >= c_i
    upper = r_i <= c_i
    tri_lo = jnp.where(lower, 1.0, 0.0).astype(bf16)
    tri_up = jnp.where(upper, 1.0, 0.0).astype(bf16)

    h_parts = [[[None] * nchunk for _ in range(2)] for _ in range(hps)]
    units = [(hh, d) for hh in range(hps) for d in range(2)]
    if has_state:
        carry = {u: (c0_ref[u[1], u[0]], m0_ref[u[1], u[0]]) for u in units}
    else:
        carry = {u: (jnp.zeros((DH, 2 * DH), f32), jnp.zeros((1, 1), f32)) for u in units}
    masks = (lower, upper)
    tri_cs = (tri_lo, tri_up)
    tri_rs = (tri_up, tri_lo)

    for step in range(nchunk):
        cidx = (step, nchunk - 1 - step)
        rsl = tuple(slice(c * L, (c + 1) * L) for c in cidx)

        g = {}
        for d in range(2):
            gcc = gc[rsl[d]]
            lf_r8 = _log_sigmoid(gr_ref[(2 * d + 1) * NH:(2 * d + 2) * NH, rsl[d]] + fbc_ref[d])
            g[d] = (gcc, _tri_matmul(tri_cs[d], gcc), _matmul_tri(lf_r8, tri_rs[d]),
                    gr_ref[2 * d * NH:(2 * d + 1) * NH, rsl[d]])
        picks = {}
        for hh, d in units:
            head = head0 + hh
            gcc, F_c_all, F_r_all, ig_r_all = g[d]
            ig_c = pick_lane(gcc, 2 * d * NH + head)
            F_c = pick_lane(F_c_all, (2 * d + 1) * NH + head)
            ig_r = pick_row(ig_r_all, head)
            F_r = pick_row(F_r_all, head)
            F_end = F_c[L - 1:L, :] if d == 0 else F_c[0:1, :]
            picks[hh, d] = (ig_c, F_c, ig_r, F_r, F_end)

        decay = {}
        for u in units:
            ig_c, F_c, ig_r, F_r, F_end = picks[u]
            m_prev = carry[u][1]
            logd = jnp.where(masks[u[1]], F_c - F_r + ig_r, -jnp.inf)
            log_inter = F_c + m_prev
            m_q = jnp.maximum(log_inter, jnp.max(logd, axis=-1, keepdims=True))
            decay[u] = (jnp.exp(logd - m_q), jnp.exp(log_inter - m_q), m_q)

        scores = {}
        for hh, d in units:
            qc, kc = qs[hh][rsl[d]], ks[hh][rsl[d]]
            scores[hh, d] = (lax.dot_general(qc, kc, NT, preferred_element_type=f32)
                             * decay[hh, d][0]).astype(bf16)

        for hh, d in units:
            _, a, m_q = decay[hh, d]
            qc = qs[hh][rsl[d]]
            va = vaugs[hh][rsl[d]]
            nd = (a * jnp.dot(qc, carry[hh, d][0].astype(bf16), preferred_element_type=f32)
                  + jnp.dot(scores[hh, d], va.astype(bf16), preferred_element_type=f32))
            num = nd[:, :DH]
            den = nd[:, DH:DH + 1]
            h_parts[hh][d][cidx[d]] = num / jnp.maximum(jnp.abs(den), jnp.exp(-m_q))

        for hh, d in units:
            ig_c, F_c, ig_r, F_r, F_end = picks[hh, d]
            caug, m_prev = carry[hh, d]
            log_w_c = F_end - F_c + ig_c
            log_w_r = F_end - F_r + ig_r
            m_new = jnp.maximum(F_end + m_prev, jnp.max(log_w_r, axis=-1, keepdims=True))
            ws_c = jnp.exp(log_w_c - m_new)
            d0 = jnp.exp(F_end + m_prev - m_new)
            upd = lax.dot_general(ks[hh][rsl[d]], (ws_c * vaugs[hh][rsl[d]]).astype(bf16), TN,
                                  preferred_element_type=f32)
            carry[hh, d] = (d0 * caug + upd, m_new)
    finals = [[carry[hh, d] for d in range(2)] for hh in range(hps)]

    for hh in range(hps):
        ls = slice(hh * DH, (hh + 1) * DH)
        hsum = jnp.concatenate([h_parts[hh][0][c] + h_parts[hh][1][c] for c in range(nchunk)], axis=0)
        mu = jnp.mean(hsum, axis=-1, keepdims=True)
        dv = hsum - mu
        var = jnp.mean(dv * dv, axis=-1, keepdims=True)
        og = og_ref[:, ls].astype(f32)
        y_ref[:, ls] = (dv * lax.rsqrt(var + 1e-6) * gain_ref[...] * _sigmoid(og)).astype(y_ref.dtype)
        if want_state:
            for d in range(2):
                caug, m_fin = finals[hh][d]
                ct_ref[d, hh] = caug[:, :DH]
                nt_ref[d, hh] = caug[:, DH:2 * DH].T[0:1, :]
                mt_ref[d, hh] = jnp.broadcast_to(m_fin, (1, DH))


def mlstm_scan(bb, gates, gates_t, conv_w9, conv_b, wq, wk, fb_row, fb_col, gain, state, y_buf, c_buf,
               layer, *, n_seq, T, row_blk0, grid_rows, name):
    has_state = state is not None
    want_state = layer is not None

    hps = _heads_per_step(T)
    wh = hps * DH

    def col(off):
        return lambda b, h: (row_blk0 + b, off // hps + h)

    in_specs = [pl.BlockSpec((T, wh), col(0)), pl.BlockSpec((T, wh), col(NH)),
                pl.BlockSpec((T, wh), col(2 * NH)),
                pl.BlockSpec((T, DH), lambda b, h: (row_blk0 + b, 0)),
                pl.BlockSpec((DH, T), lambda b, h: (0, row_blk0 + b)),
                pl.BlockSpec((9, wh), lambda b, h: (0, h)),
                pl.BlockSpec((1, wh), lambda b, h: (0, h)),
                pl.BlockSpec((hps, DH, DH), lambda b, h: (h, 0, 0)),
                pl.BlockSpec((hps, DH, DH), lambda b, h: (h, 0, 0)),
                pl.BlockSpec((1, DH), lambda b, h: (0, 0)),
                pl.BlockSpec((2, NH, 1), lambda b, h: (0, 0, 0)),
                pl.BlockSpec((1, DH), lambda b, h: (0, 0))]
    args = [bb, bb, bb, gates, gates_t, conv_w9, conv_b, wq, wk, fb_row, fb_col, gain]
    if has_state:
        c0aug, m0 = state
        in_specs += [pl.BlockSpec((None, 2, hps, DH, 2 * DH), lambda b, h: (b, 0, h, 0, 0)),
                     pl.BlockSpec((None, 2, hps, 1, 1), lambda b, h: (b, 0, h, 0, 0))]
        args += [c0aug, m0]
    out_specs = [pl.BlockSpec((T, wh), col(0))]
    out_shape = [jax.ShapeDtypeStruct((bb.shape[0], D), bf16)]
    aliases = {}
    if y_buf is not None:
        in_specs.append(pl.BlockSpec(memory_space=pl.ANY))
        args.append(y_buf)
        aliases[len(args) - 1] = 0
    if want_state:
        out_specs += [pl.BlockSpec((None, None, 2, hps, DH, DH), lambda b, h: (b, layer, 0, h, 0, 0)),
                      pl.BlockSpec((None, 2, hps, 1, DH), lambda b, h: (b, 0, h, 0, 0)),
                      pl.BlockSpec((None, 2, hps, 1, DH), lambda b, h: (b, 0, h, 0, 0))]
        out_shape += [jax.ShapeDtypeStruct((n_seq, DEPTH, 2, NH, DH, DH), f32),
                      jax.ShapeDtypeStruct((n_seq, 2, NH, 1, DH), f32),
                      jax.ShapeDtypeStruct((n_seq, 2, NH, 1, DH), f32)]
        if c_buf is not None:
            in_specs.append(pl.BlockSpec(memory_space=pl.ANY))
            args.append(c_buf)
            aliases[len(args) - 1] = 1
    return pl.pallas_call(
        functools.partial(_mlstm_kernel, grid_rows=grid_rows, has_state=has_state,
                          want_state=want_state, n_aliased=len(aliases)),
        grid=(n_seq, NH // hps),
        in_specs=in_specs, out_specs=out_specs, out_shape=out_shape,
        input_output_aliases=aliases,
        compiler_params=_cparams("parallel", "parallel"),
        name=name,
    )(*args)


def _software_pipeline(n, stages):
    vals = [None] * n
    for t in range(n + len(stages) - 1):
        for k in range(len(stages) - 1, -1, -1):
            s = t - k
            if 0 <= s < n:
                vals[s] = stages[k](s, vals[s])


SUB_ROWS = 256


def _merge_kernel(ya_ref, yb_ref, gm_ref, x_ref, mod_ref, wa_ref, wb_ref, wo_ref, lg_ref, lb_ref,
                  x1_ref, hh_ref):
    def rows(s):
        return slice(s * SUB_ROWS, (s + 1) * SUB_ROWS)

    def branches(s, _):
        return (jnp.dot(ya_ref[rows(s), :], wa_ref[...], preferred_element_type=f32),
                jnp.dot(yb_ref[rows(s), :], wb_ref[...], preferred_element_type=f32))

    def gate(s, p):
        gma = gm_ref[rows(s), :D].astype(f32)
        gmb = gm_ref[rows(s), D:].astype(f32)
        return (_sigmoid(gma) * p[0] + _sigmoid(gmb) * p[1]).astype(bf16)

    def project(s, merged):
        return jnp.dot(merged, wo_ref[...], preferred_element_type=f32)

    def norm(s, y):
        x1 = _layer_norm(ALPHA * x_ref[rows(s), :] + mod_ref[2:3, :] * y, lg_ref[...], lb_ref[...])
        x1_ref[rows(s), :] = x1
        hh_ref[rows(s), :] = (x1 * (1.0 + mod_ref[4:5, :]) + mod_ref[3:4, :]).astype(hh_ref.dtype)

    _software_pipeline(ya_ref.shape[0] // SUB_ROWS, (branches, gate, project, norm))


def merge(ya, yb, gm, x, mod_l, wa, wb, wo, lg, lb, hh_dtype):
    n = x.shape[0]
    tm = 512
    grp = _group_of_tile(tm)
    row = lambda i: (i, 0)
    const = lambda i: (0, 0)
    return pl.pallas_call(
        _merge_kernel,
        grid=(n // tm,),
        in_specs=[pl.BlockSpec((tm, D), row), pl.BlockSpec((tm, D), row),
                  pl.BlockSpec((tm, 2 * D), row), pl.BlockSpec((tm, D), row),
                  pl.BlockSpec((None, 8, D), lambda i: (grp(i), 0, 0)),
                  pl.BlockSpec((D, D), const), pl.BlockSpec((D, D), const), pl.BlockSpec((D, D), const),
                  pl.BlockSpec((1, D), const), pl.BlockSpec((1, D), const)],
        out_specs=[pl.BlockSpec((tm, D), row), pl.BlockSpec((tm, D), row)],
        out_shape=[jax.ShapeDtypeStruct((n, D), f32), jax.ShapeDtypeStruct((n, D), hh_dtype)],
        compiler_params=_cparams("parallel"),
        name="merge",
    )(ya, yb, gm, x, mod_l, wa, wb, wo, lg.reshape(1, D), lb.reshape(1, D))


def _post_ffn(f, x1, mod, mod_next, lg, lb, x2_ref, hn_ref):
    x2 = _layer_norm(ALPHA * x1 + mod[5:6, :] * f, lg, lb)
    x2_ref[...] = x2
    hn_ref[...] = (x2 * (1.0 + mod_next[1:2, :]) + mod_next[0:1, :]).astype(hn_ref.dtype)


def _ffn_kernel(hh_ref, w1_ref, w3_ref, w2_ref, x1_ref, mod_ref, modn_ref, lg_ref, lb_ref,
                x2_ref, hn_ref, acc_ref):
    j = pl.program_id(1)

    @pl.when(j == 0)
    def _():
        acc_ref[...] = jnp.zeros_like(acc_ref)

    def rows(s):
        return slice(s * SUB_ROWS, (s + 1) * SUB_ROWS)

    def up(s, _):
        hh = hh_ref[rows(s), :]
        return (jnp.dot(hh, w1_ref[...], preferred_element_type=f32),
                jnp.dot(hh, w3_ref[...], preferred_element_type=f32))

    def act(s, ab):
        return (ab[0] * _sigmoid(ab[0]) * ab[1]).astype(bf16)

    def down(s, h):
        return jnp.dot(h, w2_ref[...], preferred_element_type=f32)

    def accumulate(s, part):
        acc_ref[rows(s), :] += part

    _software_pipeline(hh_ref.shape[0] // SUB_ROWS, (up, act, down, accumulate))

    @pl.when(j == pl.num_programs(1) - 1)
    def _():
        _post_ffn(acc_ref[...], x1_ref[...], mod_ref, modn_ref, lg_ref[...], lb_ref[...],
                  x2_ref, hn_ref)


def ffn(hh, w1, w3, w2, x1, mod_l, mod_next, lg, lb):
    n = hh.shape[0]
    tm, tf = 512, 1408
    grp = _group_of_tile(tm)
    row = lambda i, j: (i, 0)
    const = lambda i, j: (0, 0)
    return pl.pallas_call(
        _ffn_kernel,
        grid=(n // tm, D_FF // tf),
        in_specs=[pl.BlockSpec((tm, D), row),
                  pl.BlockSpec((D, tf), lambda i, j: (0, j)),
                  pl.BlockSpec((D, tf), lambda i, j: (0, j)),
                  pl.BlockSpec((tf, D), lambda i, j: (j, 0)),
                  pl.BlockSpec((tm, D), row),
                  pl.BlockSpec((None, 8, D), lambda i, j: (grp(i), 0, 0)),
                  pl.BlockSpec((None, 8, D), lambda i, j: (grp(i), 0, 0)),
                  pl.BlockSpec((1, D), const), pl.BlockSpec((1, D), const)],
        out_specs=[pl.BlockSpec((tm, D), row), pl.BlockSpec((tm, D), row)],
        out_shape=[jax.ShapeDtypeStruct((n, D), f32), jax.ShapeDtypeStruct((n, D), bf16)],
        scratch_shapes=[pltpu.VMEM((tm, D), f32)],
        compiler_params=_cparams("parallel", "arbitrary"),
        name="ffn",
    )(hh, w1, w3, w2, x1, mod_l, mod_next, lg.reshape(1, D), lb.reshape(1, D))


def _router_kernel(hh_ref, wr_ref, br_ref, o_ref, cnt_ref, run_ref):
    hh = hh_ref[...].astype(f32)
    wr = wr_ref[...]
    h_hi = hh.astype(bf16)
    h_lo = (hh - h_hi.astype(f32)).astype(bf16)
    w_hi = wr.astype(bf16)
    w_lo = (wr - w_hi.astype(f32)).astype(bf16)
    logits = (jnp.dot(h_hi, w_hi, preferred_element_type=f32)
              + jnp.dot(h_hi, w_lo, preferred_element_type=f32)
              + jnp.dot(h_lo, w_hi, preferred_element_type=f32)) + br_ref[...]
    lane = lax.broadcasted_iota(jnp.int32, logits.shape, 1).astype(f32)
    m1 = jnp.max(logits, axis=-1, keepdims=True)
    i1 = jnp.min(jnp.where(logits == m1, lane, float(DH)), axis=-1, keepdims=True)
    rest = jnp.where(lane == i1, -jnp.inf, logits)
    m2 = jnp.max(rest, axis=-1, keepdims=True)
    i2 = jnp.min(jnp.where(rest == m2, lane, float(DH)), axis=-1, keepdims=True)
    e21 = jnp.exp(m2 - m1)
    p1 = 1.0 / (1.0 + e21)
    p2 = e21 * p1

    @pl.when(pl.program_id(0) == 0)
    def _():
        run_ref[...] = jnp.zeros_like(run_ref)

    tm = logits.shape[0]
    r_i = lax.broadcasted_iota(jnp.int32, (tm, tm), 0)
    c_i = lax.broadcasted_iota(jnp.int32, (tm, tm), 1)
    earlier = jnp.where(c_i < r_i, 1.0, 0.0).astype(bf16)
    oh1 = jnp.where(lane == i1, 1.0, 0.0)
    oh2 = jnp.where(lane == i2, 1.0, 0.0)
    pre1 = jnp.dot(earlier, oh1.astype(bf16), preferred_element_type=f32)
    pre2 = jnp.dot(earlier, oh2.astype(bf16), preferred_element_type=f32)
    cnt1 = jnp.sum(oh1, axis=0, keepdims=True)
    cnt2 = jnp.sum(oh2, axis=0, keepdims=True)
    run = run_ref[...]
    rank1 = jnp.sum(oh1 * (run + pre1), axis=-1, keepdims=True)
    rank2 = jnp.sum(oh2 * (run + cnt1 + pre2), axis=-1, keepdims=True)
    run_ref[...] = run + cnt1 + cnt2
    cnt_ref[...] = jnp.broadcast_to(run + cnt1 + cnt2, cnt_ref.shape)
    o_ref[...] = (jnp.where(lane == 0.0, i1, 0.0) + jnp.where(lane == 1.0, i2, 0.0)
                  + jnp.where(lane == 2.0, p1, 0.0) + jnp.where(lane == 3.0, p2, 0.0)
                  + jnp.where(lane == 4.0, rank1, 0.0) + jnp.where(lane == 5.0, rank2, 0.0))


def router(hh, wr_pad, br_pad):
    n = hh.shape[0]
    tm = 1024
    return pl.pallas_call(
        _router_kernel,
        grid=(n // tm,),
        in_specs=[pl.BlockSpec((tm, D), lambda i: (i, 0)),
                  pl.BlockSpec((D, DH), lambda i: (0, 0)),
                  pl.BlockSpec((1, DH), lambda i: (0, 0))],
        out_specs=[pl.BlockSpec((tm, DH), lambda i: (i, 0)),
                   pl.BlockSpec((8, DH), lambda i: (0, 0))],
        out_shape=[jax.ShapeDtypeStruct((n, DH), f32), jax.ShapeDtypeStruct((8, DH), f32)],
        scratch_shapes=[pltpu.VMEM((1, DH), f32)],
        compiler_params=_cparams("arbitrary"),
        name="router",
    )(hh, wr_pad, br_pad)


def _moe_up_kernel(te_ref, x_ref, w1_ref, w3_ref, a_ref):
    del te_ref
    x = x_ref[...].astype(bf16)
    a = jnp.dot(x, w1_ref[...].astype(bf16), preferred_element_type=f32)
    b = jnp.dot(x, w3_ref[...].astype(bf16), preferred_element_type=f32)
    a_ref[...] = (a * _sigmoid(a) * b).astype(a_ref.dtype)


def _moe_down_kernel(te_ref, a_ref, w2_ref, y_ref):
    del te_ref
    y_ref[...] = jnp.dot(a_ref[...], w2_ref[...].astype(bf16), preferred_element_type=f32)


def moe_experts(tile_expert, xs, w1, w3, w2, lyr):
    r = xs.shape[0]
    nt = r // MOE_TM
    tf = 1408
    up = pl.pallas_call(
        _moe_up_kernel,
        grid_spec=pltpu.PrefetchScalarGridSpec(
            num_scalar_prefetch=1,
            grid=(D_FF // tf, nt),
            in_specs=[pl.BlockSpec((MOE_TM, D), lambda j, i, te: (i, 0)),
                      pl.BlockSpec((None, None, D, tf), lambda j, i, te: (lyr, te[i], 0, j)),
                      pl.BlockSpec((None, None, D, tf), lambda j, i, te: (lyr, te[i], 0, j))],
            out_specs=pl.BlockSpec((MOE_TM, tf), lambda j, i, te: (i, j))),
        out_shape=jax.ShapeDtypeStruct((r, D_FF), bf16),
        compiler_params=_cparams("arbitrary", "arbitrary"),
        name="moe_up",
    )(tile_expert, xs, w1, w3)
    return pl.pallas_call(
        _moe_down_kernel,
        grid_spec=pltpu.PrefetchScalarGridSpec(
            num_scalar_prefetch=1,
            grid=(nt,),
            in_specs=[pl.BlockSpec((MOE_TM, D_FF), lambda i, te: (i, 0)),
                      pl.BlockSpec((None, None, D_FF, D), lambda i, te: (lyr, te[i], 0, 0))],
            out_specs=pl.BlockSpec((MOE_TM, D), lambda i, te: (i, 0))),
        out_shape=jax.ShapeDtypeStruct((r, D), f32),
        compiler_params=_cparams("arbitrary"),
        name="moe_down",
    )(tile_expert, up, w2)


def _combine_kernel(ya_ref, yb_ref, rt_ref, x1_ref, mod_ref, modn_ref, lg_ref, lb_ref, x2_ref, hn_ref):
    f = rt_ref[:, 2:3] * ya_ref[...] + rt_ref[:, 3:4] * yb_ref[...]
    _post_ffn(f, x1_ref[...], mod_ref, modn_ref, lg_ref[...], lb_ref[...], x2_ref, hn_ref)


def combine(ya, yb, rt, x1, mod_l, mod_next, lg, lb):
    n = x1.shape[0]
    tm = 1024
    grp = _group_of_tile(tm)
    row = lambda i: (i, 0)
    const = lambda i: (0, 0)
    return pl.pallas_call(
        _combine_kernel,
        grid=(n // tm,),
        in_specs=[pl.BlockSpec((tm, D), row), pl.BlockSpec((tm, D), row),
                  pl.BlockSpec((tm, DH), row), pl.BlockSpec((tm, D), row),
                  pl.BlockSpec((None, 8, D), lambda i: (grp(i), 0, 0)),
                  pl.BlockSpec((None, 8, D), lambda i: (grp(i), 0, 0)),
                  pl.BlockSpec((1, D), const), pl.BlockSpec((1, D), const)],
        out_specs=[pl.BlockSpec((tm, D), row), pl.BlockSpec((tm, D), row)],
        out_shape=[jax.ShapeDtypeStruct((n, D), f32), jax.ShapeDtypeStruct((n, D), bf16)],
        compiler_params=_cparams("parallel"),
        name="combine",
    )(ya, yb, rt, x1, mod_l, mod_next, lg.reshape(1, D), lb.reshape(1, D))


def _combine_final_kernel(ya_ref, yb_ref, rt_ref, x1_ref, mod_ref, lg_ref, lb_ref, yp_ref, ys_ref,
                          *, ctx_tiles):
    f = rt_ref[:, 2:3] * ya_ref[...] + rt_ref[:, 3:4] * yb_ref[...]
    x2 = _layer_norm(ALPHA * x1_ref[...] + mod_ref[5:6, :] * f, lg_ref[...], lb_ref[...])
    i = pl.program_id(0)

    @pl.when(i < ctx_tiles)
    def _():
        yp_ref[...] = x2

    @pl.when(i >= ctx_tiles)
    def _():
        ys_ref[...] = x2


def combine_final(ya, yb, rt, x1, mod_l, lg, lb, n_ctx):
    n = x1.shape[0]
    tm = 1024
    ctx_tiles = n_ctx // tm
    grp = _group_of_tile(tm)
    row = lambda i: (i, 0)
    const = lambda i: (0, 0)
    return pl.pallas_call(
        functools.partial(_combine_final_kernel, ctx_tiles=ctx_tiles),
        grid=(n // tm,),
        in_specs=[pl.BlockSpec((tm, D), row), pl.BlockSpec((tm, D), row),
                  pl.BlockSpec((tm, DH), row), pl.BlockSpec((tm, D), row),
                  pl.BlockSpec((None, 8, D), lambda i: (grp(i), 0, 0)),
                  pl.BlockSpec((1, D), const), pl.BlockSpec((1, D), const)],
        out_specs=[pl.BlockSpec((tm, D), lambda i: (jnp.minimum(i, ctx_tiles - 1), 0)),
                   pl.BlockSpec((tm, D), lambda i: (jnp.maximum(i - ctx_tiles, 0), 0))],
        out_shape=[jax.ShapeDtypeStruct((n_ctx, D), f32), jax.ShapeDtypeStruct((n - n_ctx, D), f32)],
        compiler_params=_cparams("arbitrary"),
        name="combine_final",
    )(ya, yb, rt, x1, mod_l, lg.reshape(1, D), lb.reshape(1, D))


def moe_dispatch(rt, counts, n):
    r_total = 2 * n + N_EXPERTS * MOE_TM
    nt = r_total // MOE_TM
    cnt = counts[0, :N_EXPERTS].astype(jnp.int32)
    padded = (cnt + MOE_TM - 1) // MOE_TM * MOE_TM
    ends = jnp.cumsum(padded)
    starts = ends - padded
    e_idx = rt[:, 0:2].astype(jnp.int32)
    rank = rt[:, 4:6].astype(jnp.int32)
    onehot = e_idx[..., None] == jnp.arange(N_EXPERTS)[None, None, :]
    dest = rank + jnp.sum(jnp.where(onehot, starts[None, None, :], 0), axis=-1)
    tok = jnp.broadcast_to(jnp.arange(n, dtype=jnp.int32)[:, None], (n, 2))
    src_tok = jnp.zeros((r_total,), jnp.int32).at[dest.reshape(-1)].set(
        tok.reshape(-1), unique_indices=True, mode="promise_in_bounds")
    tile_start = jnp.arange(nt, dtype=jnp.int32) * MOE_TM
    tile_expert = jnp.minimum(jnp.sum((tile_start[:, None] >= ends[None, :]).astype(jnp.int32), axis=1),
                              N_EXPERTS - 1)
    return src_tok, dest, tile_expert


def _take_rows(a, idx):
    return a.at[idx].get(mode="promise_in_bounds")


def kernel(x_prompt, x_sample, c, state_hgrn, state_mlstm_C, state_mlstm_n, state_mlstm_m, c_ctx,
           w_ada, b_ada, w_in, b_in, hgrn_lb_raw, hgrn_norm_g, conv_w, conv_b, w_mq, w_mk,
           mlstm_fbias, mlstm_norm_g, w_branch_a, w_branch_b, w_out, ln1_g, ln1_b, ln2_g, ln2_b,
           ffn_w1, ffn_w3, ffn_w2, moe_router_w, moe_router_b, moe_w1, moe_w3, moe_w2):
    nb, seq, _ = x_prompt.shape
    db, dseq, _ = x_sample.shape
    n_ctx, n_lat = nb * seq, db * dseq
    n = n_ctx + n_lat
    assert n_ctx == N_CTX and dseq == SEQ_GROUP and db == 2

    lb = jnp.cumsum(jax.nn.softmax(hgrn_lb_raw.astype(f32), axis=1), axis=1)
    lb = lb - lb[:, :1]

    cond8 = jnp.concatenate([c_ctx[None, :], c, jnp.zeros((5, D), f32)], axis=0)
    mod = modulation(cond8, w_ada, b_ada)[:, :3, :].reshape(DEPTH, 3, 6, D)
    mod = jnp.pad(mod, ((0, 0), (0, 0), (0, 2), (0, 0)))

    x = jnp.concatenate([x_prompt.reshape(n_ctx, D), x_sample.reshape(n_lat, D)], axis=0)
    h = modulate(x, mod[0])

    sizes = (D,) * 8 + (4 * NH, D, D)
    offs = [0]
    for s in sizes:
        offs.append(offs[-1] + s)

    def cols(w, idxs):
        return jnp.concatenate([w[..., offs[i]:offs[i + 1]] for i in idxs], axis=-1)

    out_n, out_m = [], []
    buf_a = buf_c = None
    for l in range(DEPTH):
        wl, bl = w_in[l], b_in[l]
        qig = linear(h, cols(wl, (0, 3, 4)).astype(bf16), cols(bl, (0, 3, 4)), bf16, "proj_qig")
        lbl = lb[:, l].reshape(1, 2 * D)
        lbp = jnp.concatenate([lbl, 1.0 - lbl, jnp.zeros((6, 2 * D), f32)], axis=0)
        lf, kk = hgrn_gates(h, cols(wl, (1, 2)).astype(bf16), cols(bl, (1, 2)), lbp)
        bb = linear(h, cols(wl, (5, 6, 7)).astype(bf16), cols(bl, (5, 6, 7)), bf16, "proj_b")
        gm = linear(h, cols(wl, (9, 10)).astype(bf16), cols(bl, (9, 10)), bf16, "proj_gm")
        w_g = jnp.pad(cols(wl, (8,)), ((0, 0), (0, DH - 4 * NH))).astype(bf16)
        gates = linear(h, w_g, jnp.pad(cols(bl, (8,)), (0, DH - 4 * NH)), f32, "proj_gates")
        gates_t = gates.T

        gain_a = hgrn_norm_g[l].reshape(1, DH)
        ya, buf_a = hgrn_scan(qig, lf, kk, gain_a, None, None, buf_a, l, n_seq=nb, T=seq,
                              row_blk0=0, name="hgrn_ctx")
        (ya,) = hgrn_scan(qig, lf, kk, gain_a, state_hgrn[:, l], ya, None, None, n_seq=db, T=dseq,
                          row_blk0=n_ctx // dseq, name="hgrn_lat")

        cw9 = conv_w[l].reshape(9, D)
        cb = conv_b[l].reshape(1, D)
        wq = w_mq[l].astype(bf16)
        wk = w_mk[l].astype(bf16)
        fb = mlstm_fbias[l]
        zero8 = jnp.zeros((NH,), f32)
        fb_row = jnp.concatenate([zero8, fb[0], zero8, fb[1], jnp.zeros((DH - 4 * NH,), f32)]).reshape(1, DH)
        fb_col = fb.reshape(2, NH, 1)
        gain_b = mlstm_norm_g[l].reshape(1, DH)
        yb, buf_c, s_n, s_m = mlstm_scan(bb, gates, gates_t, cw9, cb, wq, wk, fb_row, fb_col, gain_b,
                                         None, None, buf_c, l, n_seq=nb, T=seq, row_blk0=0,
                                         grid_rows=None, name="mlstm_ctx")
        c0aug = jnp.concatenate([state_mlstm_C[:, l], state_mlstm_n[:, l][..., None],
                                 jnp.zeros((db, 2, NH, DH, DH - 1), f32)], axis=-1)
        m0 = state_mlstm_m[:, l].reshape(db, 2, NH, 1, 1)
        (yb,) = mlstm_scan(bb, gates, gates_t, cw9, cb, wq, wk, fb_row, fb_col, gain_b,
                           (c0aug, m0), yb, None, None, n_seq=db, T=dseq, row_blk0=n_ctx // dseq,
                           grid_rows=dseq // GRID_W, name="mlstm_lat")

        x1, hh = merge(ya, yb, gm, x, mod[l], w_branch_a[l].astype(bf16), w_branch_b[l].astype(bf16),
                       w_out[l].astype(bf16), ln1_g[l], ln1_b[l], bf16 if l % 2 == 0 else f32)

        mod_next = mod[(l + 1) % DEPTH]
        j = l // 2
        if l % 2 == 0:
            x, h = ffn(hh, ffn_w1[j].astype(bf16), ffn_w3[j].astype(bf16), ffn_w2[j].astype(bf16),
                       x1, mod[l], mod_next, ln2_g[l], ln2_b[l])
        else:
            wr = jnp.pad(moe_router_w[j], ((0, 0), (0, DH - N_EXPERTS)))
            br = jnp.concatenate([moe_router_b[j], jnp.full((DH - N_EXPERTS,), -1e30, f32)]).reshape(1, DH)
            rt, counts = router(hh, wr, br)
            src_tok, dest, tile_expert = moe_dispatch(rt, counts, n)
            ys = moe_experts(tile_expert, _take_rows(hh, src_tok), moe_w1, moe_w3, moe_w2, j)
            ya_rows, yb_rows = _take_rows(ys, dest[:, 0]), _take_rows(ys, dest[:, 1])
            if l == DEPTH - 1:
                y_ctx, y_lat = combine_final(ya_rows, yb_rows, rt, x1, mod[l], ln2_g[l], ln2_b[l], n_ctx)
            else:
                x, h = combine(ya_rows, yb_rows, rt, x1, mod[l], mod_next, ln2_g[l], ln2_b[l])

        out_n.append(s_n.reshape(nb, 2, NH, DH))
        out_m.append(s_m[..., 0, 0])

    y_prompt = y_ctx.reshape(nb, seq, D)
    y_sample = y_lat.reshape(db, dseq, D)
    return (y_prompt, y_sample, buf_a, buf_c, jnp.stack(out_n, axis=1), jnp.stack(out_m, axis=1))
```

```python
import functools

import jax
import jax.numpy as jnp
from jax import lax
from jax.experimental import pallas as pl
from jax.experimental.pallas import tpu as pltpu

f32 = jnp.float32
bf16 = jnp.bfloat16

D = 1024
DEPTH = 4
GRID_W = 64
NH = 8
DH = 128
CH_A = 16
HALF_A = 8
SC_A = 128
SCAN_ROWS_PER_STEP = 4096
CH_B = 256
N_EXPERTS = 8
D_FF = 2816
ALPHA = (2 * DEPTH) ** 0.25
SEQ_GROUP = 1024
LOG2E = 1.4426950408889634
LOG2_F_FLOOR = -1e30
MOE_TM = 512

VMEM_LIMIT = 56 * 1024 * 1024
NT = (((1,), (1,)), ((), ()))
TN = (((0,), (0,)), ((), ()))


def _heads_per_step(seq_len):
    return max(1, min(NH, SCAN_ROWS_PER_STEP // seq_len))


def _cparams(*sem):
    return pltpu.CompilerParams(dimension_semantics=sem, vmem_limit_bytes=VMEM_LIMIT)


def _sigmoid(x):
    return 1.0 / (1.0 + jnp.exp(-x))


def _log_sigmoid(t):
    return jnp.minimum(t, 0.0) - jnp.log1p(jnp.exp(-jnp.abs(t)))


def _layer_norm(r, g, b):
    mu = jnp.mean(r, axis=-1, keepdims=True)
    d = r - mu
    var = jnp.mean(d * d, axis=-1, keepdims=True)
    return d * lax.rsqrt(var + 1e-5) * g + b


def _mod_kernel(c_ref, wt_ref, wb_ref, b_ref, o_ref):
    c = c_ref[...]
    act = (c * _sigmoid(c)).astype(bf16)
    half = D // 2
    o_ref[...] = (jnp.dot(act[:, :half], wt_ref[...].astype(bf16), preferred_element_type=f32)
                  + jnp.dot(act[:, half:], wb_ref[...].astype(bf16), preferred_element_type=f32)
                  + b_ref[...])


def modulation(cond8, w_ada, b_ada):
    L, _, n = w_ada.shape
    tn = 1536
    half = D // 2
    return pl.pallas_call(
        _mod_kernel,
        grid=(L, n // tn),
        in_specs=[pl.BlockSpec((8, D), lambda l, j: (0, 0)),
                  pl.BlockSpec((None, half, tn), lambda l, j: (l, 0, j)),
                  pl.BlockSpec((None, half, tn), lambda l, j: (l, 1, j)),
                  pl.BlockSpec((None, 1, tn), lambda l, j: (l, 0, j))],
        out_specs=pl.BlockSpec((None, 8, tn), lambda l, j: (l, 0, j)),
        out_shape=jax.ShapeDtypeStruct((L, 8, n), f32),
        compiler_params=_cparams("parallel", "parallel"),
        name="modulation",
    )(cond8, w_ada, w_ada, b_ada.reshape(L, 1, n))


N_CTX = 32 * 256


def _group_of_tile(tm):
    def idx(i):
        return jnp.maximum(i * tm // SEQ_GROUP - (N_CTX // SEQ_GROUP - 1), 0)
    return idx


def _modulate_kernel(x_ref, mod_ref, h_ref):
    h_ref[...] = (x_ref[...] * (1.0 + mod_ref[1:2, :]) + mod_ref[0:1, :]).astype(h_ref.dtype)


def modulate(x, mod_l):
    n = x.shape[0]
    tm = 1024
    grp = _group_of_tile(tm)
    return pl.pallas_call(
        _modulate_kernel,
        grid=(n // tm,),
        in_specs=[pl.BlockSpec((tm, D), lambda i: (i, 0)),
                  pl.BlockSpec((None, 8, D), lambda i: (grp(i), 0, 0))],
        out_specs=pl.BlockSpec((tm, D), lambda i: (i, 0)),
        out_shape=jax.ShapeDtypeStruct((n, D), bf16),
        compiler_params=_cparams("parallel"),
        name="modulate",
    )(x, mod_l)


def _linear_kernel(x_ref, w_ref, b_ref, o_ref):
    acc = jnp.dot(x_ref[...], w_ref[...], preferred_element_type=f32)
    o_ref[...] = (acc + b_ref[...]).astype(o_ref.dtype)


def _weight_specs(w, b, layer, n_blocks, col_of, tn):
    if layer is None:
        k, n = w.shape
        w, b, layer, col_of = w.reshape(1, k, n), b.reshape(1, 1, n), 0, (lambda j: j)
    else:
        k, n = w.shape[1], n_blocks * tn
    specs = [pl.BlockSpec((None, k, tn), lambda i, j: (layer, 0, col_of(j))),
             pl.BlockSpec((None, 1, tn), lambda i, j: (layer, 0, col_of(j)))]
    return w, b, n, specs


def linear(x, w, b, out_dtype, name, layer=None, n_blocks=None, col_of=None):
    m, k = x.shape
    tm = 2048
    tn = 1024 if (layer is not None or w.shape[1] % 1024 == 0) else w.shape[1]
    w, b, n, w_specs = _weight_specs(w, b, layer, n_blocks, col_of, tn)
    return pl.pallas_call(
        _linear_kernel,
        grid=(m // tm, n // tn),
        in_specs=[pl.BlockSpec((tm, k), lambda i, j: (i, 0))] + w_specs,
        out_specs=pl.BlockSpec((tm, tn), lambda i, j: (i, j)),
        out_shape=jax.ShapeDtypeStruct((m, n), out_dtype),
        compiler_params=_cparams("parallel", "parallel"),
        name=name,
    )(x, w, b)


def _hgrn_gate_kernel(x_ref, w_ref, b_ref, lbp_ref, lf_ref, kk_ref):
    z = jnp.dot(x_ref[...], w_ref[...], preferred_element_type=f32) + b_ref[...]
    lb = lbp_ref[0:1, :]
    one_m_lb = lbp_ref[1:2, :]
    e = jnp.exp(-jnp.abs(z))
    t = one_m_lb / (1.0 + e)
    kk_ref[...] = t * jnp.where(z >= 0.0, e, 1.0)
    f = lb + t * jnp.where(z >= 0.0, 1.0, e)
    lf_ref[...] = jnp.maximum(jnp.log(f) * LOG2E, LOG2_F_FLOOR)


def hgrn_gates(x, w, b, lbp, layer, n_blocks, col_of):
    m, k = x.shape
    tm, tn = 1024, 1024
    w, b, n, w_specs = _weight_specs(w, b, layer, n_blocks, col_of, tn)
    return pl.pallas_call(
        _hgrn_gate_kernel,
        grid=(m // tm, n // tn),
        in_specs=[pl.BlockSpec((tm, k), lambda i, j: (i, 0))] + w_specs
                 + [pl.BlockSpec((8, tn), lambda i, j: (0, j))],
        out_specs=[pl.BlockSpec((tm, tn), lambda i, j: (i, j)),
                   pl.BlockSpec((tm, tn), lambda i, j: (i, j))],
        out_shape=[jax.ShapeDtypeStruct((m, n), f32), jax.ShapeDtypeStruct((m, n), f32)],
        compiler_params=_cparams("parallel", "parallel"),
        name="proj_z",
    )(x, w, b, lbp)


def _hgrn_kernel(*refs, has_state, want_state, n_aliased):
    q_ref, v_ref, g_ref, lff_ref, lfb_ref, kkf_ref, kkb_ref, gain_ref = refs[:8]
    lf_refs = (lff_ref, lfb_ref)
    kk_refs = (kkf_ref, kkb_ref)
    pos = 8
    s0_ref = None
    if has_state:
        s0_ref = refs[pos]
        pos += 1
    pos += n_aliased
    y_ref = refs[pos]
    pos += 1
    st_ref = None
    if want_state:
        st_ref = refs[pos]
        pos += 1
    b_ref, o_ref = refs[pos:]

    T = q_ref.shape[0]
    hps = q_ref.shape[1] // DH
    nsc = T // SC_A

    row = lax.broadcasted_iota(jnp.int32, (SC_A, SC_A), 0)
    col = lax.broadcasted_iota(jnp.int32, (SC_A, SC_A), 1)
    tris = tuple(jnp.where(m, 1.0, 0.0).astype(bf16) for m in (row >= col, row <= col))
    nblk = SC_A // CH_A
    half = SC_A // 2
    r_sq = lax.broadcasted_iota(jnp.int32, (half, half), 0)
    c_sq = lax.broadcasted_iota(jnp.int32, (half, half), 1)
    r_w = lax.broadcasted_iota(jnp.int32, (half, SC_A), 0)
    c_w = lax.broadcasted_iota(jnp.int32, (half, SC_A), 1)
    masks_lo = (r_sq >= c_sq, r_w <= c_w)
    masks_hi = (r_w + half >= c_w, r_sq <= c_sq)

    for d in range(2):
        for sc in range(nsc):
            rows = slice(sc * SC_A, (sc + 1) * SC_A)
            b_ref[d, rows, :] = _tri_matmul(tris[d], lf_refs[d][rows, :])

    if has_state:
        st0 = tuple(s0_ref[d, hh].T for hh in range(hps) for d in range(2))
    else:
        st0 = tuple(jnp.zeros((DH, DH), f32) for _ in range(2 * hps))

    def body(c, carry):
        units = []
        for hd in range(2 * hps):
            hh, d = divmod(hd, 2)
            ls = slice(hh * DH, (hh + 1) * DH)
            cidx = c if d == 0 else nsc - 1 - c
            r = pl.ds(pl.multiple_of(cidx * SC_A, SC_A), SC_A)
            units.append((d, ls, r))

        scores = []
        for d, ls, r in units:
            q = q_ref[r, ls].astype(f32)
            kk = kk_refs[d][r, ls]
            b = b_ref[d, r, ls]
            att_rows = ([], [])
            for i in range(nblk):
                mid = i * CH_A + (HALF_A - 1 if d == 0 else HALF_A)
                r_i = b[mid:mid + 1, :]
                blk = slice(i * CH_A, (i + 1) * CH_A)
                qi = (q[blk] * jnp.exp2(b[blk] - r_i)).astype(bf16)
                if d == 0 and i < nblk // 2:
                    ksl = slice(0, half)
                elif d == 1 and i >= nblk // 2:
                    ksl = slice(half, SC_A)
                else:
                    ksl = slice(0, SC_A)
                ks = (kk[ksl] * jnp.exp2(r_i - b[ksl])).astype(bf16)
                att_rows[i >= nblk // 2].append(lax.dot_general(qi, ks, NT, preferred_element_type=f32))
            scores.append(att_rows)

        for hd, (d, ls, r) in enumerate(units):
            v = v_ref[r, ls]
            a_lo = jnp.where(masks_lo[d], jnp.concatenate(scores[hd][0], axis=0), 0.0).astype(bf16)
            a_hi = jnp.where(masks_hi[d], jnp.concatenate(scores[hd][1], axis=0), 0.0).astype(bf16)
            o_lo = jnp.dot(a_lo, v[:half] if d == 0 else v, preferred_element_type=f32)
            o_hi = jnp.dot(a_hi, v if d == 0 else v[half:], preferred_element_type=f32)
            qb = (q_ref[r, ls].astype(f32) * jnp.exp2(b_ref[d, r, ls])).astype(bf16)
            o_ref[d, r, ls] = (jnp.concatenate([o_lo, o_hi], axis=0)
                               + lax.dot_general(qb, carry[hd].astype(bf16), NT,
                                                 preferred_element_type=f32))

        new = []
        for hd, (d, ls, r) in enumerate(units):
            b = b_ref[d, r, ls]
            b_end = b[SC_A - 1:SC_A, :] if d == 0 else b[0:1, :]
            kl = (kk_refs[d][r, ls] * jnp.exp2(b_end - b)).astype(bf16)
            new.append(jnp.exp2(b_end) * carry[hd]
                       + lax.dot_general(v_ref[r, ls], kl, TN, preferred_element_type=f32))
        return tuple(new)

    st_fin = lax.fori_loop(0, nsc, body, st0, unroll=2)

    for hh in range(hps):
        ls = slice(hh * DH, (hh + 1) * DH)
        o = o_ref[0, :, ls] + o_ref[1, :, ls]
        ms = jnp.mean(o * o, axis=-1, keepdims=True)
        g = g_ref[:, ls].astype(f32)
        y_ref[:, ls] = (o * lax.rsqrt(ms + 1e-6) * gain_ref[...] * (g * _sigmoid(g))).astype(y_ref.dtype)
        if want_state:
            for d in range(2):
                st_ref[d, hh] = st_fin[2 * hh + d].T


def hgrn_scan(qig, lf, kk, gain, s0, y_buf, state_buf, layer, *, n_seq, T, row_blk0, name):
    has_state = s0 is not None
    want_state = layer is not None

    hps = _heads_per_step(T)
    wh = hps * DH

    def col(off):
        return lambda b, h: (row_blk0 + b, off // hps + h)

    in_specs = [pl.BlockSpec((T, wh), col(0)), pl.BlockSpec((T, wh), col(NH)),
                pl.BlockSpec((T, wh), col(2 * NH)),
                pl.BlockSpec((T, wh), col(0)), pl.BlockSpec((T, wh), col(NH)),
                pl.BlockSpec((T, wh), col(0)), pl.BlockSpec((T, wh), col(NH)),
                pl.BlockSpec((1, DH), lambda b, h: (0, 0))]
    args = [qig, qig, qig, lf, lf, kk, kk, gain]
    if has_state:
        in_specs.append(pl.BlockSpec((None, 2, hps, DH, DH), lambda b, h: (b, 0, h, 0, 0)))
        args.append(s0)
    out_specs = [pl.BlockSpec((T, wh), col(0))]
    out_shape = [jax.ShapeDtypeStruct((qig.shape[0], D), bf16)]
    aliases = {}
    if y_buf is not None:
        in_specs.append(pl.BlockSpec(memory_space=pl.ANY))
        args.append(y_buf)
        aliases[len(args) - 1] = 0
    if want_state:
        out_specs.append(pl.BlockSpec((None, None, 2, hps, DH, DH),
                                      lambda b, h: (b, layer, 0, h, 0, 0)))
        out_shape.append(jax.ShapeDtypeStruct((n_seq, DEPTH, 2, NH, DH, DH), f32))
        if state_buf is not None:
            in_specs.append(pl.BlockSpec(memory_space=pl.ANY))
            args.append(state_buf)
            aliases[len(args) - 1] = 1
    scratch = [pltpu.VMEM((2, T, wh), f32)] * 2
    return pl.pallas_call(
        functools.partial(_hgrn_kernel, has_state=has_state, want_state=want_state,
                          n_aliased=len(aliases)),
        grid=(n_seq, NH // hps),
        in_specs=in_specs, out_specs=out_specs, out_shape=out_shape,
        scratch_shapes=scratch,
        input_output_aliases=aliases,
        compiler_params=_cparams("parallel", "parallel"),
        name=name,
    )(*args)


def _split3(x):
    x0 = x.astype(bf16)
    r = x - x0.astype(f32)
    x1 = r.astype(bf16)
    x2 = (r - x1.astype(f32)).astype(bf16)
    return x0, x1, x2


def _tri_matmul(tri, x):
    return sum(jnp.dot(tri, p, preferred_element_type=f32) for p in _split3(x))


def _matmul_tri(x, tri):
    return sum(jnp.dot(p, tri, preferred_element_type=f32) for p in _split3(x))


def _mlstm_kernel(*refs, grid_rows, has_state, want_state, n_aliased):
    x_ref, v_ref, og_ref, gc_ref, gr_ref, cw_ref, cb_ref, wq_ref, wk_ref, fbr_ref, fbc_ref, gain_ref = refs[:12]
    pos = 12
    if has_state:
        c0_ref, m0_ref = refs[pos:pos + 2]
        pos += 2
    pos += n_aliased
    y_ref = refs[pos]
    pos += 1
    if want_state:
        ct_ref, nt_ref, mt_ref = refs[pos:pos + 3]

    hps = x_ref.shape[1] // DH
    head0 = pl.program_id(1) * hps
    T = x_ref.shape[0]
    L = min(CH_B, T)
    nchunk = T // L

    t_idx = lax.broadcasted_iota(jnp.int32, (T, DH), 0)
    lane = lax.broadcasted_iota(jnp.int32, (T, DH), 1)
    ones_col = jnp.where(lane == 0, 1.0, 0.0)
    if grid_rows is None:
        taps = [(0, dc) for dc in (-1, 0, 1)]
    else:
        taps = [(dr, dc) for dr in (-1, 0, 1) for dc in (-1, 0, 1)]
        c_idx = t_idx % GRID_W
    cas, vaugs = [], []
    for hh in range(hps):
        ls = slice(hh * DH, (hh + 1) * DH)
        x = x_ref[:, ls].astype(f32)
        acc = jnp.zeros((T, DH), f32) + cb_ref[:, ls]
        for dr, dc in taps:
            off = dr * GRID_W + dc
            tap = (dr + 1) * 3 + (dc + 1)
            w = cw_ref[tap:tap + 1, ls]
            xs = x if off == 0 else pltpu.roll(x, (-off) % T, axis=0)
            ok = (t_idx + off >= 0) & (t_idx + off < T)
            if grid_rows is not None:
                ok = ok & (c_idx + dc >= 0) & (c_idx + dc < GRID_W)
            acc = acc + jnp.where(ok, xs, 0.0) * w
        cas.append((acc * _sigmoid(acc)).astype(bf16))
        vaugs.append(jnp.concatenate([v_ref[:, ls].astype(f32), ones_col], axis=1))
    qs = [jnp.dot(cas[hh], wq_ref[hh], preferred_element_type=f32).astype(bf16) for hh in range(hps)]
    ks = [(jnp.dot(cas[hh], wk_ref[hh], preferred_element_type=f32) * (DH ** -0.5)).astype(bf16)
          for hh in range(hps)]

    gate_lane = lax.broadcasted_iota(jnp.int32, (T, DH), 1) // NH
    is_f = (gate_lane == 1) | (gate_lane == 3)
    gc = gc_ref[...] + fbr_ref[...]
    gc = jnp.where(is_f, _log_sigmoid(gc), gc)

    def pick_lane(a, ln):
        sel = lax.broadcasted_iota(jnp.int32, a.shape, 1) == ln
        return jnp.sum(jnp.where(sel, a, 0.0), axis=1, keepdims=True)

    def pick_row(a, rw):
        sel = lax.broadcasted_iota(jnp.int32, a.shape, 0) == rw
        return jnp.sum(jnp.where(sel, a, 0.0), axis=0, keepdims=True)

    r_i = lax.broadcasted_iota(jnp.int32, (L, L), 0)
    c_i = lax.broadcasted_iota(jnp.int32, (L, L), 1)
    lower = r_i >= c_i
    upper = r_i <= c_i
    tri_lo = jnp.where(lower, 1.0, 0.0).astype(bf16)
    tri_up = jnp.where(upper, 1.0, 0.0).astype(bf16)

    h_parts = [[[None] * nchunk for _ in range(2)] for _ in range(hps)]
    units = [(hh, d) for hh in range(hps) for d in range(2)]
    if has_state:
        carry = {u: (c0_ref[u[1], u[0]], m0_ref[u[1], u[0]]) for u in units}
    else:
        carry = {u: (jnp.zeros((DH, 2 * DH), f32), jnp.zeros((1, 1), f32)) for u in units}
    masks = (lower, upper)
    tri_cs = (tri_lo, tri_up)
    tri_rs = (tri_up, tri_lo)

    for step in range(nchunk):
        cidx = (step, nchunk - 1 - step)
        rsl = tuple(slice(c * L, (c + 1) * L) for c in cidx)

        g = {}
        for d in range(2):
            gcc = gc[rsl[d]]
            lf_r8 = _log_sigmoid(gr_ref[(2 * d + 1) * NH:(2 * d + 2) * NH, rsl[d]] + fbc_ref[d])
            g[d] = (gcc, _tri_matmul(tri_cs[d], gcc), _matmul_tri(lf_r8, tri_rs[d]),
                    gr_ref[2 * d * NH:(2 * d + 1) * NH, rsl[d]])
        picks = {}
        for hh, d in units:
            head = head0 + hh
            gcc, F_c_all, F_r_all, ig_r_all = g[d]
            ig_c = pick_lane(gcc, 2 * d * NH + head)
            F_c = pick_lane(F_c_all, (2 * d + 1) * NH + head)
            ig_r = pick_row(ig_r_all, head)
            F_r = pick_row(F_r_all, head)
            F_end = F_c[L - 1:L, :] if d == 0 else F_c[0:1, :]
            picks[hh, d] = (ig_c, F_c, ig_r, F_r, F_end)

        decay = {}
        for u in units:
            ig_c, F_c, ig_r, F_r, F_end = picks[u]
            m_prev = carry[u][1]
            logd = jnp.where(masks[u[1]], F_c - F_r + ig_r, -jnp.inf)
            log_inter = F_c + m_prev
            m_q = jnp.maximum(log_inter, jnp.max(logd, axis=-1, keepdims=True))
            decay[u] = (jnp.exp(logd - m_q), jnp.exp(log_inter - m_q), m_q)

        scores = {}
        for hh, d in units:
            qc, kc = qs[hh][rsl[d]], ks[hh][rsl[d]]
            scores[hh, d] = (lax.dot_general(qc, kc, NT, preferred_element_type=f32)
                             * decay[hh, d][0]).astype(bf16)

        for hh, d in units:
            _, a, m_q = decay[hh, d]
            qc = qs[hh][rsl[d]]
            va = vaugs[hh][rsl[d]]
            nd = (a * jnp.dot(qc, carry[hh, d][0].astype(bf16), preferred_element_type=f32)
                  + jnp.dot(scores[hh, d], va.astype(bf16), preferred_element_type=f32))
            num = nd[:, :DH]
            den = nd[:, DH:DH + 1]
            h_parts[hh][d][cidx[d]] = num / jnp.maximum(jnp.abs(den), jnp.exp(-m_q))

        for hh, d in units:
            ig_c, F_c, ig_r, F_r, F_end = picks[hh, d]
            caug, m_prev = carry[hh, d]
            log_w_c = F_end - F_c + ig_c
            log_w_r = F_end - F_r + ig_r
            m_new = jnp.maximum(F_end + m_prev, jnp.max(log_w_r, axis=-1, keepdims=True))
            ws_c = jnp.exp(log_w_c - m_new)
            d0 = jnp.exp(F_end + m_prev - m_new)
            upd = lax.dot_general(ks[hh][rsl[d]], (ws_c * vaugs[hh][rsl[d]]).astype(bf16), TN,
                                  preferred_element_type=f32)
            carry[hh, d] = (d0 * caug + upd, m_new)
    finals = [[carry[hh, d] for d in range(2)] for hh in range(hps)]

    for hh in range(hps):
        ls = slice(hh * DH, (hh + 1) * DH)
        hsum = jnp.concatenate([h_parts[hh][0][c] + h_parts[hh][1][c] for c in range(nchunk)], axis=0)
        mu = jnp.mean(hsum, axis=-1, keepdims=True)
        dv = hsum - mu
        var = jnp.mean(dv * dv, axis=-1, keepdims=True)
        og = og_ref[:, ls].astype(f32)
        y_ref[:, ls] = (dv * lax.rsqrt(var + 1e-6) * gain_ref[...] * _sigmoid(og)).astype(y_ref.dtype)
        if want_state:
            for d in range(2):
                caug, m_fin = finals[hh][d]
                ct_ref[d, hh] = caug[:, :DH]
                nt_ref[d, hh] = caug[:, DH:2 * DH].T[0:1, :]
                mt_ref[d, hh] = jnp.broadcast_to(m_fin, (1, DH))


def mlstm_scan(bb, gates, gates_t, conv_w9, conv_b, wq, wk, fb_row, fb_col, gain, state, y_buf, c_buf,
               layer, *, n_seq, T, row_blk0, grid_rows, name):
    has_state = state is not None
    want_state = layer is not None

    hps = _heads_per_step(T)
    wh = hps * DH

    def col(off):
        return lambda b, h: (row_blk0 + b, off // hps + h)

    in_specs = [pl.BlockSpec((T, wh), col(0)), pl.BlockSpec((T, wh), col(NH)),
                pl.BlockSpec((T, wh), col(2 * NH)),
                pl.BlockSpec((T, DH), lambda b, h: (row_blk0 + b, 0)),
                pl.BlockSpec((DH, T), lambda b, h: (0, row_blk0 + b)),
                pl.BlockSpec((9, wh), lambda b, h: (0, h)),
                pl.BlockSpec((1, wh), lambda b, h: (0, h)),
                pl.BlockSpec((hps, DH, DH), lambda b, h: (h, 0, 0)),
                pl.BlockSpec((hps, DH, DH), lambda b, h: (h, 0, 0)),
                pl.BlockSpec((1, DH), lambda b, h: (0, 0)),
                pl.BlockSpec((2, NH, 1), lambda b, h: (0, 0, 0)),
                pl.BlockSpec((1, DH), lambda b, h: (0, 0))]
    args = [bb, bb, bb, gates, gates_t, conv_w9, conv_b, wq, wk, fb_row, fb_col, gain]
    if has_state:
        c0aug, m0 = state
        in_specs += [pl.BlockSpec((None, 2, hps, DH, 2 * DH), lambda b, h: (b, 0, h, 0, 0)),
                     pl.BlockSpec((None, 2, hps, 1, 1), lambda b, h: (b, 0, h, 0, 0))]
        args += [c0aug, m0]
    out_specs = [pl.BlockSpec((T, wh), col(0))]
    out_shape = [jax.ShapeDtypeStruct((bb.shape[0], D), bf16)]
    aliases = {}
    if y_buf is not None:
        in_specs.append(pl.BlockSpec(memory_space=pl.ANY))
        args.append(y_buf)
        aliases[len(args) - 1] = 0
    if want_state:
        out_specs += [pl.BlockSpec((None, None, 2, hps, DH, DH), lambda b, h: (b, layer, 0, h, 0, 0)),
                      pl.BlockSpec((None, 2, hps, 1, DH), lambda b, h: (b, 0, h, 0, 0)),
                      pl.BlockSpec((None, 2, hps, 1, DH), lambda b, h: (b, 0, h, 0, 0))]
        out_shape += [jax.ShapeDtypeStruct((n_seq, DEPTH, 2, NH, DH, DH), f32),
                      jax.ShapeDtypeStruct((n_seq, 2, NH, 1, DH), f32),
                      jax.ShapeDtypeStruct((n_seq, 2, NH, 1, DH), f32)]
        if c_buf is not None:
            in_specs.append(pl.BlockSpec(memory_space=pl.ANY))
            args.append(c_buf)
            aliases[len(args) - 1] = 1
    return pl.pallas_call(
        functools.partial(_mlstm_kernel, grid_rows=grid_rows, has_state=has_state,
                          want_state=want_state, n_aliased=len(aliases)),
        grid=(n_seq, NH // hps),
        in_specs=in_specs, out_specs=out_specs, out_shape=out_shape,
        input_output_aliases=aliases,
        compiler_params=_cparams("parallel", "parallel"),
        name=name,
    )(*args)


def _software_pipeline(n, stages):
    vals = [None] * n
    for t in range(n + len(stages) - 1):
        for k in range(len(stages) - 1, -1, -1):
            s = t - k
            if 0 <= s < n:
                vals[s] = stages[k](s, vals[s])


SUB_ROWS = 256


def _merge_kernel(ya_ref, yb_ref, gm_ref, x_ref, mod_ref, wa_ref, wb_ref, wo_ref, lg_ref, lb_ref,
                  x1_ref, hh_ref):
    def rows(s):
        return slice(s * SUB_ROWS, (s + 1) * SUB_ROWS)

    def branches(s, _):
        return (jnp.dot(ya_ref[rows(s), :], wa_ref[...], preferred_element_type=f32),
                jnp.dot(yb_ref[rows(s), :], wb_ref[...], preferred_element_type=f32))

    def gate(s, p):
        gma = gm_ref[rows(s), :D].astype(f32)
        gmb = gm_ref[rows(s), D:].astype(f32)
        return (_sigmoid(gma) * p[0] + _sigmoid(gmb) * p[1]).astype(bf16)

    def project(s, merged):
        return jnp.dot(merged, wo_ref[...], preferred_element_type=f32)

    def norm(s, y):
        x1 = _layer_norm(ALPHA * x_ref[rows(s), :] + mod_ref[2:3, :] * y, lg_ref[...], lb_ref[...])
        x1_ref[rows(s), :] = x1
        hh_ref[rows(s), :] = (x1 * (1.0 + mod_ref[4:5, :]) + mod_ref[3:4, :]).astype(hh_ref.dtype)

    _software_pipeline(ya_ref.shape[0] // SUB_ROWS, (branches, gate, project, norm))


def merge(ya, yb, gm, x, mod_l, wa, wb, wo, lg, lb, hh_dtype):
    n = x.shape[0]
    tm = 512
    grp = _group_of_tile(tm)
    row = lambda i: (i, 0)
    const = lambda i: (0, 0)
    return pl.pallas_call(
        _merge_kernel,
        grid=(n // tm,),
        in_specs=[pl.BlockSpec((tm, D), row), pl.BlockSpec((tm, D), row),
                  pl.BlockSpec((tm, 2 * D), row), pl.BlockSpec((tm, D), row),
                  pl.BlockSpec((None, 8, D), lambda i: (grp(i), 0, 0)),
                  pl.BlockSpec((D, D), const), pl.BlockSpec((D, D), const), pl.BlockSpec((D, D), const),
                  pl.BlockSpec((1, D), const), pl.BlockSpec((1, D), const)],
        out_specs=[pl.BlockSpec((tm, D), row), pl.BlockSpec((tm, D), row)],
        out_shape=[jax.ShapeDtypeStruct((n, D), f32), jax.ShapeDtypeStruct((n, D), hh_dtype)],
        compiler_params=_cparams("parallel"),
        name="merge",
    )(ya, yb, gm, x, mod_l, wa, wb, wo, lg.reshape(1, D), lb.reshape(1, D))


def _post_ffn(f, x1, mod, mod_next, lg, lb, x2_ref, hn_ref):
    x2 = _layer_norm(ALPHA * x1 + mod[5:6, :] * f, lg, lb)
    x2_ref[...] = x2
    hn_ref[...] = (x2 * (1.0 + mod_next[1:2, :]) + mod_next[0:1, :]).astype(hn_ref.dtype)


def _ffn_kernel(hh_ref, w1_ref, w3_ref, w2_ref, x1_ref, mod_ref, modn_ref, lg_ref, lb_ref,
                x2_ref, hn_ref, acc_ref):
    j = pl.program_id(1)

    @pl.when(j == 0)
    def _():
        acc_ref[...] = jnp.zeros_like(acc_ref)

    def rows(s):
        return slice(s * SUB_ROWS, (s + 1) * SUB_ROWS)

    def up(s, _):
        hh = hh_ref[rows(s), :]
        return (jnp.dot(hh, w1_ref[...], preferred_element_type=f32),
                jnp.dot(hh, w3_ref[...], preferred_element_type=f32))

    def act(s, ab):
        return (ab[0] * _sigmoid(ab[0]) * ab[1]).astype(bf16)

    def down(s, h):
        return jnp.dot(h, w2_ref[...], preferred_element_type=f32)

    def accumulate(s, part):
        acc_ref[rows(s), :] += part

    _software_pipeline(hh_ref.shape[0] // SUB_ROWS, (up, act, down, accumulate))

    @pl.when(j == pl.num_programs(1) - 1)
    def _():
        _post_ffn(acc_ref[...], x1_ref[...], mod_ref, modn_ref, lg_ref[...], lb_ref[...],
                  x2_ref, hn_ref)


def ffn(hh, w1, w3, w2, x1, mod_l, mod_next, lg, lb):
    n = hh.shape[0]
    tm, tf = 512, 1408
    grp = _group_of_tile(tm)
    row = lambda i, j: (i, 0)
    const = lambda i, j: (0, 0)
    return pl.pallas_call(
        _ffn_kernel,
        grid=(n // tm, D_FF // tf),
        in_specs=[pl.BlockSpec((tm, D), row),
                  pl.BlockSpec((D, tf), lambda i, j: (0, j)),
                  pl.BlockSpec((D, tf), lambda i, j: (0, j)),
                  pl.BlockSpec((tf, D), lambda i, j: (j, 0)),
                  pl.BlockSpec((tm, D), row),
                  pl.BlockSpec((None, 8, D), lambda i, j: (grp(i), 0, 0)),
                  pl.BlockSpec((None, 8, D), lambda i, j: (grp(i), 0, 0)),
                  pl.BlockSpec((1, D), const), pl.BlockSpec((1, D), const)],
        out_specs=[pl.BlockSpec((tm, D), row), pl.BlockSpec((tm, D), row)],
        out_shape=[jax.ShapeDtypeStruct((n, D), f32), jax.ShapeDtypeStruct((n, D), bf16)],
        scratch_shapes=[pltpu.VMEM((tm, D), f32)],
        compiler_params=_cparams("parallel", "arbitrary"),
        name="ffn",
    )(hh, w1, w3, w2, x1, mod_l, mod_next, lg.reshape(1, D), lb.reshape(1, D))


def _router_kernel(hh_ref, wr_ref, br_ref, o_ref, cnt_ref, run_ref):
    hh = hh_ref[...].astype(f32)
    wr = wr_ref[...]
    h_hi = hh.astype(bf16)
    h_lo = (hh - h_hi.astype(f32)).astype(bf16)
    w_hi = wr.astype(bf16)
    w_lo = (wr - w_hi.astype(f32)).astype(bf16)
    logits = (jnp.dot(h_hi, w_hi, preferred_element_type=f32)
              + jnp.dot(h_hi, w_lo, preferred_element_type=f32)
              + jnp.dot(h_lo, w_hi, preferred_element_type=f32)) + br_ref[...]
    lane = lax.broadcasted_iota(jnp.int32, logits.shape, 1).astype(f32)
    m1 = jnp.max(logits, axis=-1, keepdims=True)
    i1 = jnp.min(jnp.where(logits == m1, lane, float(DH)), axis=-1, keepdims=True)
    rest = jnp.where(lane == i1, -jnp.inf, logits)
    m2 = jnp.max(rest, axis=-1, keepdims=True)
    i2 = jnp.min(jnp.where(rest == m2, lane, float(DH)), axis=-1, keepdims=True)
    e21 = jnp.exp(m2 - m1)
    p1 = 1.0 / (1.0 + e21)
    p2 = e21 * p1

    @pl.when(pl.program_id(0) == 0)
    def _():
        run_ref[...] = jnp.zeros_like(run_ref)

    tm = logits.shape[0]
    r_i = lax.broadcasted_iota(jnp.int32, (tm, tm), 0)
    c_i = lax.broadcasted_iota(jnp.int32, (tm, tm), 1)
    earlier = jnp.where(c_i < r_i, 1.0, 0.0).astype(bf16)
    oh1 = jnp.where(lane == i1, 1.0, 0.0)
    oh2 = jnp.where(lane == i2, 1.0, 0.0)
    pre1 = jnp.dot(earlier, oh1.astype(bf16), preferred_element_type=f32)
    pre2 = jnp.dot(earlier, oh2.astype(bf16), preferred_element_type=f32)
    cnt1 = jnp.sum(oh1, axis=0, keepdims=True)
    cnt2 = jnp.sum(oh2, axis=0, keepdims=True)
    run = run_ref[...]
    rank1 = jnp.sum(oh1 * (run + pre1), axis=-1, keepdims=True)
    rank2 = jnp.sum(oh2 * (run + cnt1 + pre2), axis=-1, keepdims=True)
    run_ref[...] = run + cnt1 + cnt2
    cnt_ref[...] = jnp.broadcast_to(run + cnt1 + cnt2, cnt_ref.shape)
    o_ref[...] = (jnp.where(lane == 0.0, i1, 0.0) + jnp.where(lane == 1.0, i2, 0.0)
                  + jnp.where(lane == 2.0, p1, 0.0) + jnp.where(lane == 3.0, p2, 0.0)
                  + jnp.where(lane == 4.0, rank1, 0.0) + jnp.where(lane == 5.0, rank2, 0.0))


def router(hh, wr_pad, br_pad):
    n = hh.shape[0]
    tm = 1024
    return pl.pallas_call(
        _router_kernel,
        grid=(n // tm,),
        in_specs=[pl.BlockSpec((tm, D), lambda i: (i, 0)),
                  pl.BlockSpec((D, DH), lambda i: (0, 0)),
                  pl.BlockSpec((1, DH), lambda i: (0, 0))],
        out_specs=[pl.BlockSpec((tm, DH), lambda i: (i, 0)),
                   pl.BlockSpec((8, DH), lambda i: (0, 0))],
        out_shape=[jax.ShapeDtypeStruct((n, DH), f32), jax.ShapeDtypeStruct((8, DH), f32)],
        scratch_shapes=[pltpu.VMEM((1, DH), f32)],
        compiler_params=_cparams("arbitrary"),
        name="router",
    )(hh, wr_pad, br_pad)


def _moe_up_kernel(te_ref, x_ref, w1_ref, w3_ref, a_ref):
    del te_ref
    x = x_ref[...].astype(bf16)
    a = jnp.dot(x, w1_ref[...].astype(bf16), preferred_element_type=f32)
    b = jnp.dot(x, w3_ref[...].astype(bf16), preferred_element_type=f32)
    a_ref[...] = (a * _sigmoid(a) * b).astype(a_ref.dtype)


def _moe_down_kernel(te_ref, a_ref, w2_ref, y_ref):
    del te_ref
    y_ref[...] = jnp.dot(a_ref[...], w2_ref[...].astype(bf16), preferred_element_type=f32)


def moe_experts(tile_expert, xs, w1, w3, w2, lyr):
    r = xs.shape[0]
    nt = r // MOE_TM
    tf = 1408
    up = pl.pallas_call(
        _moe_up_kernel,
        grid_spec=pltpu.PrefetchScalarGridSpec(
            num_scalar_prefetch=1,
            grid=(D_FF // tf, nt),
            in_specs=[pl.BlockSpec((MOE_TM, D), lambda j, i, te: (i, 0)),
                      pl.BlockSpec((None, None, D, tf), lambda j, i, te: (lyr, te[i], 0, j)),
                      pl.BlockSpec((None, None, D, tf), lambda j, i, te: (lyr, te[i], 0, j))],
            out_specs=pl.BlockSpec((MOE_TM, tf), lambda j, i, te: (i, j))),
        out_shape=jax.ShapeDtypeStruct((r, D_FF), bf16),
        compiler_params=_cparams("arbitrary", "arbitrary"),
        name="moe_up",
    )(tile_expert, xs, w1, w3)
    return pl.pallas_call(
        _moe_down_kernel,
        grid_spec=pltpu.PrefetchScalarGridSpec(
            num_scalar_prefetch=1,
            grid=(nt,),
            in_specs=[pl.BlockSpec((MOE_TM, D_FF), lambda i, te: (i, 0)),
                      pl.BlockSpec((None, None, D_FF, D), lambda i, te: (lyr, te[i], 0, 0))],
            out_specs=pl.BlockSpec((MOE_TM, D), lambda i, te: (i, 0))),
        out_shape=jax.ShapeDtypeStruct((r, D), f32),
        compiler_params=_cparams("arbitrary"),
        name="moe_down",
    )(tile_expert, up, w2)


def _combine_kernel(ya_ref, yb_ref, rt_ref, x1_ref, mod_ref, modn_ref, lg_ref, lb_ref, x2_ref, hn_ref):
    f = rt_ref[:, 2:3] * ya_ref[...] + rt_ref[:, 3:4] * yb_ref[...]
    _post_ffn(f, x1_ref[...], mod_ref, modn_ref, lg_ref[...], lb_ref[...], x2_ref, hn_ref)


def combine(ya, yb, rt, x1, mod_l, mod_next, lg, lb):
    n = x1.shape[0]
    tm = 1024
    grp = _group_of_tile(tm)
    row = lambda i: (i, 0)
    const = lambda i: (0, 0)
    return pl.pallas_call(
        _combine_kernel,
        grid=(n // tm,),
        in_specs=[pl.BlockSpec((tm, D), row), pl.BlockSpec((tm, D), row),
                  pl.BlockSpec((tm, DH), row), pl.BlockSpec((tm, D), row),
                  pl.BlockSpec((None, 8, D), lambda i: (grp(i), 0, 0)),
                  pl.BlockSpec((None, 8, D), lambda i: (grp(i), 0, 0)),
                  pl.BlockSpec((1, D), const), pl.BlockSpec((1, D), const)],
        out_specs=[pl.BlockSpec((tm, D), row), pl.BlockSpec((tm, D), row)],
        out_shape=[jax.ShapeDtypeStruct((n, D), f32), jax.ShapeDtypeStruct((n, D), bf16)],
        compiler_params=_cparams("parallel"),
        name="combine",
    )(ya, yb, rt, x1, mod_l, mod_next, lg.reshape(1, D), lb.reshape(1, D))


def _combine_final_kernel(ya_ref, yb_ref, rt_ref, x1_ref, mod_ref, lg_ref, lb_ref, yp_ref, ys_ref,
                          *, ctx_tiles):
    f = rt_ref[:, 2:3] * ya_ref[...] + rt_ref[:, 3:4] * yb_ref[...]
    x2 = _layer_norm(ALPHA * x1_ref[...] + mod_ref[5:6, :] * f, lg_ref[...], lb_ref[...])
    i = pl.program_id(0)

    @pl.when(i < ctx_tiles)
    def _():
        yp_ref[...] = x2

    @pl.when(i >= ctx_tiles)
    def _():
        ys_ref[...] = x2


def combine_final(ya, yb, rt, x1, mod_l, lg, lb, n_ctx):
    n = x1.shape[0]
    tm = 1024
    ctx_tiles = n_ctx // tm
    grp = _group_of_tile(tm)
    row = lambda i: (i, 0)
    const = lambda i: (0, 0)
    return pl.pallas_call(
        functools.partial(_combine_final_kernel, ctx_tiles=ctx_tiles),
        grid=(n // tm,),
        in_specs=[pl.BlockSpec((tm, D), row), pl.BlockSpec((tm, D), row),
                  pl.BlockSpec((tm, DH), row), pl.BlockSpec((tm, D), row),
                  pl.BlockSpec((None, 8, D), lambda i: (grp(i), 0, 0)),
                  pl.BlockSpec((1, D), const), pl.BlockSpec((1, D), const)],
        out_specs=[pl.BlockSpec((tm, D), lambda i: (jnp.minimum(i, ctx_tiles - 1), 0)),
                   pl.BlockSpec((tm, D), lambda i: (jnp.maximum(i - ctx_tiles, 0), 0))],
        out_shape=[jax.ShapeDtypeStruct((n_ctx, D), f32), jax.ShapeDtypeStruct((n - n_ctx, D), f32)],
        compiler_params=_cparams("arbitrary"),
        name="combine_final",
    )(ya, yb, rt, x1, mod_l, lg.reshape(1, D), lb.reshape(1, D))


def moe_dispatch(rt, counts, n):
    r_total = 2 * n + N_EXPERTS * MOE_TM
    nt = r_total // MOE_TM
    cnt = counts[0, :N_EXPERTS].astype(jnp.int32)
    padded = (cnt + MOE_TM - 1) // MOE_TM * MOE_TM
    ends = jnp.cumsum(padded)
    starts = ends - padded
    e_idx = rt[:, 0:2].astype(jnp.int32)
    rank = rt[:, 4:6].astype(jnp.int32)
    onehot = e_idx[..., None] == jnp.arange(N_EXPERTS)[None, None, :]
    dest = rank + jnp.sum(jnp.where(onehot, starts[None, None, :], 0), axis=-1)
    tok = jnp.broadcast_to(jnp.arange(n, dtype=jnp.int32)[:, None], (n, 2))
    src_tok = jnp.zeros((r_total,), jnp.int32).at[dest.reshape(-1)].set(
        tok.reshape(-1), unique_indices=True, mode="promise_in_bounds")
    tile_start = jnp.arange(nt, dtype=jnp.int32) * MOE_TM
    tile_expert = jnp.minimum(jnp.sum((tile_start[:, None] >= ends[None, :]).astype(jnp.int32), axis=1),
                              N_EXPERTS - 1)
    return src_tok, dest, tile_expert


def _take_rows(a, idx):
    return a.at[idx].get(mode="promise_in_bounds")


def kernel(x_prompt, x_sample, c, state_hgrn, state_mlstm_C, state_mlstm_n, state_mlstm_m, c_ctx,
           w_ada, b_ada, w_in, b_in, hgrn_lb_raw, hgrn_norm_g, conv_w, conv_b, w_mq, w_mk,
           mlstm_fbias, mlstm_norm_g, w_branch_a, w_branch_b, w_out, ln1_g, ln1_b, ln2_g, ln2_b,
           ffn_w1, ffn_w3, ffn_w2, moe_router_w, moe_router_b, moe_w1, moe_w3, moe_w2):
    nb, seq, _ = x_prompt.shape
    db, dseq, _ = x_sample.shape
    n_ctx, n_lat = nb * seq, db * dseq
    n = n_ctx + n_lat
    assert n_ctx == N_CTX and dseq == SEQ_GROUP and db == 2

    lb = jnp.cumsum(jax.nn.softmax(hgrn_lb_raw.astype(f32), axis=1), axis=1)
    lb = lb - lb[:, :1]

    cond8 = jnp.concatenate([c_ctx[None, :], c, jnp.zeros((5, D), f32)], axis=0)
    mod = modulation(cond8, w_ada, b_ada)[:, :3, :].reshape(DEPTH, 3, 6, D)
    mod = jnp.pad(mod, ((0, 0), (0, 0), (0, 2), (0, 0)))

    x = jnp.concatenate([x_prompt.reshape(n_ctx, D), x_sample.reshape(n_lat, D)], axis=0)
    h = modulate(x, mod[0])

    sizes = (D,) * 8 + (4 * NH, D, D)
    offs = [0]
    for s in sizes:
        offs.append(offs[-1] + s)

    def cols(w, idxs):
        return jnp.concatenate([w[..., offs[i]:offs[i + 1]] for i in idxs], axis=-1)

    out_n, out_m = [], []
    w_in_bf = w_in.astype(bf16)
    b_in3 = b_in.reshape(DEPTH, 1, b_in.shape[-1])
    buf_a = buf_c = None
    for l in range(DEPTH):
        wl, bl = w_in[l], b_in[l]
        qig = linear(h, w_in_bf, b_in3, bf16, "proj_qig", l, 3, lambda j: jnp.where(j == 0, 0, j + 2))
        lbl = lb[:, l].reshape(1, 2 * D)
        lbp = jnp.concatenate([lbl, 1.0 - lbl, jnp.zeros((6, 2 * D), f32)], axis=0)
        lf, kk = hgrn_gates(h, w_in_bf, b_in3, lbp, l, 2, lambda j: j + 1)
        bb = linear(h, w_in_bf, b_in3, bf16, "proj_b", l, 3, lambda j: j + 5)
        gm = linear(h, cols(wl, (9, 10)).astype(bf16), cols(bl, (9, 10)), bf16, "proj_gm")
        w_g = jnp.pad(cols(wl, (8,)), ((0, 0), (0, DH - 4 * NH))).astype(bf16)
        gates = linear(h, w_g, jnp.pad(cols(bl, (8,)), (0, DH - 4 * NH)), f32, "proj_gates")
        gates_t = gates.T

        gain_a = hgrn_norm_g[l].reshape(1, DH)
        ya, buf_a = hgrn_scan(qig, lf, kk, gain_a, None, None, buf_a, l, n_seq=nb, T=seq,
                              row_blk0=0, name="hgrn_ctx")
        (ya,) = hgrn_scan(qig, lf, kk, gain_a, state_hgrn[:, l], ya, None, None, n_seq=db, T=dseq,
                          row_blk0=n_ctx // dseq, name="hgrn_lat")

        cw9 = conv_w[l].reshape(9, D)
        cb = conv_b[l].reshape(1, D)
        wq = w_mq[l].astype(bf16)
        wk = w_mk[l].astype(bf16)
        fb = mlstm_fbias[l]
        zero8 = jnp.zeros((NH,), f32)
        fb_row = jnp.concatenate([zero8, fb[0], zero8, fb[1], jnp.zeros((DH - 4 * NH,), f32)]).reshape(1, DH)
        fb_col = fb.reshape(2, NH, 1)
        gain_b = mlstm_norm_g[l].reshape(1, DH)
        yb, buf_c, s_n, s_m = mlstm_scan(bb, gates, gates_t, cw9, cb, wq, wk, fb_row, fb_col, gain_b,
                                         None, None, buf_c, l, n_seq=nb, T=seq, row_blk0=0,
                                         grid_rows=None, name="mlstm_ctx")
        c0aug = jnp.concatenate([state_mlstm_C[:, l], state_mlstm_n[:, l][..., None],
                                 jnp.zeros((db, 2, NH, DH, DH - 1), f32)], axis=-1)
        m0 = state_mlstm_m[:, l].reshape(db, 2, NH, 1, 1)
        (yb,) = mlstm_scan(bb, gates, gates_t, cw9, cb, wq, wk, fb_row, fb_col, gain_b,
                           (c0aug, m0), yb, None, None, n_seq=db, T=dseq, row_blk0=n_ctx // dseq,
                           grid_rows=dseq // GRID_W, name="mlstm_lat")

        x1, hh = merge(ya, yb, gm, x, mod[l], w_branch_a[l].astype(bf16), w_branch_b[l].astype(bf16),
                       w_out[l].astype(bf16), ln1_g[l], ln1_b[l], bf16 if l % 2 == 0 else f32)

        mod_next = mod[(l + 1) % DEPTH]
        j = l // 2
        if l % 2 == 0:
            x, h = ffn(hh, ffn_w1[j].astype(bf16), ffn_w3[j].astype(bf16), ffn_w2[j].astype(bf16),
                       x1, mod[l], mod_next, ln2_g[l], ln2_b[l])
        else:
            wr = jnp.pad(moe_router_w[j], ((0, 0), (0, DH - N_EXPERTS)))
            br = jnp.concatenate([moe_router_b[j], jnp.full((DH - N_EXPERTS,), -1e30, f32)]).reshape(1, DH)
            rt, counts = router(hh, wr, br)
            src_tok, dest, tile_expert = moe_dispatch(rt, counts, n)
            ys = moe_experts(tile_expert, _take_rows(hh, src_tok), moe_w1, moe_w3, moe_w2, j)
            ya_rows, yb_rows = _take_rows(ys, dest[:, 0]), _take_rows(ys, dest[:, 1])
            if l == DEPTH - 1:
                y_ctx, y_lat = combine_final(ya_rows, yb_rows, rt, x1, mod[l], ln2_g[l], ln2_b[l], n_ctx)
            else:
                x, h = combine(ya_rows, yb_rows, rt, x1, mod[l], mod_next, ln2_g[l], ln2_b[l])

        out_n.append(s_n.reshape(nb, 2, NH, DH))
        out_m.append(s_m[..., 0, 0])

    y_prompt = y_ctx.reshape(nb, seq, D)
    y_sample = y_lat.reshape(db, dseq, D)
    return (y_prompt, y_sample, buf_a, buf_c, jnp.stack(out_n, axis=1), jnp.stack(out_m, axis=1))
```

```python
import functools

import jax
import jax.numpy as jnp
from jax import lax
from jax.experimental import pallas as pl
from jax.experimental.pallas import tpu as pltpu

f32 = jnp.float32
bf16 = jnp.bfloat16

D = 1024
DEPTH = 4
GRID_W = 64
NH = 8
DH = 128
CH_A = 16
HALF_A = 8
SC_A = 128
SCAN_ROWS_PER_STEP = 4096
CH_B = 256
N_EXPERTS = 8
D_FF = 2816
ALPHA = (2 * DEPTH) ** 0.25
SEQ_GROUP = 1024
LOG2E = 1.4426950408889634
LOG2_F_FLOOR = -1e30
MOE_TM = 512

VMEM_LIMIT = 56 * 1024 * 1024
NT = (((1,), (1,)), ((), ()))
TN = (((0,), (0,)), ((), ()))


def _heads_per_step(seq_len):
    return max(1, min(NH, SCAN_ROWS_PER_STEP // seq_len))


def _cparams(*sem):
    return pltpu.CompilerParams(dimension_semantics=sem, vmem_limit_bytes=VMEM_LIMIT)


def _sigmoid(x):
    return 1.0 / (1.0 + jnp.exp(-x))


def _log_sigmoid(t):
    return jnp.minimum(t, 0.0) - jnp.log1p(jnp.exp(-jnp.abs(t)))


def _layer_norm(r, g, b):
    mu = jnp.mean(r, axis=-1, keepdims=True)
    d = r - mu
    var = jnp.mean(d * d, axis=-1, keepdims=True)
    return d * lax.rsqrt(var + 1e-5) * g + b


def _mod_kernel(c_ref, wt_ref, wb_ref, b_ref, o_ref):
    c = c_ref[...]
    act = (c * _sigmoid(c)).astype(bf16)
    half = D // 2
    o_ref[...] = (jnp.dot(act[:, :half], wt_ref[...].astype(bf16), preferred_element_type=f32)
                  + jnp.dot(act[:, half:], wb_ref[...].astype(bf16), preferred_element_type=f32)
                  + b_ref[...])


def modulation(cond8, w_ada, b_ada):
    L, _, n = w_ada.shape
    tn = 1536
    half = D // 2
    return pl.pallas_call(
        _mod_kernel,
        grid=(L, n // tn),
        in_specs=[pl.BlockSpec((8, D), lambda l, j: (0, 0)),
                  pl.BlockSpec((None, half, tn), lambda l, j: (l, 0, j)),
                  pl.BlockSpec((None, half, tn), lambda l, j: (l, 1, j)),
                  pl.BlockSpec((None, 1, tn), lambda l, j: (l, 0, j))],
        out_specs=pl.BlockSpec((None, 8, tn), lambda l, j: (l, 0, j)),
        out_shape=jax.ShapeDtypeStruct((L, 8, n), f32),
        compiler_params=_cparams("parallel", "parallel"),
        name="modulation",
    )(cond8, w_ada, w_ada, b_ada.reshape(L, 1, n))


N_CTX = 32 * 256


def _group_of_tile(tm):
    def idx(i):
        return jnp.maximum(i * tm // SEQ_GROUP - (N_CTX // SEQ_GROUP - 1), 0)
    return idx


def _modulate_kernel(xc_ref, xl_ref, mod_ref, x_ref, h_ref, *, ctx_tiles):
    x = jnp.where(pl.program_id(0) < ctx_tiles, xc_ref[...], xl_ref[...])
    x_ref[...] = x
    h_ref[...] = (x * (1.0 + mod_ref[1:2, :]) + mod_ref[0:1, :]).astype(h_ref.dtype)


def modulate(x_ctx, x_lat, mod_l):
    n_ctx, n = x_ctx.shape[0], x_ctx.shape[0] + x_lat.shape[0]
    tm = 1024
    ctx_tiles = n_ctx // tm
    grp = _group_of_tile(tm)
    return pl.pallas_call(
        functools.partial(_modulate_kernel, ctx_tiles=ctx_tiles),
        grid=(n // tm,),
        in_specs=[pl.BlockSpec((tm, D), lambda i: (jnp.minimum(i, ctx_tiles - 1), 0)),
                  pl.BlockSpec((tm, D), lambda i: (jnp.maximum(i - ctx_tiles, 0), 0)),
                  pl.BlockSpec((None, 8, D), lambda i: (grp(i), 0, 0))],
        out_specs=[pl.BlockSpec((tm, D), lambda i: (i, 0)), pl.BlockSpec((tm, D), lambda i: (i, 0))],
        out_shape=[jax.ShapeDtypeStruct((n, D), f32), jax.ShapeDtypeStruct((n, D), bf16)],
        compiler_params=_cparams("parallel"),
        name="modulate",
    )(x_ctx, x_lat, mod_l)


def _linear_kernel(x_ref, w_ref, b_ref, o_ref):
    acc = jnp.dot(x_ref[...], w_ref[...], preferred_element_type=f32)
    o_ref[...] = (acc + b_ref[...]).astype(o_ref.dtype)


def _weight_specs(w, b, layer, n_blocks, col_of, tn):
    if layer is None:
        k, n = w.shape
        w, b, layer, col_of = w.reshape(1, k, n), b.reshape(1, 1, n), 0, (lambda j: j)
    else:
        k, n = w.shape[1], n_blocks * tn
    specs = [pl.BlockSpec((None, k, tn), lambda i, j: (layer, 0, col_of(j))),
             pl.BlockSpec((None, 1, tn), lambda i, j: (layer, 0, col_of(j)))]
    return w, b, n, specs


def linear(x, w, b, out_dtype, name, layer=None, n_blocks=None, col_of=None):
    m, k = x.shape
    tm = 2048
    tn = 1024 if (layer is not None or w.shape[1] % 1024 == 0) else w.shape[1]
    w, b, n, w_specs = _weight_specs(w, b, layer, n_blocks, col_of, tn)
    return pl.pallas_call(
        _linear_kernel,
        grid=(m // tm, n // tn),
        in_specs=[pl.BlockSpec((tm, k), lambda i, j: (i, 0))] + w_specs,
        out_specs=pl.BlockSpec((tm, tn), lambda i, j: (i, j)),
        out_shape=jax.ShapeDtypeStruct((m, n), out_dtype),
        compiler_params=_cparams("parallel", "parallel"),
        name=name,
    )(x, w, b)


def _hgrn_gate_kernel(x_ref, w_ref, b_ref, lbp_ref, lf_ref, kk_ref):
    z = jnp.dot(x_ref[...], w_ref[...], preferred_element_type=f32) + b_ref[...]
    lb = lbp_ref[0:1, :]
    one_m_lb = lbp_ref[1:2, :]
    e = jnp.exp(-jnp.abs(z))
    t = one_m_lb / (1.0 + e)
    kk_ref[...] = t * jnp.where(z >= 0.0, e, 1.0)
    f = lb + t * jnp.where(z >= 0.0, 1.0, e)
    lf_ref[...] = jnp.maximum(jnp.log(f) * LOG2E, LOG2_F_FLOOR)


def hgrn_gates(x, w, b, lbp, layer, n_blocks, col_of):
    m, k = x.shape
    tm, tn = 1024, 1024
    w, b, n, w_specs = _weight_specs(w, b, layer, n_blocks, col_of, tn)
    return pl.pallas_call(
        _hgrn_gate_kernel,
        grid=(m // tm, n // tn),
        in_specs=[pl.BlockSpec((tm, k), lambda i, j: (i, 0))] + w_specs
                 + [pl.BlockSpec((8, tn), lambda i, j: (0, j))],
        out_specs=[pl.BlockSpec((tm, tn), lambda i, j: (i, j)),
                   pl.BlockSpec((tm, tn), lambda i, j: (i, j))],
        out_shape=[jax.ShapeDtypeStruct((m, n), f32), jax.ShapeDtypeStruct((m, n), f32)],
        compiler_params=_cparams("parallel", "parallel"),
        name="proj_z",
    )(x, w, b, lbp)


def _hgrn_kernel(*refs, has_state, want_state, n_aliased):
    q_ref, v_ref, g_ref, lff_ref, lfb_ref, kkf_ref, kkb_ref, gain_ref = refs[:8]
    lf_refs = (lff_ref, lfb_ref)
    kk_refs = (kkf_ref, kkb_ref)
    pos = 8
    s0_ref = None
    if has_state:
        s0_ref = refs[pos]
        pos += 1
    pos += n_aliased
    y_ref = refs[pos]
    pos += 1
    st_ref = None
    if want_state:
        st_ref = refs[pos]
        pos += 1
    b_ref, o_ref = refs[pos:]

    T = q_ref.shape[0]
    hps = q_ref.shape[1] // DH
    nsc = T // SC_A

    row = lax.broadcasted_iota(jnp.int32, (SC_A, SC_A), 0)
    col = lax.broadcasted_iota(jnp.int32, (SC_A, SC_A), 1)
    tris = tuple(jnp.where(m, 1.0, 0.0).astype(bf16) for m in (row >= col, row <= col))
    nblk = SC_A // CH_A
    half = SC_A // 2
    r_sq = lax.broadcasted_iota(jnp.int32, (half, half), 0)
    c_sq = lax.broadcasted_iota(jnp.int32, (half, half), 1)
    r_w = lax.broadcasted_iota(jnp.int32, (half, SC_A), 0)
    c_w = lax.broadcasted_iota(jnp.int32, (half, SC_A), 1)
    masks_lo = (r_sq >= c_sq, r_w <= c_w)
    masks_hi = (r_w + half >= c_w, r_sq <= c_sq)

    for d in range(2):
        for sc in range(nsc):
            rows = slice(sc * SC_A, (sc + 1) * SC_A)
            b_ref[d, rows, :] = _tri_matmul(tris[d], lf_refs[d][rows, :])

    if has_state:
        st0 = tuple(s0_ref[d, hh].T for hh in range(hps) for d in range(2))
    else:
        st0 = tuple(jnp.zeros((DH, DH), f32) for _ in range(2 * hps))

    def body(c, carry):
        units = []
        for hd in range(2 * hps):
            hh, d = divmod(hd, 2)
            ls = slice(hh * DH, (hh + 1) * DH)
            cidx = c if d == 0 else nsc - 1 - c
            r = pl.ds(pl.multiple_of(cidx * SC_A, SC_A), SC_A)
            units.append((d, ls, r))

        scores = []
        for d, ls, r in units:
            q = q_ref[r, ls].astype(f32)
            kk = kk_refs[d][r, ls]
            b = b_ref[d, r, ls]
            att_rows = ([], [])
            for i in range(nblk):
                mid = i * CH_A + (HALF_A - 1 if d == 0 else HALF_A)
                r_i = b[mid:mid + 1, :]
                blk = slice(i * CH_A, (i + 1) * CH_A)
                qi = (q[blk] * jnp.exp2(b[blk] - r_i)).astype(bf16)
                if d == 0 and i < nblk // 2:
                    ksl = slice(0, half)
                elif d == 1 and i >= nblk // 2:
                    ksl = slice(half, SC_A)
                else:
                    ksl = slice(0, SC_A)
                ks = (kk[ksl] * jnp.exp2(r_i - b[ksl])).astype(bf16)
                att_rows[i >= nblk // 2].append(lax.dot_general(qi, ks, NT, preferred_element_type=f32))
            scores.append(att_rows)

        for hd, (d, ls, r) in enumerate(units):
            v = v_ref[r, ls]
            a_lo = jnp.where(masks_lo[d], jnp.concatenate(scores[hd][0], axis=0), 0.0).astype(bf16)
            a_hi = jnp.where(masks_hi[d], jnp.concatenate(scores[hd][1], axis=0), 0.0).astype(bf16)
            o_lo = jnp.dot(a_lo, v[:half] if d == 0 else v, preferred_element_type=f32)
            o_hi = jnp.dot(a_hi, v if d == 0 else v[half:], preferred_element_type=f32)
            qb = (q_ref[r, ls].astype(f32) * jnp.exp2(b_ref[d, r, ls])).astype(bf16)
            o_ref[d, r, ls] = (jnp.concatenate([o_lo, o_hi], axis=0)
                               + lax.dot_general(qb, carry[hd].astype(bf16), NT,
                                                 preferred_element_type=f32))

        new = []
        for hd, (d, ls, r) in enumerate(units):
            b = b_ref[d, r, ls]
            b_end = b[SC_A - 1:SC_A, :] if d == 0 else b[0:1, :]
            kl = (kk_refs[d][r, ls] * jnp.exp2(b_end - b)).astype(bf16)
            new.append(jnp.exp2(b_end) * carry[hd]
                       + lax.dot_general(v_ref[r, ls], kl, TN, preferred_element_type=f32))
        return tuple(new)

    st_fin = lax.fori_loop(0, nsc, body, st0, unroll=2)

    for hh in range(hps):
        ls = slice(hh * DH, (hh + 1) * DH)
        o = o_ref[0, :, ls] + o_ref[1, :, ls]
        ms = jnp.mean(o * o, axis=-1, keepdims=True)
        g = g_ref[:, ls].astype(f32)
        y_ref[:, ls] = (o * lax.rsqrt(ms + 1e-6) * gain_ref[...] * (g * _sigmoid(g))).astype(y_ref.dtype)
        if want_state:
            for d in range(2):
                st_ref[d, hh] = st_fin[2 * hh + d].T


def hgrn_scan(qig, lf, kk, gain, s0, y_buf, state_buf, layer, *, n_seq, T, row_blk0, name):
    has_state = s0 is not None
    want_state = layer is not None

    hps = _heads_per_step(T)
    wh = hps * DH

    def col(off):
        return lambda b, h: (row_blk0 + b, off // hps + h)

    in_specs = [pl.BlockSpec((T, wh), col(0)), pl.BlockSpec((T, wh), col(NH)),
                pl.BlockSpec((T, wh), col(2 * NH)),
                pl.BlockSpec((T, wh), col(0)), pl.BlockSpec((T, wh), col(NH)),
                pl.BlockSpec((T, wh), col(0)), pl.BlockSpec((T, wh), col(NH)),
                pl.BlockSpec((1, DH), lambda b, h: (0, 0))]
    args = [qig, qig, qig, lf, lf, kk, kk, gain]
    if has_state:
        in_specs.append(pl.BlockSpec((None, 2, hps, DH, DH), lambda b, h: (b, 0, h, 0, 0)))
        args.append(s0)
    out_specs = [pl.BlockSpec((T, wh), col(0))]
    out_shape = [jax.ShapeDtypeStruct((qig.shape[0], D), bf16)]
    aliases = {}
    if y_buf is not None:
        in_specs.append(pl.BlockSpec(memory_space=pl.ANY))
        args.append(y_buf)
        aliases[len(args) - 1] = 0
    if want_state:
        out_specs.append(pl.BlockSpec((None, None, 2, hps, DH, DH),
                                      lambda b, h: (b, layer, 0, h, 0, 0)))
        out_shape.append(jax.ShapeDtypeStruct((n_seq, DEPTH, 2, NH, DH, DH), f32))
        if state_buf is not None:
            in_specs.append(pl.BlockSpec(memory_space=pl.ANY))
            args.append(state_buf)
            aliases[len(args) - 1] = 1
    scratch = [pltpu.VMEM((2, T, wh), f32)] * 2
    return pl.pallas_call(
        functools.partial(_hgrn_kernel, has_state=has_state, want_state=want_state,
                          n_aliased=len(aliases)),
        grid=(n_seq, NH // hps),
        in_specs=in_specs, out_specs=out_specs, out_shape=out_shape,
        scratch_shapes=scratch,
        input_output_aliases=aliases,
        compiler_params=_cparams("parallel", "parallel"),
        name=name,
    )(*args)


def _split3(x):
    x0 = x.astype(bf16)
    r = x - x0.astype(f32)
    x1 = r.astype(bf16)
    x2 = (r - x1.astype(f32)).astype(bf16)
    return x0, x1, x2


def _tri_matmul(tri, x):
    return sum(jnp.dot(tri, p, preferred_element_type=f32) for p in _split3(x))


def _matmul_tri(x, tri):
    return sum(jnp.dot(p, tri, preferred_element_type=f32) for p in _split3(x))


def _mlstm_kernel(*refs, grid_rows, has_state, want_state, n_aliased):
    x_ref, v_ref, og_ref, gc_ref, gr_ref, cw_ref, cb_ref, wq_ref, wk_ref, fbr_ref, fbc_ref, gain_ref = refs[:12]
    pos = 12
    if has_state:
        c0_ref, m0_ref = refs[pos:pos + 2]
        pos += 2
    pos += n_aliased
    y_ref = refs[pos]
    pos += 1
    if want_state:
        ct_ref, nt_ref, mt_ref = refs[pos:pos + 3]

    hps = x_ref.shape[1] // DH
    head0 = pl.program_id(1) * hps
    T = x_ref.shape[0]
    L = min(CH_B, T)
    nchunk = T // L

    t_idx = lax.broadcasted_iota(jnp.int32, (T, DH), 0)
    lane = lax.broadcasted_iota(jnp.int32, (T, DH), 1)
    ones_col = jnp.where(lane == 0, 1.0, 0.0)
    if grid_rows is None:
        taps = [(0, dc) for dc in (-1, 0, 1)]
    else:
        taps = [(dr, dc) for dr in (-1, 0, 1) for dc in (-1, 0, 1)]
        c_idx = t_idx % GRID_W
    cas, vaugs = [], []
    for hh in range(hps):
        ls = slice(hh * DH, (hh + 1) * DH)
        x = x_ref[:, ls].astype(f32)
        acc = jnp.zeros((T, DH), f32) + cb_ref[:, ls]
        for dr, dc in taps:
            off = dr * GRID_W + dc
            tap = (dr + 1) * 3 + (dc + 1)
            w = cw_ref[tap:tap + 1, ls]
            xs = x if off == 0 else pltpu.roll(x, (-off) % T, axis=0)
            ok = (t_idx + off >= 0) & (t_idx + off < T)
            if grid_rows is not None:
                ok = ok & (c_idx + dc >= 0) & (c_idx + dc < GRID_W)
            acc = acc + jnp.where(ok, xs, 0.0) * w
        cas.append((acc * _sigmoid(acc)).astype(bf16))
        vaugs.append(jnp.concatenate([v_ref[:, ls].astype(f32), ones_col], axis=1))
    qs = [jnp.dot(cas[hh], wq_ref[hh], preferred_element_type=f32).astype(bf16) for hh in range(hps)]
    ks = [(jnp.dot(cas[hh], wk_ref[hh], preferred_element_type=f32) * (DH ** -0.5)).astype(bf16)
          for hh in range(hps)]

    gate_lane = lax.broadcasted_iota(jnp.int32, (T, DH), 1) // NH
    is_f = (gate_lane == 1) | (gate_lane == 3)
    gc = gc_ref[...] + fbr_ref[...]
    gc = jnp.where(is_f, _log_sigmoid(gc), gc)

    def pick_lane(a, ln):
        sel = lax.broadcasted_iota(jnp.int32, a.shape, 1) == ln
        return jnp.sum(jnp.where(sel, a, 0.0), axis=1, keepdims=True)

    def pick_row(a, rw):
        sel = lax.broadcasted_iota(jnp.int32, a.shape, 0) == rw
        return jnp.sum(jnp.where(sel, a, 0.0), axis=0, keepdims=True)

    r_i = lax.broadcasted_iota(jnp.int32, (L, L), 0)
    c_i = lax.broadcasted_iota(jnp.int32, (L, L), 1)
    lower = r_i >= c_i
    upper = r_i <= c_i
    tri_lo = jnp.where(lower, 1.0, 0.0).astype(bf16)
    tri_up = jnp.where(upper, 1.0, 0.0).astype(bf16)

    h_parts = [[[None] * nchunk for _ in range(2)] for _ in range(hps)]
    units = [(hh, d) for hh in range(hps) for d in range(2)]
    if has_state:
        carry = {u: (c0_ref[u[1], u[0]], m0_ref[u[1], u[0]]) for u in units}
    else:
        carry = {u: (jnp.zeros((DH, 2 * DH), f32), jnp.zeros((1, 1), f32)) for u in units}
    masks = (lower, upper)
    tri_cs = (tri_lo, tri_up)
    tri_rs = (tri_up, tri_lo)

    for step in range(nchunk):
        cidx = (step, nchunk - 1 - step)
        rsl = tuple(slice(c * L, (c + 1) * L) for c in cidx)

        g = {}
        for d in range(2):
            gcc = gc[rsl[d]]
            lf_r8 = _log_sigmoid(gr_ref[(2 * d + 1) * NH:(2 * d + 2) * NH, rsl[d]] + fbc_ref[d])
            g[d] = (gcc, _tri_matmul(tri_cs[d], gcc), _matmul_tri(lf_r8, tri_rs[d]),
                    gr_ref[2 * d * NH:(2 * d + 1) * NH, rsl[d]])
        picks = {}
        for hh, d in units:
            head = head0 + hh
            gcc, F_c_all, F_r_all, ig_r_all = g[d]
            ig_c = pick_lane(gcc, 2 * d * NH + head)
            F_c = pick_lane(F_c_all, (2 * d + 1) * NH + head)
            ig_r = pick_row(ig_r_all, head)
            F_r = pick_row(F_r_all, head)
            F_end = F_c[L - 1:L, :] if d == 0 else F_c[0:1, :]
            picks[hh, d] = (ig_c, F_c, ig_r, F_r, F_end)

        decay = {}
        for u in units:
            ig_c, F_c, ig_r, F_r, F_end = picks[u]
            m_prev = carry[u][1]
            logd = jnp.where(masks[u[1]], F_c - F_r + ig_r, -jnp.inf)
            log_inter = F_c + m_prev
            m_q = jnp.maximum(log_inter, jnp.max(logd, axis=-1, keepdims=True))
            decay[u] = (jnp.exp(logd - m_q), jnp.exp(log_inter - m_q), m_q)

        scores = {}
        for hh, d in units:
            qc, kc = qs[hh][rsl[d]], ks[hh][rsl[d]]
            scores[hh, d] = (lax.dot_general(qc, kc, NT, preferred_element_type=f32)
                             * decay[hh, d][0]).astype(bf16)

        for hh, d in units:
            _, a, m_q = decay[hh, d]
            qc = qs[hh][rsl[d]]
            va = vaugs[hh][rsl[d]]
            nd = (a * jnp.dot(qc, carry[hh, d][0].astype(bf16), preferred_element_type=f32)
                  + jnp.dot(scores[hh, d], va.astype(bf16), preferred_element_type=f32))
            num = nd[:, :DH]
            den = nd[:, DH:DH + 1]
            h_parts[hh][d][cidx[d]] = num / jnp.maximum(jnp.abs(den), jnp.exp(-m_q))

        for hh, d in units:
            ig_c, F_c, ig_r, F_r, F_end = picks[hh, d]
            caug, m_prev = carry[hh, d]
            log_w_c = F_end - F_c + ig_c
            log_w_r = F_end - F_r + ig_r
            m_new = jnp.maximum(F_end + m_prev, jnp.max(log_w_r, axis=-1, keepdims=True))
            ws_c = jnp.exp(log_w_c - m_new)
            d0 = jnp.exp(F_end + m_prev - m_new)
            upd = lax.dot_general(ks[hh][rsl[d]], (ws_c * vaugs[hh][rsl[d]]).astype(bf16), TN,
                                  preferred_element_type=f32)
            carry[hh, d] = (d0 * caug + upd, m_new)
    finals = [[carry[hh, d] for d in range(2)] for hh in range(hps)]

    for hh in range(hps):
        ls = slice(hh * DH, (hh + 1) * DH)
        hsum = jnp.concatenate([h_parts[hh][0][c] + h_parts[hh][1][c] for c in range(nchunk)], axis=0)
        mu = jnp.mean(hsum, axis=-1, keepdims=True)
        dv = hsum - mu
        var = jnp.mean(dv * dv, axis=-1, keepdims=True)
        og = og_ref[:, ls].astype(f32)
        y_ref[:, ls] = (dv * lax.rsqrt(var + 1e-6) * gain_ref[...] * _sigmoid(og)).astype(y_ref.dtype)
        if want_state:
            for d in range(2):
                caug, m_fin = finals[hh][d]
                ct_ref[d, hh] = caug[:, :DH]
                nt_ref[d, hh] = caug[:, DH:2 * DH].T[0:1, :]
                mt_ref[d, hh] = jnp.broadcast_to(m_fin, (1, DH))


def mlstm_scan(bb, gates, gates_t, conv_w9, conv_b, wq, wk, fb_row, fb_col, gain, state, y_buf, c_buf,
               layer, *, n_seq, T, row_blk0, grid_rows, name):
    has_state = state is not None
    want_state = layer is not None

    hps = _heads_per_step(T)
    wh = hps * DH

    def col(off):
        return lambda b, h: (row_blk0 + b, off // hps + h)

    in_specs = [pl.BlockSpec((T, wh), col(0)), pl.BlockSpec((T, wh), col(NH)),
                pl.BlockSpec((T, wh), col(2 * NH)),
                pl.BlockSpec((T, DH), lambda b, h: (row_blk0 + b, 0)),
                pl.BlockSpec((DH, T), lambda b, h: (0, row_blk0 + b)),
                pl.BlockSpec((9, wh), lambda b, h: (0, h)),
                pl.BlockSpec((1, wh), lambda b, h: (0, h)),
                pl.BlockSpec((hps, DH, DH), lambda b, h: (h, 0, 0)),
                pl.BlockSpec((hps, DH, DH), lambda b, h: (h, 0, 0)),
                pl.BlockSpec((1, DH), lambda b, h: (0, 0)),
                pl.BlockSpec((2, NH, 1), lambda b, h: (0, 0, 0)),
                pl.BlockSpec((1, DH), lambda b, h: (0, 0))]
    args = [bb, bb, bb, gates, gates_t, conv_w9, conv_b, wq, wk, fb_row, fb_col, gain]
    if has_state:
        c0aug, m0 = state
        in_specs += [pl.BlockSpec((None, 2, hps, DH, 2 * DH), lambda b, h: (b, 0, h, 0, 0)),
                     pl.BlockSpec((None, 2, hps, 1, 1), lambda b, h: (b, 0, h, 0, 0))]
        args += [c0aug, m0]
    out_specs = [pl.BlockSpec((T, wh), col(0))]
    out_shape = [jax.ShapeDtypeStruct((bb.shape[0], D), bf16)]
    aliases = {}
    if y_buf is not None:
        in_specs.append(pl.BlockSpec(memory_space=pl.ANY))
        args.append(y_buf)
        aliases[len(args) - 1] = 0
    if want_state:
        out_specs += [pl.BlockSpec((None, None, 2, hps, DH, DH), lambda b, h: (b, layer, 0, h, 0, 0)),
                      pl.BlockSpec((None, 2, hps, 1, DH), lambda b, h: (b, 0, h, 0, 0)),
                      pl.BlockSpec((None, 2, hps, 1, DH), lambda b, h: (b, 0, h, 0, 0))]
        out_shape += [jax.ShapeDtypeStruct((n_seq, DEPTH, 2, NH, DH, DH), f32),
                      jax.ShapeDtypeStruct((n_seq, 2, NH, 1, DH), f32),
                      jax.ShapeDtypeStruct((n_seq, 2, NH, 1, DH), f32)]
        if c_buf is not None:
            in_specs.append(pl.BlockSpec(memory_space=pl.ANY))
            args.append(c_buf)
            aliases[len(args) - 1] = 1
    return pl.pallas_call(
        functools.partial(_mlstm_kernel, grid_rows=grid_rows, has_state=has_state,
                          want_state=want_state, n_aliased=len(aliases)),
        grid=(n_seq, NH // hps),
        in_specs=in_specs, out_specs=out_specs, out_shape=out_shape,
        input_output_aliases=aliases,
        compiler_params=_cparams("parallel", "parallel"),
        name=name,
    )(*args)


def _software_pipeline(n, stages):
    vals = [None] * n
    for t in range(n + len(stages) - 1):
        for k in range(len(stages) - 1, -1, -1):
            s = t - k
            if 0 <= s < n:
                vals[s] = stages[k](s, vals[s])


SUB_ROWS = 256


def _merge_kernel(ya_ref, yb_ref, gm_ref, x_ref, mod_ref, wa_ref, wb_ref, wo_ref, lg_ref, lb_ref,
                  x1_ref, hh_ref):
    def rows(s):
        return slice(s * SUB_ROWS, (s + 1) * SUB_ROWS)

    def branches(s, _):
        return (jnp.dot(ya_ref[rows(s), :], wa_ref[...], preferred_element_type=f32),
                jnp.dot(yb_ref[rows(s), :], wb_ref[...], preferred_element_type=f32))

    def gate(s, p):
        gma = gm_ref[rows(s), :D].astype(f32)
        gmb = gm_ref[rows(s), D:].astype(f32)
        return (_sigmoid(gma) * p[0] + _sigmoid(gmb) * p[1]).astype(bf16)

    def project(s, merged):
        return jnp.dot(merged, wo_ref[...], preferred_element_type=f32)

    def norm(s, y):
        x1 = _layer_norm(ALPHA * x_ref[rows(s), :] + mod_ref[2:3, :] * y, lg_ref[...], lb_ref[...])
        x1_ref[rows(s), :] = x1
        hh_ref[rows(s), :] = (x1 * (1.0 + mod_ref[4:5, :]) + mod_ref[3:4, :]).astype(hh_ref.dtype)

    _software_pipeline(ya_ref.shape[0] // SUB_ROWS, (branches, gate, project, norm))


def merge(ya, yb, gm, x, mod_l, wa, wb, wo, lg, lb, hh_dtype):
    n = x.shape[0]
    tm = 512
    grp = _group_of_tile(tm)
    row = lambda i: (i, 0)
    const = lambda i: (0, 0)
    return pl.pallas_call(
        _merge_kernel,
        grid=(n // tm,),
        in_specs=[pl.BlockSpec((tm, D), row), pl.BlockSpec((tm, D), row),
                  pl.BlockSpec((tm, 2 * D), row), pl.BlockSpec((tm, D), row),
                  pl.BlockSpec((None, 8, D), lambda i: (grp(i), 0, 0)),
                  pl.BlockSpec((D, D), const), pl.BlockSpec((D, D), const), pl.BlockSpec((D, D), const),
                  pl.BlockSpec((1, D), const), pl.BlockSpec((1, D), const)],
        out_specs=[pl.BlockSpec((tm, D), row), pl.BlockSpec((tm, D), row)],
        out_shape=[jax.ShapeDtypeStruct((n, D), f32), jax.ShapeDtypeStruct((n, D), hh_dtype)],
        compiler_params=_cparams("parallel"),
        name="merge",
    )(ya, yb, gm, x, mod_l, wa, wb, wo, lg.reshape(1, D), lb.reshape(1, D))


def _post_ffn(f, x1, mod, mod_next, lg, lb, x2_ref, hn_ref):
    x2 = _layer_norm(ALPHA * x1 + mod[5:6, :] * f, lg, lb)
    x2_ref[...] = x2
    hn_ref[...] = (x2 * (1.0 + mod_next[1:2, :]) + mod_next[0:1, :]).astype(hn_ref.dtype)


def _ffn_kernel(hh_ref, w1_ref, w3_ref, w2_ref, x1_ref, mod_ref, modn_ref, lg_ref, lb_ref,
                x2_ref, hn_ref, acc_ref):
    j = pl.program_id(1)

    @pl.when(j == 0)
    def _():
        acc_ref[...] = jnp.zeros_like(acc_ref)

    def rows(s):
        return slice(s * SUB_ROWS, (s + 1) * SUB_ROWS)

    def up(s, _):
        hh = hh_ref[rows(s), :]
        return (jnp.dot(hh, w1_ref[...], preferred_element_type=f32),
                jnp.dot(hh, w3_ref[...], preferred_element_type=f32))

    def act(s, ab):
        return (ab[0] * _sigmoid(ab[0]) * ab[1]).astype(bf16)

    def down(s, h):
        return jnp.dot(h, w2_ref[...], preferred_element_type=f32)

    def accumulate(s, part):
        acc_ref[rows(s), :] += part

    _software_pipeline(hh_ref.shape[0] // SUB_ROWS, (up, act, down, accumulate))

    @pl.when(j == pl.num_programs(1) - 1)
    def _():
        _post_ffn(acc_ref[...], x1_ref[...], mod_ref, modn_ref, lg_ref[...], lb_ref[...],
                  x2_ref, hn_ref)


def ffn(hh, w1, w3, w2, x1, mod_l, mod_next, lg, lb):
    n = hh.shape[0]
    tm, tf = 512, 1408
    grp = _group_of_tile(tm)
    row = lambda i, j: (i, 0)
    const = lambda i, j: (0, 0)
    return pl.pallas_call(
        _ffn_kernel,
        grid=(n // tm, D_FF // tf),
        in_specs=[pl.BlockSpec((tm, D), row),
                  pl.BlockSpec((D, tf), lambda i, j: (0, j)),
                  pl.BlockSpec((D, tf), lambda i, j: (0, j)),
                  pl.BlockSpec((tf, D), lambda i, j: (j, 0)),
                  pl.BlockSpec((tm, D), row),
                  pl.BlockSpec((None, 8, D), lambda i, j: (grp(i), 0, 0)),
                  pl.BlockSpec((None, 8, D), lambda i, j: (grp(i), 0, 0)),
                  pl.BlockSpec((1, D), const), pl.BlockSpec((1, D), const)],
        out_specs=[pl.BlockSpec((tm, D), row), pl.BlockSpec((tm, D), row)],
        out_shape=[jax.ShapeDtypeStruct((n, D), f32), jax.ShapeDtypeStruct((n, D), bf16)],
        scratch_shapes=[pltpu.VMEM((tm, D), f32)],
        compiler_params=_cparams("parallel", "arbitrary"),
        name="ffn",
    )(hh, w1, w3, w2, x1, mod_l, mod_next, lg.reshape(1, D), lb.reshape(1, D))


def _router_kernel(hh_ref, wr_ref, br_ref, o_ref, cnt_ref, run_ref):
    hh = hh_ref[...].astype(f32)
    wr = wr_ref[...]
    h_hi = hh.astype(bf16)
    h_lo = (hh - h_hi.astype(f32)).astype(bf16)
    w_hi = wr.astype(bf16)
    w_lo = (wr - w_hi.astype(f32)).astype(bf16)
    logits = (jnp.dot(h_hi, w_hi, preferred_element_type=f32)
              + jnp.dot(h_hi, w_lo, preferred_element_type=f32)
              + jnp.dot(h_lo, w_hi, preferred_element_type=f32)) + br_ref[...]
    lane = lax.broadcasted_iota(jnp.int32, logits.shape, 1).astype(f32)
    m1 = jnp.max(logits, axis=-1, keepdims=True)
    i1 = jnp.min(jnp.where(logits == m1, lane, float(DH)), axis=-1, keepdims=True)
    rest = jnp.where(lane == i1, -jnp.inf, logits)
    m2 = jnp.max(rest, axis=-1, keepdims=True)
    i2 = jnp.min(jnp.where(rest == m2, lane, float(DH)), axis=-1, keepdims=True)
    e21 = jnp.exp(m2 - m1)
    p1 = 1.0 / (1.0 + e21)
    p2 = e21 * p1

    @pl.when(pl.program_id(0) == 0)
    def _():
        run_ref[...] = jnp.zeros_like(run_ref)

    tm = logits.shape[0]
    r_i = lax.broadcasted_iota(jnp.int32, (tm, tm), 0)
    c_i = lax.broadcasted_iota(jnp.int32, (tm, tm), 1)
    earlier = jnp.where(c_i < r_i, 1.0, 0.0).astype(bf16)
    oh1 = jnp.where(lane == i1, 1.0, 0.0)
    oh2 = jnp.where(lane == i2, 1.0, 0.0)
    pre1 = jnp.dot(earlier, oh1.astype(bf16), preferred_element_type=f32)
    pre2 = jnp.dot(earlier, oh2.astype(bf16), preferred_element_type=f32)
    cnt1 = jnp.sum(oh1, axis=0, keepdims=True)
    cnt2 = jnp.sum(oh2, axis=0, keepdims=True)
    run = run_ref[...]
    rank1 = jnp.sum(oh1 * (run + pre1), axis=-1, keepdims=True)
    rank2 = jnp.sum(oh2 * (run + cnt1 + pre2), axis=-1, keepdims=True)
    run_ref[...] = run + cnt1 + cnt2
    cnt_ref[...] = jnp.broadcast_to(run + cnt1 + cnt2, cnt_ref.shape)
    o_ref[...] = (jnp.where(lane == 0.0, i1, 0.0) + jnp.where(lane == 1.0, i2, 0.0)
                  + jnp.where(lane == 2.0, p1, 0.0) + jnp.where(lane == 3.0, p2, 0.0)
                  + jnp.where(lane == 4.0, rank1, 0.0) + jnp.where(lane == 5.0, rank2, 0.0))


def router(hh, wr_pad, br_pad):
    n = hh.shape[0]
    tm = 1024
    return pl.pallas_call(
        _router_kernel,
        grid=(n // tm,),
        in_specs=[pl.BlockSpec((tm, D), lambda i: (i, 0)),
                  pl.BlockSpec((D, DH), lambda i: (0, 0)),
                  pl.BlockSpec((1, DH), lambda i: (0, 0))],
        out_specs=[pl.BlockSpec((tm, DH), lambda i: (i, 0)),
                   pl.BlockSpec((8, DH), lambda i: (0, 0))],
        out_shape=[jax.ShapeDtypeStruct((n, DH), f32), jax.ShapeDtypeStruct((8, DH), f32)],
        scratch_shapes=[pltpu.VMEM((1, DH), f32)],
        compiler_params=_cparams("arbitrary"),
        name="router",
    )(hh, wr_pad, br_pad)


def _moe_up_kernel(te_ref, x_ref, w1_ref, w3_ref, a_ref):
    del te_ref
    x = x_ref[...].astype(bf16)
    a = jnp.dot(x, w1_ref[...].astype(bf16), preferred_element_type=f32)
    b = jnp.dot(x, w3_ref[...].astype(bf16), preferred_element_type=f32)
    a_ref[...] = (a * _sigmoid(a) * b).astype(a_ref.dtype)


def _moe_down_kernel(te_ref, a_ref, w2_ref, y_ref):
    del te_ref
    y_ref[...] = jnp.dot(a_ref[...], w2_ref[...].astype(bf16), preferred_element_type=f32)


def moe_experts(tile_expert, xs, w1, w3, w2, lyr):
    r = xs.shape[0]
    nt = r // MOE_TM
    tf = 1408
    up = pl.pallas_call(
        _moe_up_kernel,
        grid_spec=pltpu.PrefetchScalarGridSpec(
            num_scalar_prefetch=1,
            grid=(D_FF // tf, nt),
            in_specs=[pl.BlockSpec((MOE_TM, D), lambda j, i, te: (i, 0)),
                      pl.BlockSpec((None, None, D, tf), lambda j, i, te: (lyr, te[i], 0, j)),
                      pl.BlockSpec((None, None, D, tf), lambda j, i, te: (lyr, te[i], 0, j))],
            out_specs=pl.BlockSpec((MOE_TM, tf), lambda j, i, te: (i, j))),
        out_shape=jax.ShapeDtypeStruct((r, D_FF), bf16),
        compiler_params=_cparams("arbitrary", "arbitrary"),
        name="moe_up",
    )(tile_expert, xs, w1, w3)
    return pl.pallas_call(
        _moe_down_kernel,
        grid_spec=pltpu.PrefetchScalarGridSpec(
            num_scalar_prefetch=1,
            grid=(nt,),
            in_specs=[pl.BlockSpec((MOE_TM, D_FF), lambda i, te: (i, 0)),
                      pl.BlockSpec((None, None, D_FF, D), lambda i, te: (lyr, te[i], 0, 0))],
            out_specs=pl.BlockSpec((MOE_TM, D), lambda i, te: (i, 0))),
        out_shape=jax.ShapeDtypeStruct((r, D), f32),
        compiler_params=_cparams("arbitrary"),
        name="moe_down",
    )(tile_expert, up, w2)


def _combine_kernel(ya_ref, yb_ref, rt_ref, x1_ref, mod_ref, modn_ref, lg_ref, lb_ref, x2_ref, hn_ref):
    f = rt_ref[:, 2:3] * ya_ref[...] + rt_ref[:, 3:4] * yb_ref[...]
    _post_ffn(f, x1_ref[...], mod_ref, modn_ref, lg_ref[...], lb_ref[...], x2_ref, hn_ref)


def combine(ya, yb, rt, x1, mod_l, mod_next, lg, lb):
    n = x1.shape[0]
    tm = 1024
    grp = _group_of_tile(tm)
    row = lambda i: (i, 0)
    const = lambda i: (0, 0)
    return pl.pallas_call(
        _combine_kernel,
        grid=(n // tm,),
        in_specs=[pl.BlockSpec((tm, D), row), pl.BlockSpec((tm, D), row),
                  pl.BlockSpec((tm, DH), row), pl.BlockSpec((tm, D), row),
                  pl.BlockSpec((None, 8, D), lambda i: (grp(i), 0, 0)),
                  pl.BlockSpec((None, 8, D), lambda i: (grp(i), 0, 0)),
                  pl.BlockSpec((1, D), const), pl.BlockSpec((1, D), const)],
        out_specs=[pl.BlockSpec((tm, D), row), pl.BlockSpec((tm, D), row)],
        out_shape=[jax.ShapeDtypeStruct((n, D), f32), jax.ShapeDtypeStruct((n, D), bf16)],
        compiler_params=_cparams("parallel"),
        name="combine",
    )(ya, yb, rt, x1, mod_l, mod_next, lg.reshape(1, D), lb.reshape(1, D))


def _combine_final_kernel(ya_ref, yb_ref, rt_ref, x1_ref, mod_ref, lg_ref, lb_ref, yp_ref, ys_ref,
                          *, ctx_tiles):
    f = rt_ref[:, 2:3] * ya_ref[...] + rt_ref[:, 3:4] * yb_ref[...]
    x2 = _layer_norm(ALPHA * x1_ref[...] + mod_ref[5:6, :] * f, lg_ref[...], lb_ref[...])
    i = pl.program_id(0)

    @pl.when(i < ctx_tiles)
    def _():
        yp_ref[...] = x2

    @pl.when(i >= ctx_tiles)
    def _():
        ys_ref[...] = x2


def combine_final(ya, yb, rt, x1, mod_l, lg, lb, n_ctx):
    n = x1.shape[0]
    tm = 1024
    ctx_tiles = n_ctx // tm
    grp = _group_of_tile(tm)
    row = lambda i: (i, 0)
    const = lambda i: (0, 0)
    return pl.pallas_call(
        functools.partial(_combine_final_kernel, ctx_tiles=ctx_tiles),
        grid=(n // tm,),
        in_specs=[pl.BlockSpec((tm, D), row), pl.BlockSpec((tm, D), row),
                  pl.BlockSpec((tm, DH), row), pl.BlockSpec((tm, D), row),
                  pl.BlockSpec((None, 8, D), lambda i: (grp(i), 0, 0)),
                  pl.BlockSpec((1, D), const), pl.BlockSpec((1, D), const)],
        out_specs=[pl.BlockSpec((tm, D), lambda i: (jnp.minimum(i, ctx_tiles - 1), 0)),
                   pl.BlockSpec((tm, D), lambda i: (jnp.maximum(i - ctx_tiles, 0), 0))],
        out_shape=[jax.ShapeDtypeStruct((n_ctx, D), f32), jax.ShapeDtypeStruct((n - n_ctx, D), f32)],
        compiler_params=_cparams("arbitrary"),
        name="combine_final",
    )(ya, yb, rt, x1, mod_l, lg.reshape(1, D), lb.reshape(1, D))


def moe_dispatch(rt, counts, n):
    r_total = 2 * n + N_EXPERTS * MOE_TM
    nt = r_total // MOE_TM
    cnt = counts[0, :N_EXPERTS].astype(jnp.int32)
    padded = (cnt + MOE_TM - 1) // MOE_TM * MOE_TM
    ends = jnp.cumsum(padded)
    starts = ends - padded
    e_idx = rt[:, 0:2].astype(jnp.int32)
    rank = rt[:, 4:6].astype(jnp.int32)
    onehot = e_idx[..., None] == jnp.arange(N_EXPERTS)[None, None, :]
    dest = rank + jnp.sum(jnp.where(onehot, starts[None, None, :], 0), axis=-1)
    tok = jnp.broadcast_to(jnp.arange(n, dtype=jnp.int32)[:, None], (n, 2))
    src_tok = jnp.zeros((r_total,), jnp.int32).at[dest.reshape(-1)].set(
        tok.reshape(-1), unique_indices=True, mode="promise_in_bounds")
    tile_start = jnp.arange(nt, dtype=jnp.int32) * MOE_TM
    tile_expert = jnp.minimum(jnp.sum((tile_start[:, None] >= ends[None, :]).astype(jnp.int32), axis=1),
                              N_EXPERTS - 1)
    return src_tok, dest, tile_expert


def _take_rows(a, idx):
    return a.at[idx].get(mode="promise_in_bounds")


def kernel(x_prompt, x_sample, c, state_hgrn, state_mlstm_C, state_mlstm_n, state_mlstm_m, c_ctx,
           w_ada, b_ada, w_in, b_in, hgrn_lb_raw, hgrn_norm_g, conv_w, conv_b, w_mq, w_mk,
           mlstm_fbias, mlstm_norm_g, w_branch_a, w_branch_b, w_out, ln1_g, ln1_b, ln2_g, ln2_b,
           ffn_w1, ffn_w3, ffn_w2, moe_router_w, moe_router_b, moe_w1, moe_w3, moe_w2):
    nb, seq, _ = x_prompt.shape
    db, dseq, _ = x_sample.shape
    n_ctx, n_lat = nb * seq, db * dseq
    n = n_ctx + n_lat
    assert n_ctx == N_CTX and dseq == SEQ_GROUP and db == 2

    lb = jnp.cumsum(jax.nn.softmax(hgrn_lb_raw.astype(f32), axis=1), axis=1)
    lb = lb - lb[:, :1]

    cond8 = jnp.concatenate([c_ctx[None, :], c, jnp.zeros((5, D), f32)], axis=0)
    mod = modulation(cond8, w_ada, b_ada)[:, :3, :].reshape(DEPTH, 3, 6, D)
    mod = jnp.pad(mod, ((0, 0), (0, 0), (0, 2), (0, 0)))

    x, h = modulate(x_prompt.reshape(n_ctx, D), x_sample.reshape(n_lat, D), mod[0])

    sizes = (D,) * 8 + (4 * NH, D, D)
    offs = [0]
    for s in sizes:
        offs.append(offs[-1] + s)

    def cols(w, idxs):
        return jnp.concatenate([w[..., offs[i]:offs[i + 1]] for i in idxs], axis=-1)

    out_n, out_m = [], []
    w_in_bf = w_in.astype(bf16)
    b_in3 = b_in.reshape(DEPTH, 1, b_in.shape[-1])
    buf_a = buf_c = None
    for l in range(DEPTH):
        wl, bl = w_in[l], b_in[l]
        qig = linear(h, w_in_bf, b_in3, bf16, "proj_qig", l, 3, lambda j: jnp.where(j == 0, 0, j + 2))
        lbl = lb[:, l].reshape(1, 2 * D)
        lbp = jnp.concatenate([lbl, 1.0 - lbl, jnp.zeros((6, 2 * D), f32)], axis=0)
        lf, kk = hgrn_gates(h, w_in_bf, b_in3, lbp, l, 2, lambda j: j + 1)
        bb = linear(h, w_in_bf, b_in3, bf16, "proj_b", l, 3, lambda j: j + 5)
        gm = linear(h, cols(wl, (9, 10)).astype(bf16), cols(bl, (9, 10)), bf16, "proj_gm")
        w_g = jnp.pad(cols(wl, (8,)), ((0, 0), (0, DH - 4 * NH))).astype(bf16)
        gates = linear(h, w_g, jnp.pad(cols(bl, (8,)), (0, DH - 4 * NH)), f32, "proj_gates")
        gates_t = gates.T

        gain_a = hgrn_norm_g[l].reshape(1, DH)
        ya, buf_a = hgrn_scan(qig, lf, kk, gain_a, None, None, buf_a, l, n_seq=nb, T=seq,
                              row_blk0=0, name="hgrn_ctx")
        (ya,) = hgrn_scan(qig, lf, kk, gain_a, state_hgrn[:, l], ya, None, None, n_seq=db, T=dseq,
                          row_blk0=n_ctx // dseq, name="hgrn_lat")

        cw9 = conv_w[l].reshape(9, D)
        cb = conv_b[l].reshape(1, D)
        wq = w_mq[l].astype(bf16)
        wk = w_mk[l].astype(bf16)
        fb = mlstm_fbias[l]
        zero8 = jnp.zeros((NH,), f32)
        fb_row = jnp.concatenate([zero8, fb[0], zero8, fb[1], jnp.zeros((DH - 4 * NH,), f32)]).reshape(1, DH)
        fb_col = fb.reshape(2, NH, 1)
        gain_b = mlstm_norm_g[l].reshape(1, DH)
        yb, buf_c, s_n, s_m = mlstm_scan(bb, gates, gates_t, cw9, cb, wq, wk, fb_row, fb_col, gain_b,
                                         None, None, buf_c, l, n_seq=nb, T=seq, row_blk0=0,
                                         grid_rows=None, name="mlstm_ctx")
        c0aug = jnp.concatenate([state_mlstm_C[:, l], state_mlstm_n[:, l][..., None],
                                 jnp.zeros((db, 2, NH, DH, DH - 1), f32)], axis=-1)
        m0 = state_mlstm_m[:, l].reshape(db, 2, NH, 1, 1)
        (yb,) = mlstm_scan(bb, gates, gates_t, cw9, cb, wq, wk, fb_row, fb_col, gain_b,
                           (c0aug, m0), yb, None, None, n_seq=db, T=dseq, row_blk0=n_ctx // dseq,
                           grid_rows=dseq // GRID_W, name="mlstm_lat")

        x1, hh = merge(ya, yb, gm, x, mod[l], w_branch_a[l].astype(bf16), w_branch_b[l].astype(bf16),
                       w_out[l].astype(bf16), ln1_g[l], ln1_b[l], bf16 if l % 2 == 0 else f32)

        mod_next = mod[(l + 1) % DEPTH]
        j = l // 2
        if l % 2 == 0:
            x, h = ffn(hh, ffn_w1[j].astype(bf16), ffn_w3[j].astype(bf16), ffn_w2[j].astype(bf16),
                       x1, mod[l], mod_next, ln2_g[l], ln2_b[l])
        else:
            wr = jnp.pad(moe_router_w[j], ((0, 0), (0, DH - N_EXPERTS)))
            br = jnp.concatenate([moe_router_b[j], jnp.full((DH - N_EXPERTS,), -1e30, f32)]).reshape(1, DH)
            rt, counts = router(hh, wr, br)
            src_tok, dest, tile_expert = moe_dispatch(rt, counts, n)
            ys = moe_experts(tile_expert, _take_rows(hh, src_tok), moe_w1, moe_w3, moe_w2, j)
            ya_rows, yb_rows = _take_rows(ys, dest[:, 0]), _take_rows(ys, dest[:, 1])
            if l == DEPTH - 1:
                y_ctx, y_lat = combine_final(ya_rows, yb_rows, rt, x1, mod[l], ln2_g[l], ln2_b[l], n_ctx)
            else:
                x, h = combine(ya_rows, yb_rows, rt, x1, mod[l], mod_next, ln2_g[l], ln2_b[l])

        out_n.append(s_n.reshape(nb, 2, NH, DH))
        out_m.append(s_m[..., 0, 0])

    y_prompt = y_ctx.reshape(nb, seq, D)
    y_sample = y_lat.reshape(db, dseq, D)
    return (y_prompt, y_sample, buf_a, buf_c, jnp.stack(out_n, axis=1), jnp.stack(out_m, axis=1))
```

```python
import functools

import jax
import jax.numpy as jnp
from jax import lax
from jax.experimental import pallas as pl
from jax.experimental.pallas import tpu as pltpu

f32 = jnp.float32
bf16 = jnp.bfloat16

D = 1024
DEPTH = 4
GRID_W = 64
NH = 8
DH = 128
CH_A = 16
HALF_A = 8
SC_A = 128
SCAN_ROWS_PER_STEP = 4096
CH_B = 256
N_EXPERTS = 8
D_FF = 2816
ALPHA = (2 * DEPTH) ** 0.25
SEQ_GROUP = 1024
LOG2E = 1.4426950408889634
LOG2_F_FLOOR = -1e30
MOE_TM = 512

VMEM_LIMIT = 56 * 1024 * 1024
NT = (((1,), (1,)), ((), ()))
TN = (((0,), (0,)), ((), ()))


def _heads_per_step(seq_len):
    return max(1, min(NH, SCAN_ROWS_PER_STEP // seq_len))


def _cparams(*sem):
    return pltpu.CompilerParams(dimension_semantics=sem, vmem_limit_bytes=VMEM_LIMIT)


def _sigmoid(x):
    return 1.0 / (1.0 + jnp.exp(-x))


def _log_sigmoid(t):
    return jnp.minimum(t, 0.0) - jnp.log1p(jnp.exp(-jnp.abs(t)))


def _layer_norm(r, g, b):
    mu = jnp.mean(r, axis=-1, keepdims=True)
    d = r - mu
    var = jnp.mean(d * d, axis=-1, keepdims=True)
    return d * lax.rsqrt(var + 1e-5) * g + b


def _mod_kernel(c_ref, w_ref, b_ref, o_ref):
    c = c_ref[...]
    act = (c * _sigmoid(c)).astype(bf16)
    o_ref[...] = jnp.dot(act, w_ref[...].astype(bf16), preferred_element_type=f32) + b_ref[...]


def modulation(cond8, w_ada, b_ada):
    L, _, n = w_ada.shape
    tn = 1536
    return pl.pallas_call(
        _mod_kernel,
        grid=(L, n // tn),
        in_specs=[pl.BlockSpec((8, D), lambda l, j: (0, 0)),
                  pl.BlockSpec((None, D, tn), lambda l, j: (l, 0, j)),
                  pl.BlockSpec((None, 1, tn), lambda l, j: (l, 0, j))],
        out_specs=pl.BlockSpec((None, 8, tn), lambda l, j: (l, 0, j)),
        out_shape=jax.ShapeDtypeStruct((L, 8, n), f32),
        compiler_params=_cparams("parallel", "parallel"),
        name="modulation",
    )(cond8, w_ada, b_ada.reshape(L, 1, n))


N_CTX = 32 * 256


def _group_of_tile(tm):
    def idx(i):
        return jnp.maximum(i * tm // SEQ_GROUP - (N_CTX // SEQ_GROUP - 1), 0)
    return idx


def _modulate_kernel(xc_ref, xl_ref, mod_ref, x_ref, h_ref, *, ctx_tiles):
    x = jnp.where(pl.program_id(0) < ctx_tiles, xc_ref[...], xl_ref[...])
    x_ref[...] = x
    h_ref[...] = (x * (1.0 + mod_ref[1:2, :]) + mod_ref[0:1, :]).astype(h_ref.dtype)


def modulate(x_ctx, x_lat, mod_l):
    n_ctx, n = x_ctx.shape[0], x_ctx.shape[0] + x_lat.shape[0]
    tm = 1024
    ctx_tiles = n_ctx // tm
    grp = _group_of_tile(tm)
    return pl.pallas_call(
        functools.partial(_modulate_kernel, ctx_tiles=ctx_tiles),
        grid=(n // tm,),
        in_specs=[pl.BlockSpec((tm, D), lambda i: (jnp.minimum(i, ctx_tiles - 1), 0)),
                  pl.BlockSpec((tm, D), lambda i: (jnp.maximum(i - ctx_tiles, 0), 0)),
                  pl.BlockSpec((None, 8, D), lambda i: (grp(i), 0, 0))],
        out_specs=[pl.BlockSpec((tm, D), lambda i: (i, 0)), pl.BlockSpec((tm, D), lambda i: (i, 0))],
        out_shape=[jax.ShapeDtypeStruct((n, D), f32), jax.ShapeDtypeStruct((n, D), bf16)],
        compiler_params=_cparams("parallel"),
        name="modulate",
    )(x_ctx, x_lat, mod_l)


def _linear_kernel(x_ref, w_ref, b_ref, o_ref):
    acc = jnp.dot(x_ref[...], w_ref[...], preferred_element_type=f32)
    o_ref[...] = (acc + b_ref[...]).astype(o_ref.dtype)


def _weight_specs(w, b, layer, n_blocks, col_of, tn):
    if layer is None:
        k, n = w.shape
        w, b, layer, col_of = w.reshape(1, k, n), b.reshape(1, 1, n), 0, (lambda j: j)
    else:
        k, n = w.shape[1], n_blocks * tn
    specs = [pl.BlockSpec((None, k, tn), lambda i, j: (layer, 0, col_of(j))),
             pl.BlockSpec((None, 1, tn), lambda i, j: (layer, 0, col_of(j)))]
    return w, b, n, specs


def linear(x, w, b, out_dtype, name, layer=None, n_blocks=None, col_of=None):
    m, k = x.shape
    tm = 2048
    tn = 1024 if (layer is not None or w.shape[1] % 1024 == 0) else w.shape[1]
    w, b, n, w_specs = _weight_specs(w, b, layer, n_blocks, col_of, tn)
    return pl.pallas_call(
        _linear_kernel,
        grid=(m // tm, n // tn),
        in_specs=[pl.BlockSpec((tm, k), lambda i, j: (i, 0))] + w_specs,
        out_specs=pl.BlockSpec((tm, tn), lambda i, j: (i, j)),
        out_shape=jax.ShapeDtypeStruct((m, n), out_dtype),
        compiler_params=_cparams("parallel", "parallel"),
        name=name,
    )(x, w, b)


def _hgrn_gate_kernel(x_ref, w_ref, b_ref, lbp_ref, lf_ref, kk_ref):
    z = jnp.dot(x_ref[...], w_ref[...], preferred_element_type=f32) + b_ref[...]
    lb = lbp_ref[0:1, :]
    one_m_lb = lbp_ref[1:2, :]
    e = jnp.exp(-jnp.abs(z))
    t = one_m_lb / (1.0 + e)
    kk_ref[...] = t * jnp.where(z >= 0.0, e, 1.0)
    f = lb + t * jnp.where(z >= 0.0, 1.0, e)
    lf_ref[...] = jnp.maximum(jnp.log(f) * LOG2E, LOG2_F_FLOOR)


def hgrn_gates(x, w, b, lbp, layer, n_blocks, col_of):
    m, k = x.shape
    tm, tn = 1024, 1024
    w, b, n, w_specs = _weight_specs(w, b, layer, n_blocks, col_of, tn)
    return pl.pallas_call(
        _hgrn_gate_kernel,
        grid=(m // tm, n // tn),
        in_specs=[pl.BlockSpec((tm, k), lambda i, j: (i, 0))] + w_specs
                 + [pl.BlockSpec((8, tn), lambda i, j: (0, j))],
        out_specs=[pl.BlockSpec((tm, tn), lambda i, j: (i, j)),
                   pl.BlockSpec((tm, tn), lambda i, j: (i, j))],
        out_shape=[jax.ShapeDtypeStruct((m, n), f32), jax.ShapeDtypeStruct((m, n), f32)],
        compiler_params=_cparams("parallel", "parallel"),
        name="proj_z",
    )(x, w, b, lbp)


def _hgrn_kernel(*refs, has_state, want_state, n_aliased):
    q_ref, v_ref, g_ref, lff_ref, lfb_ref, kkf_ref, kkb_ref, gain_ref = refs[:8]
    lf_refs = (lff_ref, lfb_ref)
    kk_refs = (kkf_ref, kkb_ref)
    pos = 8
    s0_ref = None
    if has_state:
        s0_ref = refs[pos]
        pos += 1
    pos += n_aliased
    y_ref = refs[pos]
    pos += 1
    st_ref = None
    if want_state:
        st_ref = refs[pos]
        pos += 1
    b_ref, o_ref = refs[pos:]

    T = q_ref.shape[0]
    hps = q_ref.shape[1] // DH
    nsc = T // SC_A

    row = lax.broadcasted_iota(jnp.int32, (SC_A, SC_A), 0)
    col = lax.broadcasted_iota(jnp.int32, (SC_A, SC_A), 1)
    tris = tuple(jnp.where(m, 1.0, 0.0).astype(bf16) for m in (row >= col, row <= col))
    nblk = SC_A // CH_A
    half = SC_A // 2
    r_sq = lax.broadcasted_iota(jnp.int32, (half, half), 0)
    c_sq = lax.broadcasted_iota(jnp.int32, (half, half), 1)
    r_w = lax.broadcasted_iota(jnp.int32, (half, SC_A), 0)
    c_w = lax.broadcasted_iota(jnp.int32, (half, SC_A), 1)
    masks_lo = (r_sq >= c_sq, r_w <= c_w)
    masks_hi = (r_w + half >= c_w, r_sq <= c_sq)

    for d in range(2):
        for sc in range(nsc):
            rows = slice(sc * SC_A, (sc + 1) * SC_A)
            b_ref[d, rows, :] = _tri_matmul(tris[d], lf_refs[d][rows, :])

    if has_state:
        st0 = tuple(s0_ref[d, hh].T for hh in range(hps) for d in range(2))
    else:
        st0 = tuple(jnp.zeros((DH, DH), f32) for _ in range(2 * hps))

    def body(c, carry):
        units = []
        for hd in range(2 * hps):
            hh, d = divmod(hd, 2)
            ls = slice(hh * DH, (hh + 1) * DH)
            cidx = c if d == 0 else nsc - 1 - c
            r = pl.ds(pl.multiple_of(cidx * SC_A, SC_A), SC_A)
            units.append((d, ls, r))

        scores = []
        for d, ls, r in units:
            q = q_ref[r, ls].astype(f32)
            kk = kk_refs[d][r, ls]
            b = b_ref[d, r, ls]
            att_rows = ([], [])
            for i in range(nblk):
                mid = i * CH_A + (HALF_A - 1 if d == 0 else HALF_A)
                r_i = b[mid:mid + 1, :]
                blk = slice(i * CH_A, (i + 1) * CH_A)
                qi = (q[blk] * jnp.exp2(b[blk] - r_i)).astype(bf16)
                if d == 0 and i < nblk // 2:
                    ksl = slice(0, half)
                elif d == 1 and i >= nblk // 2:
                    ksl = slice(half, SC_A)
                else:
                    ksl = slice(0, SC_A)
                ks = (kk[ksl] * jnp.exp2(r_i - b[ksl])).astype(bf16)
                att_rows[i >= nblk // 2].append(lax.dot_general(qi, ks, NT, preferred_element_type=f32))
            scores.append(att_rows)

        for hd, (d, ls, r) in enumerate(units):
            v = v_ref[r, ls]
            a_lo = jnp.where(masks_lo[d], jnp.concatenate(scores[hd][0], axis=0), 0.0).astype(bf16)
            a_hi = jnp.where(masks_hi[d], jnp.concatenate(scores[hd][1], axis=0), 0.0).astype(bf16)
            o_lo = jnp.dot(a_lo, v[:half] if d == 0 else v, preferred_element_type=f32)
            o_hi = jnp.dot(a_hi, v if d == 0 else v[half:], preferred_element_type=f32)
            qb = (q_ref[r, ls].astype(f32) * jnp.exp2(b_ref[d, r, ls])).astype(bf16)
            o_ref[d, r, ls] = (jnp.concatenate([o_lo, o_hi], axis=0)
                               + lax.dot_general(qb, carry[hd].astype(bf16), NT,
                                                 preferred_element_type=f32))

        new = []
        for hd, (d, ls, r) in enumerate(units):
            b = b_ref[d, r, ls]
            b_end = b[SC_A - 1:SC_A, :] if d == 0 else b[0:1, :]
            kl = (kk_refs[d][r, ls] * jnp.exp2(b_end - b)).astype(bf16)
            new.append(jnp.exp2(b_end) * carry[hd]
                       + lax.dot_general(v_ref[r, ls], kl, TN, preferred_element_type=f32))
        return tuple(new)

    st_fin = lax.fori_loop(0, nsc, body, st0, unroll=2)

    for hh in range(hps):
        ls = slice(hh * DH, (hh + 1) * DH)
        o = o_ref[0, :, ls] + o_ref[1, :, ls]
        ms = jnp.mean(o * o, axis=-1, keepdims=True)
        g = g_ref[:, ls].astype(f32)
        y_ref[:, ls] = (o * lax.rsqrt(ms + 1e-6) * gain_ref[...] * (g * _sigmoid(g))).astype(y_ref.dtype)
        if want_state:
            for d in range(2):
                st_ref[d, hh] = st_fin[2 * hh + d].T


def hgrn_scan(qig, lf, kk, gain, s0, y_buf, state_buf, layer, *, n_seq, T, row_blk0, name):
    has_state = s0 is not None
    want_state = layer is not None

    hps = _heads_per_step(T)
    wh = hps * DH

    def col(off):
        return lambda b, h: (row_blk0 + b, off // hps + h)

    in_specs = [pl.BlockSpec((T, wh), col(0)), pl.BlockSpec((T, wh), col(NH)),
                pl.BlockSpec((T, wh), col(2 * NH)),
                pl.BlockSpec((T, wh), col(0)), pl.BlockSpec((T, wh), col(NH)),
                pl.BlockSpec((T, wh), col(0)), pl.BlockSpec((T, wh), col(NH)),
                pl.BlockSpec((1, DH), lambda b, h: (0, 0))]
    args = [qig, qig, qig, lf, lf, kk, kk, gain]
    if has_state:
        in_specs.append(pl.BlockSpec((None, 2, hps, DH, DH), lambda b, h: (b, 0, h, 0, 0)))
        args.append(s0)
    out_specs = [pl.BlockSpec((T, wh), col(0))]
    out_shape = [jax.ShapeDtypeStruct((qig.shape[0], D), bf16)]
    aliases = {}
    if y_buf is not None:
        in_specs.append(pl.BlockSpec(memory_space=pl.ANY))
        args.append(y_buf)
        aliases[len(args) - 1] = 0
    if want_state:
        out_specs.append(pl.BlockSpec((None, None, 2, hps, DH, DH),
                                      lambda b, h: (b, layer, 0, h, 0, 0)))
        out_shape.append(jax.ShapeDtypeStruct((n_seq, DEPTH, 2, NH, DH, DH), f32))
        if state_buf is not None:
            in_specs.append(pl.BlockSpec(memory_space=pl.ANY))
            args.append(state_buf)
            aliases[len(args) - 1] = 1
    scratch = [pltpu.VMEM((2, T, wh), f32)] * 2
    return pl.pallas_call(
        functools.partial(_hgrn_kernel, has_state=has_state, want_state=want_state,
                          n_aliased=len(aliases)),
        grid=(n_seq, NH // hps),
        in_specs=in_specs, out_specs=out_specs, out_shape=out_shape,
        scratch_shapes=scratch,
        input_output_aliases=aliases,
        compiler_params=_cparams("parallel", "parallel"),
        name=name,
    )(*args)


def _split3(x):
    x0 = x.astype(bf16)
    r = x - x0.astype(f32)
    x1 = r.astype(bf16)
    x2 = (r - x1.astype(f32)).astype(bf16)
    return x0, x1, x2


def _tri_matmul(tri, x):
    return sum(jnp.dot(tri, p, preferred_element_type=f32) for p in _split3(x))


def _matmul_tri(x, tri):
    return sum(jnp.dot(p, tri, preferred_element_type=f32) for p in _split3(x))


def _mlstm_kernel(*refs, grid_rows, has_state, want_state, n_aliased):
    x_ref, v_ref, og_ref, gc_ref, gr_ref, cw_ref, cb_ref, wq_ref, wk_ref, fbr_ref, fbc_ref, gain_ref = refs[:12]
    pos = 12
    if has_state:
        c0_ref, m0_ref = refs[pos:pos + 2]
        pos += 2
    pos += n_aliased
    y_ref = refs[pos]
    pos += 1
    if want_state:
        ct_ref, nt_ref, mt_ref = refs[pos:pos + 3]

    hps = x_ref.shape[1] // DH
    head0 = pl.program_id(1) * hps
    T = x_ref.shape[0]
    L = min(CH_B, T)
    nchunk = T // L

    t_idx = lax.broadcasted_iota(jnp.int32, (T, DH), 0)
    lane = lax.broadcasted_iota(jnp.int32, (T, DH), 1)
    ones_col = jnp.where(lane == 0, 1.0, 0.0)
    if grid_rows is None:
        taps = [(0, dc) for dc in (-1, 0, 1)]
    else:
        taps = [(dr, dc) for dr in (-1, 0, 1) for dc in (-1, 0, 1)]
        c_idx = t_idx % GRID_W
    cas, vaugs = [], []
    for hh in range(hps):
        ls = slice(hh * DH, (hh + 1) * DH)
        x = x_ref[:, ls].astype(f32)
        acc = jnp.zeros((T, DH), f32) + cb_ref[:, ls]
        for dr, dc in taps:
            off = dr * GRID_W + dc
            tap = (dr + 1) * 3 + (dc + 1)
            w = cw_ref[tap:tap + 1, ls]
            xs = x if off == 0 else pltpu.roll(x, (-off) % T, axis=0)
            ok = (t_idx + off >= 0) & (t_idx + off < T)
            if grid_rows is not None:
                ok = ok & (c_idx + dc >= 0) & (c_idx + dc < GRID_W)
            acc = acc + jnp.where(ok, xs, 0.0) * w
        cas.append((acc * _sigmoid(acc)).astype(bf16))
        vaugs.append(jnp.concatenate([v_ref[:, ls].astype(f32), ones_col], axis=1))
    qs = [jnp.dot(cas[hh], wq_ref[hh], preferred_element_type=f32).astype(bf16) for hh in range(hps)]
    ks = [(jnp.dot(cas[hh], wk_ref[hh], preferred_element_type=f32) * (DH ** -0.5)).astype(bf16)
          for hh in range(hps)]

    gate_lane = lax.broadcasted_iota(jnp.int32, (T, DH), 1) // NH
    is_f = (gate_lane == 1) | (gate_lane == 3)
    gc = gc_ref[...] + fbr_ref[...]
    gc = jnp.where(is_f, _log_sigmoid(gc), gc)

    def pick_lane(a, ln):
        sel = lax.broadcasted_iota(jnp.int32, a.shape, 1) == ln
        return jnp.sum(jnp.where(sel, a, 0.0), axis=1, keepdims=True)

    def pick_row(a, rw):
        sel = lax.broadcasted_iota(jnp.int32, a.shape, 0) == rw
        return jnp.sum(jnp.where(sel, a, 0.0), axis=0, keepdims=True)

    r_i = lax.broadcasted_iota(jnp.int32, (L, L), 0)
    c_i = lax.broadcasted_iota(jnp.int32, (L, L), 1)
    lower = r_i >= c_i
    upper = r_i <= c_i
    tri_lo = jnp.where(lower, 1.0, 0.0).astype(bf16)
    tri_up = jnp.where(upper, 1.0, 0.0).astype(bf16)

    h_parts = [[[None] * nchunk for _ in range(2)] for _ in range(hps)]
    units = [(hh, d) for hh in range(hps) for d in range(2)]
    if has_state:
        carry = {u: (c0_ref[u[1], u[0]], m0_ref[u[1], u[0]]) for u in units}
    else:
        carry = {u: (jnp.zeros((DH, 2 * DH), f32), jnp.zeros((1, 1), f32)) for u in units}
    masks = (lower, upper)
    tri_cs = (tri_lo, tri_up)
    tri_rs = (tri_up, tri_lo)

    for step in range(nchunk):
        cidx = (step, nchunk - 1 - step)
        rsl = tuple(slice(c * L, (c + 1) * L) for c in cidx)

        g = {}
        for d in range(2):
            gcc = gc[rsl[d]]
            lf_r8 = _log_sigmoid(gr_ref[(2 * d + 1) * NH:(2 * d + 2) * NH, rsl[d]] + fbc_ref[d])
            g[d] = (gcc, _tri_matmul(tri_cs[d], gcc), _matmul_tri(lf_r8, tri_rs[d]),
                    gr_ref[2 * d * NH:(2 * d + 1) * NH, rsl[d]])
        picks = {}
        for hh, d in units:
            head = head0 + hh
            gcc, F_c_all, F_r_all, ig_r_all = g[d]
            ig_c = pick_lane(gcc, 2 * d * NH + head)
            F_c = pick_lane(F_c_all, (2 * d + 1) * NH + head)
            ig_r = pick_row(ig_r_all, head)
            F_r = pick_row(F_r_all, head)
            F_end = F_c[L - 1:L, :] if d == 0 else F_c[0:1, :]
            picks[hh, d] = (ig_c, F_c, ig_r, F_r, F_end)

        decay = {}
        for u in units:
            ig_c, F_c, ig_r, F_r, F_end = picks[u]
            m_prev = carry[u][1]
            logd = jnp.where(masks[u[1]], F_c - F_r + ig_r, -jnp.inf)
            log_inter = F_c + m_prev
            m_q = jnp.maximum(log_inter, jnp.max(logd, axis=-1, keepdims=True))
            decay[u] = (jnp.exp(logd - m_q), jnp.exp(log_inter - m_q), m_q)

        scores = {}
        for hh, d in units:
            qc, kc = qs[hh][rsl[d]], ks[hh][rsl[d]]
            scores[hh, d] = (lax.dot_general(qc, kc, NT, preferred_element_type=f32)
                             * decay[hh, d][0]).astype(bf16)

        for hh, d in units:
            _, a, m_q = decay[hh, d]
            qc = qs[hh][rsl[d]]
            va = vaugs[hh][rsl[d]]
            nd = (a * jnp.dot(qc, carry[hh, d][0].astype(bf16), preferred_element_type=f32)
                  + jnp.dot(scores[hh, d], va.astype(bf16), preferred_element_type=f32))
            num = nd[:, :DH]
            den = nd[:, DH:DH + 1]
            h_parts[hh][d][cidx[d]] = num / jnp.maximum(jnp.abs(den), jnp.exp(-m_q))

        for hh, d in units:
            ig_c, F_c, ig_r, F_r, F_end = picks[hh, d]
            caug, m_prev = carry[hh, d]
            log_w_c = F_end - F_c + ig_c
            log_w_r = F_end - F_r + ig_r
            m_new = jnp.maximum(F_end + m_prev, jnp.max(log_w_r, axis=-1, keepdims=True))
            ws_c = jnp.exp(log_w_c - m_new)
            d0 = jnp.exp(F_end + m_prev - m_new)
            upd = lax.dot_general(ks[hh][rsl[d]], (ws_c * vaugs[hh][rsl[d]]).astype(bf16), TN,
                                  preferred_element_type=f32)
            carry[hh, d] = (d0 * caug + upd, m_new)
    finals = [[carry[hh, d] for d in range(2)] for hh in range(hps)]

    for hh in range(hps):
        ls = slice(hh * DH, (hh + 1) * DH)
        hsum = jnp.concatenate([h_parts[hh][0][c] + h_parts[hh][1][c] for c in range(nchunk)], axis=0)
        mu = jnp.mean(hsum, axis=-1, keepdims=True)
        dv = hsum - mu
        var = jnp.mean(dv * dv, axis=-1, keepdims=True)
        og = og_ref[:, ls].astype(f32)
        y_ref[:, ls] = (dv * lax.rsqrt(var + 1e-6) * gain_ref[...] * _sigmoid(og)).astype(y_ref.dtype)
        if want_state:
            for d in range(2):
                caug, m_fin = finals[hh][d]
                ct_ref[d, hh] = caug[:, :DH]
                nt_ref[d, hh] = caug[:, DH:2 * DH].T[0:1, :]
                mt_ref[d, hh] = jnp.broadcast_to(m_fin, (1, DH))


def mlstm_scan(bb, gates, gates_t, conv_w9, conv_b, wq, wk, fb_row, fb_col, gain, state, y_buf, c_buf,
               layer, *, n_seq, T, row_blk0, grid_rows, name):
    has_state = state is not None
    want_state = layer is not None

    hps = _heads_per_step(T)
    wh = hps * DH

    def col(off):
        return lambda b, h: (row_blk0 + b, off // hps + h)

    in_specs = [pl.BlockSpec((T, wh), col(0)), pl.BlockSpec((T, wh), col(NH)),
                pl.BlockSpec((T, wh), col(2 * NH)),
                pl.BlockSpec((T, DH), lambda b, h: (row_blk0 + b, 0)),
                pl.BlockSpec((DH, T), lambda b, h: (0, row_blk0 + b)),
                pl.BlockSpec((9, wh), lambda b, h: (0, h)),
                pl.BlockSpec((1, wh), lambda b, h: (0, h)),
                pl.BlockSpec((hps, DH, DH), lambda b, h: (h, 0, 0)),
                pl.BlockSpec((hps, DH, DH), lambda b, h: (h, 0, 0)),
                pl.BlockSpec((1, DH), lambda b, h: (0, 0)),
                pl.BlockSpec((2, NH, 1), lambda b, h: (0, 0, 0)),
                pl.BlockSpec((1, DH), lambda b, h: (0, 0))]
    args = [bb, bb, bb, gates, gates_t, conv_w9, conv_b, wq, wk, fb_row, fb_col, gain]
    if has_state:
        c0aug, m0 = state
        in_specs += [pl.BlockSpec((None, 2, hps, DH, 2 * DH), lambda b, h: (b, 0, h, 0, 0)),
                     pl.BlockSpec((None, 2, hps, 1, 1), lambda b, h: (b, 0, h, 0, 0))]
        args += [c0aug, m0]
    out_specs = [pl.BlockSpec((T, wh), col(0))]
    out_shape = [jax.ShapeDtypeStruct((bb.shape[0], D), bf16)]
    aliases = {}
    if y_buf is not None:
        in_specs.append(pl.BlockSpec(memory_space=pl.ANY))
        args.append(y_buf)
        aliases[len(args) - 1] = 0
    if want_state:
        out_specs += [pl.BlockSpec((None, None, 2, hps, DH, DH), lambda b, h: (b, layer, 0, h, 0, 0)),
                      pl.BlockSpec((None, 2, hps, 1, DH), lambda b, h: (b, 0, h, 0, 0)),
                      pl.BlockSpec((None, 2, hps, 1, DH), lambda b, h: (b, 0, h, 0, 0))]
        out_shape += [jax.ShapeDtypeStruct((n_seq, DEPTH, 2, NH, DH, DH), f32),
                      jax.ShapeDtypeStruct((n_seq, 2, NH, 1, DH), f32),
                      jax.ShapeDtypeStruct((n_seq, 2, NH, 1, DH), f32)]
        if c_buf is not None:
            in_specs.append(pl.BlockSpec(memory_space=pl.ANY))
            args.append(c_buf)
            aliases[len(args) - 1] = 1
    return pl.pallas_call(
        functools.partial(_mlstm_kernel, grid_rows=grid_rows, has_state=has_state,
                          want_state=want_state, n_aliased=len(aliases)),
        grid=(n_seq, NH // hps),
        in_specs=in_specs, out_specs=out_specs, out_shape=out_shape,
        input_output_aliases=aliases,
        compiler_params=_cparams("parallel", "parallel"),
        name=name,
    )(*args)


def _software_pipeline(n, stages):
    vals = [None] * n
    for t in range(n + len(stages) - 1):
        for k in range(len(stages) - 1, -1, -1):
            s = t - k
            if 0 <= s < n:
                vals[s] = stages[k](s, vals[s])


SUB_ROWS = 256


def _merge_kernel(ya_ref, yb_ref, gm_ref, x_ref, mod_ref, wa_ref, wb_ref, wo_ref, lg_ref, lb_ref,
                  x1_ref, hh_ref):
    def rows(s):
        return slice(s * SUB_ROWS, (s + 1) * SUB_ROWS)

    def branches(s, _):
        return (jnp.dot(ya_ref[rows(s), :], wa_ref[...], preferred_element_type=f32),
                jnp.dot(yb_ref[rows(s), :], wb_ref[...], preferred_element_type=f32))

    def gate(s, p):
        gma = gm_ref[rows(s), :D].astype(f32)
        gmb = gm_ref[rows(s), D:].astype(f32)
        return (_sigmoid(gma) * p[0] + _sigmoid(gmb) * p[1]).astype(bf16)

    def project(s, merged):
        return jnp.dot(merged, wo_ref[...], preferred_element_type=f32)

    def norm(s, y):
        x1 = _layer_norm(ALPHA * x_ref[rows(s), :] + mod_ref[2:3, :] * y, lg_ref[...], lb_ref[...])
        x1_ref[rows(s), :] = x1
        hh_ref[rows(s), :] = (x1 * (1.0 + mod_ref[4:5, :]) + mod_ref[3:4, :]).astype(hh_ref.dtype)

    _software_pipeline(ya_ref.shape[0] // SUB_ROWS, (branches, gate, project, norm))


def merge(ya, yb, gm, x, mod_l, wa, wb, wo, lg, lb, hh_dtype):
    n = x.shape[0]
    tm = 512
    grp = _group_of_tile(tm)
    row = lambda i: (i, 0)
    const = lambda i: (0, 0)
    return pl.pallas_call(
        _merge_kernel,
        grid=(n // tm,),
        in_specs=[pl.BlockSpec((tm, D), row), pl.BlockSpec((tm, D), row),
                  pl.BlockSpec((tm, 2 * D), row), pl.BlockSpec((tm, D), row),
                  pl.BlockSpec((None, 8, D), lambda i: (grp(i), 0, 0)),
                  pl.BlockSpec((D, D), const), pl.BlockSpec((D, D), const), pl.BlockSpec((D, D), const),
                  pl.BlockSpec((1, D), const), pl.BlockSpec((1, D), const)],
        out_specs=[pl.BlockSpec((tm, D), row), pl.BlockSpec((tm, D), row)],
        out_shape=[jax.ShapeDtypeStruct((n, D), f32), jax.ShapeDtypeStruct((n, D), hh_dtype)],
        compiler_params=_cparams("parallel"),
        name="merge",
    )(ya, yb, gm, x, mod_l, wa, wb, wo, lg.reshape(1, D), lb.reshape(1, D))


def _post_ffn(f, x1, mod, mod_next, lg, lb, x2_ref, hn_ref):
    x2 = _layer_norm(ALPHA * x1 + mod[5:6, :] * f, lg, lb)
    x2_ref[...] = x2
    hn_ref[...] = (x2 * (1.0 + mod_next[1:2, :]) + mod_next[0:1, :]).astype(hn_ref.dtype)


def _ffn_kernel(hh_ref, w1_ref, w3_ref, w2_ref, x1_ref, mod_ref, modn_ref, lg_ref, lb_ref,
                x2_ref, hn_ref, acc_ref):
    j = pl.program_id(1)

    @pl.when(j == 0)
    def _():
        acc_ref[...] = jnp.zeros_like(acc_ref)

    def rows(s):
        return slice(s * SUB_ROWS, (s + 1) * SUB_ROWS)

    def up(s, _):
        hh = hh_ref[rows(s), :]
        return (jnp.dot(hh, w1_ref[...], preferred_element_type=f32),
                jnp.dot(hh, w3_ref[...], preferred_element_type=f32))

    def act(s, ab):
        return (ab[0] * _sigmoid(ab[0]) * ab[1]).astype(bf16)

    def down(s, h):
        return jnp.dot(h, w2_ref[...], preferred_element_type=f32)

    def accumulate(s, part):
        acc_ref[rows(s), :] += part

    _software_pipeline(hh_ref.shape[0] // SUB_ROWS, (up, act, down, accumulate))

    @pl.when(j == pl.num_programs(1) - 1)
    def _():
        _post_ffn(acc_ref[...], x1_ref[...], mod_ref, modn_ref, lg_ref[...], lb_ref[...],
                  x2_ref, hn_ref)


def ffn(hh, w1, w3, w2, x1, mod_l, mod_next, lg, lb):
    n = hh.shape[0]
    tm, tf = 512, 1408
    grp = _group_of_tile(tm)
    row = lambda i, j: (i, 0)
    const = lambda i, j: (0, 0)
    return pl.pallas_call(
        _ffn_kernel,
        grid=(n // tm, D_FF // tf),
        in_specs=[pl.BlockSpec((tm, D), row),
                  pl.BlockSpec((D, tf), lambda i, j: (0, j)),
                  pl.BlockSpec((D, tf), lambda i, j: (0, j)),
                  pl.BlockSpec((tf, D), lambda i, j: (j, 0)),
                  pl.BlockSpec((tm, D), row),
                  pl.BlockSpec((None, 8, D), lambda i, j: (grp(i), 0, 0)),
                  pl.BlockSpec((None, 8, D), lambda i, j: (grp(i), 0, 0)),
                  pl.BlockSpec((1, D), const), pl.BlockSpec((1, D), const)],
        out_specs=[pl.BlockSpec((tm, D), row), pl.BlockSpec((tm, D), row)],
        out_shape=[jax.ShapeDtypeStruct((n, D), f32), jax.ShapeDtypeStruct((n, D), bf16)],
        scratch_shapes=[pltpu.VMEM((tm, D), f32)],
        compiler_params=_cparams("parallel", "arbitrary"),
        name="ffn",
    )(hh, w1, w3, w2, x1, mod_l, mod_next, lg.reshape(1, D), lb.reshape(1, D))


def _router_kernel(hh_ref, wr_ref, br_ref, o_ref, cnt_ref, run_ref):
    hh = hh_ref[...].astype(f32)
    wr = wr_ref[...]
    h_hi = hh.astype(bf16)
    h_lo = (hh - h_hi.astype(f32)).astype(bf16)
    w_hi = wr.astype(bf16)
    w_lo = (wr - w_hi.astype(f32)).astype(bf16)
    logits = (jnp.dot(h_hi, w_hi, preferred_element_type=f32)
              + jnp.dot(h_hi, w_lo, preferred_element_type=f32)
              + jnp.dot(h_lo, w_hi, preferred_element_type=f32)) + br_ref[...]
    lane = lax.broadcasted_iota(jnp.int32, logits.shape, 1).astype(f32)
    m1 = jnp.max(logits, axis=-1, keepdims=True)
    i1 = jnp.min(jnp.where(logits == m1, lane, float(DH)), axis=-1, keepdims=True)
    rest = jnp.where(lane == i1, -jnp.inf, logits)
    m2 = jnp.max(rest, axis=-1, keepdims=True)
    i2 = jnp.min(jnp.where(rest == m2, lane, float(DH)), axis=-1, keepdims=True)
    e21 = jnp.exp(m2 - m1)
    p1 = 1.0 / (1.0 + e21)
    p2 = e21 * p1

    @pl.when(pl.program_id(0) == 0)
    def _():
        run_ref[...] = jnp.zeros_like(run_ref)

    tm = logits.shape[0]
    r_i = lax.broadcasted_iota(jnp.int32, (tm, tm), 0)
    c_i = lax.broadcasted_iota(jnp.int32, (tm, tm), 1)
    earlier = jnp.where(c_i < r_i, 1.0, 0.0).astype(bf16)
    oh1 = jnp.where(lane == i1, 1.0, 0.0)
    oh2 = jnp.where(lane == i2, 1.0, 0.0)
    pre1 = jnp.dot(earlier, oh1.astype(bf16), preferred_element_type=f32)
    pre2 = jnp.dot(earlier, oh2.astype(bf16), preferred_element_type=f32)
    cnt1 = jnp.sum(oh1, axis=0, keepdims=True)
    cnt2 = jnp.sum(oh2, axis=0, keepdims=True)
    run = run_ref[...]
    rank1 = jnp.sum(oh1 * (run + pre1), axis=-1, keepdims=True)
    rank2 = jnp.sum(oh2 * (run + cnt1 + pre2), axis=-1, keepdims=True)
    run_ref[...] = run + cnt1 + cnt2
    cnt_ref[...] = jnp.broadcast_to(run + cnt1 + cnt2, cnt_ref.shape)
    o_ref[...] = (jnp.where(lane == 0.0, i1, 0.0) + jnp.where(lane == 1.0, i2, 0.0)
                  + jnp.where(lane == 2.0, p1, 0.0) + jnp.where(lane == 3.0, p2, 0.0)
                  + jnp.where(lane == 4.0, rank1, 0.0) + jnp.where(lane == 5.0, rank2, 0.0))


def router(hh, wr_pad, br_pad):
    n = hh.shape[0]
    tm = 1024
    return pl.pallas_call(
        _router_kernel,
        grid=(n // tm,),
        in_specs=[pl.BlockSpec((tm, D), lambda i: (i, 0)),
                  pl.BlockSpec((D, DH), lambda i: (0, 0)),
                  pl.BlockSpec((1, DH), lambda i: (0, 0))],
        out_specs=[pl.BlockSpec((tm, DH), lambda i: (i, 0)),
                   pl.BlockSpec((8, DH), lambda i: (0, 0))],
        out_shape=[jax.ShapeDtypeStruct((n, DH), f32), jax.ShapeDtypeStruct((8, DH), f32)],
        scratch_shapes=[pltpu.VMEM((1, DH), f32)],
        compiler_params=_cparams("arbitrary"),
        name="router",
    )(hh, wr_pad, br_pad)


def _moe_up_kernel(te_ref, x_ref, w1_ref, w3_ref, a_ref):
    del te_ref
    x = x_ref[...].astype(bf16)
    a = jnp.dot(x, w1_ref[...].astype(bf16), preferred_element_type=f32)
    b = jnp.dot(x, w3_ref[...].astype(bf16), preferred_element_type=f32)
    a_ref[...] = (a * _sigmoid(a) * b).astype(a_ref.dtype)


def _moe_down_kernel(te_ref, a_ref, w2_ref, y_ref):
    del te_ref
    y_ref[...] = jnp.dot(a_ref[...], w2_ref[...].astype(bf16), preferred_element_type=f32)


def moe_experts(tile_expert, xs, w1, w3, w2, lyr):
    r = xs.shape[0]
    nt = r // MOE_TM
    tf = 1408
    up = pl.pallas_call(
        _moe_up_kernel,
        grid_spec=pltpu.PrefetchScalarGridSpec(
            num_scalar_prefetch=1,
            grid=(D_FF // tf, nt),
            in_specs=[pl.BlockSpec((MOE_TM, D), lambda j, i, te: (i, 0)),
                      pl.BlockSpec((None, None, D, tf), lambda j, i, te: (lyr, te[i], 0, j)),
                      pl.BlockSpec((None, None, D, tf), lambda j, i, te: (lyr, te[i], 0, j))],
            out_specs=pl.BlockSpec((MOE_TM, tf), lambda j, i, te: (i, j))),
        out_shape=jax.ShapeDtypeStruct((r, D_FF), bf16),
        compiler_params=_cparams("arbitrary", "arbitrary"),
        name="moe_up",
    )(tile_expert, xs, w1, w3)
    return pl.pallas_call(
        _moe_down_kernel,
        grid_spec=pltpu.PrefetchScalarGridSpec(
            num_scalar_prefetch=1,
            grid=(nt,),
            in_specs=[pl.BlockSpec((MOE_TM, D_FF), lambda i, te: (i, 0)),
                      pl.BlockSpec((None, None, D_FF, D), lambda i, te: (lyr, te[i], 0, 0))],
            out_specs=pl.BlockSpec((MOE_TM, D), lambda i, te: (i, 0))),
        out_shape=jax.ShapeDtypeStruct((r, D), f32),
        compiler_params=_cparams("arbitrary"),
        name="moe_down",
    )(tile_expert, up, w2)


def _combine_kernel(ya_ref, yb_ref, rt_ref, x1_ref, mod_ref, modn_ref, lg_ref, lb_ref, x2_ref, hn_ref):
    f = rt_ref[:, 2:3] * ya_ref[...] + rt_ref[:, 3:4] * yb_ref[...]
    _post_ffn(f, x1_ref[...], mod_ref, modn_ref, lg_ref[...], lb_ref[...], x2_ref, hn_ref)


def combine(ya, yb, rt, x1, mod_l, mod_next, lg, lb):
    n = x1.shape[0]
    tm = 1024
    grp = _group_of_tile(tm)
    row = lambda i: (i, 0)
    const = lambda i: (0, 0)
    return pl.pallas_call(
        _combine_kernel,
        grid=(n // tm,),
        in_specs=[pl.BlockSpec((tm, D), row), pl.BlockSpec((tm, D), row),
                  pl.BlockSpec((tm, DH), row), pl.BlockSpec((tm, D), row),
                  pl.BlockSpec((None, 8, D), lambda i: (grp(i), 0, 0)),
                  pl.BlockSpec((None, 8, D), lambda i: (grp(i), 0, 0)),
                  pl.BlockSpec((1, D), const), pl.BlockSpec((1, D), const)],
        out_specs=[pl.BlockSpec((tm, D), row), pl.BlockSpec((tm, D), row)],
        out_shape=[jax.ShapeDtypeStruct((n, D), f32), jax.ShapeDtypeStruct((n, D), bf16)],
        compiler_params=_cparams("parallel"),
        name="combine",
    )(ya, yb, rt, x1, mod_l, mod_next, lg.reshape(1, D), lb.reshape(1, D))


def _combine_final_kernel(ya_ref, yb_ref, rt_ref, x1_ref, mod_ref, lg_ref, lb_ref, yp_ref, ys_ref,
                          *, ctx_tiles):
    f = rt_ref[:, 2:3] * ya_ref[...] + rt_ref[:, 3:4] * yb_ref[...]
    x2 = _layer_norm(ALPHA * x1_ref[...] + mod_ref[5:6, :] * f, lg_ref[...], lb_ref[...])
    i = pl.program_id(0)

    @pl.when(i < ctx_tiles)
    def _():
        yp_ref[...] = x2

    @pl.when(i >= ctx_tiles)
    def _():
        ys_ref[...] = x2


def combine_final(ya, yb, rt, x1, mod_l, lg, lb, n_ctx):
    n = x1.shape[0]
    tm = 1024
    ctx_tiles = n_ctx // tm
    grp = _group_of_tile(tm)
    row = lambda i: (i, 0)
    const = lambda i: (0, 0)
    return pl.pallas_call(
        functools.partial(_combine_final_kernel, ctx_tiles=ctx_tiles),
        grid=(n // tm,),
        in_specs=[pl.BlockSpec((tm, D), row), pl.BlockSpec((tm, D), row),
                  pl.BlockSpec((tm, DH), row), pl.BlockSpec((tm, D), row),
                  pl.BlockSpec((None, 8, D), lambda i: (grp(i), 0, 0)),
                  pl.BlockSpec((1, D), const), pl.BlockSpec((1, D), const)],
        out_specs=[pl.BlockSpec((tm, D), lambda i: (jnp.minimum(i, ctx_tiles - 1), 0)),
                   pl.BlockSpec((tm, D), lambda i: (jnp.maximum(i - ctx_tiles, 0), 0))],
        out_shape=[jax.ShapeDtypeStruct((n_ctx, D), f32), jax.ShapeDtypeStruct((n - n_ctx, D), f32)],
        compiler_params=_cparams("arbitrary"),
        name="combine_final",
    )(ya, yb, rt, x1, mod_l, lg.reshape(1, D), lb.reshape(1, D))


def moe_dispatch(rt, counts, n):
    r_total = 2 * n + N_EXPERTS * MOE_TM
    nt = r_total // MOE_TM
    cnt = counts[0, :N_EXPERTS].astype(jnp.int32)
    padded = (cnt + MOE_TM - 1) // MOE_TM * MOE_TM
    ends = jnp.cumsum(padded)
    starts = ends - padded
    e_idx = rt[:, 0:2].astype(jnp.int32)
    rank = rt[:, 4:6].astype(jnp.int32)
    onehot = e_idx[..., None] == jnp.arange(N_EXPERTS)[None, None, :]
    dest = rank + jnp.sum(jnp.where(onehot, starts[None, None, :], 0), axis=-1)
    tok = jnp.broadcast_to(jnp.arange(n, dtype=jnp.int32)[:, None], (n, 2))
    src_tok = jnp.zeros((r_total,), jnp.int32).at[dest.reshape(-1)].set(
        tok.reshape(-1), unique_indices=True, mode="promise_in_bounds")
    tile_start = jnp.arange(nt, dtype=jnp.int32) * MOE_TM
    tile_expert = jnp.minimum(jnp.sum((tile_start[:, None] >= ends[None, :]).astype(jnp.int32), axis=1),
                              N_EXPERTS - 1)
    return src_tok, dest, tile_expert


def _take_rows(a, idx):
    return a.at[idx].get(mode="promise_in_bounds")


def kernel(x_prompt, x_sample, c, state_hgrn, state_mlstm_C, state_mlstm_n, state_mlstm_m, c_ctx,
           w_ada, b_ada, w_in, b_in, hgrn_lb_raw, hgrn_norm_g, conv_w, conv_b, w_mq, w_mk,
           mlstm_fbias, mlstm_norm_g, w_branch_a, w_branch_b, w_out, ln1_g, ln1_b, ln2_g, ln2_b,
           ffn_w1, ffn_w3, ffn_w2, moe_router_w, moe_router_b, moe_w1, moe_w3, moe_w2):
    nb, seq, _ = x_prompt.shape
    db, dseq, _ = x_sample.shape
    n_ctx, n_lat = nb * seq, db * dseq
    n = n_ctx + n_lat
    assert n_ctx == N_CTX and dseq == SEQ_GROUP and db == 2

    lb = jnp.cumsum(jax.nn.softmax(hgrn_lb_raw.astype(f32), axis=1), axis=1)
    lb = lb - lb[:, :1]

    cond8 = jnp.concatenate([c_ctx[None, :], c, jnp.zeros((5, D), f32)], axis=0)
    mod = modulation(cond8, w_ada, b_ada)[:, :3, :].reshape(DEPTH, 3, 6, D)
    mod = jnp.pad(mod, ((0, 0), (0, 0), (0, 2), (0, 0)))

    x, h = modulate(x_prompt.reshape(n_ctx, D), x_sample.reshape(n_lat, D), mod[0])

    sizes = (D,) * 8 + (4 * NH, D, D)
    offs = [0]
    for s in sizes:
        offs.append(offs[-1] + s)

    def cols(w, idxs):
        return jnp.concatenate([w[..., offs[i]:offs[i + 1]] for i in idxs], axis=-1)

    out_n, out_m = [], []
    w_in_bf = w_in.astype(bf16)
    b_in3 = b_in.reshape(DEPTH, 1, b_in.shape[-1])
    buf_a = buf_c = None
    for l in range(DEPTH):
        wl, bl = w_in[l], b_in[l]
        qig = linear(h, w_in_bf, b_in3, bf16, "proj_qig", l, 3, lambda j: jnp.where(j == 0, 0, j + 2))
        lbl = lb[:, l].reshape(1, 2 * D)
        lbp = jnp.concatenate([lbl, 1.0 - lbl, jnp.zeros((6, 2 * D), f32)], axis=0)
        lf, kk = hgrn_gates(h, w_in_bf, b_in3, lbp, l, 2, lambda j: j + 1)
        bb = linear(h, w_in_bf, b_in3, bf16, "proj_b", l, 3, lambda j: j + 5)
        gm = linear(h, cols(wl, (9, 10)).astype(bf16), cols(bl, (9, 10)), bf16, "proj_gm")
        w_g = jnp.pad(cols(wl, (8,)), ((0, 0), (0, DH - 4 * NH))).astype(bf16)
        gates = linear(h, w_g, jnp.pad(cols(bl, (8,)), (0, DH - 4 * NH)), f32, "proj_gates")
        gates_t = gates.T

        gain_a = hgrn_norm_g[l].reshape(1, DH)
        ya, buf_a = hgrn_scan(qig, lf, kk, gain_a, None, None, buf_a, l, n_seq=nb, T=seq,
                              row_blk0=0, name="hgrn_ctx")
        (ya,) = hgrn_scan(qig, lf, kk, gain_a, state_hgrn[:, l], ya, None, None, n_seq=db, T=dseq,
                          row_blk0=n_ctx // dseq, name="hgrn_lat")

        cw9 = conv_w[l].reshape(9, D)
        cb = conv_b[l].reshape(1, D)
        wq = w_mq[l].astype(bf16)
        wk = w_mk[l].astype(bf16)
        fb = mlstm_fbias[l]
        zero8 = jnp.zeros((NH,), f32)
        fb_row = jnp.concatenate([zero8, fb[0], zero8, fb[1], jnp.zeros((DH - 4 * NH,), f32)]).reshape(1, DH)
        fb_col = fb.reshape(2, NH, 1)
        gain_b = mlstm_norm_g[l].reshape(1, DH)
        yb, buf_c, s_n, s_m = mlstm_scan(bb, gates, gates_t, cw9, cb, wq, wk, fb_row, fb_col, gain_b,
                                         None, None, buf_c, l, n_seq=nb, T=seq, row_blk0=0,
                                         grid_rows=None, name="mlstm_ctx")
        c0aug = jnp.concatenate([state_mlstm_C[:, l], state_mlstm_n[:, l][..., None],
                                 jnp.zeros((db, 2, NH, DH, DH - 1), f32)], axis=-1)
        m0 = state_mlstm_m[:, l].reshape(db, 2, NH, 1, 1)
        (yb,) = mlstm_scan(bb, gates, gates_t, cw9, cb, wq, wk, fb_row, fb_col, gain_b,
                           (c0aug, m0), yb, None, None, n_seq=db, T=dseq, row_blk0=n_ctx // dseq,
                           grid_rows=dseq // GRID_W, name="mlstm_lat")

        x1, hh = merge(ya, yb, gm, x, mod[l], w_branch_a[l].astype(bf16), w_branch_b[l].astype(bf16),
                       w_out[l].astype(bf16), ln1_g[l], ln1_b[l], bf16 if l % 2 == 0 else f32)

        mod_next = mod[(l + 1) % DEPTH]
        j = l // 2
        if l % 2 == 0:
            x, h = ffn(hh, ffn_w1[j].astype(bf16), ffn_w3[j].astype(bf16), ffn_w2[j].astype(bf16),
                       x1, mod[l], mod_next, ln2_g[l], ln2_b[l])
        else:
            wr = jnp.pad(moe_router_w[j], ((0, 0), (0, DH - N_EXPERTS)))
            br = jnp.concatenate([moe_router_b[j], jnp.full((DH - N_EXPERTS,), -1e30, f32)]).reshape(1, DH)
            rt, counts = router(hh, wr, br)
            src_tok, dest, tile_expert = moe_dispatch(rt, counts, n)
            ys = moe_experts(tile_expert, _take_rows(hh, src_tok), moe_w1, moe_w3, moe_w2, j)
            ya_rows, yb_rows = _take_rows(ys, dest[:, 0]), _take_rows(ys, dest[:, 1])
            if l == DEPTH - 1:
                y_ctx, y_lat = combine_final(ya_rows, yb_rows, rt, x1, mod[l], ln2_g[l], ln2_b[l], n_ctx)
            else:
                x, h = combine(ya_rows, yb_rows, rt, x1, mod[l], mod_next, ln2_g[l], ln2_b[l])

        out_n.append(s_n.reshape(nb, 2, NH, DH))
        out_m.append(s_m[..., 0, 0])

    y_prompt = y_ctx.reshape(nb, seq, D)
    y_sample = y_lat.reshape(db, dseq, D)
    return (y_prompt, y_sample, buf_a, buf_c, jnp.stack(out_n, axis=1), jnp.stack(out_m, axis=1))
```
